```python
import math
import jax, jax.numpy as jnp
from jax import lax
import numpy as np

D_MODEL = 1024
BATCH = 32
SEQ = 256
DEPTH = 4
DEC_BATCH = 2
DEC_SEQ = 2048
PAST_LEN = 256

GRID_W = 64
N_EVEN = (DEPTH + 1) // 2
N_ODD = DEPTH // 2
Q_BLOCK = 128
LN_EPS = 1e-5
RMS_EPS = 1e-6
DEEPNORM_ALPHA = (2.0 * DEPTH) ** 0.25
DEEPNORM_BETA = (8.0 * DEPTH) ** -0.25
ADA_INIT = 0.5
ROPE_BASE = 10000.0
RET_HEADS = 8
RET_DK = 64
RET_DV = 128
RET_CHUNK = 128
MLA_HEADS = 8
MLA_NOPE = 64
MLA_ROPE = 32
MLA_DV = 64
MLA_KV_RANK = 256
DIFF_HEADS = 8
DIFF_DH = 64
DIFF_DV = 2 * DIFF_DH
MOE_GROUPS = 4
MOE_PER_GROUP = 4
MOE_EXPERTS = MOE_GROUPS * MOE_PER_GROUP
MOE_TOPK = 2
MOE_FF = 256
EV_WIDTHS = (RET_HEADS * RET_DK, RET_HEADS * RET_DK, RET_HEADS * RET_DV, RET_HEADS * RET_DV,
             MLA_HEADS * (MLA_NOPE + MLA_ROPE), MLA_KV_RANK, MLA_ROPE)
EV_IN = sum(EV_WIDTHS)
EV_OUT = RET_HEADS * RET_DV + MLA_HEADS * MLA_DV
OD_WIDTHS = (DIFF_HEADS * 2 * DIFF_DH, DIFF_HEADS * 2 * DIFF_DH, DIFF_HEADS * DIFF_DV)
OD_IN = sum(OD_WIDTHS)
OD_OUT = DIFF_HEADS * DIFF_DV

kernel_name = 'hybrid_retention_mla_diffattn_hmoe_dit_step'


def _layer_norm(x, g, b):
    xf = x.astype(jnp.float32)
    mu = jnp.mean(xf, -1, keepdims=True)
    var = jnp.mean(jnp.square(xf - mu), -1, keepdims=True)
    y = (xf - mu) * lax.rsqrt(var + LN_EPS)
    return (y * g.astype(jnp.float32) + b.astype(jnp.float32)).astype(x.dtype)


def _head_norm(x):
    xf = x.astype(jnp.float32)
    mu = jnp.mean(xf, -1, keepdims=True)
    var = jnp.mean(jnp.square(xf - mu), -1, keepdims=True)
    return ((xf - mu) * lax.rsqrt(var + LN_EPS)).astype(x.dtype)


def _rms_norm(x, g):
    xf = x.astype(jnp.float32)
    y = xf * lax.rsqrt(jnp.mean(jnp.square(xf), -1, keepdims=True) + RMS_EPS)
    return (y * g.astype(jnp.float32)).astype(x.dtype)


def _split_cols(z, widths):
    cuts = [int(v) for v in np.cumsum(widths)[:-1]]
    return jnp.split(z, cuts, axis=-1)


def _heads(x, n):
    b, t, w = x.shape
    return x.reshape(b, t, n, w // n).transpose(0, 2, 1, 3)


def _merge_heads(x):
    b, n, t, d = x.shape
    return x.transpose(0, 2, 1, 3).reshape(b, t, n * d)


def _axial_rope(n_tokens, rot_dim):
    rows = n_tokens // GRID_W
    row = jnp.repeat(jnp.arange(rows, dtype=jnp.float32), GRID_W)
    col = jnp.tile(jnp.arange(GRID_W, dtype=jnp.float32), rows)
    n_freq = rot_dim // 4
    inv_freq = ROPE_BASE ** (-jnp.arange(n_freq, dtype=jnp.float32) / n_freq)
    ang = jnp.concatenate([row[:, None] * inv_freq, col[:, None] * inv_freq], axis=-1)
    return jnp.cos(ang), jnp.sin(ang)


def _apply_rope(x, cos, sin):
    x1, x2 = jnp.split(x, 2, axis=-1)
    cos = cos.astype(x.dtype)
    sin = sin.astype(x.dtype)
    return jnp.concatenate([x1 * cos - x2 * sin, x1 * sin + x2 * cos], axis=-1)


def _query_blocks(x):
    b, h, t, d = x.shape
    return jnp.moveaxis(x.reshape(b, h, t // Q_BLOCK, Q_BLOCK, d), 2, 0)


def _merge_blocks(y):
    nb, b, h, qb, d = y.shape
    return jnp.moveaxis(y, 0, 2).reshape(b, h, nb * qb, d)


def _softmax_attention(q, k, v, scale):
    def one(qb):
        s = jnp.einsum('bhqd,bhkd->bhqk', qb, k).astype(jnp.float32) * scale
        p = jax.nn.softmax(s, axis=-1)
        return jnp.einsum('bhqk,bhkd->bhqd', p.astype(v.dtype), v)
    return _merge_blocks(lax.map(one, _query_blocks(q)))


def _diff_attention(q1, q2, k1, k2, v, lam, scale):
    def one(qs):
        qa, qb = qs
        p1 = jax.nn.softmax(jnp.einsum('bhqd,bhkd->bhqk', qa, k1).astype(jnp.float32) * scale, axis=-1)
        p2 = jax.nn.softmax(jnp.einsum('bhqd,bhkd->bhqk', qb, k2).astype(jnp.float32) * scale, axis=-1)
        return jnp.einsum('bhqk,bhkd->bhqd', (p1 - lam * p2).astype(v.dtype), v)
    return _merge_blocks(lax.map(one, (_query_blocks(q1), _query_blocks(q2))))


def _retention_scan(q, k, v, log_gamma, s0):
    dtype = q.dtype
    b, h, t, dk = q.shape
    dv = v.shape[-1]
    c = RET_CHUNK
    n = t // c
    qc = q.astype(jnp.float32).reshape(b, h, n, c, dk)
    kc = k.astype(jnp.float32).reshape(b, h, n, c, dk)
    vc = v.astype(jnp.float32).reshape(b, h, n, c, dv)
    lg = log_gamma.astype(jnp.float32)
    idx = jnp.arange(c, dtype=jnp.float32)
    rel = idx[:, None] - idx[None, :]
    decay = jnp.where(rel >= 0, jnp.exp(lg[:, None, None] * jnp.maximum(rel, 0.0)), 0.0)
    scores = jnp.einsum('bhnid,bhnjd->bhnij', qc, kc) * decay[:, None]
    o_inner = jnp.einsum('bhnij,bhnje->bhnie', scores, vc)
    k_w = jnp.exp(lg[:, None] * (c - 1.0 - idx))
    kv = jnp.einsum('bhnjd,bhnje->bhnde', kc * k_w[:, None, :, None], vc)
    chunk_decay = jnp.exp(lg * c)[None, :, None, None]

    def step(s, kv_n):
        return chunk_decay * s + kv_n, s

    s_final, s_prev = lax.scan(step, s0.astype(jnp.float32), jnp.moveaxis(kv, 2, 0))
    q_w = jnp.exp(lg[:, None] * (idx + 1.0))
    o_cross = jnp.einsum('bhnid,nbhde->bhnie', qc * q_w[:, None, :, None], s_prev)
    o = (o_inner + o_cross).reshape(b, h, t, dv)
    return o.astype(dtype), s_final.astype(dtype)


def _bidir_retention(q, k, v, lg_f, lg_b, s_f0, s_b0):
    o_f, s_f = _retention_scan(q, k, v, lg_f, s_f0)
    o_b, s_b = _retention_scan(q[:, :, ::-1], k[:, :, ::-1], v[:, :, ::-1], lg_b, s_b0)
    return o_f + o_b[:, :, ::-1], s_f, s_b


def _even_mixer(h, wts, ctx=None, rope=None):
    w_in, w_out, dec_f, dec_b, kv_norm_g, w_uk, w_uv = wts
    b, t, _ = h.shape
    rq, rk, rv, rg, mq, ckv, kr = _split_cols(h @ w_in, EV_WIDTHS)
    q = _heads(rq, RET_HEADS)
    k = _heads(rk, RET_HEADS) * (RET_DK ** -0.5)
    v = _heads(rv, RET_HEADS)
    lg_f = jax.nn.log_sigmoid(dec_f.astype(jnp.float32))
    lg_b = jax.nn.log_sigmoid(dec_b.astype(jnp.float32))
    if ctx is None:
        s_f0 = jnp.zeros((b, RET_HEADS, RET_DK, RET_DV), jnp.float32)
        s_b0 = s_f0
    else:
        s_f0, s_b0, ckv_ctx, kr_ctx = ctx
    o, s_f, s_b = _bidir_retention(q, k, v, lg_f, lg_b, s_f0, s_b0)
    ret_out = jax.nn.silu(rg) * _merge_heads(_head_norm(o))
    ckv = _rms_norm(ckv, kv_norm_g)
    mq = _heads(mq, MLA_HEADS)
    q_nope, q_rope = mq[..., :MLA_NOPE], mq[..., MLA_NOPE:]
    ckv_all, kr_all = ckv, kr
    if ctx is not None:
        cos, sin = rope
        q_rope = _apply_rope(q_rope, cos, sin)
        ckv_all = jnp.concatenate([ckv_ctx.astype(ckv.dtype), ckv], axis=1)
        kr_all = jnp.concatenate([kr_ctx.astype(kr.dtype), _apply_rope(kr, cos, sin)], axis=1)
    n_k = ckv_all.shape[1]
    k_m = jnp.concatenate([_heads(ckv_all @ w_uk, MLA_HEADS),
                           jnp.broadcast_to(kr_all[:, None], (b, MLA_HEADS, n_k, MLA_ROPE))], axis=-1)
    v_m = _heads(ckv_all @ w_uv, MLA_HEADS)
    q_m = jnp.concatenate([q_nope, q_rope], axis=-1)
    mla_out = _merge_heads(_softmax_attention(q_m, k_m, v_m, (MLA_NOPE + MLA_ROPE) ** -0.5))
    out = jnp.concatenate([ret_out, mla_out], axis=-1) @ w_out
    extras = (s_f, s_b, ckv, kr) if ctx is None else None
    return out, extras


def _odd_mixer(h, wts, ctx=None, rope=None):
    w_in, w_out, lam_p, norm_g, lam_init = wts
    qa, ka, va = _split_cols(h @ w_in, OD_WIDTHS)
    q = _heads(qa, DIFF_HEADS)
    k = _heads(ka, DIFF_HEADS)
    v = _heads(va, DIFF_HEADS)
    extras = (k, v) if ctx is None else None
    q1, q2 = q[..., :DIFF_DH], q[..., DIFF_DH:]
    k1, k2 = k[..., :DIFF_DH], k[..., DIFF_DH:]
    if ctx is not None:
        cos, sin = rope
        k_ctx, v_ctx = ctx
        k_ctx = k_ctx.astype(k.dtype)
        q1, q2 = _apply_rope(q1, cos, sin), _apply_rope(q2, cos, sin)
        k1 = jnp.concatenate([k_ctx[..., :DIFF_DH], _apply_rope(k1, cos, sin)], axis=2)
        k2 = jnp.concatenate([k_ctx[..., DIFF_DH:], _apply_rope(k2, cos, sin)], axis=2)
        v = jnp.concatenate([v_ctx.astype(v.dtype), v], axis=2)
    lp = lam_p.astype(jnp.float32)
    lam = jnp.exp(jnp.sum(lp[0] * lp[1])) - jnp.exp(jnp.sum(lp[2] * lp[3])) + lam_init
    o = _diff_attention(q1, q2, k1, k2, v, lam, DIFF_DH ** -0.5)
    o = _rms_norm(o, norm_g) * (1.0 - lam_init)
    return _merge_heads(o) @ w_out, extras


def _hier_moe(x, w_group, b_group, w_expert, b_expert, w1, w3, w2):
    b, t, d = x.shape
    xt = x.reshape(b * t, d)
    g_prob = jax.nn.softmax((xt @ w_group + b_group).astype(jnp.float32), axis=-1)
    g_p, g_idx = lax.top_k(g_prob, 1)
    e_logits = (xt @ w_expert + b_expert).astype(jnp.float32).reshape(b * t, MOE_GROUPS, MOE_PER_GROUP)
    e_sel = jnp.einsum('ngp,ng->np', e_logits, jax.nn.one_hot(g_idx[:, 0], MOE_GROUPS, dtype=jnp.float32))
    e_p, e_idx = lax.top_k(jax.nn.softmax(e_sel, axis=-1), MOE_TOPK)
    gate_w = g_p * e_p / jnp.sum(e_p, axis=-1, keepdims=True)
    expert_ids = g_idx * MOE_PER_GROUP + e_idx
    combine = jnp.sum(jax.nn.one_hot(expert_ids, MOE_EXPERTS, dtype=jnp.float32) * gate_w[..., None], axis=1)
    hidden = jax.nn.silu(jnp.einsum('nd,edf->nef', xt, w1)) * jnp.einsum('nd,edf->nef', xt, w3)
    y = jnp.einsum('nef,efd->nd', hidden * combine[:, :, None].astype(hidden.dtype), w2)
    return y.reshape(b, t, d)


def _modulation(cond, w, b):
    return jax.nn.silu(cond) @ w + b


def _block(x, mod, mixer, ln, moe):
    sh1, sc1, g1, sh2, sc2, g2 = jnp.split(mod[:, None, :], 6, axis=-1)
    y, extras = mixer(x * (1.0 + sc1) + sh1)
    x = _layer_norm(DEEPNORM_ALPHA * x + g1 * y, ln[0], ln[1])
    x = _layer_norm(DEEPNORM_ALPHA * x + g2 * _hier_moe(x * (1.0 + sc2) + sh2, *moe), ln[2], ln[3])
    return x, extras


def setup_inputs(seed: int = 0) -> dict:
    key = jax.random.key(seed)
    ks = jax.random.split(key, 40)

    def nrm(k, shape, s=1.0):
        return s * jax.random.normal(k, shape, jnp.float32)

    gamma_logit = jnp.log(2.0 ** (5.0 + jnp.arange(RET_HEADS, dtype=jnp.float32)) - 1.0)
    return {
        'x_prompt': nrm(ks[0], (BATCH, SEQ, D_MODEL)),
        'x_sample': nrm(ks[1], (DEC_BATCH, DEC_SEQ, D_MODEL)),
        'state_ret_fwd': nrm(ks[2], (DEC_BATCH, N_EVEN, RET_HEADS, RET_DK, RET_DV)),
        'state_ret_bwd': nrm(ks[3], (DEC_BATCH, N_EVEN, RET_HEADS, RET_DK, RET_DV)),
        'cache_mla_ckv': nrm(ks[4], (DEC_BATCH, N_EVEN, PAST_LEN, MLA_KV_RANK)),
        'cache_mla_krope': nrm(ks[5], (DEC_BATCH, N_EVEN, PAST_LEN, MLA_ROPE)),
        'cache_diff_k': nrm(ks[6], (DEC_BATCH, N_ODD, DIFF_HEADS, PAST_LEN, 2 * DIFF_DH)),
        'cache_diff_v': nrm(ks[7], (DEC_BATCH, N_ODD, DIFF_HEADS, PAST_LEN, DIFF_DV)),
        'c': nrm(ks[8], (DEC_BATCH, D_MODEL)),
        'c_ctx': nrm(ks[9], (D_MODEL,)),
        'ada_w': nrm(ks[10], (DEPTH, D_MODEL, 6 * D_MODEL), ADA_INIT * D_MODEL ** -0.5),
        'ada_b': nrm(ks[11], (DEPTH, 6 * D_MODEL), 0.02),
        'ln1_g': 1.0 + nrm(ks[12], (DEPTH, D_MODEL), 0.02),
        'ln1_b': nrm(ks[13], (DEPTH, D_MODEL), 0.02),
        'ln2_g': 1.0 + nrm(ks[14], (DEPTH, D_MODEL), 0.02),
        'ln2_b': nrm(ks[15], (DEPTH, D_MODEL), 0.02),
        'ev_w_in': nrm(ks[16], (N_EVEN, D_MODEL, EV_IN), D_MODEL ** -0.5),
        'ev_w_out': nrm(ks[17], (N_EVEN, EV_OUT, D_MODEL), DEEPNORM_BETA * EV_OUT ** -0.5),
        'ret_decay_fwd': gamma_logit + nrm(ks[18], (N_EVEN, RET_HEADS), 0.1),
        'ret_decay_bwd': gamma_logit + nrm(ks[19], (N_EVEN, RET_HEADS), 0.1),
        'mla_kv_norm_g': 1.0 + nrm(ks[20], (N_EVEN, MLA_KV_RANK), 0.02),
        'mla_w_uk': nrm(ks[21], (N_EVEN, MLA_KV_RANK, MLA_HEADS * MLA_NOPE), MLA_KV_RANK ** -0.5),
        'mla_w_uv': nrm(ks[22], (N_EVEN, MLA_KV_RANK, MLA_HEADS * MLA_DV), MLA_KV_RANK ** -0.5),
        'od_w_in': nrm(ks[23], (N_ODD, D_MODEL, OD_IN), D_MODEL ** -0.5),
        'od_w_out': nrm(ks[24], (N_ODD, OD_OUT, D_MODEL), DEEPNORM_BETA * OD_OUT ** -0.5),
        'diff_lambda': nrm(ks[25], (N_ODD, 4, DIFF_DH), 0.1),
        'diff_norm_g': 1.0 + nrm(ks[26], (N_ODD, DIFF_DV), 0.02),
        'moe_w_group': nrm(ks[27], (DEPTH, D_MODEL, MOE_GROUPS), D_MODEL ** -0.5),
        'moe_b_group': nrm(ks[28], (DEPTH, MOE_GROUPS), 0.01),
        'moe_w_expert': nrm(ks[29], (DEPTH, D_MODEL, MOE_EXPERTS), D_MODEL ** -0.5),
        'moe_b_expert': nrm(ks[30], (DEPTH, MOE_EXPERTS), 0.01),
        'moe_w1': nrm(ks[31], (DEPTH, MOE_EXPERTS, D_MODEL, MOE_FF), D_MODEL ** -0.5),
        'moe_w3': nrm(ks[32], (DEPTH, MOE_EXPERTS, D_MODEL, MOE_FF), D_MODEL ** -0.5),
        'moe_w2': nrm(ks[33], (DEPTH, MOE_EXPERTS, MOE_FF, D_MODEL), DEEPNORM_BETA * MOE_FF ** -0.5),
    }


def reference(x_prompt, x_sample, state_ret_fwd, state_ret_bwd, cache_mla_ckv, cache_mla_krope,
              cache_diff_k, cache_diff_v, c, c_ctx, ada_w, ada_b, ln1_g, ln1_b, ln2_g, ln2_b,
              ev_w_in, ev_w_out, ret_decay_fwd, ret_decay_bwd, mla_kv_norm_g, mla_w_uk, mla_w_uv,
              od_w_in, od_w_out, diff_lambda, diff_norm_g, moe_w_group, moe_b_group, moe_w_expert,
              moe_b_expert, moe_w1, moe_w3, moe_w2):
    t_lat = x_sample.shape[1]
    rope_mla = _axial_rope(t_lat, MLA_ROPE)
    rope_diff = _axial_rope(t_lat, DIFF_DH)
    xp, xs = x_prompt, x_sample
    rf, rb, ck, kro, dk, dv = [], [], [], [], [], []
    for i in range(DEPTH):
        j = i // 2
        mod_p = _modulation(c_ctx[None, :], ada_w[i], ada_b[i])
        mod_s = _modulation(c, ada_w[i], ada_b[i])
        ln = (ln1_g[i], ln1_b[i], ln2_g[i], ln2_b[i])
        moe = (moe_w_group[i], moe_b_group[i], moe_w_expert[i], moe_b_expert[i], moe_w1[i], moe_w3[i], moe_w2[i])
        if i % 2 == 0:
            wts = (ev_w_in[j], ev_w_out[j], ret_decay_fwd[j], ret_decay_bwd[j], mla_kv_norm_g[j], mla_w_uk[j], mla_w_uv[j])
            ctx = (state_ret_fwd[:, j], state_ret_bwd[:, j], cache_mla_ckv[:, j], cache_mla_krope[:, j])
            xp, (s_f, s_b, ckv_new, kr_new) = _block(xp, mod_p, lambda h: _even_mixer(h, wts), ln, moe)
            xs, _ = _block(xs, mod_s, lambda h: _even_mixer(h, wts, ctx, rope_mla), ln, moe)
            rf.append(s_f)
            rb.append(s_b)
            ck.append(ckv_new)
            kro.append(kr_new)
        else:
            lam_init = 0.8 - 0.6 * math.exp(-0.3 * i)
            wts = (od_w_in[j], od_w_out[j], diff_lambda[j], diff_norm_g[j], lam_init)
            ctx = (cache_diff_k[:, j], cache_diff_v[:, j])
            xp, (k_new, v_new) = _block(xp, mod_p, lambda h: _odd_mixer(h, wts), ln, moe)
            xs, _ = _block(xs, mod_s, lambda h: _odd_mixer(h, wts, ctx, rope_diff), ln, moe)
            dk.append(k_new)
            dv.append(v_new)
    new_state_ret_fwd = jnp.stack(rf, axis=1)
    new_state_ret_bwd = jnp.stack(rb, axis=1)
    new_cache_mla_ckv = jnp.stack(ck, axis=1)
    new_cache_mla_krope = jnp.stack(kro, axis=1)
    new_cache_diff_k = jnp.stack(dk, axis=1)
    new_cache_diff_v = jnp.stack(dv, axis=1)
    return (xp, xs, new_state_ret_fwd, new_state_ret_bwd, new_cache_mla_ckv, new_cache_mla_krope, new_cache_diff_k, new_cache_diff_v)
```

```python
import functools
import math

import jax
import jax.numpy as jnp
import numpy as np
from jax import lax
from jax.experimental import pallas as pl
from jax.experimental.pallas import tpu as pltpu

D_MODEL = 1024
BATCH = 32
SEQ = 256
DEPTH = 4
DEC_BATCH = 2
DEC_SEQ = 2048
PAST_LEN = 256
GRID_W = 64
LN_EPS = 1e-5
RMS_EPS = 1e-6
DEEPNORM_ALPHA = (2.0 * DEPTH) ** 0.25
ROPE_BASE = 10000.0
RET_HEADS = 8
RET_DK = 64
RET_DV = 128
MLA_HEADS = 8
MLA_NOPE = 64
MLA_ROPE = 32
MLA_DV = 64
MLA_KV_RANK = 256
DIFF_HEADS = 8
DIFF_DH = 64
DIFF_DV = 128
MOE_GROUPS = 4
MOE_PER_GROUP = 4
MOE_EXPERTS = 16
MOE_FF = 256

NTOK_P = BATCH * SEQ
NTOK_S = DEC_BATCH * DEC_SEQ
NTOK = NTOK_P + NTOK_S
N_COND = 8
LANES = 128
RET_COLS = 3072
MLA_COLS = 1152
ATT_TQ = 256
RET_CHUNK = 256
VMEM_LIMIT = 56 * 1024 * 1024

F32 = jnp.float32
BF16 = jnp.bfloat16
NT_DIMS = (((1,), (1,)), ((), ()))
TN_DIMS = (((0,), (0,)), ((), ()))


def _params(sem):
    return pltpu.CompilerParams(dimension_semantics=sem, vmem_limit_bytes=VMEM_LIMIT)


def _group_of_tile(i, tm):
    npt = NTOK_P // tm
    nst = DEC_SEQ // tm
    return jnp.where(i < npt, 0, 1 + (i - npt) // nst)


def _silu(x):
    return x * (1.0 / (1.0 + jnp.exp(-x)))


def _lane_iota(shape):
    return lax.broadcasted_iota(jnp.int32, shape, 1)


def _div_pow2(x, d):
    assert d & (d - 1) == 0
    return jnp.right_shift(x, d.bit_length() - 1)


def _mod_pow2(x, d):
    assert d & (d - 1) == 0
    return jnp.bitwise_and(x, d - 1)


def _ada_kernel(c_ref, w_ref, b_ref, o_ref):
    h = _silu(c_ref[...]).astype(BF16)
    o_ref[...] = jnp.dot(h, w_ref[...].astype(BF16), preferred_element_type=F32) + b_ref[...]


def _ada_all(cond, ada_w, ada_b):
    tn = 768
    return pl.pallas_call(
        _ada_kernel,
        grid=(DEPTH, 6 * D_MODEL // tn),
        in_specs=[
            pl.BlockSpec((N_COND, D_MODEL), lambda l, n: (0, 0)),
            pl.BlockSpec((None, D_MODEL, tn), lambda l, n: (l, 0, n)),
            pl.BlockSpec((None, 1, tn), lambda l, n: (l, 0, n)),
        ],
        out_specs=pl.BlockSpec((None, N_COND, tn), lambda l, n: (l, 0, n)),
        out_shape=jax.ShapeDtypeStruct((DEPTH, N_COND, 6 * D_MODEL), F32),
        compiler_params=_params(("parallel", "parallel")),
        name="ada_modulation",
    )(cond, ada_w, ada_b.reshape(DEPTH, 1, 6 * D_MODEL))


def _mm_mod_kernel(x_ref, mod_ref, w_ref, o_ref, wscr):
    @pl.when(pl.program_id(1) == 0)
    def _():
        wscr[...] = w_ref[...].astype(BF16)

    sh = mod_ref[0:1, :]
    sc = mod_ref[1:2, :]
    xm = (x_ref[...] * (1.0 + sc) + sh).astype(BF16)
    o_ref[...] = jnp.dot(xm, wscr[...], preferred_element_type=F32)


def _mm_mod(x, mod, w, w_index, n_cols, tn, name):
    tm = 512
    lead = len(w_index)
    w_block = (None,) * lead + (D_MODEL, tn)
    return pl.pallas_call(
        _mm_mod_kernel,
        grid=(n_cols // tn, NTOK // tm),
        in_specs=[
            pl.BlockSpec((tm, D_MODEL), lambda n, m: (m, 0)),
            pl.BlockSpec((None, 6, D_MODEL), lambda n, m: (_group_of_tile(m, tm), 0, 0)),
            pl.BlockSpec(w_block, lambda n, m: tuple(w_index) + (0, n)),
        ],
        out_specs=pl.BlockSpec((tm, tn), lambda n, m: (m, n)),
        out_shape=jax.ShapeDtypeStruct((NTOK, n_cols), F32),
        scratch_shapes=[pltpu.VMEM((D_MODEL, tn), BF16)],
        compiler_params=_params(("arbitrary", "arbitrary")),
        name=name,
    )(x, mod, w)


def _mm_ln_kernel(n_a, k_sizes, *refs):
    a_refs = refs[:n_a]
    w_ref, x_ref, mod_ref, g_ref, b_ref, o_ref, wscr = refs[n_a:]

    @pl.when(pl.program_id(0) == 0)
    def _():
        wscr[...] = w_ref[...].astype(BF16)

    y = None
    k0 = 0
    for a_ref, ks in zip(a_refs, k_sizes):
        part = jnp.dot(a_ref[...], wscr[k0:k0 + ks, :], preferred_element_type=F32)
        y = part if y is None else y + part
        k0 += ks
    gate = mod_ref[2:3, :]
    r = DEEPNORM_ALPHA * x_ref[...] + gate * y
    mu = jnp.mean(r, axis=-1, keepdims=True)
    d = r - mu
    var = jnp.mean(d * d, axis=-1, keepdims=True)
    o_ref[...] = d * lax.rsqrt(var + LN_EPS) * g_ref[...] + b_ref[...]


def _mm_ln(a_list, w, j, x, mod, ln_g, ln_b, layer):
    tm = 512
    k_sizes = tuple(a.shape[1] for a in a_list)
    k_tot = sum(k_sizes)
    in_specs = [pl.BlockSpec((tm, ks), lambda m: (m, 0)) for ks in k_sizes]
    in_specs += [
        pl.BlockSpec((None, k_tot, D_MODEL), lambda m: (j, 0, 0)),
        pl.BlockSpec((tm, D_MODEL), lambda m: (m, 0)),
        pl.BlockSpec((None, 6, D_MODEL), lambda m: (_group_of_tile(m, tm), 0, 0)),
        pl.BlockSpec((None, 1, D_MODEL), lambda m: (layer, 0, 0)),
        pl.BlockSpec((None, 1, D_MODEL), lambda m: (layer, 0, 0)),
    ]
    return pl.pallas_call(
        functools.partial(_mm_ln_kernel, len(a_list), k_sizes),
        grid=(NTOK // tm,),
        in_specs=in_specs,
        out_specs=pl.BlockSpec((tm, D_MODEL), lambda m: (m, 0)),
        out_shape=jax.ShapeDtypeStruct((NTOK, D_MODEL), F32),
        scratch_shapes=[pltpu.VMEM((k_tot, D_MODEL), BF16)],
        compiler_params=_params(("arbitrary",)),
        name="out_proj_ln",
    )(*a_list, w, x, mod, ln_g.reshape(DEPTH, 1, D_MODEL), ln_b.reshape(DEPTH, 1, D_MODEL))


def _moe_kernel(x_ref, mod_ref, wr_ref, br_ref, w1_ref, w3_ref, w2_ref, g_ref, b_ref, o_ref,
                xm_scr, comb_scr, acc_scr):
    e = pl.program_id(1)
    tm = x_ref.shape[0]

    @pl.when(e == 0)
    def _():
        sh = mod_ref[3:4, :]
        sc = mod_ref[4:5, :]
        xm = (x_ref[...] * (1.0 + sc) + sh).astype(BF16)
        xm_scr[...] = xm
        z = jnp.dot(xm, wr_ref[...].astype(BF16), preferred_element_type=F32) + br_ref[...]
        lane_i = _lane_iota((tm, LANES))
        lane = lane_i.astype(F32)
        none = jnp.float32(LANES)
        neg = jnp.float32(-jnp.inf)
        gmask = lane_i < MOE_GROUPS
        zg = jnp.where(gmask, z, neg)
        pg = jnp.exp(zg - jnp.max(zg, axis=-1, keepdims=True))
        g_prob = pg / jnp.sum(pg, axis=-1, keepdims=True)
        g_p = jnp.max(g_prob, axis=-1, keepdims=True)
        g_idx = jnp.min(jnp.where(gmask & (g_prob == g_p), lane, none), axis=-1, keepdims=True)
        assert MOE_GROUPS % MOE_PER_GROUP == 0
        e_group = (_div_pow2(lane_i, MOE_PER_GROUP) - MOE_GROUPS // MOE_PER_GROUP).astype(F32)
        emask = (lane_i >= MOE_GROUPS) & (lane_i < MOE_GROUPS + MOE_EXPERTS) & (e_group == g_idx)
        ze = jnp.where(emask, z, neg)
        pe = jnp.exp(ze - jnp.max(ze, axis=-1, keepdims=True))
        e_prob = pe / jnp.sum(pe, axis=-1, keepdims=True)
        cand = jnp.where(emask, e_prob, -1.0)
        p1 = jnp.max(cand, axis=-1, keepdims=True)
        i1 = jnp.min(jnp.where(cand == p1, lane, none), axis=-1, keepdims=True)
        cand2 = jnp.where(lane == i1, -1.0, cand)
        p2 = jnp.max(cand2, axis=-1, keepdims=True)
        i2 = jnp.min(jnp.where(cand2 == p2, lane, none), axis=-1, keepdims=True)
        denom = p1 + p2
        comb_scr[...] = (jnp.where(lane == i1, g_p * p1 / denom, 0.0)
                         + jnp.where(lane == i2, g_p * p2 / denom, 0.0))
        acc_scr[...] = jnp.zeros_like(acc_scr)

    xm = xm_scr[...]
    h1 = jnp.dot(xm, w1_ref[...].astype(BF16), preferred_element_type=F32)
    h3 = jnp.dot(xm, w3_ref[...].astype(BF16), preferred_element_type=F32)
    lane = _lane_iota((tm, LANES))
    c = jnp.sum(jnp.where(lane == e + MOE_GROUPS, comb_scr[...], 0.0), axis=-1, keepdims=True)
    hid = (_silu(h1) * h3 * c).astype(BF16)
    acc_scr[...] += jnp.dot(hid, w2_ref[...].astype(BF16), preferred_element_type=F32)

    @pl.when(e == MOE_EXPERTS - 1)
    def _():
        gate = mod_ref[5:6, :]
        r = DEEPNORM_ALPHA * x_ref[...] + gate * acc_scr[...]
        mu = jnp.mean(r, axis=-1, keepdims=True)
        d = r - mu
        var = jnp.mean(d * d, axis=-1, keepdims=True)
        o_ref[...] = d * lax.rsqrt(var + LN_EPS) * g_ref[...] + b_ref[...]


def _moe(x, mod, w_router, b_router, w1, w3, w2, ln_g, ln_b, layer):
    tm = 1024
    return pl.pallas_call(
        _moe_kernel,
        grid=(NTOK // tm, MOE_EXPERTS),
        in_specs=[
            pl.BlockSpec((tm, D_MODEL), lambda m, e: (m, 0)),
            pl.BlockSpec((None, 6, D_MODEL), lambda m, e: (_group_of_tile(m, tm), 0, 0)),
            pl.BlockSpec((D_MODEL, LANES), lambda m, e: (0, 0)),
            pl.BlockSpec((1, LANES), lambda m, e: (0, 0)),
            pl.BlockSpec((None, None, D_MODEL, MOE_FF), lambda m, e: (layer, e, 0, 0)),
            pl.BlockSpec((None, None, D_MODEL, MOE_FF), lambda m, e: (layer, e, 0, 0)),
            pl.BlockSpec((None, None, MOE_FF, D_MODEL), lambda m, e: (layer, e, 0, 0)),
            pl.BlockSpec((None, 1, D_MODEL), lambda m, e: (layer, 0, 0)),
            pl.BlockSpec((None, 1, D_MODEL), lambda m, e: (layer, 0, 0)),
        ],
        out_specs=pl.BlockSpec((tm, D_MODEL), lambda m, e: (m, 0)),
        out_shape=jax.ShapeDtypeStruct((NTOK, D_MODEL), F32),
        scratch_shapes=[
            pltpu.VMEM((tm, D_MODEL), BF16),
            pltpu.VMEM((tm, LANES), F32),
            pltpu.VMEM((tm, D_MODEL), F32),
        ],
        compiler_params=_params(("arbitrary", "arbitrary")),
        name="hier_moe_ln",
    )(x, mod, w_router, b_router, w1, w3, w2,
      ln_g.reshape(DEPTH, 1, D_MODEL), ln_b.reshape(DEPTH, 1, D_MODEL))


def _swap_halves(x, lane, half):
    return jnp.where(_mod_pow2(lane, 2 * half) < half,
                     pltpu.roll(x, LANES - half, 1), pltpu.roll(x, half, 1))


def _rope128(x, cos, sin_signed, lane, half):
    return x * cos + _swap_halves(x, lane, half) * sin_signed


def _rope_tables(rot_dim):
    rows = DEC_SEQ // GRID_W
    row = jnp.repeat(jnp.arange(rows, dtype=F32), GRID_W)
    col = jnp.tile(jnp.arange(GRID_W, dtype=F32), rows)
    n_freq = rot_dim // 4
    inv_freq = ROPE_BASE ** (-jnp.arange(n_freq, dtype=F32) / n_freq)
    ang = jnp.concatenate([row[:, None] * inv_freq, col[:, None] * inv_freq], axis=-1)
    cos, sin = jnp.cos(ang), jnp.sin(ang)
    reps = LANES // rot_dim
    cos_full = jnp.tile(jnp.concatenate([cos, cos], axis=-1), (1, reps))
    sin_signed = jnp.tile(jnp.concatenate([-sin, sin], axis=-1), (1, reps))
    return cos_full, sin_signed


def _softmax_parts(s_list, scale):
    m = None
    for s in s_list:
        sm = jnp.max(s, axis=-1, keepdims=True)
        m = sm if m is None else jnp.maximum(m, sm)
    p_list = [jnp.exp((s - m) * scale) for s in s_list]
    l = None
    for p in p_list:
        ps = jnp.sum(p, axis=-1, keepdims=True)
        l = ps if l is None else l + ps
    return p_list, l


def _mla_kv_kernel(new_tokens, *refs):
    if new_tokens:
        ckv_ref, kr_ref, cos_ref, sin_ref, g_ref, wuk_ref, wuv_ref, ckvn_ref, kcat_ref, vm_ref = refs
        x = ckv_ref[...]
        c = x * lax.rsqrt(jnp.mean(x * x, axis=-1, keepdims=True) + RMS_EPS) * g_ref[...]
        ckvn_ref[...] = c
        lane = _lane_iota(kr_ref.shape)
        kr = _rope128(kr_ref[...], cos_ref[...], sin_ref[...], lane, MLA_ROPE // 2)
    else:
        ckv_ref, kr_ref, wuk_ref, wuv_ref, kcat_ref, vm_ref = refs
        c = ckv_ref[...]
        kr = kr_ref[...]
    cb = c.astype(BF16)
    kn = jnp.dot(cb, wuk_ref[...].astype(BF16), preferred_element_type=F32).astype(BF16)
    vm_ref[...] = jnp.dot(cb, wuv_ref[...].astype(BF16), preferred_element_type=F32).astype(BF16)
    krb = kr.astype(BF16)
    for p in range(MLA_HEADS // 2):
        kcat_ref[:, 256 * p:256 * p + LANES] = kn[:, LANES * p:LANES * (p + 1)]
        kcat_ref[:, 256 * p + LANES:256 * (p + 1)] = krb


def _mla_kv_new(zmla, cos_t, sin_t, kv_norm_g, w_uk, w_uv, j):
    tm = 512
    npt = NTOK_P // tm
    nst = DEC_SEQ // tm

    def tab(i):
        return (jnp.where(i < npt, 0, 1 + (i - npt) % nst), 0)

    return pl.pallas_call(
        functools.partial(_mla_kv_kernel, True),
        grid=(NTOK // tm,),
        in_specs=[
            pl.BlockSpec((tm, MLA_KV_RANK), lambda i: (i, 768 // MLA_KV_RANK)),
            pl.BlockSpec((tm, LANES), lambda i: (i, 1024 // LANES)),
            pl.BlockSpec((tm, LANES), tab),
            pl.BlockSpec((tm, LANES), tab),
            pl.BlockSpec((None, 1, MLA_KV_RANK), lambda i: (j, 0, 0)),
            pl.BlockSpec((None, MLA_KV_RANK, 512), lambda i: (j, 0, 0)),
            pl.BlockSpec((None, MLA_KV_RANK, 512), lambda i: (j, 0, 0)),
        ],
        out_specs=[
            pl.BlockSpec((tm, MLA_KV_RANK), lambda i: (i, 0)),
            pl.BlockSpec((tm, 1024), lambda i: (i, 0)),
            pl.BlockSpec((tm, 512), lambda i: (i, 0)),
        ],
        out_shape=[
            jax.ShapeDtypeStruct((NTOK, MLA_KV_RANK), F32),
            jax.ShapeDtypeStruct((NTOK, 1024), BF16),
            jax.ShapeDtypeStruct((NTOK, 512), BF16),
        ],
        compiler_params=_params(("parallel",)),
        name="mla_kv_new",
    )(zmla, zmla, cos_t, sin_t, kv_norm_g.reshape(-1, 1, MLA_KV_RANK), w_uk, w_uv)


def _mla_kv_ctx(cache_ckv, kr_tiled, w_uk, w_uv, j):
    return pl.pallas_call(
        functools.partial(_mla_kv_kernel, False),
        grid=(DEC_BATCH,),
        in_specs=[
            pl.BlockSpec((None, None, PAST_LEN, MLA_KV_RANK), lambda b: (b, j, 0, 0)),
            pl.BlockSpec((None, PAST_LEN, LANES), lambda b: (b, 0, 0)),
            pl.BlockSpec((None, MLA_KV_RANK, 512), lambda b: (j, 0, 0)),
            pl.BlockSpec((None, MLA_KV_RANK, 512), lambda b: (j, 0, 0)),
        ],
        out_specs=[
            pl.BlockSpec((PAST_LEN, 1024), lambda b: (b, 0)),
            pl.BlockSpec((PAST_LEN, 512), lambda b: (b, 0)),
        ],
        out_shape=[
            jax.ShapeDtypeStruct((DEC_BATCH * PAST_LEN, 1024), BF16),
            jax.ShapeDtypeStruct((DEC_BATCH * PAST_LEN, 512), BF16),
        ],
        compiler_params=_params(("parallel",)),
        name="mla_kv_ctx",
    )(cache_ckv, kr_tiled, w_uk, w_uv)


def _mla_attn_kernel(latent, *refs):
    if latent:
        qn_ref, qr_ref, cos_ref, sin_ref, kc_ref, vc_ref, kn_ref, vn_ref, o_ref = refs
        k_refs, v_refs = (kc_ref, kn_ref), (vc_ref, vn_ref)
    else:
        qn_ref, qr_ref, kn_ref, vn_ref, o_ref = refs
        k_refs, v_refs = (kn_ref,), (vn_ref,)
    tq = qn_ref.shape[0]
    lane = _lane_iota((tq, LANES))
    scale = (MLA_NOPE + MLA_ROPE) ** -0.5
    qr_cols = []
    for cidx in range(2):
        x = qr_ref[:, LANES * cidx:LANES * (cidx + 1)]
        if latent:
            x = _rope128(x, cos_ref[...], sin_ref[...], lane, MLA_ROPE // 2)
        qr_cols.append(x)
    o_prev = None
    for h in range(MLA_HEADS):
        p, half = divmod(h, 2)
        cidx, slot = divmod(h, 4)
        qa = jnp.where(_div_pow2(lane, MLA_NOPE) == half, qn_ref[:, LANES * p:LANES * (p + 1)], 0.0)
        qb = jnp.where(_div_pow2(lane, MLA_ROPE) == slot, qr_cols[cidx], 0.0)
        qcat = jnp.concatenate([qa, qb], axis=1).astype(BF16)
        s_list = [lax.dot_general(qcat, k_ref[:, 256 * p:256 * (p + 1)], NT_DIMS,
                                  preferred_element_type=F32) for k_ref in k_refs]
        p_list, l = _softmax_parts(s_list, scale)
        o = None
        for pr, v_ref in zip(p_list, v_refs):
            part = jnp.dot(pr.astype(BF16), v_ref[:, LANES * p:LANES * (p + 1)],
                           preferred_element_type=F32)
            o = part if o is None else o + part
        o = o / l
        if half == 0:
            o_prev = o
        else:
            o_ref[:, LANES * p:LANES * (p + 1)] = jnp.where(lane < MLA_DV, o_prev, o).astype(BF16)


def _mla_attn_prompt(zmla, kcat, vm):
    return pl.pallas_call(
        functools.partial(_mla_attn_kernel, False),
        grid=(BATCH,),
        in_specs=[
            pl.BlockSpec((SEQ, 512), lambda b: (b, 0)),
            pl.BlockSpec((SEQ, 256), lambda b: (b, 2)),
            pl.BlockSpec((SEQ, 1024), lambda b: (b, 0)),
            pl.BlockSpec((SEQ, 512), lambda b: (b, 0)),
        ],
        out_specs=pl.BlockSpec((SEQ, 512), lambda b: (b, 0)),
        out_shape=jax.ShapeDtypeStruct((NTOK_P, 512), BF16),
        compiler_params=_params(("parallel",)),
        name="mla_attn_prompt",
    )(zmla, zmla, kcat, vm)


def _mla_attn_latent(zmla, cos_t, sin_t, kcat_ctx, vm_ctx, kcat, vm):
    nq = DEC_SEQ // ATT_TQ
    row0 = NTOK_P // ATT_TQ
    seq0 = NTOK_P // DEC_SEQ
    return pl.pallas_call(
        functools.partial(_mla_attn_kernel, True),
        grid=(DEC_BATCH, nq),
        in_specs=[
            pl.BlockSpec((ATT_TQ, 512), lambda b, q: (row0 + b * nq + q, 0)),
            pl.BlockSpec((ATT_TQ, 256), lambda b, q: (row0 + b * nq + q, 2)),
            pl.BlockSpec((ATT_TQ, LANES), lambda b, q: (q, 0)),
            pl.BlockSpec((ATT_TQ, LANES), lambda b, q: (q, 0)),
            pl.BlockSpec((PAST_LEN, 1024), lambda b, q: (b, 0)),
            pl.BlockSpec((PAST_LEN, 512), lambda b, q: (b, 0)),
            pl.BlockSpec((DEC_SEQ, 1024), lambda b, q: (seq0 + b, 0)),
            pl.BlockSpec((DEC_SEQ, 512), lambda b, q: (seq0 + b, 0)),
        ],
        out_specs=pl.BlockSpec((ATT_TQ, 512), lambda b, q: (b * nq + q, 0)),
        out_shape=jax.ShapeDtypeStruct((NTOK_S, 512), BF16),
        compiler_params=_params(("parallel", "arbitrary")),
        name="mla_attn_latent",
    )(zmla, zmla, cos_t, sin_t, kcat_ctx, vm_ctx, kcat, vm)


def _log_sigmoid(x):
    return jnp.minimum(x, 0.0) - jnp.log1p(jnp.exp(-jnp.abs(x)))


def _retention_kernel(seq_len, has_init, emit_state, *refs):
    refs = list(refs)
    decf_ref, decb_ref, q_ref, k_ref, v_ref, g_ref = refs[:6]
    refs = refs[6:]
    if has_init:
        sf0_ref, sb0_ref = refs[:2]
        refs = refs[2:]
    o_ref = refs[0]
    refs = refs[1:]
    if emit_state:
        sf_ref, sb_ref = refs[:2]
        refs = refs[2:]
    (oacc,) = refs

    c = RET_CHUNK
    n_chunks = seq_len // c
    pair = pl.program_id(1)
    lane = _lane_iota((c, LANES))
    ri = lax.broadcasted_iota(jnp.int32, (c, c), 0)
    ci = lax.broadcasted_iota(jnp.int32, (c, c), 1)
    rel = (ri - ci).astype(F32)
    row = lax.broadcasted_iota(jnp.int32, (c, LANES), 0).astype(F32)
    zeros_half = jnp.zeros((RET_DK, RET_DV), F32)

    for half in range(2):
        head = 2 * pair + half
        lgf = _log_sigmoid(decf_ref[pl.ds(head, 1), :])
        lgb = _log_sigmoid(decb_ref[pl.ds(head, 1), :])
        lgf1 = lgf[:, 0:1]
        lgb1 = lgb[:, 0:1]
        decay = (jnp.where(rel >= 0, jnp.exp(lgf1 * jnp.maximum(rel, 0.0)), 0.0)
                 + jnp.where(rel <= 0, jnp.exp(lgb1 * jnp.maximum(-rel, 0.0)), 0.0))
        qw_f = jnp.exp(lgf * (row + 1.0))
        kw_f = jnp.exp(lgf * (c - 1.0 - row))
        qw_b = jnp.exp(lgb * (c - row))
        kw_b = jnp.exp(lgb * row)
        cd_f = jnp.exp(lgf * float(c))
        cd_b = jnp.exp(lgb * float(c))
        hmask = _div_pow2(lane, RET_DK) == half
        vsl = slice(RET_DV * half, RET_DV * (half + 1))

        def load(n):
            rows = pl.ds(n * c if isinstance(n, int) else pl.multiple_of(n * c, c), c)
            qm = jnp.where(hmask, q_ref[rows, :], 0.0)
            kk = k_ref[rows, :] * (RET_DK ** -0.5)
            vb = v_ref[rows, vsl].astype(BF16)
            return rows, qm, kk, vb

        def init_state(s0_ref):
            if not has_init:
                return jnp.zeros((LANES, RET_DV), F32)
            s0 = s0_ref[half]
            return jnp.concatenate([s0, zeros_half] if half == 0 else [zeros_half, s0], axis=0)

        def fwd_step(n, s_f):
            rows, qm, kk, vb = load(n)
            s = lax.dot_general(qm.astype(BF16), kk.astype(BF16), NT_DIMS, preferred_element_type=F32)
            o = jnp.dot((s * decay).astype(BF16), vb, preferred_element_type=F32)
            if has_init or n_chunks > 1:
                o = o + jnp.dot((qm * qw_f).astype(BF16), s_f.astype(BF16), preferred_element_type=F32)
            oacc[rows, :] = o
            kv = lax.dot_general((kk * kw_f).astype(BF16), vb, TN_DIMS, preferred_element_type=F32)
            return cd_f * s_f + kv

        def bwd_step(t, s_b):
            n = n_chunks - 1 - t
            rows, qm, kk, vb = load(n)
            o = oacc[rows, :]
            if has_init or n_chunks > 1:
                o = o + jnp.dot((qm * qw_b).astype(BF16), s_b.astype(BF16), preferred_element_type=F32)
            mu = jnp.mean(o, axis=-1, keepdims=True)
            d = o - mu
            var = jnp.mean(d * d, axis=-1, keepdims=True)
            o_ref[rows, vsl] = (_silu(g_ref[rows, vsl]) * (d * lax.rsqrt(var + LN_EPS))).astype(BF16)
            kv = lax.dot_general((kk * kw_b).astype(BF16), vb, TN_DIMS, preferred_element_type=F32)
            return cd_b * s_b + kv

        s_f = init_state(sf0_ref if has_init else None)
        s_b = init_state(sb0_ref if has_init else None)
        if n_chunks == 1:
            s_f = fwd_step(0, s_f)
            s_b = bwd_step(0, s_b)
        else:
            s_f = lax.fori_loop(0, n_chunks, fwd_step, s_f)
            s_b = lax.fori_loop(0, n_chunks, bwd_step, s_b)
        if emit_state:
            sf_ref[half] = s_f[RET_DK * half:RET_DK * (half + 1), :]
            sb_ref[half] = s_b[RET_DK * half:RET_DK * (half + 1), :]


def _retention(zret, decf, decb, j, latent, state_f=None, state_b=None):
    seq_len = DEC_SEQ if latent else SEQ
    n_b = DEC_BATCH if latent else BATCH
    row0 = NTOK_P // DEC_SEQ if latent else 0
    n_pairs = RET_HEADS // 2
    in_specs = [
        pl.BlockSpec((None, RET_HEADS, LANES), lambda b, p: (j, 0, 0)),
        pl.BlockSpec((None, RET_HEADS, LANES), lambda b, p: (j, 0, 0)),
        pl.BlockSpec((seq_len, LANES), lambda b, p: (row0 + b, p)),
        pl.BlockSpec((seq_len, LANES), lambda b, p: (row0 + b, 4 + p)),
        pl.BlockSpec((seq_len, 256), lambda b, p: (row0 + b, 4 + p)),
        pl.BlockSpec((seq_len, 256), lambda b, p: (row0 + b, 8 + p)),
    ]
    args = [decf, decb, zret, zret, zret, zret]
    out_specs = [pl.BlockSpec((seq_len, 256), lambda b, p: (b, p))]
    out_shape = [jax.ShapeDtypeStruct((n_b * seq_len, RET_HEADS * RET_DV), BF16)]
    if latent:
        st_spec = pl.BlockSpec((None, None, 2, RET_DK, RET_DV), lambda b, p: (b, j, p, 0, 0))
        in_specs += [st_spec, st_spec]
        args += [state_f, state_b]
    else:
        st_spec = pl.BlockSpec((None, 2, RET_DK, RET_DV), lambda b, p: (b, p, 0, 0))
        out_specs += [st_spec, st_spec]
        out_shape += [jax.ShapeDtypeStruct((BATCH, RET_HEADS, RET_DK, RET_DV), F32)] * 2
    return pl.pallas_call(
        functools.partial(_retention_kernel, seq_len, latent, not latent),
        grid=(n_b, n_pairs),
        in_specs=in_specs,
        out_specs=out_specs,
        out_shape=out_shape,
        scratch_shapes=[pltpu.VMEM((seq_len, RET_DV), F32)],
        compiler_params=_params(("parallel", "arbitrary")),
        name="retention_latent" if latent else "retention_prompt",
    )(*args)


def _diff_prep_kernel(k_ref, v_ref, cos_ref, sin_ref, kr_ref, vb_ref):
    lane = _lane_iota(cos_ref.shape)
    cos = cos_ref[...]
    sin = sin_ref[...]
    for h in range(DIFF_HEADS):
        sl = slice(LANES * h, LANES * (h + 1))
        kr_ref[:, sl] = _rope128(k_ref[:, sl], cos, sin, lane, DIFF_DH // 2).astype(BF16)
    vb_ref[...] = v_ref[...].astype(BF16)


def _diff_prep(zodd, cos_t, sin_t):
    tm = 512
    row0 = NTOK_P // tm
    nst = DEC_SEQ // tm
    return pl.pallas_call(
        _diff_prep_kernel,
        grid=(NTOK_S // tm,),
        in_specs=[
            pl.BlockSpec((tm, 1024), lambda i: (row0 + i, 1)),
            pl.BlockSpec((tm, 1024), lambda i: (row0 + i, 2)),
            pl.BlockSpec((tm, LANES), lambda i: (i % nst, 0)),
            pl.BlockSpec((tm, LANES), lambda i: (i % nst, 0)),
        ],
        out_specs=[pl.BlockSpec((tm, 1024), lambda i: (i, 0))] * 2,
        out_shape=[jax.ShapeDtypeStruct((NTOK_S, 1024), BF16)] * 2,
        compiler_params=_params(("parallel",)),
        name="diff_rope_keys",
    )(zodd, zodd, cos_t, sin_t)


def _diff_attn_kernel(latent, lam_init, *refs):
    if latent:
        (lam_ref, ng_ref, q_ref, cos_ref, sin_ref, kc_ref, vc_ref, kn_ref, vn_ref, o_ref) = refs
    else:
        (lam_ref, ng_ref, q_ref, k_ref, v_ref, o_ref, kout_ref, vout_ref) = refs
    tq = q_ref.shape[0]
    lane = _lane_iota((tq, LANES))
    scale = DIFF_DH ** -0.5
    lp = lam_ref[...]
    lam = (jnp.exp(jnp.sum(lp[0:1, :] * lp[1:2, :], axis=-1, keepdims=True))
           - jnp.exp(jnp.sum(lp[2:3, :] * lp[3:4, :], axis=-1, keepdims=True)) + lam_init)
    ng = ng_ref[...]
    for h in range(DIFF_HEADS):
        sl = slice(LANES * h, LANES * (h + 1))
        qh = q_ref[:, sl]
        if latent:
            qh = _rope128(qh, cos_ref[...], sin_ref[...], lane, DIFF_DH // 2)
            k_list = [kc_ref[h].astype(BF16), kn_ref[:, sl]]
            v_list = [vc_ref[h].astype(BF16), vn_ref[:, sl]]
        else:
            kh = k_ref[:, sl]
            vh = v_ref[:, sl]
            kout_ref[h] = kh
            vout_ref[h] = vh
            k_list = [kh.astype(BF16)]
            v_list = [vh.astype(BF16)]
        q1 = jnp.where(lane < DIFF_DH, qh, 0.0).astype(BF16)
        q2 = jnp.where(lane >= DIFF_DH, qh, 0.0).astype(BF16)
        s1 = [lax.dot_general(q1, kk, NT_DIMS, preferred_element_type=F32) for kk in k_list]
        s2 = [lax.dot_general(q2, kk, NT_DIMS, preferred_element_type=F32) for kk in k_list]
        p1, l1 = _softmax_parts(s1, scale)
        p2, l2 = _softmax_parts(s2, scale)
        w1 = 1.0 / l1
        w2 = lam / l2
        o = None
        for pa, pb, vv in zip(p1, p2, v_list):
            part = jnp.dot((pa * w1 - pb * w2).astype(BF16), vv, preferred_element_type=F32)
            o = part if o is None else o + part
        y = o * lax.rsqrt(jnp.mean(o * o, axis=-1, keepdims=True) + RMS_EPS) * ng
        o_ref[:, sl] = (y * (1.0 - lam_init)).astype(BF16)


def _diff_attn_prompt(zodd, lam_p, norm_g, j, lam_init):
    cache_shape = jax.ShapeDtypeStruct((BATCH, DIFF_HEADS, SEQ, LANES), F32)
    cache_spec = pl.BlockSpec((None, DIFF_HEADS, SEQ, LANES), lambda b: (b, 0, 0, 0))
    return pl.pallas_call(
        functools.partial(_diff_attn_kernel, False, lam_init),
        grid=(BATCH,),
        in_specs=[
            pl.BlockSpec((None, 4, DIFF_DH), lambda b: (j, 0, 0)),
            pl.BlockSpec((None, 1, DIFF_DV), lambda b: (j, 0, 0)),
            pl.BlockSpec((SEQ, 1024), lambda b: (b, 0)),
            pl.BlockSpec((SEQ, 1024), lambda b: (b, 1)),
            pl.BlockSpec((SEQ, 1024), lambda b: (b, 2)),
        ],
        out_specs=[pl.BlockSpec((SEQ, 1024), lambda b: (b, 0)), cache_spec, cache_spec],
        out_shape=[jax.ShapeDtypeStruct((NTOK_P, 1024), BF16), cache_shape, cache_shape],
        compiler_params=_params(("parallel",)),
        name="diff_attn_prompt",
    )(lam_p, norm_g.reshape(-1, 1, DIFF_DV), zodd, zodd, zodd)


def _diff_attn_latent(zodd, lam_p, norm_g, cos_t, sin_t, cache_k, cache_v, k_rot, v_b, j, lam_init):
    nq = DEC_SEQ // ATT_TQ
    row0 = NTOK_P // ATT_TQ
    ctx_spec = pl.BlockSpec((None, None, DIFF_HEADS, PAST_LEN, LANES), lambda b, q: (b, j, 0, 0, 0))
    return pl.pallas_call(
        functools.partial(_diff_attn_kernel, True, lam_init),
        grid=(DEC_BATCH, nq),
        in_specs=[
            pl.BlockSpec((None, 4, DIFF_DH), lambda b, q: (j, 0, 0)),
            pl.BlockSpec((None, 1, DIFF_DV), lambda b, q: (j, 0, 0)),
            pl.BlockSpec((ATT_TQ, 1024), lambda b, q: (row0 + b * nq + q, 0)),
            pl.BlockSpec((ATT_TQ, LANES), lambda b, q: (q, 0)),
            pl.BlockSpec((ATT_TQ, LANES), lambda b, q: (q, 0)),
            ctx_spec,
            ctx_spec,
            pl.BlockSpec((DEC_SEQ, 1024), lambda b, q: (b, 0)),
            pl.BlockSpec((DEC_SEQ, 1024), lambda b, q: (b, 0)),
        ],
        out_specs=pl.BlockSpec((ATT_TQ, 1024), lambda b, q: (b * nq + q, 0)),
        out_shape=jax.ShapeDtypeStruct((NTOK_S, 1024), BF16),
        compiler_params=_params(("parallel", "arbitrary")),
        name="diff_attn_latent",
    )(lam_p, norm_g.reshape(-1, 1, DIFF_DV), zodd, cos_t, sin_t, cache_k, cache_v, k_rot, v_b)


def _mla_weight(w_in):
    base = RET_COLS
    mq = w_in[:, base:base + MLA_HEADS * (MLA_NOPE + MLA_ROPE)].reshape(D_MODEL, MLA_HEADS, MLA_NOPE + MLA_ROPE)
    qn = mq[:, :, :MLA_NOPE].reshape(D_MODEL, MLA_HEADS * MLA_NOPE)
    qr = mq[:, :, MLA_NOPE:].reshape(D_MODEL, MLA_HEADS * MLA_ROPE)
    ckv0 = base + MLA_HEADS * (MLA_NOPE + MLA_ROPE)
    ckv = w_in[:, ckv0:ckv0 + MLA_KV_RANK]
    kr = w_in[:, ckv0 + MLA_KV_RANK:]
    return jnp.concatenate([qn, qr, ckv, jnp.tile(kr, (1, LANES // MLA_ROPE))], axis=1)


def kernel(x_prompt, x_sample, state_ret_fwd, state_ret_bwd, cache_mla_ckv, cache_mla_krope, cache_diff_k, cache_diff_v, c, c_ctx, ada_w, ada_b, ln1_g, ln1_b, ln2_g, ln2_b, ev_w_in, ev_w_out, ret_decay_fwd, ret_decay_bwd, mla_kv_norm_g, mla_w_uk, mla_w_uv, od_w_in, od_w_out, diff_lambda, diff_norm_g, moe_w_group, moe_b_group, moe_w_expert, moe_b_expert, moe_w1, moe_w3, moe_w2):
    x = jnp.concatenate([x_prompt.reshape(NTOK_P, D_MODEL), x_sample.reshape(NTOK_S, D_MODEL)], axis=0)
    cond = jnp.concatenate([c_ctx[None, :], c, jnp.zeros((N_COND - 1 - DEC_BATCH, D_MODEL), F32)], axis=0)
    mods = _ada_all(cond, ada_w, ada_b).reshape(DEPTH, N_COND, 6, D_MODEL)

    cos_m, sin_m = _rope_tables(MLA_ROPE)
    cos_d, sin_d = _rope_tables(DIFF_DH)
    ident = 512
    cos_m_id = jnp.concatenate([jnp.ones((ident, LANES), F32), cos_m], axis=0)
    sin_m_id = jnp.concatenate([jnp.zeros((ident, LANES), F32), sin_m], axis=0)
    decf = jnp.broadcast_to(ret_decay_fwd[:, :, None], ret_decay_fwd.shape + (LANES,))
    decb = jnp.broadcast_to(ret_decay_bwd[:, :, None], ret_decay_bwd.shape + (LANES,))

    pad = LANES - MOE_GROUPS - MOE_EXPERTS
    rf, rb, ck, kro, dk, dv = [], [], [], [], [], []
    for i in range(DEPTH):
        j = i // 2
        mod = mods[i]
        if i % 2 == 0:
            zret = _mm_mod(x, mod, ev_w_in, (j,), RET_COLS, 1024, "in_proj_retention")
            zmla = _mm_mod(x, mod, _mla_weight(ev_w_in[j]), (), MLA_COLS, MLA_COLS, "in_proj_mla")
            ckv_n, kcat, vm = _mla_kv_new(zmla, cos_m_id, sin_m_id, mla_kv_norm_g, mla_w_uk, mla_w_uv, j)
            kr_ctx = jnp.tile(cache_mla_krope[:, j], (1, 1, LANES // MLA_ROPE))
            kcat_ctx, vm_ctx = _mla_kv_ctx(cache_mla_ckv, kr_ctx, mla_w_uk, mla_w_uv, j)
            a_ret_p, s_f, s_b = _retention(zret, decf, decb, j, False)
            (a_ret_s,) = _retention(zret, decf, decb, j, True, state_ret_fwd, state_ret_bwd)
            a_mla_p = _mla_attn_prompt(zmla, kcat, vm)
            a_mla_s = _mla_attn_latent(zmla, cos_m, sin_m, kcat_ctx, vm_ctx, kcat, vm)
            a_list = [jnp.concatenate([a_ret_p, a_ret_s], axis=0), jnp.concatenate([a_mla_p, a_mla_s], axis=0)]
            x = _mm_ln(a_list, ev_w_out, j, x, mod, ln1_g, ln1_b, i)
            rf.append(s_f)
            rb.append(s_b)
            ck.append(ckv_n[:NTOK_P].reshape(BATCH, SEQ, MLA_KV_RANK))
            kro.append(zmla[:NTOK_P, 1024:1024 + MLA_ROPE].reshape(BATCH, SEQ, MLA_ROPE))
        else:
            lam_init = 0.8 - 0.6 * math.exp(-0.3 * i)
            zodd = _mm_mod(x, mod, od_w_in, (j,), 3072, 1024, "in_proj_diff")
            a_p, k_new, v_new = _diff_attn_prompt(zodd, diff_lambda, diff_norm_g, j, lam_init)
            k_rot, v_b = _diff_prep(zodd, cos_d, sin_d)
            a_s = _diff_attn_latent(zodd, diff_lambda, diff_norm_g, cos_d, sin_d, cache_diff_k, cache_diff_v,
                                    k_rot, v_b, j, lam_init)
            x = _mm_ln([jnp.concatenate([a_p, a_s], axis=0)], od_w_out, j, x, mod, ln1_g, ln1_b, i)
            dk.append(k_new)
            dv.append(v_new)
        w_router = jnp.concatenate([moe_w_group[i], moe_w_expert[i], jnp.zeros((D_MODEL, pad), F32)], axis=1)
        b_router = jnp.concatenate([moe_b_group[i], moe_b_expert[i], jnp.zeros((pad,), F32)])[None, :]
        x = _moe(x, mod, w_router, b_router, moe_w1, moe_w3, moe_w2, ln2_g, ln2_b, i)

    y_prompt = x[:NTOK_P].reshape(BATCH, SEQ, D_MODEL)
    y_sample = x[NTOK_P:].reshape(DEC_BATCH, DEC_SEQ, D_MODEL)
    return (y_prompt, y_sample, jnp.stack(rf, axis=1), jnp.stack(rb, axis=1), jnp.stack(ck, axis=1),
            jnp.stack(kro, axis=1), jnp.stack(dk, axis=1), jnp.stack(dv, axis=1))
```

```python
import functools
import math

import jax
import jax.numpy as jnp
from jax import lax
from jax.experimental import pallas as pl
from jax.experimental.pallas import tpu as pltpu

D_MODEL = 1024
BATCH = 32
SEQ = 256
DEPTH = 4
N_EVEN = 2
N_ODD = 2
DEC_BATCH = 2
DEC_SEQ = 2048
PAST_LEN = 256
GRID_W = 64
LN_EPS = 1e-5
RMS_EPS = 1e-6
DEEPNORM_ALPHA = (2.0 * DEPTH) ** 0.25
ROPE_BASE = 10000.0
RET_HEADS = 8
RET_DK = 64
RET_DV = 128
MLA_HEADS = 8
MLA_NOPE = 64
MLA_ROPE = 32
MLA_DV = 64
MLA_KV_RANK = 256
DIFF_HEADS = 8
DIFF_DH = 64
DIFF_DV = 128
MOE_GROUPS = 4
MOE_PER_GROUP = 4
MOE_EXPERTS = 16
MOE_FF = 256

NTOK_P = BATCH * SEQ
NTOK_S = DEC_BATCH * DEC_SEQ
NTOK = NTOK_P + NTOK_S
N_COND = 8
LANES = 128
RET_COLS = 3072
MLA_COLS = 1152
ATT_TQ = 256
RET_CHUNK = 256
VMEM_LIMIT = 56 * 1024 * 1024
LOG2E = 1.4426950408889634

F32 = jnp.float32
BF16 = jnp.bfloat16
NT_DIMS = (((1,), (1,)), ((), ()))
TN_DIMS = (((0,), (0,)), ((), ()))


def _params(sem):
    return pltpu.CompilerParams(dimension_semantics=sem, vmem_limit_bytes=VMEM_LIMIT)


def _group_of_tile(i, tm):
    npt = NTOK_P // tm
    nst = DEC_SEQ // tm
    return jnp.where(i < npt, 0, 1 + (i - npt) // nst)


def _split_specs(tm, width, m_of, **kw):
    npt = NTOK_P // tm
    return [pl.BlockSpec((tm, width), lambda *g: (jnp.minimum(m_of(*g), npt - 1), 0), **kw),
            pl.BlockSpec((tm, width), lambda *g: (jnp.maximum(m_of(*g) - npt, 0), 0), **kw)]


def _split_shapes(width, dtype):
    return [jax.ShapeDtypeStruct((NTOK_P, width), dtype), jax.ShapeDtypeStruct((NTOK_S, width), dtype)]


def _read_split(p_ref, s_ref, m):
    return jnp.where(m < NTOK_P // p_ref.shape[0], p_ref[...], s_ref[...])


def _write_split(p_ref, s_ref, m, value):
    npt = NTOK_P // p_ref.shape[0]

    @pl.when(m < npt)
    def _():
        p_ref[...] = value

    @pl.when(m >= npt)
    def _():
        s_ref[...] = value


def _silu(x):
    return x * (1.0 / (1.0 + jnp.exp(-x)))


def _layer_norm(r, g, b):
    mu = jnp.mean(r, axis=-1, keepdims=True)
    d = r - mu
    var = jnp.mean(d * d, axis=-1, keepdims=True)
    return d * lax.rsqrt(var + LN_EPS) * g + b


def _lane_iota(shape):
    return lax.broadcasted_iota(jnp.int32, shape, 1)


def _div_pow2(x, d):
    assert d & (d - 1) == 0
    return jnp.right_shift(x, d.bit_length() - 1)


def _mod_pow2(x, d):
    assert d & (d - 1) == 0
    return jnp.bitwise_and(x, d - 1)


def _ada_kernel(c_ref, w_ref, b_ref, o_ref):
    h = _silu(c_ref[...]).astype(BF16)
    o_ref[...] = jnp.dot(h, w_ref[...].astype(BF16), preferred_element_type=F32) + b_ref[...]


def _ada_all(cond, ada_w, ada_b):
    tn = 768
    return pl.pallas_call(
        _ada_kernel,
        grid=(DEPTH, 6 * D_MODEL // tn),
        in_specs=[
            pl.BlockSpec((N_COND, D_MODEL), lambda l, n: (0, 0)),
            pl.BlockSpec((None, D_MODEL, tn), lambda l, n: (l, 0, n)),
            pl.BlockSpec((None, 1, tn), lambda l, n: (l, 0, n)),
        ],
        out_specs=pl.BlockSpec((None, N_COND, tn), lambda l, n: (l, 0, n)),
        out_shape=jax.ShapeDtypeStruct((DEPTH, N_COND, 6 * D_MODEL), F32),
        compiler_params=_params(("parallel", "parallel")),
        name="ada_modulation",
    )(cond, ada_w, ada_b.reshape(DEPTH, 1, 6 * D_MODEL))


def _mm_mod_kernel(xp_ref, xs_ref, mod_ref, w_ref, o_ref, wscr):
    m = pl.program_id(1)

    @pl.when(m == 0)
    def _():
        wscr[...] = w_ref[...].astype(BF16)

    sh = mod_ref[0:1, :]
    sc = mod_ref[1:2, :]
    xm = (_read_split(xp_ref, xs_ref, m) * (1.0 + sc) + sh).astype(BF16)
    o_ref[...] = jnp.dot(xm, wscr[...], preferred_element_type=F32)


def _mm_mod(x, mod, w, w_index, n_cols, tn, name):
    tm = 1024
    w_block = (None,) * len(w_index) + (D_MODEL, tn)
    return pl.pallas_call(
        _mm_mod_kernel,
        grid=(n_cols // tn, NTOK // tm),
        in_specs=_split_specs(tm, D_MODEL, lambda n, m: m) + [
            pl.BlockSpec((None, 6, D_MODEL), lambda n, m: (_group_of_tile(m, tm), 0, 0)),
            pl.BlockSpec(w_block, lambda n, m: tuple(w_index) + (0, n)),
        ],
        out_specs=pl.BlockSpec((tm, tn), lambda n, m: (m, n)),
        out_shape=jax.ShapeDtypeStruct((NTOK, n_cols), F32),
        scratch_shapes=[pltpu.VMEM((D_MODEL, tn), BF16)],
        compiler_params=_params(("arbitrary", "arbitrary")),
        name=name,
    )(*x, mod, w)


def _mm_ln_kernel(k_sizes, *refs):
    n_a = len(k_sizes)
    a_refs = refs[:2 * n_a]
    w_ref, xp_ref, xs_ref, mod_ref, g_ref, b_ref, op_ref, os_ref, wscr = refs[2 * n_a:]
    m = pl.program_id(0)

    @pl.when(m == 0)
    def _():
        wscr[...] = w_ref[...].astype(BF16)

    y = None
    k0 = 0
    for i, ks in enumerate(k_sizes):
        a = _read_split(a_refs[2 * i], a_refs[2 * i + 1], m)
        part = jnp.dot(a, wscr[k0:k0 + ks, :], preferred_element_type=F32)
        y = part if y is None else y + part
        k0 += ks
    gate = mod_ref[2:3, :]
    r = DEEPNORM_ALPHA * _read_split(xp_ref, xs_ref, m) + gate * y
    _write_split(op_ref, os_ref, m, _layer_norm(r, g_ref[...], b_ref[...]))


def _mm_ln(a_pairs, w, j, x, mod, ln_g, ln_b, layer):
    tm = 512
    k_sizes = tuple(ap.shape[1] for ap, _ in a_pairs)
    k_tot = sum(k_sizes)
    in_specs = []
    args = []
    for (ap, a_s), ks in zip(a_pairs, k_sizes):
        in_specs += _split_specs(tm, ks, lambda m: m)
        args += [ap, a_s]
    in_specs += [pl.BlockSpec((None, k_tot, D_MODEL), lambda m: (j, 0, 0))]
    in_specs += _split_specs(tm, D_MODEL, lambda m: m)
    in_specs += [
        pl.BlockSpec((None, 6, D_MODEL), lambda m: (_group_of_tile(m, tm), 0, 0)),
        pl.BlockSpec((None, 1, D_MODEL), lambda m: (layer, 0, 0)),
        pl.BlockSpec((None, 1, D_MODEL), lambda m: (layer, 0, 0)),
    ]
    return pl.pallas_call(
        functools.partial(_mm_ln_kernel, k_sizes),
        grid=(NTOK // tm,),
        in_specs=in_specs,
        out_specs=_split_specs(tm, D_MODEL, lambda m: m),
        out_shape=_split_shapes(D_MODEL, F32),
        scratch_shapes=[pltpu.VMEM((k_tot, D_MODEL), BF16)],
        compiler_params=_params(("arbitrary",)),
        name="out_proj_ln",
    )(*args, w, *x, mod, ln_g.reshape(DEPTH, 1, D_MODEL), ln_b.reshape(DEPTH, 1, D_MODEL))


def _moe_kernel(xp_ref, xs_ref, mod_ref, wr_ref, br_ref, w1_ref, w3_ref, w2_ref, g_ref, b_ref,
                op_ref, os_ref, xm_scr, comb_scr, acc_scr):
    m = pl.program_id(0)
    e = pl.program_id(1)
    tm = xp_ref.shape[0]

    @pl.when(e == 0)
    def _():
        sh = mod_ref[3:4, :]
        sc = mod_ref[4:5, :]
        xm = (_read_split(xp_ref, xs_ref, m) * (1.0 + sc) + sh).astype(BF16)
        xm_scr[...] = xm
        z = jnp.dot(xm, wr_ref[...].astype(BF16), preferred_element_type=F32) + br_ref[...]
        lane_i = _lane_iota((tm, LANES))
        lane = lane_i.astype(F32)
        none = jnp.float32(LANES)
        neg = jnp.float32(-jnp.inf)
        gmask = lane_i < MOE_GROUPS
        zg = jnp.where(gmask, z, neg)
        pg = jnp.exp(zg - jnp.max(zg, axis=-1, keepdims=True))
        g_prob = pg / jnp.sum(pg, axis=-1, keepdims=True)
        g_p = jnp.max(g_prob, axis=-1, keepdims=True)
        g_idx = jnp.min(jnp.where(gmask & (g_prob == g_p), lane, none), axis=-1, keepdims=True)
        assert MOE_GROUPS % MOE_PER_GROUP == 0
        e_group = (_div_pow2(lane_i, MOE_PER_GROUP) - MOE_GROUPS // MOE_PER_GROUP).astype(F32)
        emask = (lane_i >= MOE_GROUPS) & (lane_i < MOE_GROUPS + MOE_EXPERTS) & (e_group == g_idx)
        ze = jnp.where(emask, z, neg)
        pe = jnp.exp(ze - jnp.max(ze, axis=-1, keepdims=True))
        e_prob = pe / jnp.sum(pe, axis=-1, keepdims=True)
        cand = jnp.where(emask, e_prob, -1.0)
        p1 = jnp.max(cand, axis=-1, keepdims=True)
        i1 = jnp.min(jnp.where(cand == p1, lane, none), axis=-1, keepdims=True)
        cand2 = jnp.where(lane == i1, -1.0, cand)
        p2 = jnp.max(cand2, axis=-1, keepdims=True)
        i2 = jnp.min(jnp.where(cand2 == p2, lane, none), axis=-1, keepdims=True)
        denom = p1 + p2
        comb_scr[...] = (jnp.where(lane == i1, g_p * p1 / denom, 0.0)
                         + jnp.where(lane == i2, g_p * p2 / denom, 0.0))
        acc_scr[...] = jnp.zeros_like(acc_scr)

    xm = xm_scr[...]
    h1 = jnp.dot(xm, w1_ref[...].astype(BF16), preferred_element_type=F32)
    h3 = jnp.dot(xm, w3_ref[...].astype(BF16), preferred_element_type=F32)
    lane = _lane_iota((tm, LANES))
    c = jnp.sum(jnp.where(lane == e + MOE_GROUPS, comb_scr[...], 0.0), axis=-1, keepdims=True)
    hid = (_silu(h1) * h3 * c).astype(BF16)
    acc_scr[...] += jnp.dot(hid, w2_ref[...].astype(BF16), preferred_element_type=F32)

    @pl.when(e == MOE_EXPERTS - 1)
    def _():
        gate = mod_ref[5:6, :]
        r = DEEPNORM_ALPHA * _read_split(xp_ref, xs_ref, m) + gate * acc_scr[...]
        _write_split(op_ref, os_ref, m, _layer_norm(r, g_ref[...], b_ref[...]))


def _moe(x, mod, w_router, b_router, w1, w3, w2, ln_g, ln_b, layer):
    tm = 1024
    return pl.pallas_call(
        _moe_kernel,
        grid=(NTOK // tm, MOE_EXPERTS),
        in_specs=_split_specs(tm, D_MODEL, lambda m, e: m, pipeline_mode=pl.Buffered(1)) + [
            pl.BlockSpec((None, 6, D_MODEL), lambda m, e: (_group_of_tile(m, tm), 0, 0)),
            pl.BlockSpec((D_MODEL, LANES), lambda m, e: (0, 0)),
            pl.BlockSpec((1, LANES), lambda m, e: (0, 0)),
            pl.BlockSpec((None, None, D_MODEL, MOE_FF), lambda m, e: (layer, e, 0, 0)),
            pl.BlockSpec((None, None, D_MODEL, MOE_FF), lambda m, e: (layer, e, 0, 0)),
            pl.BlockSpec((None, None, MOE_FF, D_MODEL), lambda m, e: (layer, e, 0, 0)),
            pl.BlockSpec((None, 1, D_MODEL), lambda m, e: (layer, 0, 0)),
            pl.BlockSpec((None, 1, D_MODEL), lambda m, e: (layer, 0, 0)),
        ],
        out_specs=_split_specs(tm, D_MODEL, lambda m, e: m),
        out_shape=_split_shapes(D_MODEL, F32),
        scratch_shapes=[
            pltpu.VMEM((tm, D_MODEL), BF16),
            pltpu.VMEM((tm, LANES), F32),
            pltpu.VMEM((tm, D_MODEL), F32),
        ],
        compiler_params=_params(("arbitrary", "arbitrary")),
        name="hier_moe_ln",
    )(*x, mod, w_router, b_router, w1, w3, w2,
      ln_g.reshape(DEPTH, 1, D_MODEL), ln_b.reshape(DEPTH, 1, D_MODEL))


def _swap_halves(x, lane, half):
    return jnp.where(_mod_pow2(lane, 2 * half) < half,
                     pltpu.roll(x, LANES - half, 1), pltpu.roll(x, half, 1))


def _rope128(x, cos, sin_signed, lane, half):
    return x * cos + _swap_halves(x, lane, half) * sin_signed


def _rope_tables(rot_dim):
    rows = DEC_SEQ // GRID_W
    row = jnp.repeat(jnp.arange(rows, dtype=F32), GRID_W)
    col = jnp.tile(jnp.arange(GRID_W, dtype=F32), rows)
    n_freq = rot_dim // 4
    inv_freq = ROPE_BASE ** (-jnp.arange(n_freq, dtype=F32) / n_freq)
    ang = jnp.concatenate([row[:, None] * inv_freq, col[:, None] * inv_freq], axis=-1)
    cos, sin = jnp.cos(ang), jnp.sin(ang)
    reps = LANES // rot_dim
    cos_full = jnp.tile(jnp.concatenate([cos, cos], axis=-1), (1, reps))
    sin_signed = jnp.tile(jnp.concatenate([-sin, sin], axis=-1), (1, reps))
    return cos_full, sin_signed


def _exp_parts(s_list, scale):
    m = None
    for s in s_list:
        sm = jnp.max(s, axis=-1, keepdims=True)
        m = sm if m is None else jnp.maximum(m, sm)
    return [jnp.exp2((s - m) * (scale * LOG2E)).astype(BF16) for s in s_list]


def _pv_normalised(p_list, v_list):
    o = None
    for p, v in zip(p_list, v_list):
        part = jnp.dot(p, v, preferred_element_type=F32)
        o = part if o is None else o + part
    return o[:, :LANES] / o[:, LANES:]


def _mla_kv_kernel(new_tokens, *refs):
    if new_tokens:
        (ckv_ref, kr_ref, cos_ref, sin_ref, g_ref, wuk_ref, wuv_ref) = refs[:7]
        ckvc_ref, krc_ref, kcat_ref, vm_ref = refs[-4:]
        i = pl.program_id(0)
        x = ckv_ref[...]
        c = x * lax.rsqrt(jnp.mean(x * x, axis=-1, keepdims=True) + RMS_EPS) * g_ref[...]
        kr_raw = kr_ref[...]

        @pl.when(i < NTOK_P // ckv_ref.shape[0])
        def _():
            for b in range(ckvc_ref.shape[0]):
                ckvc_ref[b] = c[SEQ * b:SEQ * (b + 1), :]
                krc_ref[b] = kr_raw[SEQ * b:SEQ * (b + 1), :MLA_ROPE]

        lane = _lane_iota(kr_ref.shape)
        kr = _rope128(kr_raw, cos_ref[...], sin_ref[...], lane, MLA_ROPE // 2)
    else:
        ckv_ref, kr_ref, wuk_ref, wuv_ref, kcat_ref, vm_ref = refs
        c = ckv_ref[...]
        kr = kr_ref[...]
    cb = c.astype(BF16)
    kn = jnp.dot(cb, wuk_ref[...].astype(BF16), preferred_element_type=F32).astype(BF16)
    vv = jnp.dot(cb, wuv_ref[...].astype(BF16), preferred_element_type=F32).astype(BF16)
    krb = kr.astype(BF16)
    ones = jnp.ones((c.shape[0], LANES), BF16)
    for p in range(MLA_HEADS // 2):
        kcat_ref[:, 256 * p:256 * p + LANES] = kn[:, LANES * p:LANES * (p + 1)]
        kcat_ref[:, 256 * p + LANES:256 * (p + 1)] = krb
        vm_ref[:, 256 * p:256 * p + LANES] = vv[:, LANES * p:LANES * (p + 1)]
        vm_ref[:, 256 * p + LANES:256 * (p + 1)] = ones


def _mla_kv_new(zmla, cos_t, sin_t, kv_norm_g, w_uk, w_uv, j, prev):
    tm = 512
    npt = NTOK_P // tm
    nst = DEC_SEQ // tm
    nb = tm // SEQ

    def tab(i):
        return (jnp.where(i < npt, 0, 1 + (i - npt) % nst), 0)

    def cache_idx(i):
        return (jnp.minimum(i, npt - 1), j, 0, 0)

    in_specs = [
        pl.BlockSpec((tm, MLA_KV_RANK), lambda i: (i, 768 // MLA_KV_RANK)),
        pl.BlockSpec((tm, LANES), lambda i: (i, 1024 // LANES)),
        pl.BlockSpec((tm, LANES), tab),
        pl.BlockSpec((tm, LANES), tab),
        pl.BlockSpec((None, 1, MLA_KV_RANK), lambda i: (j, 0, 0)),
        pl.BlockSpec((None, MLA_KV_RANK, 512), lambda i: (j, 0, 0)),
        pl.BlockSpec((None, MLA_KV_RANK, 512), lambda i: (j, 0, 0)),
    ]
    args = [zmla, zmla, cos_t, sin_t, kv_norm_g.reshape(-1, 1, MLA_KV_RANK), w_uk, w_uv]
    aliases = {}
    if prev is not None:
        aliases = {len(args): 0, len(args) + 1: 1}
        in_specs += [pl.BlockSpec(memory_space=pl.ANY)] * 2
        args += list(prev)
    return pl.pallas_call(
        functools.partial(_mla_kv_kernel, True),
        grid=(NTOK // tm,),
        in_specs=in_specs,
        out_specs=[
            pl.BlockSpec((nb, None, SEQ, MLA_KV_RANK), cache_idx),
            pl.BlockSpec((nb, None, SEQ, MLA_ROPE), cache_idx),
            pl.BlockSpec((tm, 1024), lambda i: (i, 0)),
            pl.BlockSpec((tm, 1024), lambda i: (i, 0)),
        ],
        out_shape=[
            jax.ShapeDtypeStruct((BATCH, N_EVEN, SEQ, MLA_KV_RANK), F32),
            jax.ShapeDtypeStruct((BATCH, N_EVEN, SEQ, MLA_ROPE), F32),
            jax.ShapeDtypeStruct((NTOK, 1024), BF16),
            jax.ShapeDtypeStruct((NTOK, 1024), BF16),
        ],
        input_output_aliases=aliases,
        compiler_params=_params(("arbitrary",)),
        name="mla_kv_new",
    )(*args)


def _mla_kv_ctx(cache_ckv, kr_tiled, w_uk, w_uv, j):
    return pl.pallas_call(
        functools.partial(_mla_kv_kernel, False),
        grid=(DEC_BATCH,),
        in_specs=[
            pl.BlockSpec((None, None, PAST_LEN, MLA_KV_RANK), lambda b: (b, j, 0, 0)),
            pl.BlockSpec((None, PAST_LEN, LANES), lambda b: (b, 0, 0)),
            pl.BlockSpec((None, MLA_KV_RANK, 512), lambda b: (j, 0, 0)),
            pl.BlockSpec((None, MLA_KV_RANK, 512), lambda b: (j, 0, 0)),
        ],
        out_specs=[
            pl.BlockSpec((PAST_LEN, 1024), lambda b: (b, 0)),
            pl.BlockSpec((PAST_LEN, 1024), lambda b: (b, 0)),
        ],
        out_shape=[
            jax.ShapeDtypeStruct((DEC_BATCH * PAST_LEN, 1024), BF16),
            jax.ShapeDtypeStruct((DEC_BATCH * PAST_LEN, 1024), BF16),
        ],
        compiler_params=_params(("parallel",)),
        name="mla_kv_ctx",
    )(cache_ckv, kr_tiled, w_uk, w_uv)


def _mla_attn_kernel(latent, *refs):
    if latent:
        qn_ref, qr_ref, cos_ref, sin_ref, kc_ref, vc_ref, kn_ref, vn_ref, o_ref = refs
        k_refs, v_refs = (kc_ref, kn_ref), (vc_ref, vn_ref)
    else:
        qn_ref, qr_ref, kn_ref, vn_ref, o_ref = refs
        k_refs, v_refs = (kn_ref,), (vn_ref,)
    tq = qn_ref.shape[0]
    lane = _lane_iota((tq, LANES))
    scale = (MLA_NOPE + MLA_ROPE) ** -0.5
    qr_cols = []
    for cidx in range(2):
        x = qr_ref[:, LANES * cidx:LANES * (cidx + 1)]
        if latent:
            x = _rope128(x, cos_ref[...], sin_ref[...], lane, MLA_ROPE // 2)
        qr_cols.append(x)
    o_prev = None
    for h in range(MLA_HEADS):
        p, half = divmod(h, 2)
        cidx, slot = divmod(h, 4)
        qa = jnp.where(_div_pow2(lane, MLA_NOPE) == half, qn_ref[:, LANES * p:LANES * (p + 1)], 0.0)
        qb = jnp.where(_div_pow2(lane, MLA_ROPE) == slot, qr_cols[cidx], 0.0)
        qcat = jnp.concatenate([qa, qb], axis=1).astype(BF16)
        pair_cols = slice(256 * p, 256 * (p + 1))
        s_list = [lax.dot_general(qcat, k_ref[:, pair_cols], NT_DIMS, preferred_element_type=F32)
                  for k_ref in k_refs]
        o = _pv_normalised(_exp_parts(s_list, scale), [v_ref[:, pair_cols] for v_ref in v_refs])
        if half == 0:
            o_prev = o
        else:
            o_ref[:, LANES * p:LANES * (p + 1)] = jnp.where(lane < MLA_DV, o_prev, o).astype(BF16)


def _mla_attn_prompt(zmla, kcat, vm):
    return pl.pallas_call(
        functools.partial(_mla_attn_kernel, False),
        grid=(BATCH,),
        in_specs=[
            pl.BlockSpec((SEQ, 512), lambda b: (b, 0)),
            pl.BlockSpec((SEQ, 256), lambda b: (b, 2)),
            pl.BlockSpec((SEQ, 1024), lambda b: (b, 0)),
            pl.BlockSpec((SEQ, 1024), lambda b: (b, 0)),
        ],
        out_specs=pl.BlockSpec((SEQ, 512), lambda b: (b, 0)),
        out_shape=jax.ShapeDtypeStruct((NTOK_P, 512), BF16),
        compiler_params=_params(("parallel",)),
        name="mla_attn_prompt",
    )(zmla, zmla, kcat, vm)


def _mla_attn_latent(zmla, cos_t, sin_t, kcat_ctx, vm_ctx, kcat, vm):
    nq = DEC_SEQ // ATT_TQ
    row0 = NTOK_P // ATT_TQ
    seq0 = NTOK_P // DEC_SEQ
    return pl.pallas_call(
        functools.partial(_mla_attn_kernel, True),
        grid=(DEC_BATCH, nq),
        in_specs=[
            pl.BlockSpec((ATT_TQ, 512), lambda b, q: (row0 + b * nq + q, 0)),
            pl.BlockSpec((ATT_TQ, 256), lambda b, q: (row0 + b * nq + q, 2)),
            pl.BlockSpec((ATT_TQ, LANES), lambda b, q: (q, 0)),
            pl.BlockSpec((ATT_TQ, LANES), lambda b, q: (q, 0)),
            pl.BlockSpec((PAST_LEN, 1024), lambda b, q: (b, 0)),
            pl.BlockSpec((PAST_LEN, 1024), lambda b, q: (b, 0)),
            pl.BlockSpec((DEC_SEQ, 1024), lambda b, q: (seq0 + b, 0)),
            pl.BlockSpec((DEC_SEQ, 1024), lambda b, q: (seq0 + b, 0)),
        ],
        out_specs=pl.BlockSpec((ATT_TQ, 512), lambda b, q: (b * nq + q, 0)),
        out_shape=jax.ShapeDtypeStruct((NTOK_S, 512), BF16),
        compiler_params=_params(("parallel", "arbitrary")),
        name="mla_attn_latent",
    )(zmla, zmla, cos_t, sin_t, kcat_ctx, vm_ctx, kcat, vm)


def _log_sigmoid(x):
    return jnp.minimum(x, 0.0) - jnp.log1p(jnp.exp(-jnp.abs(x)))


def _retention_kernel(seq_len, has_init, emit_state, has_prev, *refs):
    refs = list(refs)
    decf_ref, decb_ref, q_ref, k_ref, v_ref, g_ref = refs[:6]
    refs = refs[6:]
    if has_init:
        sf0_ref, sb0_ref = refs[:2]
        refs = refs[2:]
    if has_prev:
        refs = refs[2:]
    o_ref = refs[0]
    refs = refs[1:]
    if emit_state:
        sf_ref, sb_ref = refs[:2]
        refs = refs[2:]
    oacc, dec_scr, wts_scr = refs

    c = RET_CHUNK
    n_chunks = seq_len // c
    pair = pl.program_id(0)
    lane = _lane_iota((c, LANES))
    zeros_half = jnp.zeros((RET_DK, RET_DV), F32)

    def log_gammas(half):
        head = 2 * pair + half
        return (_log_sigmoid(decf_ref[pl.ds(head, 1), :]),
                _log_sigmoid(decb_ref[pl.ds(head, 1), :]))

    @pl.when(pl.program_id(1) == 0)
    def _():
        ri = lax.broadcasted_iota(jnp.int32, (c, c), 0)
        ci = lax.broadcasted_iota(jnp.int32, (c, c), 1)
        rel = (ri - ci).astype(F32)
        row = lax.broadcasted_iota(jnp.int32, (c, LANES), 0).astype(F32)
        for half in range(2):
            lgf, lgb = log_gammas(half)
            dec_scr[half] = (jnp.where(rel >= 0, jnp.exp(lgf[:, 0:1] * jnp.maximum(rel, 0.0)), 0.0)
                             + jnp.where(rel <= 0, jnp.exp(lgb[:, 0:1] * jnp.maximum(-rel, 0.0)), 0.0))
            wts_scr[half, 0] = jnp.exp(lgf * (row + 1.0))
            wts_scr[half, 1] = jnp.exp(lgf * (c - 1.0 - row))
            wts_scr[half, 2] = jnp.exp(lgb * (c - row))
            wts_scr[half, 3] = jnp.exp(lgb * row)

    for half in range(2):
        lgf, lgb = log_gammas(half)
        cd_f = jnp.exp(lgf * float(c))
        cd_b = jnp.exp(lgb * float(c))
        hmask = _div_pow2(lane, RET_DK) == half
        vsl = slice(RET_DV * half, RET_DV * (half + 1))

        def load(n):
            rows = pl.ds(n * c if isinstance(n, int) else pl.multiple_of(n * c, c), c)
            qm = jnp.where(hmask, q_ref[rows, :], 0.0)
            kk = k_ref[rows, :] * (RET_DK ** -0.5)
            vb = v_ref[rows, vsl].astype(BF16)
            return rows, qm, kk, vb

        def init_state(s0_ref):
            if not has_init:
                return jnp.zeros((LANES, RET_DV), F32)
            s0 = s0_ref[half]
            return jnp.concatenate([s0, zeros_half] if half == 0 else [zeros_half, s0], axis=0)

        def fwd_step(n, s_f):
            rows, qm, kk, vb = load(n)
            s = lax.dot_general(qm.astype(BF16), kk.astype(BF16), NT_DIMS, preferred_element_type=F32)
            o = jnp.dot((s * dec_scr[half]).astype(BF16), vb, preferred_element_type=F32)
            if has_init or n_chunks > 1:
                o = o + jnp.dot((qm * wts_scr[half, 0]).astype(BF16), s_f.astype(BF16),
                                preferred_element_type=F32)
            oacc[rows, :] = o
            kv = lax.dot_general((kk * wts_scr[half, 1]).astype(BF16), vb, TN_DIMS,
                                 preferred_element_type=F32)
            return cd_f * s_f + kv

        def bwd_step(t, s_b):
            n = n_chunks - 1 - t
            rows, qm, kk, vb = load(n)
            o = oacc[rows, :]
            if has_init or n_chunks > 1:
                o = o + jnp.dot((qm * wts_scr[half, 2]).astype(BF16), s_b.astype(BF16),
                                preferred_element_type=F32)
            mu = jnp.mean(o, axis=-1, keepdims=True)
            d = o - mu
            var = jnp.mean(d * d, axis=-1, keepdims=True)
            o_ref[rows, vsl] = (_silu(g_ref[rows, vsl]) * (d * lax.rsqrt(var + LN_EPS))).astype(BF16)
            kv = lax.dot_general((kk * wts_scr[half, 3]).astype(BF16), vb, TN_DIMS,
                                 preferred_element_type=F32)
            return cd_b * s_b + kv

        s_f = init_state(sf0_ref if has_init else None)
        s_b = init_state(sb0_ref if has_init else None)
        if n_chunks == 1:
            s_f = fwd_step(0, s_f)
            s_b = bwd_step(0, s_b)
        else:
            s_f = lax.fori_loop(0, n_chunks, fwd_step, s_f)
            s_b = lax.fori_loop(0, n_chunks, bwd_step, s_b)
        if emit_state:
            sf_ref[half] = s_f[RET_DK * half:RET_DK * (half + 1), :]
            sb_ref[half] = s_b[RET_DK * half:RET_DK * (half + 1), :]


def _retention(zret, decf, decb, j, latent, state_f=None, state_b=None, prev=None):
    seq_len = DEC_SEQ if latent else SEQ
    n_b = DEC_BATCH if latent else BATCH
    row0 = NTOK_P // DEC_SEQ if latent else 0
    n_pairs = RET_HEADS // 2
    in_specs = [
        pl.BlockSpec((None, RET_HEADS, LANES), lambda p, b: (j, 0, 0)),
        pl.BlockSpec((None, RET_HEADS, LANES), lambda p, b: (j, 0, 0)),
        pl.BlockSpec((seq_len, LANES), lambda p, b: (row0 + b, p)),
        pl.BlockSpec((seq_len, LANES), lambda p, b: (row0 + b, 4 + p)),
        pl.BlockSpec((seq_len, 256), lambda p, b: (row0 + b, 4 + p)),
        pl.BlockSpec((seq_len, 256), lambda p, b: (row0 + b, 8 + p)),
    ]
    args = [decf, decb, zret, zret, zret, zret]
    out_specs = [pl.BlockSpec((seq_len, 256), lambda p, b: (b, p))]
    out_shape = [jax.ShapeDtypeStruct((n_b * seq_len, RET_HEADS * RET_DV), BF16)]
    st_spec = pl.BlockSpec((None, None, 2, RET_DK, RET_DV), lambda p, b: (b, j, p, 0, 0))
    aliases = {}
    if latent:
        in_specs += [st_spec, st_spec]
        args += [state_f, state_b]
    else:
        out_specs += [st_spec, st_spec]
        out_shape += [jax.ShapeDtypeStruct((BATCH, N_EVEN, RET_HEADS, RET_DK, RET_DV), F32)] * 2
        if prev is not None:
            aliases = {len(args): 1, len(args) + 1: 2}
            in_specs += [pl.BlockSpec(memory_space=pl.ANY)] * 2
            args += list(prev)
    return pl.pallas_call(
        functools.partial(_retention_kernel, seq_len, latent, not latent, bool(aliases)),
        grid=(n_pairs, n_b),
        in_specs=in_specs,
        out_specs=out_specs,
        out_shape=out_shape,
        scratch_shapes=[
            pltpu.VMEM((seq_len, RET_DV), F32),
            pltpu.VMEM((2, RET_CHUNK, RET_CHUNK), F32),
            pltpu.VMEM((2, 4, RET_CHUNK, LANES), F32),
        ],
        input_output_aliases=aliases,
        compiler_params=_params(("arbitrary", "arbitrary")),
        name="retention_latent" if latent else "retention_prompt",
    )(*args)


def _diff_prep_kernel(k_ref, v_ref, cos_ref, sin_ref, kr_ref, va_ref):
    lane = _lane_iota(cos_ref.shape)
    cos = cos_ref[...]
    sin = sin_ref[...]
    ones = jnp.ones(cos_ref.shape, BF16)
    for h in range(DIFF_HEADS):
        sl = slice(LANES * h, LANES * (h + 1))
        kr_ref[:, sl] = _rope128(k_ref[:, sl], cos, sin, lane, DIFF_DH // 2).astype(BF16)
        va_ref[:, 256 * h:256 * h + LANES] = v_ref[:, sl].astype(BF16)
        va_ref[:, 256 * h + LANES:256 * (h + 1)] = ones


def _diff_prep(zodd, cos_t, sin_t):
    tm = 512
    row0 = NTOK_P // tm
    nst = DEC_SEQ // tm
    return pl.pallas_call(
        _diff_prep_kernel,
        grid=(NTOK_S // tm,),
        in_specs=[
            pl.BlockSpec((tm, 1024), lambda i: (row0 + i, 1)),
            pl.BlockSpec((tm, 1024), lambda i: (row0 + i, 2)),
            pl.BlockSpec((tm, LANES), lambda i: (i % nst, 0)),
            pl.BlockSpec((tm, LANES), lambda i: (i % nst, 0)),
        ],
        out_specs=[pl.BlockSpec((tm, 1024), lambda i: (i, 0)), pl.BlockSpec((tm, 2048), lambda i: (i, 0))],
        out_shape=[jax.ShapeDtypeStruct((NTOK_S, 1024), BF16), jax.ShapeDtypeStruct((NTOK_S, 2048), BF16)],
        compiler_params=_params(("parallel",)),
        name="diff_rope_keys",
    )(zodd, zodd, cos_t, sin_t)


def _diff_attn_kernel(latent, lam_init, *refs):
    if latent:
        (lam_ref, ng_ref, q_ref, cos_ref, sin_ref, kc_ref, vc_ref, kn_ref, vn_ref, o_ref) = refs
    else:
        lam_ref, ng_ref, q_ref, k_ref, v_ref = refs[:5]
        o_ref, kout_ref, vout_ref = refs[-3:]
    tq = q_ref.shape[0]
    lane = _lane_iota((tq, LANES))
    scale = DIFF_DH ** -0.5
    lp = lam_ref[...]
    lam = (jnp.exp(jnp.sum(lp[0:1, :] * lp[1:2, :], axis=-1, keepdims=True))
           - jnp.exp(jnp.sum(lp[2:3, :] * lp[3:4, :], axis=-1, keepdims=True)) + lam_init)
    ng = ng_ref[...]
    ones = jnp.ones((PAST_LEN if latent else tq, LANES), BF16)
    for h in range(DIFF_HEADS):
        sl = slice(LANES * h, LANES * (h + 1))
        qh = q_ref[:, sl]
        if latent:
            qh = _rope128(qh, cos_ref[...], sin_ref[...], lane, DIFF_DH // 2)
            k_list = [kc_ref[h].astype(BF16), kn_ref[:, sl]]
            v_list = [jnp.concatenate([vc_ref[h].astype(BF16), ones], axis=1),
                      vn_ref[:, 256 * h:256 * (h + 1)]]
        else:
            kh = k_ref[:, sl]
            vh = v_ref[:, sl]
            kout_ref[h] = kh
            vout_ref[h] = vh
            k_list = [kh.astype(BF16)]
            v_list = [jnp.concatenate([vh.astype(BF16), ones], axis=1)]
        q1 = jnp.where(lane < DIFF_DH, qh, 0.0).astype(BF16)
        q2 = jnp.where(lane >= DIFF_DH, qh, 0.0).astype(BF16)
        s1 = [lax.dot_general(q1, kk, NT_DIMS, preferred_element_type=F32) for kk in k_list]
        s2 = [lax.dot_general(q2, kk, NT_DIMS, preferred_element_type=F32) for kk in k_list]
        o = _pv_normalised(_exp_parts(s1, scale), v_list) - lam * _pv_normalised(_exp_parts(s2, scale), v_list)
        y = o * lax.rsqrt(jnp.mean(o * o, axis=-1, keepdims=True) + RMS_EPS) * ng
        o_ref[:, sl] = (y * (1.0 - lam_init)).astype(BF16)


def _diff_attn_prompt(zodd, lam_p, norm_g, j, lam_init, prev):
    cache_shape = jax.ShapeDtypeStruct((BATCH, N_ODD, DIFF_HEADS, SEQ, LANES), F32)
    cache_spec = pl.BlockSpec((None, None, DIFF_HEADS, SEQ, LANES), lambda b: (b, j, 0, 0, 0))
    in_specs = [
        pl.BlockSpec((None, 4, DIFF_DH), lambda b: (j, 0, 0)),
        pl.BlockSpec((None, 1, DIFF_DV), lambda b: (j, 0, 0)),
        pl.BlockSpec((SEQ, 1024), lambda b: (b, 0)),
        pl.BlockSpec((SEQ, 1024), lambda b: (b, 1)),
        pl.BlockSpec((SEQ, 1024), lambda b: (b, 2)),
    ]
    args = [lam_p, norm_g.reshape(-1, 1, DIFF_DV), zodd, zodd, zodd]
    aliases = {}
    if prev is not None:
        aliases = {len(args): 1, len(args) + 1: 2}
        in_specs += [pl.BlockSpec(memory_space=pl.ANY)] * 2
        args += list(prev)
    return pl.pallas_call(
        functools.partial(_diff_attn_kernel, False, lam_init),
        grid=(BATCH,),
        in_specs=in_specs,
        out_specs=[pl.BlockSpec((SEQ, 1024), lambda b: (b, 0)), cache_spec, cache_spec],
        out_shape=[jax.ShapeDtypeStruct((NTOK_P, 1024), BF16), cache_shape, cache_shape],
        input_output_aliases=aliases,
        compiler_params=_params(("arbitrary",)),
        name="diff_attn_prompt",
    )(*args)


def _diff_attn_latent(zodd, lam_p, norm_g, cos_t, sin_t, cache_k, cache_v, k_rot, v_aug, j, lam_init):
    nq = DEC_SEQ // ATT_TQ
    row0 = NTOK_P // ATT_TQ
    ctx_spec = pl.BlockSpec((None, None, DIFF_HEADS, PAST_LEN, LANES), lambda b, q: (b, j, 0, 0, 0))
    return pl.pallas_call(
        functools.partial(_diff_attn_kernel, True, lam_init),
        grid=(DEC_BATCH, nq),
        in_specs=[
            pl.BlockSpec((None, 4, DIFF_DH), lambda b, q: (j, 0, 0)),
            pl.BlockSpec((None, 1, DIFF_DV), lambda b, q: (j, 0, 0)),
            pl.BlockSpec((ATT_TQ, 1024), lambda b, q: (row0 + b * nq + q, 0)),
            pl.BlockSpec((ATT_TQ, LANES), lambda b, q: (q, 0)),
            pl.BlockSpec((ATT_TQ, LANES), lambda b, q: (q, 0)),
            ctx_spec,
            ctx_spec,
            pl.BlockSpec((DEC_SEQ, 1024), lambda b, q: (b, 0)),
            pl.BlockSpec((DEC_SEQ, 2048), lambda b, q: (b, 0)),
        ],
        out_specs=pl.BlockSpec((ATT_TQ, 1024), lambda b, q: (b * nq + q, 0)),
        out_shape=jax.ShapeDtypeStruct((NTOK_S, 1024), BF16),
        compiler_params=_params(("parallel", "arbitrary")),
        name="diff_attn_latent",
    )(lam_p, norm_g.reshape(-1, 1, DIFF_DV), zodd, cos_t, sin_t, cache_k, cache_v, k_rot, v_aug)


def _mla_weight(w_in):
    base = RET_COLS
    mq = w_in[:, base:base + MLA_HEADS * (MLA_NOPE + MLA_ROPE)].reshape(D_MODEL, MLA_HEADS, MLA_NOPE + MLA_ROPE)
    qn = mq[:, :, :MLA_NOPE].reshape(D_MODEL, MLA_HEADS * MLA_NOPE)
    qr = mq[:, :, MLA_NOPE:].reshape(D_MODEL, MLA_HEADS * MLA_ROPE)
    ckv0 = base + MLA_HEADS * (MLA_NOPE + MLA_ROPE)
    ckv = w_in[:, ckv0:ckv0 + MLA_KV_RANK]
    kr = w_in[:, ckv0 + MLA_KV_RANK:]
    return jnp.concatenate([qn, qr, ckv, jnp.tile(kr, (1, LANES // MLA_ROPE))], axis=1)


def kernel(x_prompt, x_sample, state_ret_fwd, state_ret_bwd, cache_mla_ckv, cache_mla_krope, cache_diff_k, cache_diff_v, c, c_ctx, ada_w, ada_b, ln1_g, ln1_b, ln2_g, ln2_b, ev_w_in, ev_w_out, ret_decay_fwd, ret_decay_bwd, mla_kv_norm_g, mla_w_uk, mla_w_uv, od_w_in, od_w_out, diff_lambda, diff_norm_g, moe_w_group, moe_b_group, moe_w_expert, moe_b_expert, moe_w1, moe_w3, moe_w2):
    x = (x_prompt.reshape(NTOK_P, D_MODEL), x_sample.reshape(NTOK_S, D_MODEL))
    cond = jnp.concatenate([c_ctx[None, :], c, jnp.zeros((N_COND - 1 - DEC_BATCH, D_MODEL), F32)], axis=0)
    mods = _ada_all(cond, ada_w, ada_b).reshape(DEPTH, N_COND, 6, D_MODEL)

    cos_m, sin_m = _rope_tables(MLA_ROPE)
    cos_d, sin_d = _rope_tables(DIFF_DH)
    ident = 512
    cos_m_id = jnp.concatenate([jnp.ones((ident, LANES), F32), cos_m], axis=0)
    sin_m_id = jnp.concatenate([jnp.zeros((ident, LANES), F32), sin_m], axis=0)
    decf = jnp.broadcast_to(ret_decay_fwd[:, :, None], ret_decay_fwd.shape + (LANES,))
    decb = jnp.broadcast_to(ret_decay_bwd[:, :, None], ret_decay_bwd.shape + (LANES,))

    pad = LANES - MOE_GROUPS - MOE_EXPERTS
    ret_states = mla_caches = diff_caches = None
    for i in range(DEPTH):
        j = i // 2
        mod = mods[i]
        if i % 2 == 0:
            zret = _mm_mod(x, mod, ev_w_in, (j,), RET_COLS, 1024, "in_proj_retention")
            zmla = _mm_mod(x, mod, _mla_weight(ev_w_in[j]), (), MLA_COLS, MLA_COLS, "in_proj_mla")
            *mla_caches, kcat, vm = _mla_kv_new(zmla, cos_m_id, sin_m_id, mla_kv_norm_g, mla_w_uk, mla_w_uv,
                                                j, mla_caches)
            kr_ctx = jnp.tile(cache_mla_krope[:, j], (1, 1, LANES // MLA_ROPE))
            kcat_ctx, vm_ctx = _mla_kv_ctx(cache_mla_ckv, kr_ctx, mla_w_uk, mla_w_uv, j)
            a_ret_p, *ret_states = _retention(zret, decf, decb, j, False, prev=ret_states)
            (a_ret_s,) = _retention(zret, decf, decb, j, True, state_ret_fwd, state_ret_bwd)
            a_mla_p = _mla_attn_prompt(zmla, kcat, vm)
            a_mla_s = _mla_attn_latent(zmla, cos_m, sin_m, kcat_ctx, vm_ctx, kcat, vm)
            x = _mm_ln([(a_ret_p, a_ret_s), (a_mla_p, a_mla_s)], ev_w_out, j, x, mod, ln1_g, ln1_b, i)
        else:
            lam_init = 0.8 - 0.6 * math.exp(-0.3 * i)
            zodd = _mm_mod(x, mod, od_w_in, (j,), 3072, 1024, "in_proj_diff")
            a_p, *diff_caches = _diff_attn_prompt(zodd, diff_lambda, diff_norm_g, j, lam_init, diff_caches)
            k_rot, v_aug = _diff_prep(zodd, cos_d, sin_d)
            a_s = _diff_attn_latent(zodd, diff_lambda, diff_norm_g, cos_d, sin_d, cache_diff_k, cache_diff_v,
                                    k_rot, v_aug, j, lam_init)
            x = _mm_ln([(a_p, a_s)], od_w_out, j, x, mod, ln1_g, ln1_b, i)
        w_router = jnp.concatenate([moe_w_group[i], moe_w_expert[i], jnp.zeros((D_MODEL, pad), F32)], axis=1)
        b_router = jnp.concatenate([moe_b_group[i], moe_b_expert[i], jnp.zeros((pad,), F32)])[None, :]
        x = _moe(x, mod, w_router, b_router, moe_w1, moe_w3, moe_w2, ln2_g, ln2_b, i)

    y_prompt = x[0].reshape(BATCH, SEQ, D_MODEL)
    y_sample = x[1].reshape(DEC_BATCH, DEC_SEQ, D_MODEL)
    return (y_prompt, y_sample, ret_states[0], ret_states[1], mla_caches[0], mla_caches[1],
            diff_caches[0], diff_caches[1])
```

```python
import functools
import math

import jax
import jax.numpy as jnp
from jax import lax
from jax.experimental import pallas as pl
from jax.experimental.pallas import tpu as pltpu

D_MODEL = 1024
BATCH = 32
SEQ = 256
DEPTH = 4
N_EVEN = 2
N_ODD = 2
DEC_BATCH = 2
DEC_SEQ = 2048
PAST_LEN = 256
GRID_W = 64
LN_EPS = 1e-5
RMS_EPS = 1e-6
DEEPNORM_ALPHA = (2.0 * DEPTH) ** 0.25
ROPE_BASE = 10000.0
RET_HEADS = 8
RET_DK = 64
RET_DV = 128
MLA_HEADS = 8
MLA_NOPE = 64
MLA_ROPE = 32
MLA_DV = 64
MLA_KV_RANK = 256
DIFF_HEADS = 8
DIFF_DH = 64
DIFF_DV = 128
MOE_GROUPS = 4
MOE_PER_GROUP = 4
MOE_EXPERTS = 16
MOE_FF = 256

NTOK_P = BATCH * SEQ
NTOK_S = DEC_BATCH * DEC_SEQ
NTOK = NTOK_P + NTOK_S
N_COND = 8
LANES = 128
RET_COLS = 3072
MLA_COLS = 1152
ATT_TQ = 256
RET_CHUNK = 256
MOE_TILE = 512
MOE_TILES = (NTOK + MOE_GROUPS * (MOE_TILE - 1)) // MOE_TILE
MOE_ROWS = MOE_TILES * MOE_TILE
VMEM_LIMIT = 56 * 1024 * 1024
LOG2E = 1.4426950408889634

F32 = jnp.float32
BF16 = jnp.bfloat16
NT_DIMS = (((1,), (1,)), ((), ()))
TN_DIMS = (((0,), (0,)), ((), ()))


def _params(sem):
    return pltpu.CompilerParams(dimension_semantics=sem, vmem_limit_bytes=VMEM_LIMIT)


def _group_of_tile(i, tm):
    npt = NTOK_P // tm
    nst = DEC_SEQ // tm
    return jnp.where(i < npt, 0, 1 + (i - npt) // nst)


def _split_specs(tm, width, m_of, s_row0=0):
    npt = NTOK_P // tm
    s_blk0 = s_row0 // tm
    return [pl.BlockSpec((tm, width), lambda *g: (jnp.minimum(m_of(*g), npt - 1), 0)),
            pl.BlockSpec((tm, width), lambda *g: (jnp.maximum(m_of(*g) - npt, 0) + s_blk0, 0))]


def _read_split(p_ref, s_ref, m):
    return jnp.where(m < NTOK_P // p_ref.shape[0], p_ref[...], s_ref[...])


def _silu(x):
    return x * (1.0 / (1.0 + jnp.exp(-x)))


def _layer_norm(r, g, b):
    mu = jnp.mean(r, axis=-1, keepdims=True)
    d = r - mu
    var = jnp.mean(d * d, axis=-1, keepdims=True)
    return d * lax.rsqrt(var + LN_EPS) * g + b


def _lane_iota(shape):
    return lax.broadcasted_iota(jnp.int32, shape, 1)


def _div_pow2(x, d):
    assert d & (d - 1) == 0
    return jnp.right_shift(x, d.bit_length() - 1)


def _mod_pow2(x, d):
    assert d & (d - 1) == 0
    return jnp.bitwise_and(x, d - 1)


def _ada_kernel(c_ref, w_ref, b_ref, o_ref):
    h = _silu(c_ref[...]).astype(BF16)
    o_ref[...] = jnp.dot(h, w_ref[...].astype(BF16), preferred_element_type=F32) + b_ref[...]


def _ada_all(cond, ada_w, ada_b):
    tn = 768
    return pl.pallas_call(
        _ada_kernel,
        grid=(DEPTH, 6 * D_MODEL // tn),
        in_specs=[
            pl.BlockSpec((N_COND, D_MODEL), lambda l, n: (0, 0)),
            pl.BlockSpec((None, D_MODEL, tn), lambda l, n: (l, 0, n)),
            pl.BlockSpec((None, 1, tn), lambda l, n: (l, 0, n)),
        ],
        out_specs=pl.BlockSpec((None, N_COND, tn), lambda l, n: (l, 0, n)),
        out_shape=jax.ShapeDtypeStruct((DEPTH, N_COND, 6 * D_MODEL), F32),
        compiler_params=_params(("parallel", "parallel")),
        name="ada_modulation",
    )(cond, ada_w, ada_b.reshape(DEPTH, 1, 6 * D_MODEL))


def _mm_mod_kernel(xp_ref, xs_ref, mod_ref, w_ref, o_ref, wscr):
    m = pl.program_id(1)

    @pl.when(m == 0)
    def _():
        wscr[...] = w_ref[...].astype(BF16)

    sh = mod_ref[0:1, :]
    sc = mod_ref[1:2, :]
    xm = (_read_split(xp_ref, xs_ref, m) * (1.0 + sc) + sh).astype(BF16)
    o_ref[...] = jnp.dot(xm, wscr[...], preferred_element_type=F32)


def _mm_mod(x, mod, w, w_index, n_cols, tn, name):
    tm = 1024
    w_block = (None,) * len(w_index) + (D_MODEL, tn)
    return pl.pallas_call(
        _mm_mod_kernel,
        grid=(n_cols // tn, NTOK // tm),
        in_specs=_split_specs(tm, D_MODEL, lambda n, m: m, x[2]) + [
            pl.BlockSpec((None, 6, D_MODEL), lambda n, m: (_group_of_tile(m, tm), 0, 0)),
            pl.BlockSpec(w_block, lambda n, m: tuple(w_index) + (0, n)),
        ],
        out_specs=pl.BlockSpec((tm, tn), lambda n, m: (m, n)),
        out_shape=jax.ShapeDtypeStruct((NTOK, n_cols), F32),
        scratch_shapes=[pltpu.VMEM((D_MODEL, tn), BF16)],
        compiler_params=_params(("arbitrary", "arbitrary")),
        name=name,
    )(x[0], x[1], mod, w)


def _router_probs(xm, wr_ref, br_ref):
    rows = xm.shape[0]
    z = jnp.dot(xm, wr_ref[...].astype(BF16), preferred_element_type=F32) + br_ref[...]
    lane_i = _lane_iota((rows, LANES))
    lane = lane_i.astype(F32)
    gmask = lane_i < MOE_GROUPS
    zg = jnp.where(gmask, z, -jnp.inf)
    pg = jnp.exp(zg - jnp.max(zg, axis=-1, keepdims=True))
    g_prob = pg / jnp.sum(pg, axis=-1, keepdims=True)
    g_p = jnp.max(g_prob, axis=-1, keepdims=True)
    g_idx = jnp.min(jnp.where(gmask & (g_prob == g_p), lane, float(LANES)), axis=-1, keepdims=True)
    return z, lane_i, lane, g_p, g_idx


def _mm_ln_kernel(k_sizes, *refs):
    n_a = len(k_sizes)
    a_refs = refs[:2 * n_a]
    (w_ref, xp_ref, xs_ref, mod_ref, g_ref, b_ref, wr_ref, br_ref,
     o_ref, meta_ref, cnt_ref, wscr, tri_scr, carry_scr) = refs[2 * n_a:]
    m = pl.program_id(0)
    tm = o_ref.shape[0]

    @pl.when(m == 0)
    def _():
        wscr[...] = w_ref[...].astype(BF16)
        ri = lax.broadcasted_iota(jnp.int32, (tm, tm), 0)
        ci = lax.broadcasted_iota(jnp.int32, (tm, tm), 1)
        tri_scr[...] = jnp.where(ci < ri, 1.0, 0.0).astype(BF16)
        carry_scr[...] = jnp.zeros_like(carry_scr)

    y = None
    k0 = 0
    for i, ks in enumerate(k_sizes):
        a = _read_split(a_refs[2 * i], a_refs[2 * i + 1], m)
        part = jnp.dot(a, wscr[k0:k0 + ks, :], preferred_element_type=F32)
        y = part if y is None else y + part
        k0 += ks
    gate = mod_ref[2:3, :]
    r = DEEPNORM_ALPHA * _read_split(xp_ref, xs_ref, m) + gate * y
    x1 = _layer_norm(r, g_ref[...], b_ref[...])
    o_ref[...] = x1

    xm = (x1 * (1.0 + mod_ref[4:5, :]) + mod_ref[3:4, :]).astype(BF16)
    _, lane_i, lane, _, g_idx = _router_probs(xm, wr_ref, br_ref)
    onehot = jnp.where(lane == g_idx, 1.0, 0.0)
    before = jnp.dot(tri_scr[...], onehot.astype(BF16), preferred_element_type=F32) + carry_scr[0:1, :]
    rank = jnp.sum(jnp.where(lane == g_idx, before, 0.0), axis=-1, keepdims=True)
    meta_ref[...] = jnp.where(lane_i == 0, g_idx, jnp.where(lane_i == 1, rank, 0.0))
    total = carry_scr[0:1, :] + jnp.sum(onehot, axis=0, keepdims=True)
    carry_scr[...] = jnp.broadcast_to(total, carry_scr.shape)
    cnt_ref[...] = jnp.broadcast_to(total, cnt_ref.shape)


def _mm_ln(a_pairs, w, j, x, mod, ln_g, ln_b, w_router, b_router, layer):
    tm = 512
    k_sizes = tuple(ap.shape[1] for ap, _ in a_pairs)
    k_tot = sum(k_sizes)
    in_specs = []
    args = []
    for (ap, a_s), ks in zip(a_pairs, k_sizes):
        in_specs += _split_specs(tm, ks, lambda m: m)
        args += [ap, a_s]
    in_specs += [pl.BlockSpec((None, k_tot, D_MODEL), lambda m: (j, 0, 0))]
    in_specs += _split_specs(tm, D_MODEL, lambda m: m, x[2])
    in_specs += [
        pl.BlockSpec((None, 6, D_MODEL), lambda m: (_group_of_tile(m, tm), 0, 0)),
        pl.BlockSpec((None, 1, D_MODEL), lambda m: (layer, 0, 0)),
        pl.BlockSpec((None, 1, D_MODEL), lambda m: (layer, 0, 0)),
        pl.BlockSpec((D_MODEL, LANES), lambda m: (0, 0)),
        pl.BlockSpec((1, LANES), lambda m: (0, 0)),
    ]
    return pl.pallas_call(
        functools.partial(_mm_ln_kernel, k_sizes),
        grid=(NTOK // tm,),
        in_specs=in_specs,
        out_specs=[
            pl.BlockSpec((tm, D_MODEL), lambda m: (m, 0)),
            pl.BlockSpec((tm, LANES), lambda m: (m, 0)),
            pl.BlockSpec((N_COND, LANES), lambda m: (0, 0)),
        ],
        out_shape=[
            jax.ShapeDtypeStruct((NTOK, D_MODEL), F32),
            jax.ShapeDtypeStruct((NTOK, LANES), F32),
            jax.ShapeDtypeStruct((N_COND, LANES), F32),
        ],
        scratch_shapes=[
            pltpu.VMEM((k_tot, D_MODEL), BF16),
            pltpu.VMEM((tm, tm), BF16),
            pltpu.VMEM((N_COND, LANES), F32),
        ],
        compiler_params=_params(("arbitrary",)),
        name="out_proj_ln",
    )(*args, w, x[0], x[1], mod, ln_g.reshape(DEPTH, 1, D_MODEL), ln_b.reshape(DEPTH, 1, D_MODEL),
      w_router, b_router)


def _moe_kernel(tgrp_ref, ntile_ref, src0_ref, src1_ref, dst_ref, mid_ref, x_hbm, mod_ref, wr_ref, br_ref,
                w1_ref, w3_ref, w2_ref, g_ref, b_ref, y_hbm,
                gbuf, obuf, gsem, ssem, w1s, w3s, w2s):
    i = pl.program_id(0)
    n_steps = pl.num_programs(0)
    n_tiles = ntile_ref[0]
    ts = gbuf.shape[1]
    slot = lax.rem(i, 2)

    def gather_copy(row_src, row, s):
        return pltpu.make_async_copy(x_hbm.at[pl.ds(row_src, 1), :], gbuf.at[s, pl.ds(row, 1), :], gsem.at[s])

    def scatter_copy(row, row_dst, s):
        return pltpu.make_async_copy(obuf.at[s, pl.ds(row, 1), :], y_hbm.at[pl.ds(row_dst, 1), :], ssem.at[s])

    def start_gather(src_ref, s):
        def body(r, carry):
            gather_copy(src_ref[0, r], r, s).start()
            return carry
        lax.fori_loop(0, ts, body, 0, unroll=8)

    def wait_gather(s):
        pltpu.make_async_copy(x_hbm.at[pl.ds(0, ts), :], gbuf.at[s], gsem.at[s]).wait()

    def start_scatter(s):
        def body(r, carry):
            scatter_copy(r, dst_ref[0, r], s).start()
            return carry
        lax.fori_loop(0, ts, body, 0, unroll=8)

    def wait_scatter(s):
        pltpu.make_async_copy(obuf.at[s], y_hbm.at[pl.ds(0, ts), :], ssem.at[s]).wait()

    @pl.when(i == 0)
    def _():
        start_gather(src0_ref, 0)

    @pl.when(i < n_tiles)
    def _():
        grp = tgrp_ref[i]
        wait_gather(slot)

        @pl.when(i + 1 < n_tiles)
        def _():
            start_gather(src1_ref, 1 - slot)

        @pl.when((i == 0) | (grp != tgrp_ref[jnp.maximum(i - 1, 0)]))
        def _():
            w1s[...] = w1_ref[...].astype(BF16)
            w3s[...] = w3_ref[...].astype(BF16)
            w2s[...] = w2_ref[...].astype(BF16)

        mid = mid_ref[...]

        def mod_row(k):
            return jnp.where(mid == 0, mod_ref[0, k:k + 1, :],
                             jnp.where(mid == 1, mod_ref[1, k:k + 1, :], mod_ref[2, k:k + 1, :]))

        x1 = gbuf[slot]
        xm = (x1 * (1.0 + mod_row(4)) + mod_row(3)).astype(BF16)
        z, lane_i, lane, g_p, _ = _router_probs(xm, wr_ref, br_ref)
        e0 = MOE_GROUPS + MOE_PER_GROUP * grp
        emask = (lane_i >= e0) & (lane_i < e0 + MOE_PER_GROUP)
        ze = jnp.where(emask, z, -jnp.inf)
        pe = jnp.exp(ze - jnp.max(ze, axis=-1, keepdims=True))
        e_prob = pe / jnp.sum(pe, axis=-1, keepdims=True)
        cand = jnp.where(emask, e_prob, -1.0)
        p1 = jnp.max(cand, axis=-1, keepdims=True)
        i1 = jnp.min(jnp.where(cand == p1, lane, float(LANES)), axis=-1, keepdims=True)
        cand2 = jnp.where(lane == i1, -1.0, cand)
        p2 = jnp.max(cand2, axis=-1, keepdims=True)
        i2 = jnp.min(jnp.where(cand2 == p2, lane, float(LANES)), axis=-1, keepdims=True)
        denom = p1 + p2
        comb = jnp.where(lane == i1, g_p * p1 / denom, 0.0) + jnp.where(lane == i2, g_p * p2 / denom, 0.0)
        y = None
        for e in range(MOE_PER_GROUP):
            c = jnp.sum(jnp.where(lane_i == e0 + e, comb, 0.0), axis=-1, keepdims=True)
            h1 = jnp.dot(xm, w1s[e], preferred_element_type=F32)
            h3 = jnp.dot(xm, w3s[e], preferred_element_type=F32)
            part = jnp.dot((_silu(h1) * h3 * c).astype(BF16), w2s[e], preferred_element_type=F32)
            y = part if y is None else y + part
        r = DEEPNORM_ALPHA * x1 + mod_row(5) * y

        @pl.when(i >= 2)
        def _():
            wait_scatter(slot)

        obuf[slot] = _layer_norm(r, g_ref[...], b_ref[...])
        start_scatter(slot)

    @pl.when(i == n_steps - 1)
    def _():
        @pl.when(n_tiles >= 2)
        def _():
            wait_scatter(lax.rem(n_tiles, 2))

        @pl.when(n_tiles >= 1)
        def _():
            wait_scatter(lax.rem(n_tiles + 1, 2))


def _moe_plan(meta, counts):
    ts = MOE_TILE
    cnt = counts[0, :MOE_GROUPS].astype(jnp.int32)
    tiles_g = (cnt + ts - 1) // ts
    tile_end = jnp.cumsum(tiles_g)
    row0_g = (tile_end - tiles_g) * ts
    gid = meta[:, 0].astype(jnp.int32)
    rank = meta[:, 1].astype(jnp.int32)
    pos = row0_g[gid] + rank
    tok1 = jnp.zeros((MOE_ROWS,), jnp.int32).at[pos].set(jnp.arange(1, NTOK + 1, dtype=jnp.int32))
    rows = jnp.arange(MOE_ROWS, dtype=jnp.int32)
    src = jnp.maximum(tok1 - 1, 0)
    spare = NTOK + ((rows // ts) % 2) * ts + rows % ts
    dst = jnp.where(tok1 > 0, src, spare)
    mid = jnp.where(src < NTOK_P, 0, 1 + (src - NTOK_P) // DEC_SEQ)
    tile_group = jnp.minimum(jnp.sum(jnp.arange(MOE_TILES)[:, None] >= tile_end[None, :], axis=1),
                             MOE_GROUPS - 1).astype(jnp.int32)
    n_tiles = tile_end[-1:].astype(jnp.int32)
    return (tile_group, n_tiles, src.reshape(MOE_TILES, 1, ts), dst.reshape(MOE_TILES, 1, ts),
            mid.reshape(MOE_ROWS, 1))


def _moe(x1, plan, mods, w_router, b_router, w1, w3, w2, ln_g, ln_b, layer):
    ts = MOE_TILE
    tile_group, n_tiles, src, dst, mid = plan
    grp_shape = (DEPTH, MOE_GROUPS, MOE_PER_GROUP)
    w_in_spec = pl.BlockSpec((None, None, MOE_PER_GROUP, D_MODEL, MOE_FF), lambda i, tg, nt: (layer, tg[i], 0, 0, 0))
    w_out_spec = pl.BlockSpec((None, None, MOE_PER_GROUP, MOE_FF, D_MODEL), lambda i, tg, nt: (layer, tg[i], 0, 0, 0))
    smem_tile = functools.partial(pl.BlockSpec, (None, 1, ts), memory_space=pltpu.SMEM)
    grid_spec = pltpu.PrefetchScalarGridSpec(
        num_scalar_prefetch=2,
        grid=(MOE_TILES,),
        in_specs=[
            smem_tile(lambda i, tg, nt: (i, 0, 0)),
            smem_tile(lambda i, tg, nt: (jnp.minimum(i + 1, MOE_TILES - 1), 0, 0)),
            smem_tile(lambda i, tg, nt: (i, 0, 0)),
            pl.BlockSpec((ts, 1), lambda i, tg, nt: (i, 0)),
            pl.BlockSpec(memory_space=pl.ANY),
            pl.BlockSpec((None, N_COND, 6, D_MODEL), lambda i, tg, nt: (layer, 0, 0, 0)),
            pl.BlockSpec((D_MODEL, LANES), lambda i, tg, nt: (0, 0)),
            pl.BlockSpec((1, LANES), lambda i, tg, nt: (0, 0)),
            w_in_spec,
            w_in_spec,
            w_out_spec,
            pl.BlockSpec((None, 1, D_MODEL), lambda i, tg, nt: (layer, 0, 0)),
            pl.BlockSpec((None, 1, D_MODEL), lambda i, tg, nt: (layer, 0, 0)),
        ],
        out_specs=pl.BlockSpec(memory_space=pl.ANY),
        scratch_shapes=[
            pltpu.VMEM((2, ts, D_MODEL), F32),
            pltpu.VMEM((2, ts, D_MODEL), F32),
            pltpu.SemaphoreType.DMA((2,)),
            pltpu.SemaphoreType.DMA((2,)),
            pltpu.VMEM((MOE_PER_GROUP, D_MODEL, MOE_FF), BF16),
            pltpu.VMEM((MOE_PER_GROUP, D_MODEL, MOE_FF), BF16),
            pltpu.VMEM((MOE_PER_GROUP, MOE_FF, D_MODEL), BF16),
        ],
    )
    return pl.pallas_call(
        _moe_kernel,
        grid_spec=grid_spec,
        out_shape=jax.ShapeDtypeStruct((NTOK + 2 * ts, D_MODEL), F32),
        compiler_params=_params(("arbitrary",)),
        name="hier_moe_ln",
    )(tile_group, n_tiles, src, src, dst, mid, x1, mods, w_router, b_router,
      w1.reshape(grp_shape + (D_MODEL, MOE_FF)), w3.reshape(grp_shape + (D_MODEL, MOE_FF)),
      w2.reshape(grp_shape + (MOE_FF, D_MODEL)),
      ln_g.reshape(DEPTH, 1, D_MODEL), ln_b.reshape(DEPTH, 1, D_MODEL))


def _swap_halves(x, lane, half):
    return jnp.where(_mod_pow2(lane, 2 * half) < half,
                     pltpu.roll(x, LANES - half, 1), pltpu.roll(x, half, 1))


def _rope128(x, cos, sin_signed, lane, half):
    return x * cos + _swap_halves(x, lane, half) * sin_signed


def _rope_tables(rot_dim):
    rows = DEC_SEQ // GRID_W
    row = jnp.repeat(jnp.arange(rows, dtype=F32), GRID_W)
    col = jnp.tile(jnp.arange(GRID_W, dtype=F32), rows)
    n_freq = rot_dim // 4
    inv_freq = ROPE_BASE ** (-jnp.arange(n_freq, dtype=F32) / n_freq)
    ang = jnp.concatenate([row[:, None] * inv_freq, col[:, None] * inv_freq], axis=-1)
    cos, sin = jnp.cos(ang), jnp.sin(ang)
    reps = LANES // rot_dim
    cos_full = jnp.tile(jnp.concatenate([cos, cos], axis=-1), (1, reps))
    sin_signed = jnp.tile(jnp.concatenate([-sin, sin], axis=-1), (1, reps))
    return cos_full, sin_signed


def _exp_parts(s_list, scale):
    m = None
    for s in s_list:
        sm = jnp.max(s, axis=-1, keepdims=True)
        m = sm if m is None else jnp.maximum(m, sm)
    return [jnp.exp2((s - m) * (scale * LOG2E)).astype(BF16) for s in s_list]


def _pv_normalised(p_list, v_list):
    o = None
    for p, v in zip(p_list, v_list):
        part = jnp.dot(p, v, preferred_element_type=F32)
        o = part if o is None else o + part
    return o[:, :LANES] / o[:, LANES:]


def _mla_kv_kernel(new_tokens, *refs):
    if new_tokens:
        (ckv_ref, kr_ref, cos_ref, sin_ref, g_ref, wuk_ref, wuv_ref) = refs[:7]
        ckvc_ref, krc_ref, kcat_ref, vm_ref = refs[-4:]
        i = pl.program_id(0)
        x = ckv_ref[...]
        c = x * lax.rsqrt(jnp.mean(x * x, axis=-1, keepdims=True) + RMS_EPS) * g_ref[...]
        kr_raw = kr_ref[...]

        @pl.when(i < NTOK_P // ckv_ref.shape[0])
        def _():
            for b in range(ckvc_ref.shape[0]):
                ckvc_ref[b] = c[SEQ * b:SEQ * (b + 1), :]
                krc_ref[b] = kr_raw[SEQ * b:SEQ * (b + 1), :MLA_ROPE]

        lane = _lane_iota(kr_ref.shape)
        kr = _rope128(kr_raw, cos_ref[...], sin_ref[...], lane, MLA_ROPE // 2)
    else:
        ckv_ref, kr_ref, wuk_ref, wuv_ref, kcat_ref, vm_ref = refs
        c = ckv_ref[...]
        kr = kr_ref[...]
    cb = c.astype(BF16)
    kn = jnp.dot(cb, wuk_ref[...].astype(BF16), preferred_element_type=F32).astype(BF16)
    vv = jnp.dot(cb, wuv_ref[...].astype(BF16), preferred_element_type=F32).astype(BF16)
    krb = kr.astype(BF16)
    ones = jnp.ones((c.shape[0], LANES), BF16)
    for p in range(MLA_HEADS // 2):
        kcat_ref[:, 256 * p:256 * p + LANES] = kn[:, LANES * p:LANES * (p + 1)]
        kcat_ref[:, 256 * p + LANES:256 * (p + 1)] = krb
        vm_ref[:, 256 * p:256 * p + LANES] = vv[:, LANES * p:LANES * (p + 1)]
        vm_ref[:, 256 * p + LANES:256 * (p + 1)] = ones


def _mla_kv_new(zmla, cos_t, sin_t, kv_norm_g, w_uk, w_uv, j, prev):
    tm = 512
    npt = NTOK_P // tm
    nst = DEC_SEQ // tm
    nb = tm // SEQ

    def tab(i):
        return (jnp.where(i < npt, 0, 1 + (i - npt) % nst), 0)

    def cache_idx(i):
        return (jnp.minimum(i, npt - 1), j, 0, 0)

    in_specs = [
        pl.BlockSpec((tm, MLA_KV_RANK), lambda i: (i, 768 // MLA_KV_RANK)),
        pl.BlockSpec((tm, LANES), lambda i: (i, 1024 // LANES)),
        pl.BlockSpec((tm, LANES), tab),
        pl.BlockSpec((tm, LANES), tab),
        pl.BlockSpec((None, 1, MLA_KV_RANK), lambda i: (j, 0, 0)),
        pl.BlockSpec((None, MLA_KV_RANK, 512), lambda i: (j, 0, 0)),
        pl.BlockSpec((None, MLA_KV_RANK, 512), lambda i: (j, 0, 0)),
    ]
    args = [zmla, zmla, cos_t, sin_t, kv_norm_g.reshape(-1, 1, MLA_KV_RANK), w_uk, w_uv]
    aliases = {}
    if prev is not None:
        aliases = {len(args): 0, len(args) + 1: 1}
        in_specs += [pl.BlockSpec(memory_space=pl.ANY)] * 2
        args += list(prev)
    return pl.pallas_call(
        functools.partial(_mla_kv_kernel, True),
        grid=(NTOK // tm,),
        in_specs=in_specs,
        out_specs=[
            pl.BlockSpec((nb, None, SEQ, MLA_KV_RANK), cache_idx),
            pl.BlockSpec((nb, None, SEQ, MLA_ROPE), cache_idx),
            pl.BlockSpec((tm, 1024), lambda i: (i, 0)),
            pl.BlockSpec((tm, 1024), lambda i: (i, 0)),
        ],
        out_shape=[
            jax.ShapeDtypeStruct((BATCH, N_EVEN, SEQ, MLA_KV_RANK), F32),
            jax.ShapeDtypeStruct((BATCH, N_EVEN, SEQ, MLA_ROPE), F32),
            jax.ShapeDtypeStruct((NTOK, 1024), BF16),
            jax.ShapeDtypeStruct((NTOK, 1024), BF16),
        ],
        input_output_aliases=aliases,
        compiler_params=_params(("arbitrary",)),
        name="mla_kv_new",
    )(*args)


def _mla_kv_ctx(cache_ckv, kr_tiled, w_uk, w_uv, j):
    return pl.pallas_call(
        functools.partial(_mla_kv_kernel, False),
        grid=(DEC_BATCH,),
        in_specs=[
            pl.BlockSpec((None, None, PAST_LEN, MLA_KV_RANK), lambda b: (b, j, 0, 0)),
            pl.BlockSpec((None, PAST_LEN, LANES), lambda b: (b, 0, 0)),
            pl.BlockSpec((None, MLA_KV_RANK, 512), lambda b: (j, 0, 0)),
            pl.BlockSpec((None, MLA_KV_RANK, 512), lambda b: (j, 0, 0)),
        ],
        out_specs=[
            pl.BlockSpec((PAST_LEN, 1024), lambda b: (b, 0)),
            pl.BlockSpec((PAST_LEN, 1024), lambda b: (b, 0)),
        ],
        out_shape=[
            jax.ShapeDtypeStruct((DEC_BATCH * PAST_LEN, 1024), BF16),
            jax.ShapeDtypeStruct((DEC_BATCH * PAST_LEN, 1024), BF16),
        ],
        compiler_params=_params(("parallel",)),
        name="mla_kv_ctx",
    )(cache_ckv, kr_tiled, w_uk, w_uv)


def _mla_attn_kernel(latent, *refs):
    if latent:
        qn_ref, qr_ref, cos_ref, sin_ref, kc_ref, vc_ref, kn_ref, vn_ref, o_ref = refs
        k_refs, v_refs = (kc_ref, kn_ref), (vc_ref, vn_ref)
    else:
        qn_ref, qr_ref, kn_ref, vn_ref, o_ref = refs
        k_refs, v_refs = (kn_ref,), (vn_ref,)
    tq = qn_ref.shape[0]
    lane = _lane_iota((tq, LANES))
    scale = (MLA_NOPE + MLA_ROPE) ** -0.5
    qr_cols = []
    for cidx in range(2):
        x = qr_ref[:, LANES * cidx:LANES * (cidx + 1)]
        if latent:
            x = _rope128(x, cos_ref[...], sin_ref[...], lane, MLA_ROPE // 2)
        qr_cols.append(x)
    o_prev = None
    for h in range(MLA_HEADS):
        p, half = divmod(h, 2)
        cidx, slot = divmod(h, 4)
        qa = jnp.where(_div_pow2(lane, MLA_NOPE) == half, qn_ref[:, LANES * p:LANES * (p + 1)], 0.0)
        qb = jnp.where(_div_pow2(lane, MLA_ROPE) == slot, qr_cols[cidx], 0.0)
        qcat = jnp.concatenate([qa, qb], axis=1).astype(BF16)
        pair_cols = slice(256 * p, 256 * (p + 1))
        s_list = [lax.dot_general(qcat, k_ref[:, pair_cols], NT_DIMS, preferred_element_type=F32)
                  for k_ref in k_refs]
        o = _pv_normalised(_exp_parts(s_list, scale), [v_ref[:, pair_cols] for v_ref in v_refs])
        if half == 0:
            o_prev = o
        else:
            o_ref[:, LANES * p:LANES * (p + 1)] = jnp.where(lane < MLA_DV, o_prev, o).astype(BF16)


def _mla_attn_prompt(zmla, kcat, vm):
    return pl.pallas_call(
        functools.partial(_mla_attn_kernel, False),
        grid=(BATCH,),
        in_specs=[
            pl.BlockSpec((SEQ, 512), lambda b: (b, 0)),
            pl.BlockSpec((SEQ, 256), lambda b: (b, 2)),
            pl.BlockSpec((SEQ, 1024), lambda b: (b, 0)),
            pl.BlockSpec((SEQ, 1024), lambda b: (b, 0)),
        ],
        out_specs=pl.BlockSpec((SEQ, 512), lambda b: (b, 0)),
        out_shape=jax.ShapeDtypeStruct((NTOK_P, 512), BF16),
        compiler_params=_params(("parallel",)),
        name="mla_attn_prompt",
    )(zmla, zmla, kcat, vm)


def _mla_attn_latent(zmla, cos_t, sin_t, kcat_ctx, vm_ctx, kcat, vm):
    nq = DEC_SEQ // ATT_TQ
    row0 = NTOK_P // ATT_TQ
    seq0 = NTOK_P // DEC_SEQ
    return pl.pallas_call(
        functools.partial(_mla_attn_kernel, True),
        grid=(DEC_BATCH, nq),
        in_specs=[
            pl.BlockSpec((ATT_TQ, 512), lambda b, q: (row0 + b * nq + q, 0)),
            pl.BlockSpec((ATT_TQ, 256), lambda b, q: (row0 + b * nq + q, 2)),
            pl.BlockSpec((ATT_TQ, LANES), lambda b, q: (q, 0)),
            pl.BlockSpec((ATT_TQ, LANES), lambda b, q: (q, 0)),
            pl.BlockSpec((PAST_LEN, 1024), lambda b, q: (b, 0)),
            pl.BlockSpec((PAST_LEN, 1024), lambda b, q: (b, 0)),
            pl.BlockSpec((DEC_SEQ, 1024), lambda b, q: (seq0 + b, 0)),
            pl.BlockSpec((DEC_SEQ, 1024), lambda b, q: (seq0 + b, 0)),
        ],
        out_specs=pl.BlockSpec((ATT_TQ, 512), lambda b, q: (b * nq + q, 0)),
        out_shape=jax.ShapeDtypeStruct((NTOK_S, 512), BF16),
        compiler_params=_params(("parallel", "arbitrary")),
        name="mla_attn_latent",
    )(zmla, zmla, cos_t, sin_t, kcat_ctx, vm_ctx, kcat, vm)


def _log_sigmoid(x):
    return jnp.minimum(x, 0.0) - jnp.log1p(jnp.exp(-jnp.abs(x)))


def _retention_kernel(seq_len, has_init, emit_state, has_prev, *refs):
    refs = list(refs)
    decf_ref, decb_ref, q_ref, k_ref, v_ref, g_ref = refs[:6]
    refs = refs[6:]
    if has_init:
        sf0_ref, sb0_ref = refs[:2]
        refs = refs[2:]
    if has_prev:
        refs = refs[2:]
    o_ref = refs[0]
    refs = refs[1:]
    if emit_state:
        sf_ref, sb_ref = refs[:2]
        refs = refs[2:]
    oacc, dec_scr, wts_scr = refs

    c = RET_CHUNK
    n_chunks = seq_len // c
    pair = pl.program_id(0)
    lane = _lane_iota((c, LANES))
    zeros_half = jnp.zeros((RET_DK, RET_DV), F32)

    def log_gammas(half):
        head = 2 * pair + half
        return (_log_sigmoid(decf_ref[pl.ds(head, 1), :]),
                _log_sigmoid(decb_ref[pl.ds(head, 1), :]))

    @pl.when(pl.program_id(1) == 0)
    def _():
        ri = lax.broadcasted_iota(jnp.int32, (c, c), 0)
        ci = lax.broadcasted_iota(jnp.int32, (c, c), 1)
        rel = (ri - ci).astype(F32)
        row = lax.broadcasted_iota(jnp.int32, (c, LANES), 0).astype(F32)
        for half in range(2):
            lgf, lgb = log_gammas(half)
            dec_scr[half] = (jnp.where(rel >= 0, jnp.exp(lgf[:, 0:1] * jnp.maximum(rel, 0.0)), 0.0)
                             + jnp.where(rel <= 0, jnp.exp(lgb[:, 0:1] * jnp.maximum(-rel, 0.0)), 0.0))
            wts_scr[half, 0] = jnp.exp(lgf * (row + 1.0))
            wts_scr[half, 1] = jnp.exp(lgf * (c - 1.0 - row))
            wts_scr[half, 2] = jnp.exp(lgb * (c - row))
            wts_scr[half, 3] = jnp.exp(lgb * row)

    for half in range(2):
        lgf, lgb = log_gammas(half)
        cd_f = jnp.exp(lgf * float(c))
        cd_b = jnp.exp(lgb * float(c))
        hmask = _div_pow2(lane, RET_DK) == half
        vsl = slice(RET_DV * half, RET_DV * (half + 1))

        def load(n):
            rows = pl.ds(n * c if isinstance(n, int) else pl.multiple_of(n * c, c), c)
            qm = jnp.where(hmask, q_ref[rows, :], 0.0)
            kk = k_ref[rows, :] * (RET_DK ** -0.5)
            vb = v_ref[rows, vsl].astype(BF16)
            return rows, qm, kk, vb

        def init_state(s0_ref):
            if not has_init:
                return jnp.zeros((LANES, RET_DV), F32)
            s0 = s0_ref[half]
            return jnp.concatenate([s0, zeros_half] if half == 0 else [zeros_half, s0], axis=0)

        def fwd_step(n, s_f):
            rows, qm, kk, vb = load(n)
            s = lax.dot_general(qm.astype(BF16), kk.astype(BF16), NT_DIMS, preferred_element_type=F32)
            o = jnp.dot((s * dec_scr[half]).astype(BF16), vb, preferred_element_type=F32)
            if has_init or n_chunks > 1:
                o = o + jnp.dot((qm * wts_scr[half, 0]).astype(BF16), s_f.astype(BF16),
                                preferred_element_type=F32)
            oacc[rows, :] = o
            kv = lax.dot_general((kk * wts_scr[half, 1]).astype(BF16), vb, TN_DIMS,
                                 preferred_element_type=F32)
            return cd_f * s_f + kv

        def bwd_step(t, s_b):
            n = n_chunks - 1 - t
            rows, qm, kk, vb = load(n)
            o = oacc[rows, :]
            if has_init or n_chunks > 1:
                o = o + jnp.dot((qm * wts_scr[half, 2]).astype(BF16), s_b.astype(BF16),
                                preferred_element_type=F32)
            mu = jnp.mean(o, axis=-1, keepdims=True)
            d = o - mu
            var = jnp.mean(d * d, axis=-1, keepdims=True)
            o_ref[rows, vsl] = (_silu(g_ref[rows, vsl]) * (d * lax.rsqrt(var + LN_EPS))).astype(BF16)
            kv = lax.dot_general((kk * wts_scr[half, 3]).astype(BF16), vb, TN_DIMS,
                                 preferred_element_type=F32)
            return cd_b * s_b + kv

        s_f = init_state(sf0_ref if has_init else None)
        s_b = init_state(sb0_ref if has_init else None)
        if n_chunks == 1:
            s_f = fwd_step(0, s_f)
            s_b = bwd_step(0, s_b)
        else:
            s_f = lax.fori_loop(0, n_chunks, fwd_step, s_f)
            s_b = lax.fori_loop(0, n_chunks, bwd_step, s_b)
        if emit_state:
            sf_ref[half] = s_f[RET_DK * half:RET_DK * (half + 1), :]
            sb_ref[half] = s_b[RET_DK * half:RET_DK * (half + 1), :]


def _retention(zret, decf, decb, j, latent, state_f=None, state_b=None, prev=None):
    seq_len = DEC_SEQ if latent else SEQ
    n_b = DEC_BATCH if latent else BATCH
    row0 = NTOK_P // DEC_SEQ if latent else 0
    n_pairs = RET_HEADS // 2
    in_specs = [
        pl.BlockSpec((None, RET_HEADS, LANES), lambda p, b: (j, 0, 0)),
        pl.BlockSpec((None, RET_HEADS, LANES), lambda p, b: (j, 0, 0)),
        pl.BlockSpec((seq_len, LANES), lambda p, b: (row0 + b, p)),
        pl.BlockSpec((seq_len, LANES), lambda p, b: (row0 + b, 4 + p)),
        pl.BlockSpec((seq_len, 256), lambda p, b: (row0 + b, 4 + p)),
        pl.BlockSpec((seq_len, 256), lambda p, b: (row0 + b, 8 + p)),
    ]
    args = [decf, decb, zret, zret, zret, zret]
    out_specs = [pl.BlockSpec((seq_len, 256), lambda p, b: (b, p))]
    out_shape = [jax.ShapeDtypeStruct((n_b * seq_len, RET_HEADS * RET_DV), BF16)]
    st_spec = pl.BlockSpec((None, None, 2, RET_DK, RET_DV), lambda p, b: (b, j, p, 0, 0))
    aliases = {}
    if latent:
        in_specs += [st_spec, st_spec]
        args += [state_f, state_b]
    else:
        out_specs += [st_spec, st_spec]
        out_shape += [jax.ShapeDtypeStruct((BATCH, N_EVEN, RET_HEADS, RET_DK, RET_DV), F32)] * 2
        if prev is not None:
            aliases = {len(args): 1, len(args) + 1: 2}
            in_specs += [pl.BlockSpec(memory_space=pl.ANY)] * 2
            args += list(prev)
    return pl.pallas_call(
        functools.partial(_retention_kernel, seq_len, latent, not latent, bool(aliases)),
        grid=(n_pairs, n_b),
        in_specs=in_specs,
        out_specs=out_specs,
        out_shape=out_shape,
        scratch_shapes=[
            pltpu.VMEM((seq_len, RET_DV), F32),
            pltpu.VMEM((2, RET_CHUNK, RET_CHUNK), F32),
            pltpu.VMEM((2, 4, RET_CHUNK, LANES), F32),
        ],
        input_output_aliases=aliases,
        compiler_params=_params(("arbitrary", "arbitrary")),
        name="retention_latent" if latent else "retention_prompt",
    )(*args)


def _diff_prep_kernel(k_ref, v_ref, cos_ref, sin_ref, kr_ref, va_ref):
    lane = _lane_iota(cos_ref.shape)
    cos = cos_ref[...]
    sin = sin_ref[...]
    ones = jnp.ones(cos_ref.shape, BF16)
    for h in range(DIFF_HEADS):
        sl = slice(LANES * h, LANES * (h + 1))
        kr_ref[:, sl] = _rope128(k_ref[:, sl], cos, sin, lane, DIFF_DH // 2).astype(BF16)
        va_ref[:, 256 * h:256 * h + LANES] = v_ref[:, sl].astype(BF16)
        va_ref[:, 256 * h + LANES:256 * (h + 1)] = ones


def _diff_prep(zodd, cos_t, sin_t):
    tm = 512
    row0 = NTOK_P // tm
    nst = DEC_SEQ // tm
    return pl.pallas_call(
        _diff_prep_kernel,
        grid=(NTOK_S // tm,),
        in_specs=[
            pl.BlockSpec((tm, 1024), lambda i: (row0 + i, 1)),
            pl.BlockSpec((tm, 1024), lambda i: (row0 + i, 2)),
            pl.BlockSpec((tm, LANES), lambda i: (i % nst, 0)),
            pl.BlockSpec((tm, LANES), lambda i: (i % nst, 0)),
        ],
        out_specs=[pl.BlockSpec((tm, 1024), lambda i: (i, 0)), pl.BlockSpec((tm, 2048), lambda i: (i, 0))],
        out_shape=[jax.ShapeDtypeStruct((NTOK_S, 1024), BF16), jax.ShapeDtypeStruct((NTOK_S, 2048), BF16)],
        compiler_params=_params(("parallel",)),
        name="diff_rope_keys",
    )(zodd, zodd, cos_t, sin_t)


def _diff_attn_kernel(latent, lam_init, *refs):
    if latent:
        (lam_ref, ng_ref, q_ref, cos_ref, sin_ref, kc_ref, vc_ref, kn_ref, vn_ref, o_ref) = refs
    else:
        lam_ref, ng_ref, q_ref, k_ref, v_ref = refs[:5]
        o_ref, kout_ref, vout_ref = refs[-3:]
    tq = q_ref.shape[0]
    lane = _lane_iota((tq, LANES))
    scale = DIFF_DH ** -0.5
    lp = lam_ref[...]
    lam = (jnp.exp(jnp.sum(lp[0:1, :] * lp[1:2, :], axis=-1, keepdims=True))
           - jnp.exp(jnp.sum(lp[2:3, :] * lp[3:4, :], axis=-1, keepdims=True)) + lam_init)
    ng = ng_ref[...]
    ones = jnp.ones((PAST_LEN if latent else tq, LANES), BF16)
    for h in range(DIFF_HEADS):
        sl = slice(LANES * h, LANES * (h + 1))
        qh = q_ref[:, sl]
        if latent:
            qh = _rope128(qh, cos_ref[...], sin_ref[...], lane, DIFF_DH // 2)
            k_list = [kc_ref[h].astype(BF16), kn_ref[:, sl]]
            v_list = [jnp.concatenate([vc_ref[h].astype(BF16), ones], axis=1),
                      vn_ref[:, 256 * h:256 * (h + 1)]]
        else:
            kh = k_ref[:, sl]
            vh = v_ref[:, sl]
            kout_ref[h] = kh
            vout_ref[h] = vh
            k_list = [kh.astype(BF16)]
            v_list = [jnp.concatenate([vh.astype(BF16), ones], axis=1)]
        q1 = jnp.where(lane < DIFF_DH, qh, 0.0).astype(BF16)
        q2 = jnp.where(lane >= DIFF_DH, qh, 0.0).astype(BF16)
        s1 = [lax.dot_general(q1, kk, NT_DIMS, preferred_element_type=F32) for kk in k_list]
        s2 = [lax.dot_general(q2, kk, NT_DIMS, preferred_element_type=F32) for kk in k_list]
        o = _pv_normalised(_exp_parts(s1, scale), v_list) - lam * _pv_normalised(_exp_parts(s2, scale), v_list)
        y = o * lax.rsqrt(jnp.mean(o * o, axis=-1, keepdims=True) + RMS_EPS) * ng
        o_ref[:, sl] = (y * (1.0 - lam_init)).astype(BF16)


def _diff_attn_prompt(zodd, lam_p, norm_g, j, lam_init, prev):
    cache_shape = jax.ShapeDtypeStruct((BATCH, N_ODD, DIFF_HEADS, SEQ, LANES), F32)
    cache_spec = pl.BlockSpec((None, None, DIFF_HEADS, SEQ, LANES), lambda b: (b, j, 0, 0, 0))
    in_specs = [
        pl.BlockSpec((None, 4, DIFF_DH), lambda b: (j, 0, 0)),
        pl.BlockSpec((None, 1, DIFF_DV), lambda b: (j, 0, 0)),
        pl.BlockSpec((SEQ, 1024), lambda b: (b, 0)),
        pl.BlockSpec((SEQ, 1024), lambda b: (b, 1)),
        pl.BlockSpec((SEQ, 1024), lambda b: (b, 2)),
    ]
    args = [lam_p, norm_g.reshape(-1, 1, DIFF_DV), zodd, zodd, zodd]
    aliases = {}
    if prev is not None:
        aliases = {len(args): 1, len(args) + 1: 2}
        in_specs += [pl.BlockSpec(memory_space=pl.ANY)] * 2
        args += list(prev)
    return pl.pallas_call(
        functools.partial(_diff_attn_kernel, False, lam_init),
        grid=(BATCH,),
        in_specs=in_specs,
        out_specs=[pl.BlockSpec((SEQ, 1024), lambda b: (b, 0)), cache_spec, cache_spec],
        out_shape=[jax.ShapeDtypeStruct((NTOK_P, 1024), BF16), cache_shape, cache_shape],
        input_output_aliases=aliases,
        compiler_params=_params(("arbitrary",)),
        name="diff_attn_prompt",
    )(*args)


def _diff_attn_latent(zodd, lam_p, norm_g, cos_t, sin_t, cache_k, cache_v, k_rot, v_aug, j, lam_init):
    nq = DEC_SEQ // ATT_TQ
    row0 = NTOK_P // ATT_TQ
    ctx_spec = pl.BlockSpec((None, None, DIFF_HEADS, PAST_LEN, LANES), lambda b, q: (b, j, 0, 0, 0))
    return pl.pallas_call(
        functools.partial(_diff_attn_kernel, True, lam_init),
        grid=(DEC_BATCH, nq),
        in_specs=[
            pl.BlockSpec((None, 4, DIFF_DH), lambda b, q: (j, 0, 0)),
            pl.BlockSpec((None, 1, DIFF_DV), lambda b, q: (j, 0, 0)),
            pl.BlockSpec((ATT_TQ, 1024), lambda b, q: (row0 + b * nq + q, 0)),
            pl.BlockSpec((ATT_TQ, LANES), lambda b, q: (q, 0)),
            pl.BlockSpec((ATT_TQ, LANES), lambda b, q: (q, 0)),
            ctx_spec,
            ctx_spec,
            pl.BlockSpec((DEC_SEQ, 1024), lambda b, q: (b, 0)),
            pl.BlockSpec((DEC_SEQ, 2048), lambda b, q: (b, 0)),
        ],
        out_specs=pl.BlockSpec((ATT_TQ, 1024), lambda b, q: (b * nq + q, 0)),
        out_shape=jax.ShapeDtypeStruct((NTOK_S, 1024), BF16),
        compiler_params=_params(("parallel", "arbitrary")),
        name="diff_attn_latent",
    )(lam_p, norm_g.reshape(-1, 1, DIFF_DV), zodd, cos_t, sin_t, cache_k, cache_v, k_rot, v_aug)


def _mla_weight(w_in):
    base = RET_COLS
    mq = w_in[:, base:base + MLA_HEADS * (MLA_NOPE + MLA_ROPE)].reshape(D_MODEL, MLA_HEADS, MLA_NOPE + MLA_ROPE)
    qn = mq[:, :, :MLA_NOPE].reshape(D_MODEL, MLA_HEADS * MLA_NOPE)
    qr = mq[:, :, MLA_NOPE:].reshape(D_MODEL, MLA_HEADS * MLA_ROPE)
    ckv0 = base + MLA_HEADS * (MLA_NOPE + MLA_ROPE)
    ckv = w_in[:, ckv0:ckv0 + MLA_KV_RANK]
    kr = w_in[:, ckv0 + MLA_KV_RANK:]
    return jnp.concatenate([qn, qr, ckv, jnp.tile(kr, (1, LANES // MLA_ROPE))], axis=1)


def kernel(x_prompt, x_sample, state_ret_fwd, state_ret_bwd, cache_mla_ckv, cache_mla_krope, cache_diff_k, cache_diff_v, c, c_ctx, ada_w, ada_b, ln1_g, ln1_b, ln2_g, ln2_b, ev_w_in, ev_w_out, ret_decay_fwd, ret_decay_bwd, mla_kv_norm_g, mla_w_uk, mla_w_uv, od_w_in, od_w_out, diff_lambda, diff_norm_g, moe_w_group, moe_b_group, moe_w_expert, moe_b_expert, moe_w1, moe_w3, moe_w2):
    x = (x_prompt.reshape(NTOK_P, D_MODEL), x_sample.reshape(NTOK_S, D_MODEL), 0)
    cond =jnp.concatenate([c_ctx[None, :], c, jnp.zeros((N_COND - 1 - DEC_BATCH, D_MODEL), F32)], axis=0)
    mods = _ada_all(cond, ada_w, ada_b).reshape(DEPTH, N_COND, 6, D_MODEL)

    cos_m, sin_m = _rope_tables(MLA_ROPE)
    cos_d, sin_d = _rope_tables(DIFF_DH)
    ident = 512
    cos_m_id = jnp.concatenate([jnp.ones((ident, LANES), F32), cos_m], axis=0)
    sin_m_id = jnp.concatenate([jnp.zeros((ident, LANES), F32), sin_m], axis=0)
    decf = jnp.broadcast_to(ret_decay_fwd[:, :, None], ret_decay_fwd.shape + (LANES,))
    decb = jnp.broadcast_to(ret_decay_bwd[:, :, None], ret_decay_bwd.shape + (LANES,))

    pad = LANES - MOE_GROUPS - MOE_EXPERTS
    ret_states = mla_caches = diff_caches = None
    for i in range(DEPTH):
        j = i // 2
        mod = mods[i]
        w_router = jnp.concatenate([moe_w_group[i], moe_w_expert[i], jnp.zeros((D_MODEL, pad), F32)], axis=1)
        b_router = jnp.concatenate([moe_b_group[i], moe_b_expert[i], jnp.zeros((pad,), F32)])[None, :]
        if i % 2 == 0:
            zret = _mm_mod(x, mod, ev_w_in, (j,), RET_COLS, 1024, "in_proj_retention")
            zmla = _mm_mod(x, mod, _mla_weight(ev_w_in[j]), (), MLA_COLS, MLA_COLS, "in_proj_mla")
            *mla_caches, kcat, vm = _mla_kv_new(zmla, cos_m_id, sin_m_id, mla_kv_norm_g, mla_w_uk, mla_w_uv,
                                                j, mla_caches)
            kr_ctx = jnp.tile(cache_mla_krope[:, j], (1, 1, LANES // MLA_ROPE))
            kcat_ctx, vm_ctx = _mla_kv_ctx(cache_mla_ckv, kr_ctx, mla_w_uk, mla_w_uv, j)
            a_ret_p, *ret_states = _retention(zret, decf, decb, j, False, prev=ret_states)
            (a_ret_s,) = _retention(zret, decf, decb, j, True, state_ret_fwd, state_ret_bwd)
            a_mla_p = _mla_attn_prompt(zmla, kcat, vm)
            a_mla_s = _mla_attn_latent(zmla, cos_m, sin_m, kcat_ctx, vm_ctx, kcat, vm)
            x1, meta, counts = _mm_ln([(a_ret_p, a_ret_s), (a_mla_p, a_mla_s)], ev_w_out, j, x, mod,
                                      ln1_g, ln1_b, w_router, b_router, i)
        else:
            lam_init = 0.8 - 0.6 * math.exp(-0.3 * i)
            zodd = _mm_mod(x, mod, od_w_in, (j,), 3072, 1024, "in_proj_diff")
            a_p, *diff_caches = _diff_attn_prompt(zodd, diff_lambda, diff_norm_g, j, lam_init, diff_caches)
            k_rot, v_aug = _diff_prep(zodd, cos_d, sin_d)
            a_s = _diff_attn_latent(zodd, diff_lambda, diff_norm_g, cos_d, sin_d, cache_diff_k, cache_diff_v,
                                    k_rot, v_aug, j, lam_init)
            x1, meta, counts = _mm_ln([(a_p, a_s)], od_w_out, j, x, mod, ln1_g, ln1_b, w_router, b_router, i)
        y = _moe(x1, _moe_plan(meta, counts), mods, w_router, b_router, moe_w1, moe_w3, moe_w2, ln2_g, ln2_b, i)
        x = (y, y, NTOK_P)

    y_prompt = x[0][:NTOK_P].reshape(BATCH, SEQ, D_MODEL)
    y_sample = x[1][NTOK_P:NTOK].reshape(DEC_BATCH, DEC_SEQ, D_MODEL)
    return (y_prompt, y_sample, ret_states[0], ret_states[1], mla_caches[0], mla_caches[1],
            diff_caches[0], diff_caches[1])
```

```python
import functools
import math

import jax
import jax.numpy as jnp
from jax import lax
from jax.experimental import pallas as pl
from jax.experimental.pallas import tpu as pltpu

D_MODEL = 1024
BATCH = 32
SEQ = 256
DEPTH = 4
N_EVEN = 2
N_ODD = 2
DEC_BATCH = 2
DEC_SEQ = 2048
PAST_LEN = 256
GRID_W = 64
LN_EPS = 1e-5
RMS_EPS = 1e-6
DEEPNORM_ALPHA = (2.0 * DEPTH) ** 0.25
ROPE_BASE = 10000.0
RET_HEADS = 8
RET_DK = 64
RET_DV = 128
MLA_HEADS = 8
MLA_NOPE = 64
MLA_ROPE = 32
MLA_DV = 64
MLA_KV_RANK = 256
DIFF_HEADS = 8
DIFF_DH = 64
DIFF_DV = 128
MOE_GROUPS = 4
MOE_PER_GROUP = 4
MOE_EXPERTS = 16
MOE_FF = 256

NTOK_P = BATCH * SEQ
NTOK_S = DEC_BATCH * DEC_SEQ
NTOK = NTOK_P + NTOK_S
N_COND = 8
LANES = 128
SUBLANES = 8
RET_COLS = 3072
MLA_COLS = 1152
ATT_TQ = 256
RET_CHUNK = 256
MOE_TILE = 512
MOE_TILES = (NTOK + MOE_GROUPS * (MOE_TILE - 1)) // MOE_TILE
MOE_ROWS = MOE_TILES * MOE_TILE
VMEM_LIMIT = 56 * 1024 * 1024
LOG2E = 1.4426950408889634

F32 = jnp.float32
BF16 = jnp.bfloat16
NT_DIMS = (((1,), (1,)), ((), ()))
TN_DIMS = (((0,), (0,)), ((), ()))


def _params(sem):
    return pltpu.CompilerParams(dimension_semantics=sem, vmem_limit_bytes=VMEM_LIMIT)


def _group_of_tile(i, tm):
    npt = NTOK_P // tm
    nst = DEC_SEQ // tm
    return jnp.where(i < npt, 0, 1 + (i - npt) // nst)


def _split_specs(tm, width, m_of, s_row0=0):
    npt = NTOK_P // tm
    s_blk0 = s_row0 // tm
    return [pl.BlockSpec((tm, width), lambda *g: (jnp.minimum(m_of(*g), npt - 1), 0)),
            pl.BlockSpec((tm, width), lambda *g: (jnp.maximum(m_of(*g) - npt, 0) + s_blk0, 0))]


def _read_split(p_ref, s_ref, m):
    return jnp.where(m < NTOK_P // p_ref.shape[0], p_ref[...], s_ref[...])


def _silu(x):
    return x * (1.0 / (1.0 + jnp.exp(-x)))


def _layer_norm(r, g, b):
    mu = jnp.mean(r, axis=-1, keepdims=True)
    d = r - mu
    var = jnp.mean(d * d, axis=-1, keepdims=True)
    return d * lax.rsqrt(var + LN_EPS) * g + b


def _lane_iota(shape):
    return lax.broadcasted_iota(jnp.int32, shape, 1)


def _div_pow2(x, d):
    assert d & (d - 1) == 0
    return jnp.right_shift(x, d.bit_length() - 1)


def _mod_pow2(x, d):
    assert d & (d - 1) == 0
    return jnp.bitwise_and(x, d - 1)


def _ada_kernel(c_ref, w_ref, b_ref, o_ref):
    h = _silu(c_ref[...]).astype(BF16)
    o_ref[...] = jnp.dot(h, w_ref[...].astype(BF16), preferred_element_type=F32) + b_ref[...]


def _ada_all(cond, ada_w, ada_b):
    tn = 768
    return pl.pallas_call(
        _ada_kernel,
        grid=(DEPTH, 6 * D_MODEL // tn),
        in_specs=[
            pl.BlockSpec((N_COND, D_MODEL), lambda l, n: (0, 0)),
            pl.BlockSpec((None, D_MODEL, tn), lambda l, n: (l, 0, n)),
            pl.BlockSpec((None, 1, tn), lambda l, n: (l, 0, n)),
        ],
        out_specs=pl.BlockSpec((None, N_COND, tn), lambda l, n: (l, 0, n)),
        out_shape=jax.ShapeDtypeStruct((DEPTH, N_COND, 6 * D_MODEL), F32),
        compiler_params=_params(("parallel", "parallel")),
        name="ada_modulation",
    )(cond, ada_w, ada_b.reshape(DEPTH, 1, 6 * D_MODEL))


def _mm_mod_kernel(xp_ref, xs_ref, mod_ref, w_ref, o_ref, wscr):
    m = pl.program_id(1)

    @pl.when(m == 0)
    def _():
        wscr[...] = w_ref[...].astype(BF16)

    sh = mod_ref[0:1, :]
    sc = mod_ref[1:2, :]
    xm = (_read_split(xp_ref, xs_ref, m) * (1.0 + sc) + sh).astype(BF16)
    o_ref[...] = jnp.dot(xm, wscr[...], preferred_element_type=F32).astype(o_ref.dtype)


def _mm_mod(x, mod, w, w_index, n_cols, tn, name):
    tm = 1024
    w_block = (None,) * len(w_index) + (D_MODEL, tn)
    return pl.pallas_call(
        _mm_mod_kernel,
        grid=(n_cols // tn, NTOK // tm),
        in_specs=_split_specs(tm, D_MODEL, lambda n, m: m, x[2]) + [
            pl.BlockSpec((None, 6, D_MODEL), lambda n, m: (_group_of_tile(m, tm), 0, 0)),
            pl.BlockSpec(w_block, lambda n, m: tuple(w_index) + (0, n)),
        ],
        out_specs=pl.BlockSpec((tm, tn), lambda n, m: (m, n)),
        out_shape=jax.ShapeDtypeStruct((NTOK, n_cols), BF16),
        scratch_shapes=[pltpu.VMEM((D_MODEL, tn), BF16)],
        compiler_params=_params(("arbitrary", "arbitrary")),
        name=name,
    )(x[0], x[1], mod, w)


def _router_probs(xm, wr_ref, br_ref):
    rows = xm.shape[0]
    z = jnp.dot(xm, wr_ref[...].astype(BF16), preferred_element_type=F32) + br_ref[...]
    lane_i = _lane_iota((rows, LANES))
    lane = lane_i.astype(F32)
    gmask = lane_i < MOE_GROUPS
    zg = jnp.where(gmask, z, -jnp.inf)
    pg = jnp.exp(zg - jnp.max(zg, axis=-1, keepdims=True))
    g_prob = pg / jnp.sum(pg, axis=-1, keepdims=True)
    g_p = jnp.max(g_prob, axis=-1, keepdims=True)
    g_idx = jnp.min(jnp.where(gmask & (g_prob == g_p), lane, float(LANES)), axis=-1, keepdims=True)
    return z, lane_i, lane, g_p, g_idx


def _mm_ln_kernel(k_sizes, *refs):
    n_a = len(k_sizes)
    a_refs = refs[:2 * n_a]
    (w_ref, xp_ref, xs_ref, mod_ref, g_ref, b_ref, wr_ref, br_ref,
     o_ref, meta_ref, cnt_ref, wscr, tri_scr, carry_scr) = refs[2 * n_a:]
    m = pl.program_id(0)
    tm = o_ref.shape[0]

    @pl.when(m == 0)
    def _():
        wscr[...] = w_ref[...].astype(BF16)
        ri = lax.broadcasted_iota(jnp.int32, (tm, tm), 0)
        ci = lax.broadcasted_iota(jnp.int32, (tm, tm), 1)
        tri_scr[...] = jnp.where(ci < ri, 1.0, 0.0).astype(BF16)
        carry_scr[...] = jnp.zeros_like(carry_scr)

    y = None
    k0 = 0
    for i, ks in enumerate(k_sizes):
        a = _read_split(a_refs[2 * i], a_refs[2 * i + 1], m)
        part = jnp.dot(a, wscr[k0:k0 + ks, :], preferred_element_type=F32)
        y = part if y is None else y + part
        k0 += ks
    gate = mod_ref[2:3, :]
    r = DEEPNORM_ALPHA * _read_split(xp_ref, xs_ref, m) + gate * y
    x1 = _layer_norm(r, g_ref[...], b_ref[...])
    o_ref[...] = x1

    xm = (x1 * (1.0 + mod_ref[4:5, :]) + mod_ref[3:4, :]).astype(BF16)
    _, lane_i, lane, _, g_idx = _router_probs(xm, wr_ref, br_ref)
    onehot = jnp.where(lane == g_idx, 1.0, 0.0)
    before = jnp.dot(tri_scr[...], onehot.astype(BF16), preferred_element_type=F32) + carry_scr[0:1, :]
    rank = jnp.sum(jnp.where(lane == g_idx, before, 0.0), axis=-1, keepdims=True)
    meta_ref[...] = jnp.where(lane_i == 0, g_idx, jnp.where(lane_i == 1, rank, 0.0))
    total = carry_scr[0:1, :] + jnp.sum(onehot, axis=0, keepdims=True)
    carry_scr[...] = jnp.broadcast_to(total, carry_scr.shape)
    cnt_ref[...] = jnp.broadcast_to(total, cnt_ref.shape)


def _mm_ln(a_pairs, w, j, x, mod, ln_g, ln_b, w_router, b_router, layer):
    tm = 512
    k_sizes = tuple(ap.shape[1] for ap, _ in a_pairs)
    k_tot = sum(k_sizes)
    in_specs = []
    args = []
    for (ap, a_s), ks in zip(a_pairs, k_sizes):
        in_specs += _split_specs(tm, ks, lambda m: m)
        args += [ap, a_s]
    in_specs += [pl.BlockSpec((None, k_tot, D_MODEL), lambda m: (j, 0, 0))]
    in_specs += _split_specs(tm, D_MODEL, lambda m: m, x[2])
    in_specs += [
        pl.BlockSpec((None, 6, D_MODEL), lambda m: (_group_of_tile(m, tm), 0, 0)),
        pl.BlockSpec((None, 1, D_MODEL), lambda m: (layer, 0, 0)),
        pl.BlockSpec((None, 1, D_MODEL), lambda m: (layer, 0, 0)),
        pl.BlockSpec((D_MODEL, LANES), lambda m: (0, 0)),
        pl.BlockSpec((1, LANES), lambda m: (0, 0)),
    ]
    return pl.pallas_call(
        functools.partial(_mm_ln_kernel, k_sizes),
        grid=(NTOK // tm,),
        in_specs=in_specs,
        out_specs=[
            pl.BlockSpec((tm, D_MODEL), lambda m: (m, 0)),
            pl.BlockSpec((tm, LANES), lambda m: (m, 0)),
            pl.BlockSpec((N_COND, LANES), lambda m: (0, 0)),
        ],
        out_shape=[
            jax.ShapeDtypeStruct((NTOK, D_MODEL), F32),
            jax.ShapeDtypeStruct((NTOK, LANES), F32),
            jax.ShapeDtypeStruct((N_COND, LANES), F32),
        ],
        scratch_shapes=[
            pltpu.VMEM((k_tot, D_MODEL), BF16),
            pltpu.VMEM((tm, tm), BF16),
            pltpu.VMEM((N_COND, LANES), F32),
        ],
        compiler_params=_params(("arbitrary",)),
        name="out_proj_ln",
    )(*args, w, x[0], x[1], mod, ln_g.reshape(DEPTH, 1, D_MODEL), ln_b.reshape(DEPTH, 1, D_MODEL),
      w_router, b_router)


def _moe_kernel(tgrp_ref, ntile_ref, src0_ref, src1_ref, dst_ref, mid_ref, x_hbm, mod_ref, wr_ref, br_ref,
                w1_ref, w3_ref, w2_ref, g_ref, b_ref, y_hbm,
                gbuf, obuf, gsem, ssem, w13s, w2s):
    i = pl.program_id(0)
    n_steps = pl.num_programs(0)
    n_tiles = ntile_ref[0]
    n_blk = gbuf.shape[1]
    ts = n_blk * SUBLANES
    slot = lax.rem(i, 2)

    def start_gather(src_ref, s):
        def body(k, carry):
            for u in range(SUBLANES):
                tok = src_ref[0, k * SUBLANES + u]
                pltpu.make_async_copy(x_hbm.at[pl.ds(tok, 1), :], gbuf.at[s, k, pl.ds(u, 1), :],
                                      gsem.at[s]).start()
            return carry
        lax.fori_loop(0, n_blk, body, 0)

    def wait_gather(s):
        pltpu.make_async_copy(gbuf.at[s], gbuf.at[s], gsem.at[s]).wait()

    def start_scatter(s):
        def body(k, carry):
            for u in range(SUBLANES):
                tok = dst_ref[0, k * SUBLANES + u]
                pltpu.make_async_copy(obuf.at[s, k, pl.ds(u, 1), :], y_hbm.at[pl.ds(tok, 1), :],
                                      ssem.at[s]).start()
            return carry
        lax.fori_loop(0, n_blk, body, 0)

    def wait_scatter(s):
        pltpu.make_async_copy(obuf.at[s], obuf.at[s], ssem.at[s]).wait()

    @pl.when(i == 0)
    def _():
        start_gather(src0_ref, 0)

    @pl.when(i < n_tiles)
    def _():
        grp = tgrp_ref[i]
        wait_gather(slot)

        @pl.when(i + 1 < n_tiles)
        def _():
            start_gather(src1_ref, 1 - slot)

        @pl.when((i == 0) | (grp != tgrp_ref[jnp.maximum(i - 1, 0)]))
        def _():
            w13s[:, :, :MOE_FF] = w1_ref[...].astype(BF16)
            w13s[:, :, MOE_FF:] = w3_ref[...].astype(BF16)
            w2s[...] = w2_ref[...].astype(BF16)

        mid = mid_ref[...]

        def mod_row(k):
            return jnp.where(mid == 0, mod_ref[0, k:k + 1, :],
                             jnp.where(mid == 1, mod_ref[1, k:k + 1, :], mod_ref[2, k:k + 1, :]))

        x1 = gbuf[slot].reshape(ts, D_MODEL)
        xm = (x1 * (1.0 + mod_row(4)) + mod_row(3)).astype(BF16)
        z, lane_i, lane, g_p, _ = _router_probs(xm, wr_ref, br_ref)
        e0 = MOE_GROUPS + MOE_PER_GROUP * grp
        emask = (lane_i >= e0) & (lane_i < e0 + MOE_PER_GROUP)
        ze = jnp.where(emask, z, -jnp.inf)
        pe = jnp.exp(ze - jnp.max(ze, axis=-1, keepdims=True))
        e_prob = pe / jnp.sum(pe, axis=-1, keepdims=True)
        cand = jnp.where(emask, e_prob, -1.0)
        p1 = jnp.max(cand, axis=-1, keepdims=True)
        i1 = jnp.min(jnp.where(cand == p1, lane, float(LANES)), axis=-1, keepdims=True)
        cand2 = jnp.where(lane == i1, -1.0, cand)
        p2 = jnp.max(cand2, axis=-1, keepdims=True)
        i2 = jnp.min(jnp.where(cand2 == p2, lane, float(LANES)), axis=-1, keepdims=True)
        denom = p1 + p2
        comb = jnp.where(lane == i1, g_p * p1 / denom, 0.0) + jnp.where(lane == i2, g_p * p2 / denom, 0.0)
        y = None
        for e in range(MOE_PER_GROUP):
            c = jnp.sum(jnp.where(lane_i == e0 + e, comb, 0.0), axis=-1, keepdims=True)
            h = jnp.dot(xm, w13s[e], preferred_element_type=F32)
            hid = (_silu(h[:, :MOE_FF]) * h[:, MOE_FF:] * c).astype(BF16)
            part = jnp.dot(hid, w2s[e], preferred_element_type=F32)
            y = part if y is None else y + part
        r = DEEPNORM_ALPHA * x1 + mod_row(5) * y

        @pl.when(i >= 2)
        def _():
            wait_scatter(slot)

        obuf[slot] = _layer_norm(r, g_ref[...], b_ref[...]).reshape(n_blk, SUBLANES, D_MODEL)
        start_scatter(slot)

    @pl.when(i == n_steps - 1)
    def _():
        @pl.when(n_tiles >= 2)
        def _():
            wait_scatter(lax.rem(n_tiles, 2))

        @pl.when(n_tiles >= 1)
        def _():
            wait_scatter(lax.rem(n_tiles + 1, 2))


def _moe_plan(meta, counts):
    ts = MOE_TILE
    cnt = counts[0, :MOE_GROUPS].astype(jnp.int32)
    tiles_g = (cnt + ts - 1) // ts
    tile_end = jnp.cumsum(tiles_g)
    row0_g = (tile_end - tiles_g) * ts
    gid = meta[:, 0].astype(jnp.int32)
    rank = meta[:, 1].astype(jnp.int32)
    pos = row0_g[gid] + rank
    tok1 = jnp.zeros((MOE_ROWS,), jnp.int32).at[pos].set(jnp.arange(1, NTOK + 1, dtype=jnp.int32))
    rows = jnp.arange(MOE_ROWS, dtype=jnp.int32)
    src = jnp.maximum(tok1 - 1, 0)
    spare = NTOK + ((rows // ts) % 2) * ts + rows % ts
    dst = jnp.where(tok1 > 0, src, spare)
    mid = jnp.where(src < NTOK_P, 0, 1 + (src - NTOK_P) // DEC_SEQ)
    tile_group = jnp.minimum(jnp.sum(jnp.arange(MOE_TILES)[:, None] >= tile_end[None, :], axis=1),
                             MOE_GROUPS - 1).astype(jnp.int32)
    n_tiles = tile_end[-1:].astype(jnp.int32)
    return (tile_group, n_tiles, src.reshape(MOE_TILES, 1, ts), dst.reshape(MOE_TILES, 1, ts),
            mid.reshape(MOE_ROWS, 1))


def _moe(x1, plan, mods, w_router, b_router, w1, w3, w2, ln_g, ln_b, layer):
    ts = MOE_TILE
    tile_group, n_tiles, src, dst, mid = plan
    grp_shape = (DEPTH, MOE_GROUPS, MOE_PER_GROUP)
    w_in_spec = pl.BlockSpec((None, None, MOE_PER_GROUP, D_MODEL, MOE_FF), lambda i, tg, nt: (layer, tg[i], 0, 0, 0))
    w_out_spec = pl.BlockSpec((None, None, MOE_PER_GROUP, MOE_FF, D_MODEL), lambda i, tg, nt: (layer, tg[i], 0, 0, 0))
    smem_tile = functools.partial(pl.BlockSpec, (None, 1, ts), memory_space=pltpu.SMEM)
    grid_spec = pltpu.PrefetchScalarGridSpec(
        num_scalar_prefetch=2,
        grid=(MOE_TILES,),
        in_specs=[
            smem_tile(lambda i, tg, nt: (i, 0, 0)),
            smem_tile(lambda i, tg, nt: (jnp.minimum(i + 1, MOE_TILES - 1), 0, 0)),
            smem_tile(lambda i, tg, nt: (i, 0, 0)),
            pl.BlockSpec((ts, 1), lambda i, tg, nt: (i, 0)),
            pl.BlockSpec(memory_space=pl.ANY),
            pl.BlockSpec((None, N_COND, 6, D_MODEL), lambda i, tg, nt: (layer, 0, 0, 0)),
            pl.BlockSpec((D_MODEL, LANES), lambda i, tg, nt: (0, 0)),
            pl.BlockSpec((1, LANES), lambda i, tg, nt: (0, 0)),
            w_in_spec,
            w_in_spec,
            w_out_spec,
            pl.BlockSpec((None, 1, D_MODEL), lambda i, tg, nt: (layer, 0, 0)),
            pl.BlockSpec((None, 1, D_MODEL), lambda i, tg, nt: (layer, 0, 0)),
        ],
        out_specs=pl.BlockSpec(memory_space=pl.ANY),
        scratch_shapes=[
            pltpu.VMEM((2, ts // SUBLANES, SUBLANES, D_MODEL), F32),
            pltpu.VMEM((2, ts // SUBLANES, SUBLANES, D_MODEL), F32),
            pltpu.SemaphoreType.DMA((2,)),
            pltpu.SemaphoreType.DMA((2,)),
            pltpu.VMEM((MOE_PER_GROUP, D_MODEL, 2 * MOE_FF), BF16),
            pltpu.VMEM((MOE_PER_GROUP, MOE_FF, D_MODEL), BF16),
        ],
    )
    return pl.pallas_call(
        _moe_kernel,
        grid_spec=grid_spec,
        out_shape=jax.ShapeDtypeStruct((NTOK + 2 * ts, D_MODEL), F32),
        compiler_params=_params(("arbitrary",)),
        name="hier_moe_ln",
    )(tile_group, n_tiles, src, src, dst, mid, x1, mods, w_router, b_router,
      w1.reshape(grp_shape + (D_MODEL, MOE_FF)), w3.reshape(grp_shape + (D_MODEL, MOE_FF)),
      w2.reshape(grp_shape + (MOE_FF, D_MODEL)),
      ln_g.reshape(DEPTH, 1, D_MODEL), ln_b.reshape(DEPTH, 1, D_MODEL))


def _swap_halves(x, lane, half):
    return jnp.where(_mod_pow2(lane, 2 * half) < half,
                     pltpu.roll(x, LANES - half, 1), pltpu.roll(x, half, 1))


def _rope128(x, cos, sin_signed, lane, half):
    return x * cos + _swap_halves(x, lane, half) * sin_signed


def _rope_tables(rot_dim):
    rows = DEC_SEQ // GRID_W
    row = jnp.repeat(jnp.arange(rows, dtype=F32), GRID_W)
    col = jnp.tile(jnp.arange(GRID_W, dtype=F32), rows)
    n_freq = rot_dim // 4
    inv_freq = ROPE_BASE ** (-jnp.arange(n_freq, dtype=F32) / n_freq)
    ang = jnp.concatenate([row[:, None] * inv_freq, col[:, None] * inv_freq], axis=-1)
    cos, sin = jnp.cos(ang), jnp.sin(ang)
    reps = LANES // rot_dim
    cos_full = jnp.tile(jnp.concatenate([cos, cos], axis=-1), (1, reps))
    sin_signed = jnp.tile(jnp.concatenate([-sin, sin], axis=-1), (1, reps))
    return cos_full, sin_signed


def _exp_parts(s_list, scale):
    m = None
    for s in s_list:
        sm = jnp.max(s, axis=-1, keepdims=True)
        m = sm if m is None else jnp.maximum(m, sm)
    return [jnp.exp2((s - m) * (scale * LOG2E)).astype(BF16) for s in s_list]


def _pv_normalised(p_list, v_list):
    o = None
    for p, v in zip(p_list, v_list):
        part = jnp.dot(p, v, preferred_element_type=F32)
        o = part if o is None else o + part
    return o[:, :LANES] / o[:, LANES:]


def _mla_kv_kernel(new_tokens, *refs):
    if new_tokens:
        (ckv_ref, kr_ref, cos_ref, sin_ref, g_ref, wuk_ref, wuv_ref) = refs[:7]
        ckvc_ref, krc_ref, kcat_ref, vm_ref = refs[-4:]
        i = pl.program_id(0)
        x = ckv_ref[...].astype(F32)
        c = x * lax.rsqrt(jnp.mean(x * x, axis=-1, keepdims=True) + RMS_EPS) * g_ref[...]
        kr_raw = kr_ref[...].astype(F32)

        @pl.when(i < NTOK_P // ckv_ref.shape[0])
        def _():
            for b in range(ckvc_ref.shape[0]):
                ckvc_ref[b] = c[SEQ * b:SEQ * (b + 1), :]
                krc_ref[b] = kr_raw[SEQ * b:SEQ * (b + 1), :MLA_ROPE]

        lane = _lane_iota(kr_ref.shape)
        kr = _rope128(kr_raw, cos_ref[...], sin_ref[...], lane, MLA_ROPE // 2)
    else:
        ckv_ref, kr_ref, wuk_ref, wuv_ref, kcat_ref, vm_ref = refs
        c = ckv_ref[...]
        kr = kr_ref[...]
    cb = c.astype(BF16)
    kn = jnp.dot(cb, wuk_ref[...].astype(BF16), preferred_element_type=F32).astype(BF16)
    vv = jnp.dot(cb, wuv_ref[...].astype(BF16), preferred_element_type=F32).astype(BF16)
    krb = kr.astype(BF16)
    ones = jnp.ones((c.shape[0], LANES), BF16)
    for p in range(MLA_HEADS // 2):
        kcat_ref[:, 256 * p:256 * p + LANES] = kn[:, LANES * p:LANES * (p + 1)]
        kcat_ref[:, 256 * p + LANES:256 * (p + 1)] = krb
        vm_ref[:, 256 * p:256 * p + LANES] = vv[:, LANES * p:LANES * (p + 1)]
        vm_ref[:, 256 * p + LANES:256 * (p + 1)] = ones


def _mla_kv_new(zmla, cos_t, sin_t, kv_norm_g, w_uk, w_uv, j, prev):
    tm = 512
    npt = NTOK_P // tm
    nst = DEC_SEQ // tm
    nb = tm // SEQ

    def tab(i):
        return (jnp.where(i < npt, 0, 1 + (i - npt) % nst), 0)

    def cache_idx(i):
        return (jnp.minimum(i, npt - 1), j, 0, 0)

    in_specs = [
        pl.BlockSpec((tm, MLA_KV_RANK), lambda i: (i, 768 // MLA_KV_RANK)),
        pl.BlockSpec((tm, LANES), lambda i: (i, 1024 // LANES)),
        pl.BlockSpec((tm, LANES), tab),
        pl.BlockSpec((tm, LANES), tab),
        pl.BlockSpec((None, 1, MLA_KV_RANK), lambda i: (j, 0, 0)),
        pl.BlockSpec((None, MLA_KV_RANK, 512), lambda i: (j, 0, 0)),
        pl.BlockSpec((None, MLA_KV_RANK, 512), lambda i: (j, 0, 0)),
    ]
    args = [zmla, zmla, cos_t, sin_t, kv_norm_g.reshape(-1, 1, MLA_KV_RANK), w_uk, w_uv]
    aliases = {}
    if prev is not None:
        aliases = {len(args): 0, len(args) + 1: 1}
        in_specs += [pl.BlockSpec(memory_space=pl.ANY)] * 2
        args += list(prev)
    return pl.pallas_call(
        functools.partial(_mla_kv_kernel, True),
        grid=(NTOK // tm,),
        in_specs=in_specs,
        out_specs=[
            pl.BlockSpec((nb, None, SEQ, MLA_KV_RANK), cache_idx),
            pl.BlockSpec((nb, None, SEQ, MLA_ROPE), cache_idx),
            pl.BlockSpec((tm, 1024), lambda i: (i, 0)),
            pl.BlockSpec((tm, 1024), lambda i: (i, 0)),
        ],
        out_shape=[
            jax.ShapeDtypeStruct((BATCH, N_EVEN, SEQ, MLA_KV_RANK), F32),
            jax.ShapeDtypeStruct((BATCH, N_EVEN, SEQ, MLA_ROPE), F32),
            jax.ShapeDtypeStruct((NTOK, 1024), BF16),
            jax.ShapeDtypeStruct((NTOK, 1024), BF16),
        ],
        input_output_aliases=aliases,
        compiler_params=_params(("arbitrary",)),
        name="mla_kv_new",
    )(*args)


def _mla_kv_ctx(cache_ckv, kr_tiled, w_uk, w_uv, j):
    return pl.pallas_call(
        functools.partial(_mla_kv_kernel, False),
        grid=(DEC_BATCH,),
        in_specs=[
            pl.BlockSpec((None, None, PAST_LEN, MLA_KV_RANK), lambda b: (b, j, 0, 0)),
            pl.BlockSpec((None, PAST_LEN, LANES), lambda b: (b, 0, 0)),
            pl.BlockSpec((None, MLA_KV_RANK, 512), lambda b: (j, 0, 0)),
            pl.BlockSpec((None, MLA_KV_RANK, 512), lambda b: (j, 0, 0)),
        ],
        out_specs=[
            pl.BlockSpec((PAST_LEN, 1024), lambda b: (b, 0)),
            pl.BlockSpec((PAST_LEN, 1024), lambda b: (b, 0)),
        ],
        out_shape=[
            jax.ShapeDtypeStruct((DEC_BATCH * PAST_LEN, 1024), BF16),
            jax.ShapeDtypeStruct((DEC_BATCH * PAST_LEN, 1024), BF16),
        ],
        compiler_params=_params(("parallel",)),
        name="mla_kv_ctx",
    )(cache_ckv, kr_tiled, w_uk, w_uv)


def _mla_attn_kernel(latent, *refs):
    if latent:
        qn_ref, qr_ref, cos_ref, sin_ref, kc_ref, vc_ref, kn_ref, vn_ref, o_ref = refs
        k_refs, v_refs = (kc_ref, kn_ref), (vc_ref, vn_ref)
    else:
        qn_ref, qr_ref, kn_ref, vn_ref, o_ref = refs
        k_refs, v_refs = (kn_ref,), (vn_ref,)
    tq = qn_ref.shape[0]
    lane = _lane_iota((tq, LANES))
    scale = (MLA_NOPE + MLA_ROPE) ** -0.5
    qr_cols = []
    for cidx in range(2):
        x = qr_ref[:, LANES * cidx:LANES * (cidx + 1)].astype(F32)
        if latent:
            x = _rope128(x, cos_ref[...], sin_ref[...], lane, MLA_ROPE // 2)
        qr_cols.append(x)
    o_prev = None
    for h in range(MLA_HEADS):
        p, half = divmod(h, 2)
        cidx, slot = divmod(h, 4)
        qa = jnp.where(_div_pow2(lane, MLA_NOPE) == half, qn_ref[:, LANES * p:LANES * (p + 1)].astype(F32), 0.0)
        qb = jnp.where(_div_pow2(lane, MLA_ROPE) == slot, qr_cols[cidx], 0.0)
        qcat = jnp.concatenate([qa, qb], axis=1).astype(BF16)
        pair_cols = slice(256 * p, 256 * (p + 1))
        s_list = [lax.dot_general(qcat, k_ref[:, pair_cols], NT_DIMS, preferred_element_type=F32)
                  for k_ref in k_refs]
        o = _pv_normalised(_exp_parts(s_list, scale), [v_ref[:, pair_cols] for v_ref in v_refs])
        if half == 0:
            o_prev = o
        else:
            o_ref[:, LANES * p:LANES * (p + 1)] = jnp.where(lane < MLA_DV, o_prev, o).astype(BF16)


def _mla_attn_prompt(zmla, kcat, vm):
    return pl.pallas_call(
        functools.partial(_mla_attn_kernel, False),
        grid=(BATCH,),
        in_specs=[
            pl.BlockSpec((SEQ, 512), lambda b: (b, 0)),
            pl.BlockSpec((SEQ, 256), lambda b: (b, 2)),
            pl.BlockSpec((SEQ, 1024), lambda b: (b, 0)),
            pl.BlockSpec((SEQ, 1024), lambda b: (b, 0)),
        ],
        out_specs=pl.BlockSpec((SEQ, 512), lambda b: (b, 0)),
        out_shape=jax.ShapeDtypeStruct((NTOK_P, 512), BF16),
        compiler_params=_params(("parallel",)),
        name="mla_attn_prompt",
    )(zmla, zmla, kcat, vm)


def _mla_attn_latent(zmla, cos_t, sin_t, kcat_ctx, vm_ctx, kcat, vm):
    nq = DEC_SEQ // ATT_TQ
    row0 = NTOK_P // ATT_TQ
    seq0 = NTOK_P // DEC_SEQ
    return pl.pallas_call(
        functools.partial(_mla_attn_kernel, True),
        grid=(DEC_BATCH, nq),
        in_specs=[
            pl.BlockSpec((ATT_TQ, 512), lambda b, q: (row0 + b * nq + q, 0)),
            pl.BlockSpec((ATT_TQ, 256), lambda b, q: (row0 + b * nq + q, 2)),
            pl.BlockSpec((ATT_TQ, LANES), lambda b, q: (q, 0)),
            pl.BlockSpec((ATT_TQ, LANES), lambda b, q: (q, 0)),
            pl.BlockSpec((PAST_LEN, 1024), lambda b, q: (b, 0)),
            pl.BlockSpec((PAST_LEN, 1024), lambda b, q: (b, 0)),
            pl.BlockSpec((DEC_SEQ, 1024), lambda b, q: (seq0 + b, 0)),
            pl.BlockSpec((DEC_SEQ, 1024), lambda b, q: (seq0 + b, 0)),
        ],
        out_specs=pl.BlockSpec((ATT_TQ, 512), lambda b, q: (b * nq + q, 0)),
        out_shape=jax.ShapeDtypeStruct((NTOK_S, 512), BF16),
        compiler_params=_params(("parallel", "arbitrary")),
        name="mla_attn_latent",
    )(zmla, zmla, cos_t, sin_t, kcat_ctx, vm_ctx, kcat, vm)


def _log_sigmoid(x):
    return jnp.minimum(x, 0.0) - jnp.log1p(jnp.exp(-jnp.abs(x)))


def _retention_kernel(seq_len, has_init, emit_state, has_prev, *refs):
    refs = list(refs)
    decf_ref, decb_ref, q_ref, k_ref, v_ref, g_ref = refs[:6]
    refs = refs[6:]
    if has_init:
        sf0_ref, sb0_ref = refs[:2]
        refs = refs[2:]
    if has_prev:
        refs = refs[2:]
    o_ref = refs[0]
    refs = refs[1:]
    if emit_state:
        sf_ref, sb_ref = refs[:2]
        refs = refs[2:]
    oacc, dec_scr, wts_scr = refs

    c = RET_CHUNK
    n_chunks = seq_len // c
    pair = pl.program_id(0)
    lane = _lane_iota((c, LANES))
    zeros_half = jnp.zeros((RET_DK, RET_DV), F32)

    def log_gammas(half):
        head = 2 * pair + half
        return (_log_sigmoid(decf_ref[pl.ds(head, 1), :]),
                _log_sigmoid(decb_ref[pl.ds(head, 1), :]))

    @pl.when(pl.program_id(1) == 0)
    def _():
        ri = lax.broadcasted_iota(jnp.int32, (c, c), 0)
        ci = lax.broadcasted_iota(jnp.int32, (c, c), 1)
        rel = (ri - ci).astype(F32)
        row = lax.broadcasted_iota(jnp.int32, (c, LANES), 0).astype(F32)
        for half in range(2):
            lgf, lgb = log_gammas(half)
            dec_scr[half] = (jnp.where(rel >= 0, jnp.exp(lgf[:, 0:1] * jnp.maximum(rel, 0.0)), 0.0)
                             + jnp.where(rel <= 0, jnp.exp(lgb[:, 0:1] * jnp.maximum(-rel, 0.0)), 0.0))
            wts_scr[half, 0] = jnp.exp(lgf * (row + 1.0))
            wts_scr[half, 1] = jnp.exp(lgf * (c - 1.0 - row))
            wts_scr[half, 2] = jnp.exp(lgb * (c - row))
            wts_scr[half, 3] = jnp.exp(lgb * row)

    for half in range(2):
        lgf, lgb = log_gammas(half)
        cd_f = jnp.exp(lgf * float(c))
        cd_b = jnp.exp(lgb * float(c))
        hmask = _div_pow2(lane, RET_DK) == half
        vsl = slice(RET_DV * half, RET_DV * (half + 1))

        def load(n):
            rows = pl.ds(n * c if isinstance(n, int) else pl.multiple_of(n * c, c), c)
            qm = jnp.where(hmask, q_ref[rows, :].astype(F32), 0.0)
            kk = k_ref[rows, :].astype(F32) * (RET_DK ** -0.5)
            vb = v_ref[rows, vsl].astype(BF16)
            return rows, qm, kk, vb

        def init_state(s0_ref):
            if not has_init:
                return jnp.zeros((LANES, RET_DV), F32)
            s0 = s0_ref[half]
            return jnp.concatenate([s0, zeros_half] if half == 0 else [zeros_half, s0], axis=0)

        def fwd_step(n, s_f):
            rows, qm, kk, vb = load(n)
            s = lax.dot_general(qm.astype(BF16), kk.astype(BF16), NT_DIMS, preferred_element_type=F32)
            o = jnp.dot((s * dec_scr[half]).astype(BF16), vb, preferred_element_type=F32)
            if has_init or n_chunks > 1:
                o = o + jnp.dot((qm * wts_scr[half, 0]).astype(BF16), s_f.astype(BF16),
                                preferred_element_type=F32)
            oacc[rows, :] = o
            kv = lax.dot_general((kk * wts_scr[half, 1]).astype(BF16), vb, TN_DIMS,
                                 preferred_element_type=F32)
            return cd_f * s_f + kv

        def bwd_step(t, s_b):
            n = n_chunks - 1 - t
            rows, qm, kk, vb = load(n)
            o = oacc[rows, :]
            if has_init or n_chunks > 1:
                o = o + jnp.dot((qm * wts_scr[half, 2]).astype(BF16), s_b.astype(BF16),
                                preferred_element_type=F32)
            mu = jnp.mean(o, axis=-1, keepdims=True)
            d = o - mu
            var = jnp.mean(d * d, axis=-1, keepdims=True)
            o_ref[rows, vsl] = (_silu(g_ref[rows, vsl].astype(F32)) * (d * lax.rsqrt(var + LN_EPS))).astype(BF16)
            kv = lax.dot_general((kk * wts_scr[half, 3]).astype(BF16), vb, TN_DIMS,
                                 preferred_element_type=F32)
            return cd_b * s_b + kv

        s_f = init_state(sf0_ref if has_init else None)
        s_b = init_state(sb0_ref if has_init else None)
        if n_chunks == 1:
            s_f = fwd_step(0, s_f)
            s_b = bwd_step(0, s_b)
        else:
            s_f = lax.fori_loop(0, n_chunks, fwd_step, s_f)
            s_b = lax.fori_loop(0, n_chunks, bwd_step, s_b)
        if emit_state:
            sf_ref[half] = s_f[RET_DK * half:RET_DK * (half + 1), :]
            sb_ref[half] = s_b[RET_DK * half:RET_DK * (half + 1), :]


def _retention(zret, decf, decb, j, latent, state_f=None, state_b=None, prev=None):
    seq_len = DEC_SEQ if latent else SEQ
    n_b = DEC_BATCH if latent else BATCH
    row0 = NTOK_P // DEC_SEQ if latent else 0
    n_pairs = RET_HEADS // 2
    in_specs = [
        pl.BlockSpec((None, RET_HEADS, LANES), lambda p, b: (j, 0, 0)),
        pl.BlockSpec((None, RET_HEADS, LANES), lambda p, b: (j, 0, 0)),
        pl.BlockSpec((seq_len, LANES), lambda p, b: (row0 + b, p)),
        pl.BlockSpec((seq_len, LANES), lambda p, b: (row0 + b, 4 + p)),
        pl.BlockSpec((seq_len, 256), lambda p, b: (row0 + b, 4 + p)),
        pl.BlockSpec((seq_len, 256), lambda p, b: (row0 + b, 8 + p)),
    ]
    args = [decf, decb, zret, zret, zret, zret]
    out_specs = [pl.BlockSpec((seq_len, 256), lambda p, b: (b, p))]
    out_shape = [jax.ShapeDtypeStruct((n_b * seq_len, RET_HEADS * RET_DV), BF16)]
    st_spec = pl.BlockSpec((None, None, 2, RET_DK, RET_DV), lambda p, b: (b, j, p, 0, 0))
    aliases = {}
    if latent:
        in_specs += [st_spec, st_spec]
        args += [state_f, state_b]
    else:
        out_specs += [st_spec, st_spec]
        out_shape += [jax.ShapeDtypeStruct((BATCH, N_EVEN, RET_HEADS, RET_DK, RET_DV), F32)] * 2
        if prev is not None:
            aliases = {len(args): 1, len(args) + 1: 2}
            in_specs += [pl.BlockSpec(memory_space=pl.ANY)] * 2
            args += list(prev)
    return pl.pallas_call(
        functools.partial(_retention_kernel, seq_len, latent, not latent, bool(aliases)),
        grid=(n_pairs, n_b),
        in_specs=in_specs,
        out_specs=out_specs,
        out_shape=out_shape,
        scratch_shapes=[
            pltpu.VMEM((seq_len, RET_DV), F32),
            pltpu.VMEM((2, RET_CHUNK, RET_CHUNK), F32),
            pltpu.VMEM((2, 4, RET_CHUNK, LANES), F32),
        ],
        input_output_aliases=aliases,
        compiler_params=_params(("arbitrary", "arbitrary")),
        name="retention_latent" if latent else "retention_prompt",
    )(*args)


def _diff_prep_kernel(k_ref, v_ref, cos_ref, sin_ref, kr_ref, va_ref):
    lane = _lane_iota(cos_ref.shape)
    cos = cos_ref[...]
    sin = sin_ref[...]
    ones = jnp.ones(cos_ref.shape, BF16)
    for h in range(DIFF_HEADS):
        sl = slice(LANES * h, LANES * (h + 1))
        kr_ref[:, sl] = _rope128(k_ref[:, sl].astype(F32), cos, sin, lane, DIFF_DH // 2).astype(BF16)
        va_ref[:, 256 * h:256 * h + LANES] = v_ref[:, sl].astype(BF16)
        va_ref[:, 256 * h + LANES:256 * (h + 1)] = ones


def _diff_prep(zodd, cos_t, sin_t):
    tm = 512
    row0 = NTOK_P // tm
    nst = DEC_SEQ // tm
    return pl.pallas_call(
        _diff_prep_kernel,
        grid=(NTOK_S // tm,),
        in_specs=[
            pl.BlockSpec((tm, 1024), lambda i: (row0 + i, 1)),
            pl.BlockSpec((tm, 1024), lambda i: (row0 + i, 2)),
            pl.BlockSpec((tm, LANES), lambda i: (i % nst, 0)),
            pl.BlockSpec((tm, LANES), lambda i: (i % nst, 0)),
        ],
        out_specs=[pl.BlockSpec((tm, 1024), lambda i: (i, 0)), pl.BlockSpec((tm, 2048), lambda i: (i, 0))],
        out_shape=[jax.ShapeDtypeStruct((NTOK_S, 1024), BF16), jax.ShapeDtypeStruct((NTOK_S, 2048), BF16)],
        compiler_params=_params(("parallel",)),
        name="diff_rope_keys",
    )(zodd, zodd, cos_t, sin_t)


def _diff_attn_kernel(latent, lam_init, *refs):
    if latent:
        (lam_ref, ng_ref, q_ref, cos_ref, sin_ref, kc_ref, vc_ref, kn_ref, vn_ref, o_ref) = refs
    else:
        lam_ref, ng_ref, q_ref, k_ref, v_ref = refs[:5]
        o_ref, kout_ref, vout_ref = refs[-3:]
    tq = q_ref.shape[0]
    lane = _lane_iota((tq, LANES))
    scale = DIFF_DH ** -0.5
    lp = lam_ref[...]
    lam = (jnp.exp(jnp.sum(lp[0:1, :] * lp[1:2, :], axis=-1, keepdims=True))
           - jnp.exp(jnp.sum(lp[2:3, :] * lp[3:4, :], axis=-1, keepdims=True)) + lam_init)
    ng = ng_ref[...]
    ones = jnp.ones((PAST_LEN if latent else tq, LANES), BF16)
    for h in range(DIFF_HEADS):
        sl = slice(LANES * h, LANES * (h + 1))
        qh = q_ref[:, sl].astype(F32)
        if latent:
            qh = _rope128(qh, cos_ref[...], sin_ref[...], lane, DIFF_DH // 2)
            k_list = [kc_ref[h].astype(BF16), kn_ref[:, sl]]
            v_list = [jnp.concatenate([vc_ref[h].astype(BF16), ones], axis=1),
                      vn_ref[:, 256 * h:256 * (h + 1)]]
        else:
            kh = k_ref[:, sl]
            vh = v_ref[:, sl]
            kout_ref[h] = kh.astype(F32)
            vout_ref[h] = vh.astype(F32)
            k_list = [kh.astype(BF16)]
            v_list = [jnp.concatenate([vh.astype(BF16), ones], axis=1)]
        q1 = jnp.where(lane < DIFF_DH, qh, 0.0).astype(BF16)
        q2 = jnp.where(lane >= DIFF_DH, qh, 0.0).astype(BF16)
        s1 = [lax.dot_general(q1, kk, NT_DIMS, preferred_element_type=F32) for kk in k_list]
        s2 = [lax.dot_general(q2, kk, NT_DIMS, preferred_element_type=F32) for kk in k_list]
        o = _pv_normalised(_exp_parts(s1, scale), v_list) - lam * _pv_normalised(_exp_parts(s2, scale), v_list)
        y = o * lax.rsqrt(jnp.mean(o * o, axis=-1, keepdims=True) + RMS_EPS) * ng
        o_ref[:, sl] = (y * (1.0 - lam_init)).astype(BF16)


def _diff_attn_prompt(zodd, lam_p, norm_g, j, lam_init, prev):
    cache_shape = jax.ShapeDtypeStruct((BATCH, N_ODD, DIFF_HEADS, SEQ, LANES), F32)
    cache_spec = pl.BlockSpec((None, None, DIFF_HEADS, SEQ, LANES), lambda b: (b, j, 0, 0, 0))
    in_specs = [
        pl.BlockSpec((None, 4, DIFF_DH), lambda b: (j, 0, 0)),
        pl.BlockSpec((None, 1, DIFF_DV), lambda b: (j, 0, 0)),
        pl.BlockSpec((SEQ, 1024), lambda b: (b, 0)),
        pl.BlockSpec((SEQ, 1024), lambda b: (b, 1)),
        pl.BlockSpec((SEQ, 1024), lambda b: (b, 2)),
    ]
    args = [lam_p, norm_g.reshape(-1, 1, DIFF_DV), zodd, zodd, zodd]
    aliases = {}
    if prev is not None:
        aliases = {len(args): 1, len(args) + 1: 2}
        in_specs += [pl.BlockSpec(memory_space=pl.ANY)] * 2
        args += list(prev)
    return pl.pallas_call(
        functools.partial(_diff_attn_kernel, False, lam_init),
        grid=(BATCH,),
        in_specs=in_specs,
        out_specs=[pl.BlockSpec((SEQ, 1024), lambda b: (b, 0)), cache_spec, cache_spec],
        out_shape=[jax.ShapeDtypeStruct((NTOK_P, 1024), BF16), cache_shape, cache_shape],
        input_output_aliases=aliases,
        compiler_params=_params(("arbitrary",)),
        name="diff_attn_prompt",
    )(*args)


def _diff_attn_latent(zodd, lam_p, norm_g, cos_t, sin_t, cache_k, cache_v, k_rot, v_aug, j, lam_init):
    nq = DEC_SEQ // ATT_TQ
    row0 = NTOK_P // ATT_TQ
    ctx_spec = pl.BlockSpec((None, None, DIFF_HEADS, PAST_LEN, LANES), lambda b, q: (b, j, 0, 0, 0))
    return pl.pallas_call(
        functools.partial(_diff_attn_kernel, True, lam_init),
        grid=(DEC_BATCH, nq),
        in_specs=[
            pl.BlockSpec((None, 4, DIFF_DH), lambda b, q: (j, 0, 0)),
            pl.BlockSpec((None, 1, DIFF_DV), lambda b, q: (j, 0, 0)),
            pl.BlockSpec((ATT_TQ, 1024), lambda b, q: (row0 + b * nq + q, 0)),
            pl.BlockSpec((ATT_TQ, LANES), lambda b, q: (q, 0)),
            pl.BlockSpec((ATT_TQ, LANES), lambda b, q: (q, 0)),
            ctx_spec,
            ctx_spec,
            pl.BlockSpec((DEC_SEQ, 1024), lambda b, q: (b, 0)),
            pl.BlockSpec((DEC_SEQ, 2048), lambda b, q: (b, 0)),
        ],
        out_specs=pl.BlockSpec((ATT_TQ, 1024), lambda b, q: (b * nq + q, 0)),
        out_shape=jax.ShapeDtypeStruct((NTOK_S, 1024), BF16),
        compiler_params=_params(("parallel", "arbitrary")),
        name="diff_attn_latent",
    )(lam_p, norm_g.reshape(-1, 1, DIFF_DV), zodd, cos_t, sin_t, cache_k, cache_v, k_rot, v_aug)


def _mla_weight(w_in):
    base = RET_COLS
    mq = w_in[:, base:base + MLA_HEADS * (MLA_NOPE + MLA_ROPE)].reshape(D_MODEL, MLA_HEADS, MLA_NOPE + MLA_ROPE)
    qn = mq[:, :, :MLA_NOPE].reshape(D_MODEL, MLA_HEADS * MLA_NOPE)
    qr = mq[:, :, MLA_NOPE:].reshape(D_MODEL, MLA_HEADS * MLA_ROPE)
    ckv0 = base + MLA_HEADS * (MLA_NOPE + MLA_ROPE)
    ckv = w_in[:, ckv0:ckv0 + MLA_KV_RANK]
    kr = w_in[:, ckv0 + MLA_KV_RANK:]
    return jnp.concatenate([qn, qr, ckv, jnp.tile(kr, (1, LANES // MLA_ROPE))], axis=1)


def kernel(x_prompt, x_sample, state_ret_fwd, state_ret_bwd, cache_mla_ckv, cache_mla_krope, cache_diff_k, cache_diff_v, c, c_ctx, ada_w, ada_b, ln1_g, ln1_b, ln2_g, ln2_b, ev_w_in, ev_w_out, ret_decay_fwd, ret_decay_bwd, mla_kv_norm_g, mla_w_uk, mla_w_uv, od_w_in, od_w_out, diff_lambda, diff_norm_g, moe_w_group, moe_b_group, moe_w_expert, moe_b_expert, moe_w1, moe_w3, moe_w2):
    x = (x_prompt.reshape(NTOK_P, D_MODEL), x_sample.reshape(NTOK_S, D_MODEL), 0)
    cond =jnp.concatenate([c_ctx[None, :], c, jnp.zeros((N_COND - 1 - DEC_BATCH, D_MODEL), F32)], axis=0)
    mods = _ada_all(cond, ada_w, ada_b).reshape(DEPTH, N_COND, 6, D_MODEL)

    cos_m, sin_m = _rope_tables(MLA_ROPE)
    cos_d, sin_d = _rope_tables(DIFF_DH)
    ident = 512
    cos_m_id = jnp.concatenate([jnp.ones((ident, LANES), F32), cos_m], axis=0)
    sin_m_id = jnp.concatenate([jnp.zeros((ident, LANES), F32), sin_m], axis=0)
    decf = jnp.broadcast_to(ret_decay_fwd[:, :, None], ret_decay_fwd.shape + (LANES,))
    decb = jnp.broadcast_to(ret_decay_bwd[:, :, None], ret_decay_bwd.shape + (LANES,))

    pad = LANES - MOE_GROUPS - MOE_EXPERTS
    ret_states = mla_caches = diff_caches = None
    for i in range(DEPTH):
        j = i // 2
        mod = mods[i]
        w_router = jnp.concatenate([moe_w_group[i], moe_w_expert[i], jnp.zeros((D_MODEL, pad), F32)], axis=1)
        b_router = jnp.concatenate([moe_b_group[i], moe_b_expert[i], jnp.zeros((pad,), F32)])[None, :]
        if i % 2 == 0:
            zret = _mm_mod(x, mod, ev_w_in, (j,), RET_COLS, 1024, "in_proj_retention")
            zmla = _mm_mod(x, mod, _mla_weight(ev_w_in[j]), (), MLA_COLS, MLA_COLS, "in_proj_mla")
            *mla_caches, kcat, vm = _mla_kv_new(zmla, cos_m_id, sin_m_id, mla_kv_norm_g, mla_w_uk, mla_w_uv,
                                                j, mla_caches)
            kr_ctx = jnp.tile(cache_mla_krope[:, j], (1, 1, LANES // MLA_ROPE))
            kcat_ctx, vm_ctx = _mla_kv_ctx(cache_mla_ckv, kr_ctx, mla_w_uk, mla_w_uv, j)
            a_ret_p, *ret_states = _retention(zret, decf, decb, j, False, prev=ret_states)
            (a_ret_s,) = _retention(zret, decf, decb, j, True, state_ret_fwd, state_ret_bwd)
            a_mla_p = _mla_attn_prompt(zmla, kcat, vm)
            a_mla_s = _mla_attn_latent(zmla, cos_m, sin_m, kcat_ctx, vm_ctx, kcat, vm)
            x1, meta, counts = _mm_ln([(a_ret_p, a_ret_s), (a_mla_p, a_mla_s)], ev_w_out, j, x, mod,
                                      ln1_g, ln1_b, w_router, b_router, i)
        else:
            lam_init = 0.8 - 0.6 * math.exp(-0.3 * i)
            zodd = _mm_mod(x, mod, od_w_in, (j,), 3072, 1024, "in_proj_diff")
            a_p, *diff_caches = _diff_attn_prompt(zodd, diff_lambda, diff_norm_g, j, lam_init, diff_caches)
            k_rot, v_aug = _diff_prep(zodd, cos_d, sin_d)
            a_s = _diff_attn_latent(zodd, diff_lambda, diff_norm_g, cos_d, sin_d, cache_diff_k, cache_diff_v,
                                    k_rot, v_aug, j, lam_init)
            x1, meta, counts = _mm_ln([(a_p, a_s)], od_w_out, j, x, mod, ln1_g, ln1_b, w_router, b_router, i)
        y = _moe(x1, _moe_plan(meta, counts), mods, w_router, b_router, moe_w1, moe_w3, moe_w2, ln2_g, ln2_b, i)
        x = (y, y, NTOK_P)

    y_prompt = x[0][:NTOK_P].reshape(BATCH, SEQ, D_MODEL)
    y_sample = x[1][NTOK_P:NTOK].reshape(DEC_BATCH, DEC_SEQ, D_MODEL)
    return (y_prompt, y_sample, ret_states[0], ret_states[1], mla_caches[0], mla_caches[1],
            diff_caches[0], diff_caches[1])
```

```python
import functools
import math

import jax
import jax.numpy as jnp
from jax import lax
from jax.experimental import pallas as pl
from jax.experimental.pallas import tpu as pltpu

D_MODEL = 1024
BATCH = 32
SEQ = 256
DEPTH = 4
N_EVEN = 2
N_ODD = 2
DEC_BATCH = 2
DEC_SEQ = 2048
PAST_LEN = 256
GRID_W = 64
LN_EPS = 1e-5
RMS_EPS = 1e-6
DEEPNORM_ALPHA = (2.0 * DEPTH) ** 0.25
ROPE_BASE = 10000.0
RET_HEADS = 8
RET_DK = 64
RET_DV = 128
MLA_HEADS = 8
MLA_NOPE = 64
MLA_ROPE = 32
MLA_DV = 64
MLA_KV_RANK = 256
DIFF_HEADS = 8
DIFF_DH = 64
DIFF_DV = 128
MOE_GROUPS = 4
MOE_PER_GROUP = 4
MOE_EXPERTS = 16
MOE_FF = 256

NTOK_P = BATCH * SEQ
NTOK_S = DEC_BATCH * DEC_SEQ
NTOK = NTOK_P + NTOK_S
N_COND = 8
LANES = 128
SUBLANES = 8
RET_COLS = 3072
MLA_COLS = 1152
ATT_TQ = 256
RET_CHUNK = 256
MOE_TILE = 512
MOE_TILES = (NTOK + MOE_GROUPS * (MOE_TILE - 1)) // MOE_TILE
MOE_ROWS = MOE_TILES * MOE_TILE
VMEM_LIMIT = 56 * 1024 * 1024
LOG2E = 1.4426950408889634

F32 = jnp.float32
BF16 = jnp.bfloat16
NT_DIMS = (((1,), (1,)), ((), ()))
TN_DIMS = (((0,), (0,)), ((), ()))


def _params(sem):
    return pltpu.CompilerParams(dimension_semantics=sem, vmem_limit_bytes=VMEM_LIMIT)


def _group_of_tile(i, tm):
    npt = NTOK_P // tm
    nst = DEC_SEQ // tm
    return jnp.where(i < npt, 0, 1 + (i - npt) // nst)


def _split_specs(tm, width, m_of, s_row0=0):
    npt = NTOK_P // tm
    s_blk0 = s_row0 // tm
    return [pl.BlockSpec((tm, width), lambda *g: (jnp.minimum(m_of(*g), npt - 1), 0)),
            pl.BlockSpec((tm, width), lambda *g: (jnp.maximum(m_of(*g) - npt, 0) + s_blk0, 0))]


def _read_split(p_ref, s_ref, m):
    return jnp.where(m < NTOK_P // p_ref.shape[0], p_ref[...], s_ref[...])


def _silu(x):
    return x * (1.0 / (1.0 + jnp.exp(-x)))


def _layer_norm(r, g, b):
    mu = jnp.mean(r, axis=-1, keepdims=True)
    d = r - mu
    var = jnp.mean(d * d, axis=-1, keepdims=True)
    return d * lax.rsqrt(var + LN_EPS) * g + b


def _lane_iota(shape):
    return lax.broadcasted_iota(jnp.int32, shape, 1)


def _div_pow2(x, d):
    assert d & (d - 1) == 0
    return jnp.right_shift(x, d.bit_length() - 1)


def _mod_pow2(x, d):
    assert d & (d - 1) == 0
    return jnp.bitwise_and(x, d - 1)


def _ada_kernel(c_ref, w_ref, b_ref, o_ref):
    h = _silu(c_ref[...]).astype(BF16)
    o_ref[...] = jnp.dot(h, w_ref[...].astype(BF16), preferred_element_type=F32) + b_ref[...]


def _ada_all(cond, ada_w, ada_b):
    tn = 768
    return pl.pallas_call(
        _ada_kernel,
        grid=(DEPTH, 6 * D_MODEL // tn),
        in_specs=[
            pl.BlockSpec((N_COND, D_MODEL), lambda l, n: (0, 0)),
            pl.BlockSpec((None, D_MODEL, tn), lambda l, n: (l, 0, n)),
            pl.BlockSpec((None, 1, tn), lambda l, n: (l, 0, n)),
        ],
        out_specs=pl.BlockSpec((None, N_COND, tn), lambda l, n: (l, 0, n)),
        out_shape=jax.ShapeDtypeStruct((DEPTH, N_COND, 6 * D_MODEL), F32),
        compiler_params=_params(("parallel", "parallel")),
        name="ada_modulation",
    )(cond, ada_w, ada_b.reshape(DEPTH, 1, 6 * D_MODEL))


def _mm_mod_kernel(xp_ref, xs_ref, mod_ref, w_ref, o_ref, wscr):
    m = pl.program_id(1)

    @pl.when(m == 0)
    def _():
        wscr[...] = w_ref[...].astype(BF16)

    sh = mod_ref[0:1, :]
    sc = mod_ref[1:2, :]
    xm = (_read_split(xp_ref, xs_ref, m) * (1.0 + sc) + sh).astype(BF16)
    o_ref[...] = jnp.dot(xm, wscr[...], preferred_element_type=F32).astype(o_ref.dtype)


def _mm_mod(x, mod, w, w_index, n_cols, tn, name):
    tm = 1024
    w_block = (None,) * len(w_index) + (D_MODEL, tn)
    return pl.pallas_call(
        _mm_mod_kernel,
        grid=(n_cols // tn, NTOK // tm),
        in_specs=_split_specs(tm, D_MODEL, lambda n, m: m, x[2]) + [
            pl.BlockSpec((None, 6, D_MODEL), lambda n, m: (_group_of_tile(m, tm), 0, 0)),
            pl.BlockSpec(w_block, lambda n, m: tuple(w_index) + (0, n)),
        ],
        out_specs=pl.BlockSpec((tm, tn), lambda n, m: (m, n)),
        out_shape=jax.ShapeDtypeStruct((NTOK, n_cols), BF16),
        scratch_shapes=[pltpu.VMEM((D_MODEL, tn), BF16)],
        compiler_params=_params(("arbitrary", "arbitrary")),
        name=name,
    )(x[0], x[1], mod, w)


def _router_probs(xm, wr_ref, br_ref):
    rows = xm.shape[0]
    z = jnp.dot(xm, wr_ref[...].astype(BF16), preferred_element_type=F32) + br_ref[...]
    lane_i = _lane_iota((rows, LANES))
    lane = lane_i.astype(F32)
    gmask = lane_i < MOE_GROUPS
    zg = jnp.where(gmask, z, -jnp.inf)
    pg = jnp.exp(zg - jnp.max(zg, axis=-1, keepdims=True))
    g_prob = pg / jnp.sum(pg, axis=-1, keepdims=True)
    g_p = jnp.max(g_prob, axis=-1, keepdims=True)
    g_idx = jnp.min(jnp.where(gmask & (g_prob == g_p), lane, float(LANES)), axis=-1, keepdims=True)
    return z, lane_i, lane, g_p, g_idx


def _mm_ln_kernel(k_sizes, *refs):
    n_a = len(k_sizes)
    a_refs = refs[:2 * n_a]
    (w_ref, xp_ref, xs_ref, mod_ref, g_ref, b_ref, wr_ref, br_ref,
     o_ref, meta_ref, cnt_ref, wscr, tri_scr, carry_scr) = refs[2 * n_a:]
    m = pl.program_id(0)
    tm = o_ref.shape[0]

    @pl.when(m == 0)
    def _():
        wscr[...] = w_ref[...].astype(BF16)
        ri = lax.broadcasted_iota(jnp.int32, (tm, tm), 0)
        ci = lax.broadcasted_iota(jnp.int32, (tm, tm), 1)
        tri_scr[...] = jnp.where(ci < ri, 1.0, 0.0).astype(BF16)
        carry_scr[...] = jnp.zeros_like(carry_scr)

    y = None
    k0 = 0
    for i, ks in enumerate(k_sizes):
        a = _read_split(a_refs[2 * i], a_refs[2 * i + 1], m)
        part = jnp.dot(a, wscr[k0:k0 + ks, :], preferred_element_type=F32)
        y = part if y is None else y + part
        k0 += ks
    gate = mod_ref[2:3, :]
    r = DEEPNORM_ALPHA * _read_split(xp_ref, xs_ref, m) + gate * y
    x1 = _layer_norm(r, g_ref[...], b_ref[...])
    o_ref[...] = x1

    xm = (x1 * (1.0 + mod_ref[4:5, :]) + mod_ref[3:4, :]).astype(BF16)
    _, lane_i, lane, _, g_idx = _router_probs(xm, wr_ref, br_ref)
    onehot = jnp.where(lane == g_idx, 1.0, 0.0)
    before = jnp.dot(tri_scr[...], onehot.astype(BF16), preferred_element_type=F32) + carry_scr[0:1, :]
    rank = jnp.sum(jnp.where(lane == g_idx, before, 0.0), axis=-1, keepdims=True)
    meta_ref[...] = jnp.where(lane_i == 0, g_idx, jnp.where(lane_i == 1, rank, 0.0))
    total = carry_scr[0:1, :] + jnp.sum(onehot, axis=0, keepdims=True)
    carry_scr[...] = jnp.broadcast_to(total, carry_scr.shape)
    cnt_ref[...] = jnp.broadcast_to(total, cnt_ref.shape)


def _mm_ln(a_pairs, w, j, x, mod, ln_g, ln_b, w_router, b_router, layer):
    tm = 512
    k_sizes = tuple(ap.shape[1] for ap, _ in a_pairs)
    k_tot = sum(k_sizes)
    in_specs = []
    args = []
    for (ap, a_s), ks in zip(a_pairs, k_sizes):
        in_specs += _split_specs(tm, ks, lambda m: m)
        args += [ap, a_s]
    in_specs += [pl.BlockSpec((None, k_tot, D_MODEL), lambda m: (j, 0, 0))]
    in_specs += _split_specs(tm, D_MODEL, lambda m: m, x[2])
    in_specs += [
        pl.BlockSpec((None, 6, D_MODEL), lambda m: (_group_of_tile(m, tm), 0, 0)),
        pl.BlockSpec((None, 1, D_MODEL), lambda m: (layer, 0, 0)),
        pl.BlockSpec((None, 1, D_MODEL), lambda m: (layer, 0, 0)),
        pl.BlockSpec((D_MODEL, LANES), lambda m: (0, 0)),
        pl.BlockSpec((1, LANES), lambda m: (0, 0)),
    ]
    return pl.pallas_call(
        functools.partial(_mm_ln_kernel, k_sizes),
        grid=(NTOK // tm,),
        in_specs=in_specs,
        out_specs=[
            pl.BlockSpec((tm, D_MODEL), lambda m: (m, 0)),
            pl.BlockSpec((tm, LANES), lambda m: (m, 0)),
            pl.BlockSpec((N_COND, LANES), lambda m: (0, 0)),
        ],
        out_shape=[
            jax.ShapeDtypeStruct((NTOK, D_MODEL), F32),
            jax.ShapeDtypeStruct((NTOK, LANES), F32),
            jax.ShapeDtypeStruct((N_COND, LANES), F32),
        ],
        scratch_shapes=[
            pltpu.VMEM((k_tot, D_MODEL), BF16),
            pltpu.VMEM((tm, tm), BF16),
            pltpu.VMEM((N_COND, LANES), F32),
        ],
        compiler_params=_params(("arbitrary",)),
        name="out_proj_ln",
    )(*args, w, x[0], x[1], mod, ln_g.reshape(DEPTH, 1, D_MODEL), ln_b.reshape(DEPTH, 1, D_MODEL),
      w_router, b_router)


def _moe_kernel(tgrp_ref, ntile_ref, src0_ref, src1_ref, dst_ref, mid_ref, x_hbm, mod_ref, wr_ref, br_ref,
                w1_ref, w3_ref, w2_ref, g_ref, b_ref, y_hbm,
                gbuf, obuf, gsem, ssem, w13s, w2s):
    i = pl.program_id(0)
    n_steps = pl.num_programs(0)
    n_tiles = ntile_ref[0]
    n_blk = gbuf.shape[1]
    ts = n_blk * SUBLANES
    slot = lax.rem(i, 2)

    def start_gather(src_ref, s):
        def body(k, carry):
            for u in range(SUBLANES):
                tok = src_ref[0, k * SUBLANES + u]
                pltpu.make_async_copy(x_hbm.at[pl.ds(tok, 1), :], gbuf.at[s, k, pl.ds(u, 1), :],
                                      gsem.at[s]).start()
            return carry
        lax.fori_loop(0, n_blk, body, 0)

    def wait_gather(s):
        pltpu.make_async_copy(gbuf.at[s], gbuf.at[s], gsem.at[s]).wait()

    def start_scatter(s):
        def body(k, carry):
            for u in range(SUBLANES):
                tok = dst_ref[0, k * SUBLANES + u]
                pltpu.make_async_copy(obuf.at[s, k, pl.ds(u, 1), :], y_hbm.at[pl.ds(tok, 1), :],
                                      ssem.at[s]).start()
            return carry
        lax.fori_loop(0, n_blk, body, 0)

    def wait_scatter(s):
        pltpu.make_async_copy(obuf.at[s], obuf.at[s], ssem.at[s]).wait()

    @pl.when(i == 0)
    def _():
        start_gather(src0_ref, 0)

    @pl.when(i < n_tiles)
    def _():
        grp = tgrp_ref[i]
        wait_gather(slot)

        @pl.when(i + 1 < n_tiles)
        def _():
            start_gather(src1_ref, 1 - slot)

        @pl.when((i == 0) | (grp != tgrp_ref[jnp.maximum(i - 1, 0)]))
        def _():
            w13s[:, :, :MOE_FF] = w1_ref[...].astype(BF16)
            w13s[:, :, MOE_FF:] = w3_ref[...].astype(BF16)
            w2s[...] = w2_ref[...].astype(BF16)

        mid_rows = mid_ref[...].astype(F32)
        spread = jnp.concatenate([jnp.broadcast_to(mid_rows[a:a + 1, :], (LANES, LANES))
                                  for a in range(ts // LANES)], axis=0)
        row_i = lax.broadcasted_iota(jnp.int32, (ts, LANES), 0)
        mid = jnp.sum(jnp.where(_lane_iota((ts, LANES)) == _mod_pow2(row_i, LANES), spread, 0.0),
                      axis=-1, keepdims=True)

        def mod_row(k):
            return jnp.where(mid == 0, mod_ref[0, k:k + 1, :],
                             jnp.where(mid == 1, mod_ref[1, k:k + 1, :], mod_ref[2, k:k + 1, :]))

        x1 = gbuf[slot].reshape(ts, D_MODEL)
        xm = (x1 * (1.0 + mod_row(4)) + mod_row(3)).astype(BF16)
        z, lane_i, lane, g_p, _ = _router_probs(xm, wr_ref, br_ref)
        e0 = MOE_GROUPS + MOE_PER_GROUP * grp
        emask = (lane_i >= e0) & (lane_i < e0 + MOE_PER_GROUP)
        ze = jnp.where(emask, z, -jnp.inf)
        pe = jnp.exp(ze - jnp.max(ze, axis=-1, keepdims=True))
        e_prob = pe / jnp.sum(pe, axis=-1, keepdims=True)
        cand = jnp.where(emask, e_prob, -1.0)
        p1 = jnp.max(cand, axis=-1, keepdims=True)
        i1 = jnp.min(jnp.where(cand == p1, lane, float(LANES)), axis=-1, keepdims=True)
        cand2 = jnp.where(lane == i1, -1.0, cand)
        p2 = jnp.max(cand2, axis=-1, keepdims=True)
        i2 = jnp.min(jnp.where(cand2 == p2, lane, float(LANES)), axis=-1, keepdims=True)
        denom = p1 + p2
        comb = jnp.where(lane == i1, g_p * p1 / denom, 0.0) + jnp.where(lane == i2, g_p * p2 / denom, 0.0)
        y = None
        for e in range(MOE_PER_GROUP):
            c = jnp.sum(jnp.where(lane_i == e0 + e, comb, 0.0), axis=-1, keepdims=True)
            h = jnp.dot(xm, w13s[e], preferred_element_type=F32)
            hid = (_silu(h[:, :MOE_FF]) * h[:, MOE_FF:] * c).astype(BF16)
            part = jnp.dot(hid, w2s[e], preferred_element_type=F32)
            y = part if y is None else y + part
        r = DEEPNORM_ALPHA * x1 + mod_row(5) * y

        @pl.when(i >= 2)
        def _():
            wait_scatter(slot)

        obuf[slot] = _layer_norm(r, g_ref[...], b_ref[...]).reshape(n_blk, SUBLANES, D_MODEL)
        start_scatter(slot)

    @pl.when(i == n_steps - 1)
    def _():
        @pl.when(n_tiles >= 2)
        def _():
            wait_scatter(lax.rem(n_tiles, 2))

        @pl.when(n_tiles >= 1)
        def _():
            wait_scatter(lax.rem(n_tiles + 1, 2))


def _inverse_perm_kernel(pos_ref, out_ref):
    def clear(s, carry):
        out_ref[s] = 0
        return carry

    def place(t, carry):
        out_ref[pos_ref[t]] = t + 1
        return carry

    lax.fori_loop(0, out_ref.shape[0], clear, 0, unroll=8)
    lax.fori_loop(0, pos_ref.shape[0], place, 0, unroll=8)


def _inverse_perm(pos):
    return pl.pallas_call(
        _inverse_perm_kernel,
        in_specs=[pl.BlockSpec(memory_space=pltpu.SMEM)],
        out_specs=pl.BlockSpec(memory_space=pltpu.SMEM),
        out_shape=jax.ShapeDtypeStruct((MOE_ROWS,), jnp.int32),
        name="moe_inverse_perm",
    )(pos)


def _moe_plan(meta, counts):
    ts = MOE_TILE
    cnt = counts[0, :MOE_GROUPS].astype(jnp.int32)
    tiles_g = (cnt + ts - 1) // ts
    tile_end = jnp.cumsum(tiles_g)
    row0_g = (tile_end - tiles_g) * ts
    gid = meta[:, 0].astype(jnp.int32)
    rank = meta[:, 1].astype(jnp.int32)
    pos = row0_g[gid] + rank
    tok1 = _inverse_perm(pos)
    rows = jnp.arange(MOE_ROWS, dtype=jnp.int32)
    src = jnp.maximum(tok1 - 1, 0)
    spare = NTOK + ((rows // ts) % 2) * ts + rows % ts
    dst = jnp.where(tok1 > 0, src, spare)
    mid = jnp.where(src < NTOK_P, 0, 1 + (src - NTOK_P) // DEC_SEQ)
    tile_group = jnp.minimum(jnp.sum(jnp.arange(MOE_TILES)[:, None] >= tile_end[None, :], axis=1),
                             MOE_GROUPS - 1).astype(jnp.int32)
    n_tiles = tile_end[-1:].astype(jnp.int32)
    return (tile_group, n_tiles, src.reshape(MOE_TILES, 1, ts), dst.reshape(MOE_TILES, 1, ts),
            mid.reshape(MOE_TILES, ts // LANES, LANES))


def _moe(x1, plan, mods, w_router, b_router, w1, w3, w2, ln_g, ln_b, layer):
    ts = MOE_TILE
    tile_group, n_tiles, src, dst, mid = plan
    grp_shape = (DEPTH, MOE_GROUPS, MOE_PER_GROUP)
    w_in_spec = pl.BlockSpec((None, None, MOE_PER_GROUP, D_MODEL, MOE_FF), lambda i, tg, nt: (layer, tg[i], 0, 0, 0))
    w_out_spec = pl.BlockSpec((None, None, MOE_PER_GROUP, MOE_FF, D_MODEL), lambda i, tg, nt: (layer, tg[i], 0, 0, 0))
    smem_tile = functools.partial(pl.BlockSpec, (None, 1, ts), memory_space=pltpu.SMEM)
    grid_spec = pltpu.PrefetchScalarGridSpec(
        num_scalar_prefetch=2,
        grid=(MOE_TILES,),
        in_specs=[
            smem_tile(lambda i, tg, nt: (i, 0, 0)),
            smem_tile(lambda i, tg, nt: (jnp.minimum(i + 1, MOE_TILES - 1), 0, 0)),
            smem_tile(lambda i, tg, nt: (i, 0, 0)),
            pl.BlockSpec((None, ts // LANES, LANES), lambda i, tg, nt: (i, 0, 0)),
            pl.BlockSpec(memory_space=pl.ANY),
            pl.BlockSpec((None, N_COND, 6, D_MODEL), lambda i, tg, nt: (layer, 0, 0, 0)),
            pl.BlockSpec((D_MODEL, LANES), lambda i, tg, nt: (0, 0)),
            pl.BlockSpec((1, LANES), lambda i, tg, nt: (0, 0)),
            w_in_spec,
            w_in_spec,
            w_out_spec,
            pl.BlockSpec((None, 1, D_MODEL), lambda i, tg, nt: (layer, 0, 0)),
            pl.BlockSpec((None, 1, D_MODEL), lambda i, tg, nt: (layer, 0, 0)),
        ],
        out_specs=pl.BlockSpec(memory_space=pl.ANY),
        scratch_shapes=[
            pltpu.VMEM((2, ts // SUBLANES, SUBLANES, D_MODEL), F32),
            pltpu.VMEM((2, ts // SUBLANES, SUBLANES, D_MODEL), F32),
            pltpu.SemaphoreType.DMA((2,)),
            pltpu.SemaphoreType.DMA((2,)),
            pltpu.VMEM((MOE_PER_GROUP, D_MODEL, 2 * MOE_FF), BF16),
            pltpu.VMEM((MOE_PER_GROUP, MOE_FF, D_MODEL), BF16),
        ],
    )
    return pl.pallas_call(
        _moe_kernel,
        grid_spec=grid_spec,
        out_shape=jax.ShapeDtypeStruct((NTOK + 2 * ts, D_MODEL), F32),
        compiler_params=_params(("arbitrary",)),
        name="hier_moe_ln",
    )(tile_group, n_tiles, src, src, dst, mid, x1, mods, w_router, b_router,
      w1.reshape(grp_shape + (D_MODEL, MOE_FF)), w3.reshape(grp_shape + (D_MODEL, MOE_FF)),
      w2.reshape(grp_shape + (MOE_FF, D_MODEL)),
      ln_g.reshape(DEPTH, 1, D_MODEL), ln_b.reshape(DEPTH, 1, D_MODEL))


def _swap_halves(x, lane, half):
    return jnp.where(_mod_pow2(lane, 2 * half) < half,
                     pltpu.roll(x, LANES - half, 1), pltpu.roll(x, half, 1))


def _rope128(x, cos, sin_signed, lane, half):
    return x * cos + _swap_halves(x, lane, half) * sin_signed


def _rope_tables(rot_dim):
    rows = DEC_SEQ // GRID_W
    row = jnp.repeat(jnp.arange(rows, dtype=F32), GRID_W)
    col = jnp.tile(jnp.arange(GRID_W, dtype=F32), rows)
    n_freq = rot_dim // 4
    inv_freq = ROPE_BASE ** (-jnp.arange(n_freq, dtype=F32) / n_freq)
    ang = jnp.concatenate([row[:, None] * inv_freq, col[:, None] * inv_freq], axis=-1)
    cos, sin = jnp.cos(ang), jnp.sin(ang)
    reps = LANES // rot_dim
    cos_full = jnp.tile(jnp.concatenate([cos, cos], axis=-1), (1, reps))
    sin_signed = jnp.tile(jnp.concatenate([-sin, sin], axis=-1), (1, reps))
    return cos_full, sin_signed


def _exp_parts(s_list, scale):
    m = None
    for s in s_list:
        sm = jnp.max(s, axis=-1, keepdims=True)
        m = sm if m is None else jnp.maximum(m, sm)
    return [jnp.exp2((s - m) * (scale * LOG2E)).astype(BF16) for s in s_list]


def _pv_normalised(p_list, v_list):
    o = None
    for p, v in zip(p_list, v_list):
        part = jnp.dot(p, v, preferred_element_type=F32)
        o = part if o is None else o + part
    return o[:, :LANES] / o[:, LANES:]


def _mla_kv_kernel(new_tokens, *refs):
    if new_tokens:
        (ckv_ref, kr_ref, cos_ref, sin_ref, g_ref, wuk_ref, wuv_ref) = refs[:7]
        ckvc_ref, krc_ref, kcat_ref, vm_ref = refs[-4:]
        i = pl.program_id(0)
        x = ckv_ref[...].astype(F32)
        c = x * lax.rsqrt(jnp.mean(x * x, axis=-1, keepdims=True) + RMS_EPS) * g_ref[...]
        kr_raw = kr_ref[...].astype(F32)

        @pl.when(i < NTOK_P // ckv_ref.shape[0])
        def _():
            for b in range(ckvc_ref.shape[0]):
                ckvc_ref[b] = c[SEQ * b:SEQ * (b + 1), :]
                krc_ref[b] = kr_raw[SEQ * b:SEQ * (b + 1), :MLA_ROPE]

        lane = _lane_iota(kr_ref.shape)
        kr = _rope128(kr_raw, cos_ref[...], sin_ref[...], lane, MLA_ROPE // 2)
    else:
        ckv_ref, kr_ref, wuk_ref, wuv_ref, kcat_ref, vm_ref = refs
        c = ckv_ref[...]
        kr = kr_ref[...]
    cb = c.astype(BF16)
    kn = jnp.dot(cb, wuk_ref[...].astype(BF16), preferred_element_type=F32).astype(BF16)
    vv = jnp.dot(cb, wuv_ref[...].astype(BF16), preferred_element_type=F32).astype(BF16)
    krb = kr.astype(BF16)
    ones = jnp.ones((c.shape[0], LANES), BF16)
    for p in range(MLA_HEADS // 2):
        kcat_ref[:, 256 * p:256 * p + LANES] = kn[:, LANES * p:LANES * (p + 1)]
        kcat_ref[:, 256 * p + LANES:256 * (p + 1)] = krb
        vm_ref[:, 256 * p:256 * p + LANES] = vv[:, LANES * p:LANES * (p + 1)]
        vm_ref[:, 256 * p + LANES:256 * (p + 1)] = ones


def _mla_kv_new(zmla, cos_t, sin_t, kv_norm_g, w_uk, w_uv, j, prev):
    tm = 512
    npt = NTOK_P // tm
    nst = DEC_SEQ // tm
    nb = tm // SEQ

    def tab(i):
        return (jnp.where(i < npt, 0, 1 + (i - npt) % nst), 0)

    def cache_idx(i):
        return (jnp.minimum(i, npt - 1), j, 0, 0)

    in_specs = [
        pl.BlockSpec((tm, MLA_KV_RANK), lambda i: (i, 768 // MLA_KV_RANK)),
        pl.BlockSpec((tm, LANES), lambda i: (i, 1024 // LANES)),
        pl.BlockSpec((tm, LANES), tab),
        pl.BlockSpec((tm, LANES), tab),
        pl.BlockSpec((None, 1, MLA_KV_RANK), lambda i: (j, 0, 0)),
        pl.BlockSpec((None, MLA_KV_RANK, 512), lambda i: (j, 0, 0)),
        pl.BlockSpec((None, MLA_KV_RANK, 512), lambda i: (j, 0, 0)),
    ]
    args = [zmla, zmla, cos_t, sin_t, kv_norm_g.reshape(-1, 1, MLA_KV_RANK), w_uk, w_uv]
    aliases = {}
    if prev is not None:
        aliases = {len(args): 0, len(args) + 1: 1}
        in_specs += [pl.BlockSpec(memory_space=pl.ANY)] * 2
        args += list(prev)
    return pl.pallas_call(
        functools.partial(_mla_kv_kernel, True),
        grid=(NTOK // tm,),
        in_specs=in_specs,
        out_specs=[
            pl.BlockSpec((nb, None, SEQ, MLA_KV_RANK), cache_idx),
            pl.BlockSpec((nb, None, SEQ, MLA_ROPE), cache_idx),
            pl.BlockSpec((tm, 1024), lambda i: (i, 0)),
            pl.BlockSpec((tm, 1024), lambda i: (i, 0)),
        ],
        out_shape=[
            jax.ShapeDtypeStruct((BATCH, N_EVEN, SEQ, MLA_KV_RANK), F32),
            jax.ShapeDtypeStruct((BATCH, N_EVEN, SEQ, MLA_ROPE), F32),
            jax.ShapeDtypeStruct((NTOK, 1024), BF16),
            jax.ShapeDtypeStruct((NTOK, 1024), BF16),
        ],
        input_output_aliases=aliases,
        compiler_params=_params(("arbitrary",)),
        name="mla_kv_new",
    )(*args)


def _mla_kv_ctx(cache_ckv, kr_tiled, w_uk, w_uv, j):
    return pl.pallas_call(
        functools.partial(_mla_kv_kernel, False),
        grid=(DEC_BATCH,),
        in_specs=[
            pl.BlockSpec((None, None, PAST_LEN, MLA_KV_RANK), lambda b: (b, j, 0, 0)),
            pl.BlockSpec((None, PAST_LEN, LANES), lambda b: (b, 0, 0)),
            pl.BlockSpec((None, MLA_KV_RANK, 512), lambda b: (j, 0, 0)),
            pl.BlockSpec((None, MLA_KV_RANK, 512), lambda b: (j, 0, 0)),
        ],
        out_specs=[
            pl.BlockSpec((PAST_LEN, 1024), lambda b: (b, 0)),
            pl.BlockSpec((PAST_LEN, 1024), lambda b: (b, 0)),
        ],
        out_shape=[
            jax.ShapeDtypeStruct((DEC_BATCH * PAST_LEN, 1024), BF16),
            jax.ShapeDtypeStruct((DEC_BATCH * PAST_LEN, 1024), BF16),
        ],
        compiler_params=_params(("parallel",)),
        name="mla_kv_ctx",
    )(cache_ckv, kr_tiled, w_uk, w_uv)


def _mla_attn_kernel(latent, *refs):
    if latent:
        qn_ref, qr_ref, cos_ref, sin_ref, kc_ref, vc_ref, kn_ref, vn_ref, o_ref = refs
        k_refs, v_refs = (kc_ref, kn_ref), (vc_ref, vn_ref)
    else:
        qn_ref, qr_ref, kn_ref, vn_ref, o_ref = refs
        k_refs, v_refs = (kn_ref,), (vn_ref,)
    tq = qn_ref.shape[0]
    lane = _lane_iota((tq, LANES))
    scale = (MLA_NOPE + MLA_ROPE) ** -0.5
    qr_cols = []
    for cidx in range(2):
        x = qr_ref[:, LANES * cidx:LANES * (cidx + 1)].astype(F32)
        if latent:
            x = _rope128(x, cos_ref[...], sin_ref[...], lane, MLA_ROPE // 2)
        qr_cols.append(x)
    o_prev = None
    for h in range(MLA_HEADS):
        p, half = divmod(h, 2)
        cidx, slot = divmod(h, 4)
        qa = jnp.where(_div_pow2(lane, MLA_NOPE) == half, qn_ref[:, LANES * p:LANES * (p + 1)].astype(F32), 0.0)
        qb = jnp.where(_div_pow2(lane, MLA_ROPE) == slot, qr_cols[cidx], 0.0)
        qcat = jnp.concatenate([qa, qb], axis=1).astype(BF16)
        pair_cols = slice(256 * p, 256 * (p + 1))
        s_list = [lax.dot_general(qcat, k_ref[:, pair_cols], NT_DIMS, preferred_element_type=F32)
                  for k_ref in k_refs]
        o = _pv_normalised(_exp_parts(s_list, scale), [v_ref[:, pair_cols] for v_ref in v_refs])
        if half == 0:
            o_prev = o
        else:
            o_ref[:, LANES * p:LANES * (p + 1)] = jnp.where(lane < MLA_DV, o_prev, o).astype(BF16)


def _mla_attn_prompt(zmla, kcat, vm):
    return pl.pallas_call(
        functools.partial(_mla_attn_kernel, False),
        grid=(BATCH,),
        in_specs=[
            pl.BlockSpec((SEQ, 512), lambda b: (b, 0)),
            pl.BlockSpec((SEQ, 256), lambda b: (b, 2)),
            pl.BlockSpec((SEQ, 1024), lambda b: (b, 0)),
            pl.BlockSpec((SEQ, 1024), lambda b: (b, 0)),
        ],
        out_specs=pl.BlockSpec((SEQ, 512), lambda b: (b, 0)),
        out_shape=jax.ShapeDtypeStruct((NTOK_P, 512), BF16),
        compiler_params=_params(("parallel",)),
        name="mla_attn_prompt",
    )(zmla, zmla, kcat, vm)


def _mla_attn_latent(zmla, cos_t, sin_t, kcat_ctx, vm_ctx, kcat, vm):
    nq = DEC_SEQ // ATT_TQ
    row0 = NTOK_P // ATT_TQ
    seq0 = NTOK_P // DEC_SEQ
    return pl.pallas_call(
        functools.partial(_mla_attn_kernel, True),
        grid=(DEC_BATCH, nq),
        in_specs=[
            pl.BlockSpec((ATT_TQ, 512), lambda b, q: (row0 + b * nq + q, 0)),
            pl.BlockSpec((ATT_TQ, 256), lambda b, q: (row0 + b * nq + q, 2)),
            pl.BlockSpec((ATT_TQ, LANES), lambda b, q: (q, 0)),
            pl.BlockSpec((ATT_TQ, LANES), lambda b, q: (q, 0)),
            pl.BlockSpec((PAST_LEN, 1024), lambda b, q: (b, 0)),
            pl.BlockSpec((PAST_LEN, 1024), lambda b, q: (b, 0)),
            pl.BlockSpec((DEC_SEQ, 1024), lambda b, q: (seq0 + b, 0)),
            pl.BlockSpec((DEC_SEQ, 1024), lambda b, q: (seq0 + b, 0)),
        ],
        out_specs=pl.BlockSpec((ATT_TQ, 512), lambda b, q: (b * nq + q, 0)),
        out_shape=jax.ShapeDtypeStruct((NTOK_S, 512), BF16),
        compiler_params=_params(("parallel", "arbitrary")),
        name="mla_attn_latent",
    )(zmla, zmla, cos_t, sin_t, kcat_ctx, vm_ctx, kcat, vm)


def _log_sigmoid(x):
    return jnp.minimum(x, 0.0) - jnp.log1p(jnp.exp(-jnp.abs(x)))


def _retention_kernel(seq_len, n_seq, has_init, emit_state, has_prev, *refs):
    refs = list(refs)
    decf_ref, decb_ref, q_ref, k_ref, v_ref, g_ref = refs[:6]
    refs = refs[6:]
    if has_init:
        sf0_ref, sb0_ref = refs[:2]
        refs = refs[2:]
    if has_prev:
        refs = refs[2:]
    o_ref = refs[0]
    refs = refs[1:]
    if emit_state:
        sf_ref, sb_ref = refs[:2]
        refs = refs[2:]
    of_scr, ob_scr, dec_scr, wts_scr = refs

    c = RET_CHUNK
    n_chunks = seq_len // c
    pair = pl.program_id(0)
    lane = _lane_iota((c, LANES))
    zeros_half = jnp.zeros((RET_DK, RET_DV), F32)

    def log_gammas(half):
        head = 2 * pair + half
        return (_log_sigmoid(decf_ref[pl.ds(head, 1), :]),
                _log_sigmoid(decb_ref[pl.ds(head, 1), :]))

    @pl.when(pl.program_id(1) == 0)
    def _():
        ri = lax.broadcasted_iota(jnp.int32, (c, c), 0)
        ci = lax.broadcasted_iota(jnp.int32, (c, c), 1)
        rel = (ri - ci).astype(F32)
        row = lax.broadcasted_iota(jnp.int32, (c, LANES), 0).astype(F32)
        for half in range(2):
            lgf, lgb = log_gammas(half)
            dec_scr[half] = (jnp.where(rel >= 0, jnp.exp(lgf[:, 0:1] * jnp.maximum(rel, 0.0)), 0.0)
                             + jnp.where(rel <= 0, jnp.exp(lgb[:, 0:1] * jnp.maximum(-rel, 0.0)), 0.0))
            wts_scr[half, 0] = jnp.exp(lgf * (row + 1.0))
            wts_scr[half, 1] = jnp.exp(lgf * (c - 1.0 - row))
            wts_scr[half, 2] = jnp.exp(lgb * (c - row))
            wts_scr[half, 3] = jnp.exp(lgb * row)

    cross = has_init or n_chunks > 1
    chains = [(sq, half) for sq in range(n_seq) for half in range(2)]
    chunk_decay = []
    for half in range(2):
        lgf, lgb = log_gammas(half)
        chunk_decay.append((jnp.exp(lgf * float(c)), jnp.exp(lgb * float(c))))

    def rows_of(sq, n):
        start = sq * seq_len + n * c
        return pl.ds(start if isinstance(n, int) else pl.multiple_of(start, c), c)

    def load(sq, half, n):
        rows = rows_of(sq, n)
        vsl = slice(RET_DV * half, RET_DV * (half + 1))
        qm = jnp.where(_div_pow2(lane, RET_DK) == half, q_ref[rows, :].astype(F32), 0.0)
        kk = k_ref[rows, :].astype(F32) * (RET_DK ** -0.5)
        return rows, vsl, qm, kk, v_ref[rows, vsl]

    def init_state(s0_ref, half):
        if not has_init:
            return jnp.zeros((LANES, RET_DV), F32)
        s0 = s0_ref[half]
        return jnp.concatenate([s0, zeros_half] if half == 0 else [zeros_half, s0], axis=0)

    def fwd_step(sq, half, n, s_f):
        rows, vsl, qm, kk, vb = load(sq, half, n)
        s = lax.dot_general(qm.astype(BF16), kk.astype(BF16), NT_DIMS, preferred_element_type=F32)
        o = jnp.dot((s * dec_scr[half]).astype(BF16), vb, preferred_element_type=F32)
        if cross:
            o = o + jnp.dot((qm * wts_scr[half, 0]).astype(BF16), s_f.astype(BF16), preferred_element_type=F32)
        of_scr[rows, vsl] = o
        kv = lax.dot_general((kk * wts_scr[half, 1]).astype(BF16), vb, TN_DIMS, preferred_element_type=F32)
        return chunk_decay[half][0] * s_f + kv

    def bwd_step(sq, half, n, s_b):
        rows, vsl, qm, kk, vb = load(sq, half, n)
        if cross:
            ob_scr[rows, vsl] = jnp.dot((qm * wts_scr[half, 2]).astype(BF16), s_b.astype(BF16),
                                        preferred_element_type=F32)
        kv = lax.dot_general((kk * wts_scr[half, 3]).astype(BF16), vb, TN_DIMS, preferred_element_type=F32)
        return chunk_decay[half][1] * s_b + kv

    def finish(sq, n):
        rows = rows_of(sq, n)
        o2 = of_scr[rows, :] + ob_scr[rows, :] if cross else of_scr[rows, :]
        for half in range(2):
            vsl = slice(RET_DV * half, RET_DV * (half + 1))
            o = o2[:, vsl]
            mu = jnp.mean(o, axis=-1, keepdims=True)
            d = o - mu
            var = jnp.mean(d * d, axis=-1, keepdims=True)
            o_ref[rows, vsl] = (_silu(g_ref[rows, vsl].astype(F32)) * (d * lax.rsqrt(var + LN_EPS))).astype(BF16)

    s_f = tuple(init_state(sf0_ref if has_init else None, half) for _, half in chains)
    s_b = tuple(init_state(sb0_ref if has_init else None, half) for _, half in chains)
    if n_chunks == 1:
        s_f = tuple(fwd_step(sq, half, 0, s) for (sq, half), s in zip(chains, s_f))
        s_b = tuple(bwd_step(sq, half, 0, s) for (sq, half), s in zip(chains, s_b))
        for sq in range(n_seq):
            finish(sq, 0)
    else:
        def scan_step(n, carry):
            sf, sb = carry
            sf = tuple(fwd_step(sq, half, n, s) for (sq, half), s in zip(chains, sf))
            sb = tuple(bwd_step(sq, half, n_chunks - 1 - n, s) for (sq, half), s in zip(chains, sb))
            return sf, sb

        s_f, s_b = lax.fori_loop(0, n_chunks, scan_step, (s_f, s_b))

        def finish_step(n, carry):
            for sq in range(n_seq):
                finish(sq, n)
            return carry

        lax.fori_loop(0, n_chunks, finish_step, 0)
    if emit_state:
        for (sq, half), sf, sb in zip(chains, s_f, s_b):
            sf_ref[sq, half] = sf[RET_DK * half:RET_DK * (half + 1), :]
            sb_ref[sq, half] = sb[RET_DK * half:RET_DK * (half + 1), :]


def _retention(zret, decf, decb, j, latent, state_f=None, state_b=None, prev=None):
    seq_len = DEC_SEQ if latent else SEQ
    n_seq = 1 if latent else 4
    n_b = (DEC_BATCH if latent else BATCH) // n_seq
    rows = n_seq * seq_len
    row0 = NTOK_P // rows if latent else 0
    n_pairs = RET_HEADS // 2
    in_specs = [
        pl.BlockSpec((None, RET_HEADS, LANES), lambda p, b: (j, 0, 0)),
        pl.BlockSpec((None, RET_HEADS, LANES), lambda p, b: (j, 0, 0)),
        pl.BlockSpec((rows, LANES), lambda p, b: (row0 + b, p)),
        pl.BlockSpec((rows, LANES), lambda p, b: (row0 + b, 4 + p)),
        pl.BlockSpec((rows, 256), lambda p, b: (row0 + b, 4 + p)),
        pl.BlockSpec((rows, 256), lambda p, b: (row0 + b, 8 + p)),
    ]
    args = [decf, decb, zret, zret, zret, zret]
    out_specs = [pl.BlockSpec((rows, 256), lambda p, b: (b, p))]
    out_shape = [jax.ShapeDtypeStruct((n_b * rows, RET_HEADS * RET_DV), BF16)]
    aliases = {}
    if latent:
        st_spec = pl.BlockSpec((None, None, 2, RET_DK, RET_DV), lambda p, b: (b, j, p, 0, 0))
        in_specs += [st_spec, st_spec]
        args += [state_f, state_b]
    else:
        st_spec = pl.BlockSpec((n_seq, None, 2, RET_DK, RET_DV), lambda p, b: (b, j, p, 0, 0))
        out_specs += [st_spec, st_spec]
        out_shape += [jax.ShapeDtypeStruct((BATCH, N_EVEN, RET_HEADS, RET_DK, RET_DV), F32)] * 2
        if prev is not None:
            aliases = {len(args): 1, len(args) + 1: 2}
            in_specs += [pl.BlockSpec(memory_space=pl.ANY)] * 2
            args += list(prev)
    return pl.pallas_call(
        functools.partial(_retention_kernel, seq_len, n_seq, latent, not latent, bool(aliases)),
        grid=(n_pairs, n_b),
        in_specs=in_specs,
        out_specs=out_specs,
        out_shape=out_shape,
        scratch_shapes=[
            pltpu.VMEM((rows, 2 * RET_DV), F32),
            pltpu.VMEM((rows, 2 * RET_DV), F32),
            pltpu.VMEM((2, RET_CHUNK, RET_CHUNK), F32),
            pltpu.VMEM((2, 4, RET_CHUNK, LANES), F32),
        ],
        input_output_aliases=aliases,
        compiler_params=_params(("arbitrary", "arbitrary")),
        name="retention_latent" if latent else "retention_prompt",
    )(*args)


def _diff_prep_kernel(k_ref, v_ref, cos_ref, sin_ref, kr_ref, va_ref):
    lane = _lane_iota(cos_ref.shape)
    cos = cos_ref[...]
    sin = sin_ref[...]
    ones = jnp.ones(cos_ref.shape, BF16)
    for h in range(DIFF_HEADS):
        sl = slice(LANES * h, LANES * (h + 1))
        kr_ref[:, sl] = _rope128(k_ref[:, sl].astype(F32), cos, sin, lane, DIFF_DH // 2).astype(BF16)
        va_ref[:, 256 * h:256 * h + LANES] = v_ref[:, sl].astype(BF16)
        va_ref[:, 256 * h + LANES:256 * (h + 1)] = ones


def _diff_prep(zodd, cos_t, sin_t):
    tm = 512
    row0 = NTOK_P // tm
    nst = DEC_SEQ // tm
    return pl.pallas_call(
        _diff_prep_kernel,
        grid=(NTOK_S // tm,),
        in_specs=[
            pl.BlockSpec((tm, 1024), lambda i: (row0 + i, 1)),
            pl.BlockSpec((tm, 1024), lambda i: (row0 + i, 2)),
            pl.BlockSpec((tm, LANES), lambda i: (i % nst, 0)),
            pl.BlockSpec((tm, LANES), lambda i: (i % nst, 0)),
        ],
        out_specs=[pl.BlockSpec((tm, 1024), lambda i: (i, 0)), pl.BlockSpec((tm, 2048), lambda i: (i, 0))],
        out_shape=[jax.ShapeDtypeStruct((NTOK_S, 1024), BF16), jax.ShapeDtypeStruct((NTOK_S, 2048), BF16)],
        compiler_params=_params(("parallel",)),
        name="diff_rope_keys",
    )(zodd, zodd, cos_t, sin_t)


def _diff_attn_kernel(latent, lam_init, *refs):
    if latent:
        (lam_ref, ng_ref, q_ref, cos_ref, sin_ref, kc_ref, vc_ref, kn_ref, vn_ref, o_ref) = refs
    else:
        lam_ref, ng_ref, q_ref, k_ref, v_ref = refs[:5]
        o_ref, kout_ref, vout_ref = refs[-3:]
    tq = q_ref.shape[0]
    lane = _lane_iota((tq, LANES))
    scale = DIFF_DH ** -0.5
    lp = lam_ref[...]
    lam = (jnp.exp(jnp.sum(lp[0:1, :] * lp[1:2, :], axis=-1, keepdims=True))
           - jnp.exp(jnp.sum(lp[2:3, :] * lp[3:4, :], axis=-1, keepdims=True)) + lam_init)
    ng = ng_ref[...]
    ones = jnp.ones((PAST_LEN if latent else tq, LANES), BF16)
    for h in range(DIFF_HEADS):
        sl = slice(LANES * h, LANES * (h + 1))
        qh = q_ref[:, sl].astype(F32)
        if latent:
            qh = _rope128(qh, cos_ref[...], sin_ref[...], lane, DIFF_DH // 2)
            k_list = [kc_ref[h].astype(BF16), kn_ref[:, sl]]
            v_list = [jnp.concatenate([vc_ref[h].astype(BF16), ones], axis=1),
                      vn_ref[:, 256 * h:256 * (h + 1)]]
        else:
            kh = k_ref[:, sl]
            vh = v_ref[:, sl]
            kout_ref[h] = kh.astype(F32)
            vout_ref[h] = vh.astype(F32)
            k_list = [kh.astype(BF16)]
            v_list = [jnp.concatenate([vh.astype(BF16), ones], axis=1)]
        q1 = jnp.where(lane < DIFF_DH, qh, 0.0).astype(BF16)
        q2 = jnp.where(lane >= DIFF_DH, qh, 0.0).astype(BF16)
        s1 = [lax.dot_general(q1, kk, NT_DIMS, preferred_element_type=F32) for kk in k_list]
        s2 = [lax.dot_general(q2, kk, NT_DIMS, preferred_element_type=F32) for kk in k_list]
        o = _pv_normalised(_exp_parts(s1, scale), v_list) - lam * _pv_normalised(_exp_parts(s2, scale), v_list)
        y = o * lax.rsqrt(jnp.mean(o * o, axis=-1, keepdims=True) + RMS_EPS) * ng
        o_ref[:, sl] = (y * (1.0 - lam_init)).astype(BF16)


def _diff_attn_prompt(zodd, lam_p, norm_g, j, lam_init, prev):
    cache_shape = jax.ShapeDtypeStruct((BATCH, N_ODD, DIFF_HEADS, SEQ, LANES), F32)
    cache_spec = pl.BlockSpec((None, None, DIFF_HEADS, SEQ, LANES), lambda b: (b, j, 0, 0, 0))
    in_specs = [
        pl.BlockSpec((None, 4, DIFF_DH), lambda b: (j, 0, 0)),
        pl.BlockSpec((None, 1, DIFF_DV), lambda b: (j, 0, 0)),
        pl.BlockSpec((SEQ, 1024), lambda b: (b, 0)),
        pl.BlockSpec((SEQ, 1024), lambda b: (b, 1)),
        pl.BlockSpec((SEQ, 1024), lambda b: (b, 2)),
    ]
    args = [lam_p, norm_g.reshape(-1, 1, DIFF_DV), zodd, zodd, zodd]
    aliases = {}
    if prev is not None:
        aliases = {len(args): 1, len(args) + 1: 2}
        in_specs += [pl.BlockSpec(memory_space=pl.ANY)] * 2
        args += list(prev)
    return pl.pallas_call(
        functools.partial(_diff_attn_kernel, False, lam_init),
        grid=(BATCH,),
        in_specs=in_specs,
        out_specs=[pl.BlockSpec((SEQ, 1024), lambda b: (b, 0)), cache_spec, cache_spec],
        out_shape=[jax.ShapeDtypeStruct((NTOK_P, 1024), BF16), cache_shape, cache_shape],
        input_output_aliases=aliases,
        compiler_params=_params(("arbitrary",)),
        name="diff_attn_prompt",
    )(*args)


def _diff_attn_latent(zodd, lam_p, norm_g, cos_t, sin_t, cache_k, cache_v, k_rot, v_aug, j, lam_init):
    nq = DEC_SEQ // ATT_TQ
    row0 = NTOK_P // ATT_TQ
    ctx_spec = pl.BlockSpec((None, None, DIFF_HEADS, PAST_LEN, LANES), lambda b, q: (b, j, 0, 0, 0))
    return pl.pallas_call(
        functools.partial(_diff_attn_kernel, True, lam_init),
        grid=(DEC_BATCH, nq),
        in_specs=[
            pl.BlockSpec((None, 4, DIFF_DH), lambda b, q: (j, 0, 0)),
            pl.BlockSpec((None, 1, DIFF_DV), lambda b, q: (j, 0, 0)),
            pl.BlockSpec((ATT_TQ, 1024), lambda b, q: (row0 + b * nq + q, 0)),
            pl.BlockSpec((ATT_TQ, LANES), lambda b, q: (q, 0)),
            pl.BlockSpec((ATT_TQ, LANES), lambda b, q: (q, 0)),
            ctx_spec,
            ctx_spec,
            pl.BlockSpec((DEC_SEQ, 1024), lambda b, q: (b, 0)),
            pl.BlockSpec((DEC_SEQ, 2048), lambda b, q: (b, 0)),
        ],
        out_specs=pl.BlockSpec((ATT_TQ, 1024), lambda b, q: (b * nq + q, 0)),
        out_shape=jax.ShapeDtypeStruct((NTOK_S, 1024), BF16),
        compiler_params=_params(("parallel", "arbitrary")),
        name="diff_attn_latent",
    )(lam_p, norm_g.reshape(-1, 1, DIFF_DV), zodd, cos_t, sin_t, cache_k, cache_v, k_rot, v_aug)


def _mla_weight(w_in):
    base = RET_COLS
    mq = w_in[:, base:base + MLA_HEADS * (MLA_NOPE + MLA_ROPE)].reshape(D_MODEL, MLA_HEADS, MLA_NOPE + MLA_ROPE)
    qn = mq[:, :, :MLA_NOPE].reshape(D_MODEL, MLA_HEADS * MLA_NOPE)
    qr = mq[:, :, MLA_NOPE:].reshape(D_MODEL, MLA_HEADS * MLA_ROPE)
    ckv0 = base + MLA_HEADS * (MLA_NOPE + MLA_ROPE)
    ckv = w_in[:, ckv0:ckv0 + MLA_KV_RANK]
    kr = w_in[:, ckv0 + MLA_KV_RANK:]
    return jnp.concatenate([qn, qr, ckv, jnp.tile(kr, (1, LANES // MLA_ROPE))], axis=1)


def kernel(x_prompt, x_sample, state_ret_fwd, state_ret_bwd, cache_mla_ckv, cache_mla_krope, cache_diff_k, cache_diff_v, c, c_ctx, ada_w, ada_b, ln1_g, ln1_b, ln2_g, ln2_b, ev_w_in, ev_w_out, ret_decay_fwd, ret_decay_bwd, mla_kv_norm_g, mla_w_uk, mla_w_uv, od_w_in, od_w_out, diff_lambda, diff_norm_g, moe_w_group, moe_b_group, moe_w_expert, moe_b_expert, moe_w1, moe_w3, moe_w2):
    x = (x_prompt.reshape(NTOK_P, D_MODEL), x_sample.reshape(NTOK_S, D_MODEL), 0)
    cond =jnp.concatenate([c_ctx[None, :], c, jnp.zeros((N_COND - 1 - DEC_BATCH, D_MODEL), F32)], axis=0)
    mods = _ada_all(cond, ada_w, ada_b).reshape(DEPTH, N_COND, 6, D_MODEL)

    cos_m, sin_m = _rope_tables(MLA_ROPE)
    cos_d, sin_d = _rope_tables(DIFF_DH)
    ident = 512
    cos_m_id = jnp.concatenate([jnp.ones((ident, LANES), F32), cos_m], axis=0)
    sin_m_id = jnp.concatenate([jnp.zeros((ident, LANES), F32), sin_m], axis=0)
    decf = jnp.broadcast_to(ret_decay_fwd[:, :, None], ret_decay_fwd.shape + (LANES,))
    decb = jnp.broadcast_to(ret_decay_bwd[:, :, None], ret_decay_bwd.shape + (LANES,))

    pad = LANES - MOE_GROUPS - MOE_EXPERTS
    ret_states = mla_caches = diff_caches = None
    for i in range(DEPTH):
        j = i // 2
        mod = mods[i]
        w_router = jnp.concatenate([moe_w_group[i], moe_w_expert[i], jnp.zeros((D_MODEL, pad), F32)], axis=1)
        b_router = jnp.concatenate([moe_b_group[i], moe_b_expert[i], jnp.zeros((pad,), F32)])[None, :]
        if i % 2 == 0:
            zret = _mm_mod(x, mod, ev_w_in, (j,), RET_COLS, 1024, "in_proj_retention")
            zmla = _mm_mod(x, mod, _mla_weight(ev_w_in[j]), (), MLA_COLS, MLA_COLS, "in_proj_mla")
            *mla_caches, kcat, vm = _mla_kv_new(zmla, cos_m_id, sin_m_id, mla_kv_norm_g, mla_w_uk, mla_w_uv,
                                                j, mla_caches)
            kr_ctx = jnp.tile(cache_mla_krope[:, j], (1, 1, LANES // MLA_ROPE))
            kcat_ctx, vm_ctx = _mla_kv_ctx(cache_mla_ckv, kr_ctx, mla_w_uk, mla_w_uv, j)
            a_ret_p, *ret_states = _retention(zret, decf, decb, j, False, prev=ret_states)
            (a_ret_s,) = _retention(zret, decf, decb, j, True, state_ret_fwd, state_ret_bwd)
            a_mla_p = _mla_attn_prompt(zmla, kcat, vm)
            a_mla_s = _mla_attn_latent(zmla, cos_m, sin_m, kcat_ctx, vm_ctx, kcat, vm)
            x1, meta, counts = _mm_ln([(a_ret_p, a_ret_s), (a_mla_p, a_mla_s)], ev_w_out, j, x, mod,
                                      ln1_g, ln1_b, w_router, b_router, i)
        else:
            lam_init = 0.8 - 0.6 * math.exp(-0.3 * i)
            zodd = _mm_mod(x, mod, od_w_in, (j,), 3072, 1024, "in_proj_diff")
            a_p, *diff_caches = _diff_attn_prompt(zodd, diff_lambda, diff_norm_g, j, lam_init, diff_caches)
            k_rot, v_aug = _diff_prep(zodd, cos_d, sin_d)
            a_s = _diff_attn_latent(zodd, diff_lambda, diff_norm_g, cos_d, sin_d, cache_diff_k, cache_diff_v,
                                    k_rot, v_aug, j, lam_init)
            x1, meta, counts = _mm_ln([(a_p, a_s)], od_w_out, j, x, mod, ln1_g, ln1_b, w_router, b_router, i)
        y = _moe(x1, _moe_plan(meta, counts), mods, w_router, b_router, moe_w1, moe_w3, moe_w2, ln2_g, ln2_b, i)
        x = (y, y, NTOK_P)

    y_prompt = x[0][:NTOK_P].reshape(BATCH, SEQ, D_MODEL)
    y_sample = x[1][NTOK_P:NTOK].reshape(DEC_BATCH, DEC_SEQ, D_MODEL)
    return (y_prompt, y_sample, ret_states[0], ret_states[1], mla_caches[0], mla_caches[1],
            diff_caches[0], diff_caches[1])
```

```python
import functools
import math

import jax
import jax.numpy as jnp
from jax import lax
from jax.experimental import pallas as pl
from jax.experimental.pallas import tpu as pltpu

D_MODEL = 1024
BATCH = 32
SEQ = 256
DEPTH = 4
N_EVEN = 2
N_ODD = 2
DEC_BATCH = 2
DEC_SEQ = 2048
PAST_LEN = 256
GRID_W = 64
LN_EPS = 1e-5
RMS_EPS = 1e-6
DEEPNORM_ALPHA = (2.0 * DEPTH) ** 0.25
ROPE_BASE = 10000.0
RET_HEADS = 8
RET_DK = 64
RET_DV = 128
MLA_HEADS = 8
MLA_NOPE = 64
MLA_ROPE = 32
MLA_DV = 64
MLA_KV_RANK = 256
DIFF_HEADS = 8
DIFF_DH = 64
DIFF_DV = 128
MOE_GROUPS = 4
MOE_PER_GROUP = 4
MOE_EXPERTS = 16
MOE_FF = 256

NTOK_P = BATCH * SEQ
NTOK_S = DEC_BATCH * DEC_SEQ
NTOK = NTOK_P + NTOK_S
N_COND = 8
LANES = 128
SUBLANES = 8
RET_COLS = 3072
MLA_COLS = 1152
ATT_TQ = 256
RET_CHUNK = 256
MOE_TILE = 512
MOE_TILES = (NTOK + MOE_GROUPS * (MOE_TILE - 1)) // MOE_TILE
MOE_ROWS = MOE_TILES * MOE_TILE
VMEM_LIMIT = 56 * 1024 * 1024
LOG2E = 1.4426950408889634

F32 = jnp.float32
BF16 = jnp.bfloat16
NT_DIMS = (((1,), (1,)), ((), ()))
TN_DIMS = (((0,), (0,)), ((), ()))


def _params(sem):
    return pltpu.CompilerParams(dimension_semantics=sem, vmem_limit_bytes=VMEM_LIMIT)


def _group_of_tile(i, tm):
    npt = NTOK_P // tm
    nst = DEC_SEQ // tm
    return jnp.where(i < npt, 0, 1 + (i - npt) // nst)


def _split_specs(tm, width, m_of, s_row0=0):
    npt = NTOK_P // tm
    s_blk0 = s_row0 // tm
    return [pl.BlockSpec((tm, width), lambda *g: (jnp.minimum(m_of(*g), npt - 1), 0)),
            pl.BlockSpec((tm, width), lambda *g: (jnp.maximum(m_of(*g) - npt, 0) + s_blk0, 0))]


def _read_split(p_ref, s_ref, m):
    return jnp.where(m < NTOK_P // p_ref.shape[0], p_ref[...], s_ref[...])


def _silu(x):
    return x * (1.0 / (1.0 + jnp.exp(-x)))


def _layer_norm(r, g, b):
    mu = jnp.mean(r, axis=-1, keepdims=True)
    d = r - mu
    var = jnp.mean(d * d, axis=-1, keepdims=True)
    return d * lax.rsqrt(var + LN_EPS) * g + b


def _lane_iota(shape):
    return lax.broadcasted_iota(jnp.int32, shape, 1)


def _div_pow2(x, d):
    assert d & (d - 1) == 0
    return jnp.right_shift(x, d.bit_length() - 1)


def _mod_pow2(x, d):
    assert d & (d - 1) == 0
    return jnp.bitwise_and(x, d - 1)


def _ada_kernel(c_ref, w_ref, b_ref, o_ref):
    h = _silu(c_ref[...]).astype(BF16)
    o_ref[...] = jnp.dot(h, w_ref[...].astype(BF16), preferred_element_type=F32) + b_ref[...]


def _ada_all(cond, ada_w, ada_b):
    tn = 768
    return pl.pallas_call(
        _ada_kernel,
        grid=(DEPTH, 6 * D_MODEL // tn),
        in_specs=[
            pl.BlockSpec((N_COND, D_MODEL), lambda l, n: (0, 0)),
            pl.BlockSpec((None, D_MODEL, tn), lambda l, n: (l, 0, n)),
            pl.BlockSpec((None, 1, tn), lambda l, n: (l, 0, n)),
        ],
        out_specs=pl.BlockSpec((None, N_COND, tn), lambda l, n: (l, 0, n)),
        out_shape=jax.ShapeDtypeStruct((DEPTH, N_COND, 6 * D_MODEL), F32),
        compiler_params=_params(("parallel", "parallel")),
        name="ada_modulation",
    )(cond, ada_w, ada_b.reshape(DEPTH, 1, 6 * D_MODEL))


def _mm_mod_kernel(xp_ref, xs_ref, mod_ref, w_ref, o_ref, wscr):
    m = pl.program_id(1)

    @pl.when(m == 0)
    def _():
        wscr[...] = w_ref[...].astype(BF16)

    sh = mod_ref[0:1, :]
    sc = mod_ref[1:2, :]
    xm = (_read_split(xp_ref, xs_ref, m) * (1.0 + sc) + sh).astype(BF16)
    o_ref[...] = jnp.dot(xm, wscr[...], preferred_element_type=F32).astype(o_ref.dtype)


def _mm_mod(x, mod, w, w_index, n_cols, tn, name):
    tm = 1024
    w_block = (None,) * len(w_index) + (D_MODEL, tn)
    return pl.pallas_call(
        _mm_mod_kernel,
        grid=(n_cols // tn, NTOK // tm),
        in_specs=_split_specs(tm, D_MODEL, lambda n, m: m, x[2]) + [
            pl.BlockSpec((None, 6, D_MODEL), lambda n, m: (_group_of_tile(m, tm), 0, 0)),
            pl.BlockSpec(w_block, lambda n, m: tuple(w_index) + (0, n)),
        ],
        out_specs=pl.BlockSpec((tm, tn), lambda n, m: (m, n)),
        out_shape=jax.ShapeDtypeStruct((NTOK, n_cols), BF16),
        scratch_shapes=[pltpu.VMEM((D_MODEL, tn), BF16)],
        compiler_params=_params(("arbitrary", "arbitrary")),
        name=name,
    )(x[0], x[1], mod, w)


def _router_probs(xm, wr_ref, br_ref):
    rows = xm.shape[0]
    z = jnp.dot(xm, wr_ref[...].astype(BF16), preferred_element_type=F32) + br_ref[...]
    lane_i = _lane_iota((rows, LANES))
    lane = lane_i.astype(F32)
    gmask = lane_i < MOE_GROUPS
    zg = jnp.where(gmask, z, -jnp.inf)
    pg = jnp.exp(zg - jnp.max(zg, axis=-1, keepdims=True))
    g_prob = pg / jnp.sum(pg, axis=-1, keepdims=True)
    g_p = jnp.max(g_prob, axis=-1, keepdims=True)
    g_idx = jnp.min(jnp.where(gmask & (g_prob == g_p), lane, float(LANES)), axis=-1, keepdims=True)
    return z, lane_i, lane, g_p, g_idx


def _mm_ln_kernel(k_sizes, *refs):
    n_a = len(k_sizes)
    a_refs = refs[:2 * n_a]
    (w_ref, xp_ref, xs_ref, mod_ref, g_ref, b_ref, wr_ref, br_ref,
     o_ref, meta_ref, cnt_ref, wscr, tri_scr, carry_scr) = refs[2 * n_a:]
    m = pl.program_id(0)
    tm = o_ref.shape[0]

    @pl.when(m == 0)
    def _():
        wscr[...] = w_ref[...].astype(BF16)
        ri = lax.broadcasted_iota(jnp.int32, (tm, tm), 0)
        ci = lax.broadcasted_iota(jnp.int32, (tm, tm), 1)
        tri_scr[...] = jnp.where(ci < ri, 1.0, 0.0).astype(BF16)
        carry_scr[...] = jnp.zeros_like(carry_scr)

    y = None
    k0 = 0
    for i, ks in enumerate(k_sizes):
        a = _read_split(a_refs[2 * i], a_refs[2 * i + 1], m)
        part = jnp.dot(a, wscr[k0:k0 + ks, :], preferred_element_type=F32)
        y = part if y is None else y + part
        k0 += ks
    gate = mod_ref[2:3, :]
    r = DEEPNORM_ALPHA * _read_split(xp_ref, xs_ref, m) + gate * y
    x1 = _layer_norm(r, g_ref[...], b_ref[...])
    o_ref[...] = x1

    xm = (x1 * (1.0 + mod_ref[4:5, :]) + mod_ref[3:4, :]).astype(BF16)
    _, lane_i, lane, _, g_idx = _router_probs(xm, wr_ref, br_ref)
    onehot = jnp.where(lane == g_idx, 1.0, 0.0)
    before = jnp.dot(tri_scr[...], onehot.astype(BF16), preferred_element_type=F32) + carry_scr[0:1, :]
    rank = jnp.sum(jnp.where(lane == g_idx, before, 0.0), axis=-1, keepdims=True)
    meta_ref[...] = jnp.where(lane_i == 0, g_idx, jnp.where(lane_i == 1, rank, 0.0))
    total = carry_scr[0:1, :] + jnp.sum(onehot, axis=0, keepdims=True)
    carry_scr[...] = jnp.broadcast_to(total, carry_scr.shape)
    cnt_ref[...] = jnp.broadcast_to(total, cnt_ref.shape)


def _mm_ln(a_pairs, w, j, x, mod, ln_g, ln_b, w_router, b_router, layer):
    tm = 512
    k_sizes = tuple(ap.shape[1] for ap, _ in a_pairs)
    k_tot = sum(k_sizes)
    in_specs = []
    args = []
    for (ap, a_s), ks in zip(a_pairs, k_sizes):
        in_specs += _split_specs(tm, ks, lambda m: m)
        args += [ap, a_s]
    in_specs += [pl.BlockSpec((None, k_tot, D_MODEL), lambda m: (j, 0, 0))]
    in_specs += _split_specs(tm, D_MODEL, lambda m: m, x[2])
    in_specs += [
        pl.BlockSpec((None, 6, D_MODEL), lambda m: (_group_of_tile(m, tm), 0, 0)),
        pl.BlockSpec((None, 1, D_MODEL), lambda m: (layer, 0, 0)),
        pl.BlockSpec((None, 1, D_MODEL), lambda m: (layer, 0, 0)),
        pl.BlockSpec((D_MODEL, LANES), lambda m: (0, 0)),
        pl.BlockSpec((1, LANES), lambda m: (0, 0)),
    ]
    return pl.pallas_call(
        functools.partial(_mm_ln_kernel, k_sizes),
        grid=(NTOK // tm,),
        in_specs=in_specs,
        out_specs=[
            pl.BlockSpec((tm, D_MODEL), lambda m: (m, 0)),
            pl.BlockSpec((tm, LANES), lambda m: (m, 0)),
            pl.BlockSpec((N_COND, LANES), lambda m: (0, 0)),
        ],
        out_shape=[
            jax.ShapeDtypeStruct((NTOK, D_MODEL), F32),
            jax.ShapeDtypeStruct((NTOK, LANES), F32),
            jax.ShapeDtypeStruct((N_COND, LANES), F32),
        ],
        scratch_shapes=[
            pltpu.VMEM((k_tot, D_MODEL), BF16),
            pltpu.VMEM((tm, tm), BF16),
            pltpu.VMEM((N_COND, LANES), F32),
        ],
        compiler_params=_params(("arbitrary",)),
        name="out_proj_ln",
    )(*args, w, x[0], x[1], mod, ln_g.reshape(DEPTH, 1, D_MODEL), ln_b.reshape(DEPTH, 1, D_MODEL),
      w_router, b_router)


def _moe_kernel(tgrp_ref, ntile_ref, src0_ref, src1_ref, dst_ref, mid_ref, x_hbm, mod_ref, wr_ref, br_ref,
                w1_ref, w3_ref, w2_ref, g_ref, b_ref, y_hbm,
                gbuf, obuf, gsem, ssem, w13s, w2s):
    i = pl.program_id(0)
    n_steps = pl.num_programs(0)
    n_tiles = ntile_ref[0]
    n_blk = gbuf.shape[1]
    ts = n_blk * SUBLANES
    slot = lax.rem(i, 2)

    def start_gather(src_ref, s):
        def body(k, carry):
            for u in range(SUBLANES):
                tok = src_ref[0, k * SUBLANES + u]
                pltpu.make_async_copy(x_hbm.at[pl.ds(tok, 1), :], gbuf.at[s, k, pl.ds(u, 1), :],
                                      gsem.at[s]).start()
            return carry
        lax.fori_loop(0, n_blk, body, 0)

    def wait_gather(s):
        pltpu.make_async_copy(gbuf.at[s], gbuf.at[s], gsem.at[s]).wait()

    def start_scatter(s):
        def body(k, carry):
            for u in range(SUBLANES):
                tok = dst_ref[0, k * SUBLANES + u]
                pltpu.make_async_copy(obuf.at[s, k, pl.ds(u, 1), :], y_hbm.at[pl.ds(tok, 1), :],
                                      ssem.at[s]).start()
            return carry
        lax.fori_loop(0, n_blk, body, 0)

    def wait_scatter(s):
        pltpu.make_async_copy(obuf.at[s], obuf.at[s], ssem.at[s]).wait()

    @pl.when(i == 0)
    def _():
        start_gather(src0_ref, 0)

    @pl.when(i < n_tiles)
    def _():
        grp = tgrp_ref[i]
        wait_gather(slot)

        @pl.when(i + 1 < n_tiles)
        def _():
            start_gather(src1_ref, 1 - slot)

        @pl.when((i == 0) | (grp != tgrp_ref[jnp.maximum(i - 1, 0)]))
        def _():
            w13s[:, :, :MOE_FF] = w1_ref[...].astype(BF16)
            w13s[:, :, MOE_FF:] = w3_ref[...].astype(BF16)
            w2s[...] = w2_ref[...].astype(BF16)

        mid_rows = mid_ref[...].astype(F32)
        spread = jnp.concatenate([jnp.broadcast_to(mid_rows[a:a + 1, :], (LANES, LANES))
                                  for a in range(ts // LANES)], axis=0)
        row_i = lax.broadcasted_iota(jnp.int32, (ts, LANES), 0)
        mid = jnp.sum(jnp.where(_lane_iota((ts, LANES)) == _mod_pow2(row_i, LANES), spread, 0.0),
                      axis=-1, keepdims=True)

        def mod_row(k):
            return jnp.where(mid == 0, mod_ref[0, k:k + 1, :],
                             jnp.where(mid == 1, mod_ref[1, k:k + 1, :], mod_ref[2, k:k + 1, :]))

        x1 = gbuf[slot].reshape(ts, D_MODEL)
        xm = (x1 * (1.0 + mod_row(4)) + mod_row(3)).astype(BF16)
        z, lane_i, lane, g_p, _ = _router_probs(xm, wr_ref, br_ref)
        e0 = MOE_GROUPS + MOE_PER_GROUP * grp
        emask = (lane_i >= e0) & (lane_i < e0 + MOE_PER_GROUP)
        ze = jnp.where(emask, z, -jnp.inf)
        pe = jnp.exp(ze - jnp.max(ze, axis=-1, keepdims=True))
        e_prob = pe / jnp.sum(pe, axis=-1, keepdims=True)
        cand = jnp.where(emask, e_prob, -1.0)
        p1 = jnp.max(cand, axis=-1, keepdims=True)
        i1 = jnp.min(jnp.where(cand == p1, lane, float(LANES)), axis=-1, keepdims=True)
        cand2 = jnp.where(lane == i1, -1.0, cand)
        p2 = jnp.max(cand2, axis=-1, keepdims=True)
        i2 = jnp.min(jnp.where(cand2 == p2, lane, float(LANES)), axis=-1, keepdims=True)
        denom = p1 + p2
        comb = jnp.where(lane == i1, g_p * p1 / denom, 0.0) + jnp.where(lane == i2, g_p * p2 / denom, 0.0)
        y = None
        for e in range(MOE_PER_GROUP):
            c = jnp.sum(jnp.where(lane_i == e0 + e, comb, 0.0), axis=-1, keepdims=True)
            h = jnp.dot(xm, w13s[e], preferred_element_type=F32)
            hid = (_silu(h[:, :MOE_FF]) * h[:, MOE_FF:] * c).astype(BF16)
            part = jnp.dot(hid, w2s[e], preferred_element_type=F32)
            y = part if y is None else y + part
        r = DEEPNORM_ALPHA * x1 + mod_row(5) * y

        @pl.when(i >= 2)
        def _():
            wait_scatter(slot)

        obuf[slot] = _layer_norm(r, g_ref[...], b_ref[...]).reshape(n_blk, SUBLANES, D_MODEL)
        start_scatter(slot)

    @pl.when(i == n_steps - 1)
    def _():
        @pl.when(n_tiles >= 2)
        def _():
            wait_scatter(lax.rem(n_tiles, 2))

        @pl.when(n_tiles >= 1)
        def _():
            wait_scatter(lax.rem(n_tiles + 1, 2))


def _inverse_perm_kernel(pos_ref, out_ref):
    def clear(s, carry):
        out_ref[s] = 0
        return carry

    def place(t, carry):
        out_ref[pos_ref[t]] = t + 1
        return carry

    lax.fori_loop(0, out_ref.shape[0], clear, 0, unroll=8)
    lax.fori_loop(0, pos_ref.shape[0], place, 0, unroll=8)


def _inverse_perm(pos):
    return pl.pallas_call(
        _inverse_perm_kernel,
        in_specs=[pl.BlockSpec(memory_space=pltpu.SMEM)],
        out_specs=pl.BlockSpec(memory_space=pltpu.SMEM),
        out_shape=jax.ShapeDtypeStruct((MOE_ROWS,), jnp.int32),
        name="moe_inverse_perm",
    )(pos)


def _moe_plan(meta, counts):
    ts = MOE_TILE
    cnt = counts[0, :MOE_GROUPS].astype(jnp.int32)
    tiles_g = (cnt + ts - 1) // ts
    tile_end = jnp.cumsum(tiles_g)
    row0_g = (tile_end - tiles_g) * ts
    gid = meta[:, 0].astype(jnp.int32)
    rank = meta[:, 1].astype(jnp.int32)
    pos = row0_g[gid] + rank
    tok1 = _inverse_perm(pos)
    rows = jnp.arange(MOE_ROWS, dtype=jnp.int32)
    src = jnp.maximum(tok1 - 1, 0)
    spare = NTOK + ((rows // ts) % 2) * ts + rows % ts
    dst = jnp.where(tok1 > 0, src, spare)
    mid = jnp.where(src < NTOK_P, 0, 1 + (src - NTOK_P) // DEC_SEQ)
    tile_group = jnp.minimum(jnp.sum(jnp.arange(MOE_TILES)[:, None] >= tile_end[None, :], axis=1),
                             MOE_GROUPS - 1).astype(jnp.int32)
    n_tiles = tile_end[-1:].astype(jnp.int32)
    return (tile_group, n_tiles, src.reshape(MOE_TILES, 1, ts), dst.reshape(MOE_TILES, 1, ts),
            mid.reshape(MOE_TILES, ts // LANES, LANES))


def _moe(x1, plan, mods, w_router, b_router, w1, w3, w2, ln_g, ln_b, layer):
    ts = MOE_TILE
    tile_group, n_tiles, src, dst, mid = plan
    grp_shape = (DEPTH, MOE_GROUPS, MOE_PER_GROUP)
    w_in_spec = pl.BlockSpec((None, None, MOE_PER_GROUP, D_MODEL, MOE_FF), lambda i, tg, nt: (layer, tg[i], 0, 0, 0))
    w_out_spec = pl.BlockSpec((None, None, MOE_PER_GROUP, MOE_FF, D_MODEL), lambda i, tg, nt: (layer, tg[i], 0, 0, 0))
    smem_tile = functools.partial(pl.BlockSpec, (None, 1, ts), memory_space=pltpu.SMEM)
    grid_spec = pltpu.PrefetchScalarGridSpec(
        num_scalar_prefetch=2,
        grid=(MOE_TILES,),
        in_specs=[
            smem_tile(lambda i, tg, nt: (i, 0, 0)),
            smem_tile(lambda i, tg, nt: (jnp.minimum(i + 1, MOE_TILES - 1), 0, 0)),
            smem_tile(lambda i, tg, nt: (i, 0, 0)),
            pl.BlockSpec((None, ts // LANES, LANES), lambda i, tg, nt: (i, 0, 0)),
            pl.BlockSpec(memory_space=pl.ANY),
            pl.BlockSpec((None, N_COND, 6, D_MODEL), lambda i, tg, nt: (layer, 0, 0, 0)),
            pl.BlockSpec((D_MODEL, LANES), lambda i, tg, nt: (0, 0)),
            pl.BlockSpec((1, LANES), lambda i, tg, nt: (0, 0)),
            w_in_spec,
            w_in_spec,
            w_out_spec,
            pl.BlockSpec((None, 1, D_MODEL), lambda i, tg, nt: (layer, 0, 0)),
            pl.BlockSpec((None, 1, D_MODEL), lambda i, tg, nt: (layer, 0, 0)),
        ],
        out_specs=pl.BlockSpec(memory_space=pl.ANY),
        scratch_shapes=[
            pltpu.VMEM((2, ts // SUBLANES, SUBLANES, D_MODEL), F32),
            pltpu.VMEM((2, ts // SUBLANES, SUBLANES, D_MODEL), F32),
            pltpu.SemaphoreType.DMA((2,)),
            pltpu.SemaphoreType.DMA((2,)),
            pltpu.VMEM((MOE_PER_GROUP, D_MODEL, 2 * MOE_FF), BF16),
            pltpu.VMEM((MOE_PER_GROUP, MOE_FF, D_MODEL), BF16),
        ],
    )
    return pl.pallas_call(
        _moe_kernel,
        grid_spec=grid_spec,
        out_shape=jax.ShapeDtypeStruct((NTOK + 2 * ts, D_MODEL), F32),
        compiler_params=_params(("arbitrary",)),
        name="hier_moe_ln",
    )(tile_group, n_tiles, src, src, dst, mid, x1, mods, w_router, b_router,
      w1.reshape(grp_shape + (D_MODEL, MOE_FF)), w3.reshape(grp_shape + (D_MODEL, MOE_FF)),
      w2.reshape(grp_shape + (MOE_FF, D_MODEL)),
      ln_g.reshape(DEPTH, 1, D_MODEL), ln_b.reshape(DEPTH, 1, D_MODEL))


def _swap_halves(x, lane, half):
    return jnp.where(_mod_pow2(lane, 2 * half) < half,
                     pltpu.roll(x, LANES - half, 1), pltpu.roll(x, half, 1))


def _rope128(x, cos, sin_signed, lane, half):
    return x * cos + _swap_halves(x, lane, half) * sin_signed


def _rope_tables(rot_dim):
    rows = DEC_SEQ // GRID_W
    row = jnp.repeat(jnp.arange(rows, dtype=F32), GRID_W)
    col = jnp.tile(jnp.arange(GRID_W, dtype=F32), rows)
    n_freq = rot_dim // 4
    inv_freq = ROPE_BASE ** (-jnp.arange(n_freq, dtype=F32) / n_freq)
    ang = jnp.concatenate([row[:, None] * inv_freq, col[:, None] * inv_freq], axis=-1)
    cos, sin = jnp.cos(ang), jnp.sin(ang)
    reps = LANES // rot_dim
    cos_full = jnp.tile(jnp.concatenate([cos, cos], axis=-1), (1, reps))
    sin_signed = jnp.tile(jnp.concatenate([-sin, sin], axis=-1), (1, reps))
    return cos_full, sin_signed


def _exp_parts(s_list, scale):
    m = None
    for s in s_list:
        sm = jnp.max(s, axis=-1, keepdims=True)
        m = sm if m is None else jnp.maximum(m, sm)
    return [jnp.exp2((s - m) * (scale * LOG2E)).astype(BF16) for s in s_list]


def _pv_normalised(p_list, v_list):
    o = None
    for p, v in zip(p_list, v_list):
        part = jnp.dot(p, v, preferred_element_type=F32)
        o = part if o is None else o + part
    return o[:, :LANES] / o[:, LANES:]


def _mla_kv_kernel(new_tokens, *refs):
    if new_tokens:
        (ckv_ref, kr_ref, cos_ref, sin_ref, g_ref, wuk_ref, wuv_ref) = refs[:7]
        ckvc_ref, krc_ref, kcat_ref, vm_ref = refs[-4:]
        i = pl.program_id(0)
        x = ckv_ref[...].astype(F32)
        c = x * lax.rsqrt(jnp.mean(x * x, axis=-1, keepdims=True) + RMS_EPS) * g_ref[...]
        kr_raw = kr_ref[...].astype(F32)

        @pl.when(i < NTOK_P // ckv_ref.shape[0])
        def _():
            for b in range(ckvc_ref.shape[0]):
                ckvc_ref[b] = c[SEQ * b:SEQ * (b + 1), :]
                krc_ref[b] = kr_raw[SEQ * b:SEQ * (b + 1), :MLA_ROPE]

        lane = _lane_iota(kr_ref.shape)
        kr = _rope128(kr_raw, cos_ref[...], sin_ref[...], lane, MLA_ROPE // 2)
    else:
        ckv_ref, kr_ref, wuk_ref, wuv_ref, kcat_ref, vm_ref = refs
        c = ckv_ref[...]
        kr = kr_ref[...]
    cb = c.astype(BF16)
    kn = jnp.dot(cb, wuk_ref[...].astype(BF16), preferred_element_type=F32).astype(BF16)
    vv = jnp.dot(cb, wuv_ref[...].astype(BF16), preferred_element_type=F32).astype(BF16)
    krb = kr.astype(BF16)
    ones = jnp.ones((c.shape[0], LANES), BF16)
    for p in range(MLA_HEADS // 2):
        kcat_ref[:, 256 * p:256 * p + LANES] = kn[:, LANES * p:LANES * (p + 1)]
        kcat_ref[:, 256 * p + LANES:256 * (p + 1)] = krb
        vm_ref[:, 256 * p:256 * p + LANES] = vv[:, LANES * p:LANES * (p + 1)]
        vm_ref[:, 256 * p + LANES:256 * (p + 1)] = ones


def _mla_kv_new(zmla, cos_t, sin_t, kv_norm_g, w_uk, w_uv, j, prev):
    tm = 512
    npt = NTOK_P // tm
    nst = DEC_SEQ // tm
    nb = tm // SEQ

    def tab(i):
        return (jnp.where(i < npt, 0, 1 + (i - npt) % nst), 0)

    def cache_idx(i):
        return (jnp.minimum(i, npt - 1), j, 0, 0)

    in_specs = [
        pl.BlockSpec((tm, MLA_KV_RANK), lambda i: (i, 768 // MLA_KV_RANK)),
        pl.BlockSpec((tm, LANES), lambda i: (i, 1024 // LANES)),
        pl.BlockSpec((tm, LANES), tab),
        pl.BlockSpec((tm, LANES), tab),
        pl.BlockSpec((None, 1, MLA_KV_RANK), lambda i: (j, 0, 0)),
        pl.BlockSpec((None, MLA_KV_RANK, 512), lambda i: (j, 0, 0)),
        pl.BlockSpec((None, MLA_KV_RANK, 512), lambda i: (j, 0, 0)),
    ]
    args = [zmla, zmla, cos_t, sin_t, kv_norm_g.reshape(-1, 1, MLA_KV_RANK), w_uk, w_uv]
    aliases = {}
    if prev is not None:
        aliases = {len(args): 0, len(args) + 1: 1}
        in_specs += [pl.BlockSpec(memory_space=pl.ANY)] * 2
        args += list(prev)
    return pl.pallas_call(
        functools.partial(_mla_kv_kernel, True),
        grid=(NTOK // tm,),
        in_specs=in_specs,
        out_specs=[
            pl.BlockSpec((nb, None, SEQ, MLA_KV_RANK), cache_idx),
            pl.BlockSpec((nb, None, SEQ, MLA_ROPE), cache_idx),
            pl.BlockSpec((tm, 1024), lambda i: (i, 0)),
            pl.BlockSpec((tm, 1024), lambda i: (i, 0)),
        ],
        out_shape=[
            jax.ShapeDtypeStruct((BATCH, N_EVEN, SEQ, MLA_KV_RANK), F32),
            jax.ShapeDtypeStruct((BATCH, N_EVEN, SEQ, MLA_ROPE), F32),
            jax.ShapeDtypeStruct((NTOK, 1024), BF16),
            jax.ShapeDtypeStruct((NTOK, 1024), BF16),
        ],
        input_output_aliases=aliases,
        compiler_params=_params(("arbitrary",)),
        name="mla_kv_new",
    )(*args)


def _mla_kv_ctx(cache_ckv, kr_tiled, w_uk, w_uv, j):
    return pl.pallas_call(
        functools.partial(_mla_kv_kernel, False),
        grid=(DEC_BATCH,),
        in_specs=[
            pl.BlockSpec((None, None, PAST_LEN, MLA_KV_RANK), lambda b: (b, j, 0, 0)),
            pl.BlockSpec((None, PAST_LEN, LANES), lambda b: (b, 0, 0)),
            pl.BlockSpec((None, MLA_KV_RANK, 512), lambda b: (j, 0, 0)),
            pl.BlockSpec((None, MLA_KV_RANK, 512), lambda b: (j, 0, 0)),
        ],
        out_specs=[
            pl.BlockSpec((PAST_LEN, 1024), lambda b: (b, 0)),
            pl.BlockSpec((PAST_LEN, 1024), lambda b: (b, 0)),
        ],
        out_shape=[
            jax.ShapeDtypeStruct((DEC_BATCH * PAST_LEN, 1024), BF16),
            jax.ShapeDtypeStruct((DEC_BATCH * PAST_LEN, 1024), BF16),
        ],
        compiler_params=_params(("parallel",)),
        name="mla_kv_ctx",
    )(cache_ckv, kr_tiled, w_uk, w_uv)


def _mla_attn_kernel(latent, *refs):
    if latent:
        qn_ref, qr_ref, cos_ref, sin_ref, kc_ref, vc_ref, kn_ref, vn_ref, o_ref = refs
        k_refs, v_refs = (kc_ref, kn_ref), (vc_ref, vn_ref)
    else:
        qn_ref, qr_ref, kn_ref, vn_ref, o_ref = refs
        k_refs, v_refs = (kn_ref,), (vn_ref,)
    tq = ATT_TQ if latent else SEQ
    n_seq = qn_ref.shape[0] // tq
    lane = _lane_iota((tq, LANES))
    scale = (MLA_NOPE + MLA_ROPE) ** -0.5
    for sq in range(n_seq):
        rows = slice(tq * sq, tq * (sq + 1))
        krows = slice(None) if latent else rows
        qr_cols = []
        for cidx in range(2):
            x = qr_ref[rows, LANES * cidx:LANES * (cidx + 1)].astype(F32)
            if latent:
                x = _rope128(x, cos_ref[...], sin_ref[...], lane, MLA_ROPE // 2)
            qr_cols.append(x)
        o_prev = None
        for h in range(MLA_HEADS):
            p, half = divmod(h, 2)
            cidx, slot = divmod(h, 4)
            qa = jnp.where(_div_pow2(lane, MLA_NOPE) == half,
                           qn_ref[rows, LANES * p:LANES * (p + 1)].astype(F32), 0.0)
            qb = jnp.where(_div_pow2(lane, MLA_ROPE) == slot, qr_cols[cidx], 0.0)
            qcat = jnp.concatenate([qa, qb], axis=1).astype(BF16)
            pair_cols = slice(256 * p, 256 * (p + 1))
            s_list = [lax.dot_general(qcat, k_ref[krows, pair_cols], NT_DIMS, preferred_element_type=F32)
                      for k_ref in k_refs]
            o = _pv_normalised(_exp_parts(s_list, scale), [v_ref[krows, pair_cols] for v_ref in v_refs])
            if half == 0:
                o_prev = o
            else:
                o_ref[rows, LANES * p:LANES * (p + 1)] = jnp.where(lane < MLA_DV, o_prev, o).astype(BF16)


def _mla_attn_prompt(zmla, kcat, vm):
    rows = 4 * SEQ
    return pl.pallas_call(
        functools.partial(_mla_attn_kernel, False),
        grid=(NTOK_P // rows,),
        in_specs=[
            pl.BlockSpec((rows, 512), lambda b: (b, 0)),
            pl.BlockSpec((rows, 256), lambda b: (b, 2)),
            pl.BlockSpec((rows, 1024), lambda b: (b, 0)),
            pl.BlockSpec((rows, 1024), lambda b: (b, 0)),
        ],
        out_specs=pl.BlockSpec((rows, 512), lambda b: (b, 0)),
        out_shape=jax.ShapeDtypeStruct((NTOK_P, 512), BF16),
        compiler_params=_params(("parallel",)),
        name="mla_attn_prompt",
    )(zmla, zmla, kcat, vm)


def _mla_attn_latent(zmla, cos_t, sin_t, kcat_ctx, vm_ctx, kcat, vm):
    nq = DEC_SEQ // ATT_TQ
    row0 = NTOK_P // ATT_TQ
    seq0 = NTOK_P // DEC_SEQ
    return pl.pallas_call(
        functools.partial(_mla_attn_kernel, True),
        grid=(DEC_BATCH, nq),
        in_specs=[
            pl.BlockSpec((ATT_TQ, 512), lambda b, q: (row0 + b * nq + q, 0)),
            pl.BlockSpec((ATT_TQ, 256), lambda b, q: (row0 + b * nq + q, 2)),
            pl.BlockSpec((ATT_TQ, LANES), lambda b, q: (q, 0)),
            pl.BlockSpec((ATT_TQ, LANES), lambda b, q: (q, 0)),
            pl.BlockSpec((PAST_LEN, 1024), lambda b, q: (b, 0)),
            pl.BlockSpec((PAST_LEN, 1024), lambda b, q: (b, 0)),
            pl.BlockSpec((DEC_SEQ, 1024), lambda b, q: (seq0 + b, 0)),
            pl.BlockSpec((DEC_SEQ, 1024), lambda b, q: (seq0 + b, 0)),
        ],
        out_specs=pl.BlockSpec((ATT_TQ, 512), lambda b, q: (b * nq + q, 0)),
        out_shape=jax.ShapeDtypeStruct((NTOK_S, 512), BF16),
        compiler_params=_params(("parallel", "arbitrary")),
        name="mla_attn_latent",
    )(zmla, zmla, cos_t, sin_t, kcat_ctx, vm_ctx, kcat, vm)


def _log_sigmoid(x):
    return jnp.minimum(x, 0.0) - jnp.log1p(jnp.exp(-jnp.abs(x)))


def _retention_kernel(seq_len, n_seq, has_init, emit_state, has_prev, *refs):
    refs = list(refs)
    decf_ref, decb_ref, q_ref, k_ref, v_ref, g_ref = refs[:6]
    refs = refs[6:]
    if has_init:
        sf0_ref, sb0_ref = refs[:2]
        refs = refs[2:]
    if has_prev:
        refs = refs[2:]
    o_ref = refs[0]
    refs = refs[1:]
    if emit_state:
        sf_ref, sb_ref = refs[:2]
        refs = refs[2:]
    of_scr, ob_scr, dec_scr, wts_scr = refs

    c = RET_CHUNK
    n_chunks = seq_len // c
    pair = pl.program_id(0)
    lane = _lane_iota((c, LANES))
    zeros_half = jnp.zeros((RET_DK, RET_DV), F32)

    def log_gammas(half):
        head = 2 * pair + half
        return (_log_sigmoid(decf_ref[pl.ds(head, 1), :]),
                _log_sigmoid(decb_ref[pl.ds(head, 1), :]))

    @pl.when(pl.program_id(1) == 0)
    def _():
        ri = lax.broadcasted_iota(jnp.int32, (c, c), 0)
        ci = lax.broadcasted_iota(jnp.int32, (c, c), 1)
        rel = (ri - ci).astype(F32)
        row = lax.broadcasted_iota(jnp.int32, (c, LANES), 0).astype(F32)
        for half in range(2):
            lgf, lgb = log_gammas(half)
            dec_scr[half] = (jnp.where(rel >= 0, jnp.exp(lgf[:, 0:1] * jnp.maximum(rel, 0.0)), 0.0)
                             + jnp.where(rel <= 0, jnp.exp(lgb[:, 0:1] * jnp.maximum(-rel, 0.0)), 0.0))
            wts_scr[half, 0] = jnp.exp(lgf * (row + 1.0))
            wts_scr[half, 1] = jnp.exp(lgf * (c - 1.0 - row))
            wts_scr[half, 2] = jnp.exp(lgb * (c - row))
            wts_scr[half, 3] = jnp.exp(lgb * row)

    cross = has_init or n_chunks > 1
    chains = [(sq, half) for sq in range(n_seq) for half in range(2)]
    chunk_decay = []
    for half in range(2):
        lgf, lgb = log_gammas(half)
        chunk_decay.append((jnp.exp(lgf * float(c)), jnp.exp(lgb * float(c))))

    def rows_of(sq, n):
        start = sq * seq_len + n * c
        return pl.ds(start if isinstance(n, int) else pl.multiple_of(start, c), c)

    def load(sq, half, n):
        rows = rows_of(sq, n)
        vsl = slice(RET_DV * half, RET_DV * (half + 1))
        qm = jnp.where(_div_pow2(lane, RET_DK) == half, q_ref[rows, :].astype(F32), 0.0)
        kk = k_ref[rows, :].astype(F32) * (RET_DK ** -0.5)
        return rows, vsl, qm, kk, v_ref[rows, vsl]

    def init_state(s0_ref, half):
        if not has_init:
            return jnp.zeros((LANES, RET_DV), F32)
        s0 = s0_ref[half]
        return jnp.concatenate([s0, zeros_half] if half == 0 else [zeros_half, s0], axis=0)

    def fwd_step(sq, half, n, s_f):
        rows, vsl, qm, kk, vb = load(sq, half, n)
        s = lax.dot_general(qm.astype(BF16), kk.astype(BF16), NT_DIMS, preferred_element_type=F32)
        o = jnp.dot((s * dec_scr[half]).astype(BF16), vb, preferred_element_type=F32)
        if cross:
            o = o + jnp.dot((qm * wts_scr[half, 0]).astype(BF16), s_f.astype(BF16), preferred_element_type=F32)
        of_scr[rows, vsl] = o
        kv = lax.dot_general((kk * wts_scr[half, 1]).astype(BF16), vb, TN_DIMS, preferred_element_type=F32)
        return chunk_decay[half][0] * s_f + kv

    def bwd_step(sq, half, n, s_b):
        rows, vsl, qm, kk, vb = load(sq, half, n)
        if cross:
            ob_scr[rows, vsl] = jnp.dot((qm * wts_scr[half, 2]).astype(BF16), s_b.astype(BF16),
                                        preferred_element_type=F32)
        kv = lax.dot_general((kk * wts_scr[half, 3]).astype(BF16), vb, TN_DIMS, preferred_element_type=F32)
        return chunk_decay[half][1] * s_b + kv

    def finish(sq, n):
        rows = rows_of(sq, n)
        o2 = of_scr[rows, :] + ob_scr[rows, :] if cross else of_scr[rows, :]
        for half in range(2):
            vsl = slice(RET_DV * half, RET_DV * (half + 1))
            o = o2[:, vsl]
            mu = jnp.mean(o, axis=-1, keepdims=True)
            d = o - mu
            var = jnp.mean(d * d, axis=-1, keepdims=True)
            o_ref[rows, vsl] = (_silu(g_ref[rows, vsl].astype(F32)) * (d * lax.rsqrt(var + LN_EPS))).astype(BF16)

    s_f = tuple(init_state(sf0_ref if has_init else None, half) for _, half in chains)
    s_b = tuple(init_state(sb0_ref if has_init else None, half) for _, half in chains)
    if n_chunks == 1:
        s_f = tuple(fwd_step(sq, half, 0, s) for (sq, half), s in zip(chains, s_f))
        s_b = tuple(bwd_step(sq, half, 0, s) for (sq, half), s in zip(chains, s_b))
        for sq in range(n_seq):
            finish(sq, 0)
    else:
        def scan_step(n, carry):
            sf, sb = carry
            sf = tuple(fwd_step(sq, half, n, s) for (sq, half), s in zip(chains, sf))
            sb = tuple(bwd_step(sq, half, n_chunks - 1 - n, s) for (sq, half), s in zip(chains, sb))
            return sf, sb

        s_f, s_b = lax.fori_loop(0, n_chunks, scan_step, (s_f, s_b))

        def finish_step(n, carry):
            for sq in range(n_seq):
                finish(sq, n)
            return carry

        lax.fori_loop(0, n_chunks, finish_step, 0)
    if emit_state:
        for (sq, half), sf, sb in zip(chains, s_f, s_b):
            sf_ref[sq, half] = sf[RET_DK * half:RET_DK * (half + 1), :]
            sb_ref[sq, half] = sb[RET_DK * half:RET_DK * (half + 1), :]


def _retention(zret, decf, decb, j, latent, state_f=None, state_b=None, prev=None):
    seq_len = DEC_SEQ if latent else SEQ
    n_seq = 1 if latent else 4
    n_b = (DEC_BATCH if latent else BATCH) // n_seq
    rows = n_seq * seq_len
    row0 = NTOK_P // rows if latent else 0
    n_pairs = RET_HEADS // 2
    in_specs = [
        pl.BlockSpec((None, RET_HEADS, LANES), lambda p, b: (j, 0, 0)),
        pl.BlockSpec((None, RET_HEADS, LANES), lambda p, b: (j, 0, 0)),
        pl.BlockSpec((rows, LANES), lambda p, b: (row0 + b, p)),
        pl.BlockSpec((rows, LANES), lambda p, b: (row0 + b, 4 + p)),
        pl.BlockSpec((rows, 256), lambda p, b: (row0 + b, 4 + p)),
        pl.BlockSpec((rows, 256), lambda p, b: (row0 + b, 8 + p)),
    ]
    args = [decf, decb, zret, zret, zret, zret]
    out_specs = [pl.BlockSpec((rows, 256), lambda p, b: (b, p))]
    out_shape = [jax.ShapeDtypeStruct((n_b * rows, RET_HEADS * RET_DV), BF16)]
    aliases = {}
    if latent:
        st_spec = pl.BlockSpec((None, None, 2, RET_DK, RET_DV), lambda p, b: (b, j, p, 0, 0))
        in_specs += [st_spec, st_spec]
        args += [state_f, state_b]
    else:
        st_spec = pl.BlockSpec((n_seq, None, 2, RET_DK, RET_DV), lambda p, b: (b, j, p, 0, 0))
        out_specs += [st_spec, st_spec]
        out_shape += [jax.ShapeDtypeStruct((BATCH, N_EVEN, RET_HEADS, RET_DK, RET_DV), F32)] * 2
        if prev is not None:
            aliases = {len(args): 1, len(args) + 1: 2}
            in_specs += [pl.BlockSpec(memory_space=pl.ANY)] * 2
            args += list(prev)
    return pl.pallas_call(
        functools.partial(_retention_kernel, seq_len, n_seq, latent, not latent, bool(aliases)),
        grid=(n_pairs, n_b),
        in_specs=in_specs,
        out_specs=out_specs,
        out_shape=out_shape,
        scratch_shapes=[
            pltpu.VMEM((rows, 2 * RET_DV), F32),
            pltpu.VMEM((rows, 2 * RET_DV), F32),
            pltpu.VMEM((2, RET_CHUNK, RET_CHUNK), F32),
            pltpu.VMEM((2, 4, RET_CHUNK, LANES), F32),
        ],
        input_output_aliases=aliases,
        compiler_params=_params(("arbitrary", "arbitrary")),
        name="retention_latent" if latent else "retention_prompt",
    )(*args)


def _diff_prep_kernel(k_ref, v_ref, cos_ref, sin_ref, kr_ref, va_ref):
    lane = _lane_iota(cos_ref.shape)
    cos = cos_ref[...]
    sin = sin_ref[...]
    ones = jnp.ones(cos_ref.shape, BF16)
    for h in range(DIFF_HEADS):
        sl = slice(LANES * h, LANES * (h + 1))
        kr_ref[:, sl] = _rope128(k_ref[:, sl].astype(F32), cos, sin, lane, DIFF_DH // 2).astype(BF16)
        va_ref[:, 256 * h:256 * h + LANES] = v_ref[:, sl].astype(BF16)
        va_ref[:, 256 * h + LANES:256 * (h + 1)] = ones


def _diff_prep(zodd, cos_t, sin_t):
    tm = 512
    row0 = NTOK_P // tm
    nst = DEC_SEQ // tm
    return pl.pallas_call(
        _diff_prep_kernel,
        grid=(NTOK_S // tm,),
        in_specs=[
            pl.BlockSpec((tm, 1024), lambda i: (row0 + i, 1)),
            pl.BlockSpec((tm, 1024), lambda i: (row0 + i, 2)),
            pl.BlockSpec((tm, LANES), lambda i: (i % nst, 0)),
            pl.BlockSpec((tm, LANES), lambda i: (i % nst, 0)),
        ],
        out_specs=[pl.BlockSpec((tm, 1024), lambda i: (i, 0)), pl.BlockSpec((tm, 2048), lambda i: (i, 0))],
        out_shape=[jax.ShapeDtypeStruct((NTOK_S, 1024), BF16), jax.ShapeDtypeStruct((NTOK_S, 2048), BF16)],
        compiler_params=_params(("parallel",)),
        name="diff_rope_keys",
    )(zodd, zodd, cos_t, sin_t)


def _diff_attn_kernel(latent, lam_init, *refs):
    if latent:
        (lam_ref, ng_ref, q_ref, cos_ref, sin_ref, kc_ref, vc_ref, kn_ref, vn_ref, o_ref) = refs
    else:
        lam_ref, ng_ref, q_ref, k_ref, v_ref = refs[:5]
        o_ref, kout_ref, vout_ref = refs[-3:]
    tq = ATT_TQ if latent else SEQ
    n_seq = q_ref.shape[0] // tq
    lane = _lane_iota((tq, LANES))
    scale = DIFF_DH ** -0.5
    lp = lam_ref[...]
    lam = (jnp.exp(jnp.sum(lp[0:1, :] * lp[1:2, :], axis=-1, keepdims=True))
           - jnp.exp(jnp.sum(lp[2:3, :] * lp[3:4, :], axis=-1, keepdims=True)) + lam_init)
    ng = ng_ref[...]
    ones = jnp.ones((PAST_LEN if latent else tq, LANES), BF16)
    for sq, h in [(sq, h) for sq in range(n_seq) for h in range(DIFF_HEADS)]:
        sl = slice(LANES * h, LANES * (h + 1))
        rows = slice(tq * sq, tq * (sq + 1))
        qh = q_ref[rows, sl].astype(F32)
        if latent:
            qh = _rope128(qh, cos_ref[...], sin_ref[...], lane, DIFF_DH // 2)
            k_list = [kc_ref[h].astype(BF16), kn_ref[:, sl]]
            v_list = [jnp.concatenate([vc_ref[h].astype(BF16), ones], axis=1),
                      vn_ref[:, 256 * h:256 * (h + 1)]]
        else:
            kh = k_ref[rows, sl]
            vh = v_ref[rows, sl]
            kout_ref[sq, h] = kh.astype(F32)
            vout_ref[sq, h] = vh.astype(F32)
            k_list = [kh.astype(BF16)]
            v_list = [jnp.concatenate([vh.astype(BF16), ones], axis=1)]
        q1 = jnp.where(lane < DIFF_DH, qh, 0.0).astype(BF16)
        q2 = jnp.where(lane >= DIFF_DH, qh, 0.0).astype(BF16)
        s1 = [lax.dot_general(q1, kk, NT_DIMS, preferred_element_type=F32) for kk in k_list]
        s2 = [lax.dot_general(q2, kk, NT_DIMS, preferred_element_type=F32) for kk in k_list]
        o = _pv_normalised(_exp_parts(s1, scale), v_list) - lam * _pv_normalised(_exp_parts(s2, scale), v_list)
        y = o * lax.rsqrt(jnp.mean(o * o, axis=-1, keepdims=True) + RMS_EPS) * ng
        o_ref[rows, sl] = (y * (1.0 - lam_init)).astype(BF16)


def _diff_attn_prompt(zodd, lam_p, norm_g, j, lam_init, prev):
    n_seq = 2
    rows = n_seq * SEQ
    cache_shape = jax.ShapeDtypeStruct((BATCH, N_ODD, DIFF_HEADS, SEQ, LANES), F32)
    cache_spec = pl.BlockSpec((n_seq, None, DIFF_HEADS, SEQ, LANES), lambda b: (b, j, 0, 0, 0))
    in_specs = [
        pl.BlockSpec((None, 4, DIFF_DH), lambda b: (j, 0, 0)),
        pl.BlockSpec((None, 1, DIFF_DV), lambda b: (j, 0, 0)),
        pl.BlockSpec((rows, 1024), lambda b: (b, 0)),
        pl.BlockSpec((rows, 1024), lambda b: (b, 1)),
        pl.BlockSpec((rows, 1024), lambda b: (b, 2)),
    ]
    args = [lam_p, norm_g.reshape(-1, 1, DIFF_DV), zodd, zodd, zodd]
    aliases = {}
    if prev is not None:
        aliases = {len(args): 1, len(args) + 1: 2}
        in_specs += [pl.BlockSpec(memory_space=pl.ANY)] * 2
        args += list(prev)
    return pl.pallas_call(
        functools.partial(_diff_attn_kernel, False, lam_init),
        grid=(BATCH // n_seq,),
        in_specs=in_specs,
        out_specs=[pl.BlockSpec((rows, 1024), lambda b: (b, 0)), cache_spec, cache_spec],
        out_shape=[jax.ShapeDtypeStruct((NTOK_P, 1024), BF16), cache_shape, cache_shape],
        input_output_aliases=aliases,
        compiler_params=_params(("arbitrary",)),
        name="diff_attn_prompt",
    )(*args)


def _diff_attn_latent(zodd, lam_p, norm_g, cos_t, sin_t, cache_k, cache_v, k_rot, v_aug, j, lam_init):
    nq = DEC_SEQ // ATT_TQ
    row0 = NTOK_P // ATT_TQ
    ctx_spec = pl.BlockSpec((None, None, DIFF_HEADS, PAST_LEN, LANES), lambda b, q: (b, j, 0, 0, 0))
    return pl.pallas_call(
        functools.partial(_diff_attn_kernel, True, lam_init),
        grid=(DEC_BATCH, nq),
        in_specs=[
            pl.BlockSpec((None, 4, DIFF_DH), lambda b, q: (j, 0, 0)),
            pl.BlockSpec((None, 1, DIFF_DV), lambda b, q: (j, 0, 0)),
            pl.BlockSpec((ATT_TQ, 1024), lambda b, q: (row0 + b * nq + q, 0)),
            pl.BlockSpec((ATT_TQ, LANES), lambda b, q: (q, 0)),
            pl.BlockSpec((ATT_TQ, LANES), lambda b, q: (q, 0)),
            ctx_spec,
            ctx_spec,
            pl.BlockSpec((DEC_SEQ, 1024), lambda b, q: (b, 0)),
            pl.BlockSpec((DEC_SEQ, 2048), lambda b, q: (b, 0)),
        ],
        out_specs=pl.BlockSpec((ATT_TQ, 1024), lambda b, q: (b * nq + q, 0)),
        out_shape=jax.ShapeDtypeStruct((NTOK_S, 1024), BF16),
        compiler_params=_params(("parallel", "arbitrary")),
        name="diff_attn_latent",
    )(lam_p, norm_g.reshape(-1, 1, DIFF_DV), zodd, cos_t, sin_t, cache_k, cache_v, k_rot, v_aug)


def _mla_weight(w_in):
    base = RET_COLS
    mq = w_in[:, base:base + MLA_HEADS * (MLA_NOPE + MLA_ROPE)].reshape(D_MODEL, MLA_HEADS, MLA_NOPE + MLA_ROPE)
    qn = mq[:, :, :MLA_NOPE].reshape(D_MODEL, MLA_HEADS * MLA_NOPE)
    qr = mq[:, :, MLA_NOPE:].reshape(D_MODEL, MLA_HEADS * MLA_ROPE)
    ckv0 = base + MLA_HEADS * (MLA_NOPE + MLA_ROPE)
    ckv = w_in[:, ckv0:ckv0 + MLA_KV_RANK]
    kr = w_in[:, ckv0 + MLA_KV_RANK:]
    return jnp.concatenate([qn, qr, ckv, jnp.tile(kr, (1, LANES // MLA_ROPE))], axis=1)


def kernel(x_prompt, x_sample, state_ret_fwd, state_ret_bwd, cache_mla_ckv, cache_mla_krope, cache_diff_k, cache_diff_v, c, c_ctx, ada_w, ada_b, ln1_g, ln1_b, ln2_g, ln2_b, ev_w_in, ev_w_out, ret_decay_fwd, ret_decay_bwd, mla_kv_norm_g, mla_w_uk, mla_w_uv, od_w_in, od_w_out, diff_lambda, diff_norm_g, moe_w_group, moe_b_group, moe_w_expert, moe_b_expert, moe_w1, moe_w3, moe_w2):
    x = (x_prompt.reshape(NTOK_P, D_MODEL), x_sample.reshape(NTOK_S, D_MODEL), 0)
    cond =jnp.concatenate([c_ctx[None, :], c, jnp.zeros((N_COND - 1 - DEC_BATCH, D_MODEL), F32)], axis=0)
    mods = _ada_all(cond, ada_w, ada_b).reshape(DEPTH, N_COND, 6, D_MODEL)

    cos_m, sin_m = _rope_tables(MLA_ROPE)
    cos_d, sin_d = _rope_tables(DIFF_DH)
    ident = 512
    cos_m_id = jnp.concatenate([jnp.ones((ident, LANES), F32), cos_m], axis=0)
    sin_m_id = jnp.concatenate([jnp.zeros((ident, LANES), F32), sin_m], axis=0)
    decf = jnp.broadcast_to(ret_decay_fwd[:, :, None], ret_decay_fwd.shape + (LANES,))
    decb = jnp.broadcast_to(ret_decay_bwd[:, :, None], ret_decay_bwd.shape + (LANES,))

    pad = LANES - MOE_GROUPS - MOE_EXPERTS
    ret_states = mla_caches = diff_caches = None
    for i in range(DEPTH):
        j = i // 2
        mod = mods[i]
        w_router = jnp.concatenate([moe_w_group[i], moe_w_expert[i], jnp.zeros((D_MODEL, pad), F32)], axis=1)
        b_router = jnp.concatenate([moe_b_group[i], moe_b_expert[i], jnp.zeros((pad,), F32)])[None, :]
        if i % 2 == 0:
            zret = _mm_mod(x, mod, ev_w_in, (j,), RET_COLS, 1024, "in_proj_retention")
            zmla = _mm_mod(x, mod, _mla_weight(ev_w_in[j]), (), MLA_COLS, MLA_COLS, "in_proj_mla")
            *mla_caches, kcat, vm = _mla_kv_new(zmla, cos_m_id, sin_m_id, mla_kv_norm_g, mla_w_uk, mla_w_uv,
                                                j, mla_caches)
            kr_ctx = jnp.tile(cache_mla_krope[:, j], (1, 1, LANES // MLA_ROPE))
            kcat_ctx, vm_ctx = _mla_kv_ctx(cache_mla_ckv, kr_ctx, mla_w_uk, mla_w_uv, j)
            a_ret_p, *ret_states = _retention(zret, decf, decb, j, False, prev=ret_states)
            (a_ret_s,) = _retention(zret, decf, decb, j, True, state_ret_fwd, state_ret_bwd)
            a_mla_p = _mla_attn_prompt(zmla, kcat, vm)
            a_mla_s = _mla_attn_latent(zmla, cos_m, sin_m, kcat_ctx, vm_ctx, kcat, vm)
            x1, meta, counts = _mm_ln([(a_ret_p, a_ret_s), (a_mla_p, a_mla_s)], ev_w_out, j, x, mod,
                                      ln1_g, ln1_b, w_router, b_router, i)
        else:
            lam_init = 0.8 - 0.6 * math.exp(-0.3 * i)
            zodd = _mm_mod(x, mod, od_w_in, (j,), 3072, 1024, "in_proj_diff")
            a_p, *diff_caches = _diff_attn_prompt(zodd, diff_lambda, diff_norm_g, j, lam_init, diff_caches)
            k_rot, v_aug = _diff_prep(zodd, cos_d, sin_d)
            a_s = _diff_attn_latent(zodd, diff_lambda, diff_norm_g, cos_d, sin_d, cache_diff_k, cache_diff_v,
                                    k_rot, v_aug, j, lam_init)
            x1, meta, counts = _mm_ln([(a_p, a_s)], od_w_out, j, x, mod, ln1_g, ln1_b, w_router, b_router, i)
        y = _moe(x1, _moe_plan(meta, counts), mods, w_router, b_router, moe_w1, moe_w3, moe_w2, ln2_g, ln2_b, i)
        x = (y, y, NTOK_P)

    y_prompt = x[0][:NTOK_P].reshape(BATCH, SEQ, D_MODEL)
    y_sample = x[1][NTOK_P:NTOK].reshape(DEC_BATCH, DEC_SEQ, D_MODEL)
    return (y_prompt, y_sample, ret_states[0], ret_states[1], mla_caches[0], mla_caches[1],
            diff_caches[0], diff_caches[1])
```

```python
import functools
import math

import jax
import jax.numpy as jnp
from jax import lax
from jax.experimental import pallas as pl
from jax.experimental.pallas import tpu as pltpu

D_MODEL = 1024
BATCH = 32
SEQ = 256
DEPTH = 4
N_EVEN = 2
N_ODD = 2
DEC_BATCH = 2
DEC_SEQ = 2048
PAST_LEN = 256
GRID_W = 64
LN_EPS = 1e-5
RMS_EPS = 1e-6
DEEPNORM_ALPHA = (2.0 * DEPTH) ** 0.25
ROPE_BASE = 10000.0
RET_HEADS = 8
RET_DK = 64
RET_DV = 128
MLA_HEADS = 8
MLA_NOPE = 64
MLA_ROPE = 32
MLA_DV = 64
MLA_KV_RANK = 256
DIFF_HEADS = 8
DIFF_DH = 64
DIFF_DV = 128
MOE_GROUPS = 4
MOE_PER_GROUP = 4
MOE_EXPERTS = 16
MOE_FF = 256

NTOK_P = BATCH * SEQ
NTOK_S = DEC_BATCH * DEC_SEQ
NTOK = NTOK_P + NTOK_S
N_COND = 8
LANES = 128
SUBLANES = 8
RET_COLS = 3072
MLA_COLS = 1152
ATT_TQ = 256
RET_CHUNK = 256
MOE_TILE = 512
MOE_TILES = (NTOK + MOE_GROUPS * (MOE_TILE - 1)) // MOE_TILE
MOE_ROWS = MOE_TILES * MOE_TILE
VMEM_LIMIT = 56 * 1024 * 1024
LOG2E = 1.4426950408889634

F32 = jnp.float32
BF16 = jnp.bfloat16
NT_DIMS = (((1,), (1,)), ((), ()))
TN_DIMS = (((0,), (0,)), ((), ()))


def _params(sem):
    return pltpu.CompilerParams(dimension_semantics=sem, vmem_limit_bytes=VMEM_LIMIT)


def _group_of_tile(i, tm):
    npt = NTOK_P // tm
    nst = DEC_SEQ // tm
    return jnp.where(i < npt, 0, 1 + (i - npt) // nst)


def _split_specs(tm, width, m_of, s_row0=0):
    npt = NTOK_P // tm
    s_blk0 = s_row0 // tm
    return [pl.BlockSpec((tm, width), lambda *g: (jnp.minimum(m_of(*g), npt - 1), 0)),
            pl.BlockSpec((tm, width), lambda *g: (jnp.maximum(m_of(*g) - npt, 0) + s_blk0, 0))]


def _read_split(p_ref, s_ref, m):
    return jnp.where(m < NTOK_P // p_ref.shape[0], p_ref[...], s_ref[...])


def _silu(x):
    return x * (1.0 / (1.0 + jnp.exp(-x)))


def _layer_norm(r, g, b):
    mu = jnp.mean(r, axis=-1, keepdims=True)
    d = r - mu
    var = jnp.mean(d * d, axis=-1, keepdims=True)
    return d * lax.rsqrt(var + LN_EPS) * g + b


def _lane_iota(shape):
    return lax.broadcasted_iota(jnp.int32, shape, 1)


def _div_pow2(x, d):
    assert d & (d - 1) == 0
    return jnp.right_shift(x, d.bit_length() - 1)


def _mod_pow2(x, d):
    assert d & (d - 1) == 0
    return jnp.bitwise_and(x, d - 1)


def _ada_kernel(c_ref, w_ref, b_ref, o_ref):
    h = _silu(c_ref[...]).astype(BF16)
    o_ref[...] = jnp.dot(h, w_ref[...].astype(BF16), preferred_element_type=F32) + b_ref[...]


def _ada_all(cond, ada_w, ada_b):
    tn = 768
    return pl.pallas_call(
        _ada_kernel,
        grid=(DEPTH, 6 * D_MODEL // tn),
        in_specs=[
            pl.BlockSpec((N_COND, D_MODEL), lambda l, n: (0, 0)),
            pl.BlockSpec((None, D_MODEL, tn), lambda l, n: (l, 0, n)),
            pl.BlockSpec((None, 1, tn), lambda l, n: (l, 0, n)),
        ],
        out_specs=pl.BlockSpec((None, N_COND, tn), lambda l, n: (l, 0, n)),
        out_shape=jax.ShapeDtypeStruct((DEPTH, N_COND, 6 * D_MODEL), F32),
        compiler_params=_params(("parallel", "parallel")),
        name="ada_modulation",
    )(cond, ada_w, ada_b.reshape(DEPTH, 1, 6 * D_MODEL))


def _mm_mod_kernel(w_is_transposed, xp_ref, xs_ref, mod_ref, w_ref, o_ref, wscr):
    m = pl.program_id(1)

    @pl.when(m == 0)
    def _():
        wscr[...] = w_ref[...].astype(BF16)

    sh = mod_ref[0:1, :]
    sc = mod_ref[1:2, :]
    xm = (_read_split(xp_ref, xs_ref, m) * (1.0 + sc) + sh).astype(BF16)
    if w_is_transposed:
        z = lax.dot_general(xm, wscr[...], NT_DIMS, preferred_element_type=F32)
    else:
        z = jnp.dot(xm, wscr[...], preferred_element_type=F32)
    o_ref[...] = z.astype(o_ref.dtype)


def _mm_mod(x, mod, w, w_index, n_cols, tn, name, w_is_transposed=False):
    tm = 1024
    if w_is_transposed:
        w_spec = pl.BlockSpec((None,) * len(w_index) + (tn, D_MODEL), lambda n, m: tuple(w_index) + (n, 0))
        w_scratch = pltpu.VMEM((tn, D_MODEL), BF16)
    else:
        w_spec = pl.BlockSpec((None,) * len(w_index) + (D_MODEL, tn), lambda n, m: tuple(w_index) + (0, n))
        w_scratch = pltpu.VMEM((D_MODEL, tn), BF16)
    return pl.pallas_call(
        functools.partial(_mm_mod_kernel, w_is_transposed),
        grid=(n_cols // tn, NTOK // tm),
        in_specs=_split_specs(tm, D_MODEL, lambda n, m: m, x[2]) + [
            pl.BlockSpec((None, 6, D_MODEL), lambda n, m: (_group_of_tile(m, tm), 0, 0)),
            w_spec,
        ],
        out_specs=pl.BlockSpec((tm, tn), lambda n, m: (m, n)),
        out_shape=jax.ShapeDtypeStruct((NTOK, n_cols), BF16),
        scratch_shapes=[w_scratch],
        compiler_params=_params(("arbitrary", "arbitrary")),
        name=name,
    )(x[0], x[1], mod, w)


def _router_probs(xm, wr_ref, br_ref):
    rows = xm.shape[0]
    z = jnp.dot(xm, wr_ref[...].astype(BF16), preferred_element_type=F32) + br_ref[...]
    lane_i = _lane_iota((rows, LANES))
    lane = lane_i.astype(F32)
    gmask = lane_i < MOE_GROUPS
    zg = jnp.where(gmask, z, -jnp.inf)
    pg = jnp.exp(zg - jnp.max(zg, axis=-1, keepdims=True))
    g_prob = pg / jnp.sum(pg, axis=-1, keepdims=True)
    g_p = jnp.max(g_prob, axis=-1, keepdims=True)
    g_idx = jnp.min(jnp.where(gmask & (g_prob == g_p), lane, float(LANES)), axis=-1, keepdims=True)
    return z, lane_i, lane, g_p, g_idx


def _mm_ln_kernel(k_sizes, *refs):
    n_a = len(k_sizes)
    a_refs = refs[:2 * n_a]
    (w_ref, xp_ref, xs_ref, mod_ref, g_ref, b_ref, wr_ref, br_ref,
     o_ref, meta_ref, cnt_ref, wscr, tri_scr, carry_scr) = refs[2 * n_a:]
    m = pl.program_id(0)
    tm = o_ref.shape[0]

    @pl.when(m == 0)
    def _():
        wscr[...] = w_ref[...].astype(BF16)
        ri = lax.broadcasted_iota(jnp.int32, (tm, tm), 0)
        ci = lax.broadcasted_iota(jnp.int32, (tm, tm), 1)
        tri_scr[...] = jnp.where(ci < ri, 1.0, 0.0).astype(BF16)
        carry_scr[...] = jnp.zeros_like(carry_scr)

    y = None
    k0 = 0
    for i, ks in enumerate(k_sizes):
        a = _read_split(a_refs[2 * i], a_refs[2 * i + 1], m)
        part = jnp.dot(a, wscr[k0:k0 + ks, :], preferred_element_type=F32)
        y = part if y is None else y + part
        k0 += ks
    gate = mod_ref[2:3, :]
    r = DEEPNORM_ALPHA * _read_split(xp_ref, xs_ref, m) + gate * y
    x1 = _layer_norm(r, g_ref[...], b_ref[...])
    o_ref[...] = x1

    xm = (x1 * (1.0 + mod_ref[4:5, :]) + mod_ref[3:4, :]).astype(BF16)
    _, lane_i, lane, _, g_idx = _router_probs(xm, wr_ref, br_ref)
    onehot = jnp.where(lane == g_idx, 1.0, 0.0)
    before = jnp.dot(tri_scr[...], onehot.astype(BF16), preferred_element_type=F32) + carry_scr[0:1, :]
    rank = jnp.sum(jnp.where(lane == g_idx, before, 0.0), axis=-1, keepdims=True)
    meta_ref[...] = jnp.where(lane_i == 0, g_idx, jnp.where(lane_i == 1, rank, 0.0))
    total = carry_scr[0:1, :] + jnp.sum(onehot, axis=0, keepdims=True)
    carry_scr[...] = jnp.broadcast_to(total, carry_scr.shape)
    cnt_ref[...] = jnp.broadcast_to(total, cnt_ref.shape)


def _mm_ln(a_pairs, w, j, x, mod, ln_g, ln_b, w_router, b_router, layer):
    tm = 512
    k_sizes = tuple(ap.shape[1] for ap, _ in a_pairs)
    k_tot = sum(k_sizes)
    in_specs = []
    args = []
    for (ap, a_s), ks in zip(a_pairs, k_sizes):
        in_specs += _split_specs(tm, ks, lambda m: m)
        args += [ap, a_s]
    in_specs += [pl.BlockSpec((None, k_tot, D_MODEL), lambda m: (j, 0, 0))]
    in_specs += _split_specs(tm, D_MODEL, lambda m: m, x[2])
    in_specs += [
        pl.BlockSpec((None, 6, D_MODEL), lambda m: (_group_of_tile(m, tm), 0, 0)),
        pl.BlockSpec((None, 1, D_MODEL), lambda m: (layer, 0, 0)),
        pl.BlockSpec((None, 1, D_MODEL), lambda m: (layer, 0, 0)),
        pl.BlockSpec((D_MODEL, LANES), lambda m: (0, 0)),
        pl.BlockSpec((1, LANES), lambda m: (0, 0)),
    ]
    return pl.pallas_call(
        functools.partial(_mm_ln_kernel, k_sizes),
        grid=(NTOK // tm,),
        in_specs=in_specs,
        out_specs=[
            pl.BlockSpec((tm, D_MODEL), lambda m: (m, 0)),
            pl.BlockSpec((tm, LANES), lambda m: (m, 0)),
            pl.BlockSpec((N_COND, LANES), lambda m: (0, 0)),
        ],
        out_shape=[
            jax.ShapeDtypeStruct((NTOK, D_MODEL), F32),
            jax.ShapeDtypeStruct((NTOK, LANES), F32),
            jax.ShapeDtypeStruct((N_COND, LANES), F32),
        ],
        scratch_shapes=[
            pltpu.VMEM((k_tot, D_MODEL), BF16),
            pltpu.VMEM((tm, tm), BF16),
            pltpu.VMEM((N_COND, LANES), F32),
        ],
        compiler_params=_params(("arbitrary",)),
        name="out_proj_ln",
    )(*args, w, x[0], x[1], mod, ln_g.reshape(DEPTH, 1, D_MODEL), ln_b.reshape(DEPTH, 1, D_MODEL),
      w_router, b_router)


def _moe_kernel(tgrp_ref, ntile_ref, src0_ref, src1_ref, dst_ref, mid_ref, x_hbm, mod_ref, wr_ref, br_ref,
                w1_ref, w3_ref, w2_ref, g_ref, b_ref, y_hbm,
                gbuf, obuf, gsem, ssem, w13s, w2s):
    i = pl.program_id(0)
    n_steps = pl.num_programs(0)
    n_tiles = ntile_ref[0]
    n_blk = gbuf.shape[1]
    ts = n_blk * SUBLANES
    slot = lax.rem(i, 2)

    def start_gather(src_ref, s):
        def body(k, carry):
            for u in range(SUBLANES):
                tok = src_ref[0, k * SUBLANES + u]
                pltpu.make_async_copy(x_hbm.at[pl.ds(tok, 1), :], gbuf.at[s, k, pl.ds(u, 1), :],
                                      gsem.at[s]).start()
            return carry
        lax.fori_loop(0, n_blk, body, 0)

    def wait_gather(s):
        pltpu.make_async_copy(gbuf.at[s], gbuf.at[s], gsem.at[s]).wait()

    def start_scatter(s):
        def body(k, carry):
            for u in range(SUBLANES):
                tok = dst_ref[0, k * SUBLANES + u]
                pltpu.make_async_copy(obuf.at[s, k, pl.ds(u, 1), :], y_hbm.at[pl.ds(tok, 1), :],
                                      ssem.at[s]).start()
            return carry
        lax.fori_loop(0, n_blk, body, 0)

    def wait_scatter(s):
        pltpu.make_async_copy(obuf.at[s], obuf.at[s], ssem.at[s]).wait()

    @pl.when(i == 0)
    def _():
        start_gather(src0_ref, 0)

    @pl.when(i < n_tiles)
    def _():
        grp = tgrp_ref[i]
        wait_gather(slot)

        @pl.when(i + 1 < n_tiles)
        def _():
            start_gather(src1_ref, 1 - slot)

        @pl.when((i == 0) | (grp != tgrp_ref[jnp.maximum(i - 1, 0)]))
        def _():
            w13s[:, :, :MOE_FF] = w1_ref[...].astype(BF16)
            w13s[:, :, MOE_FF:] = w3_ref[...].astype(BF16)
            w2s[...] = w2_ref[...].astype(BF16)

        mid_rows = mid_ref[...].astype(F32)
        spread = jnp.concatenate([jnp.broadcast_to(mid_rows[a:a + 1, :], (LANES, LANES))
                                  for a in range(ts // LANES)], axis=0)
        row_i = lax.broadcasted_iota(jnp.int32, (ts, LANES), 0)
        mid = jnp.sum(jnp.where(_lane_iota((ts, LANES)) == _mod_pow2(row_i, LANES), spread, 0.0),
                      axis=-1, keepdims=True)

        def mod_row(k):
            return jnp.where(mid == 0, mod_ref[0, k:k + 1, :],
                             jnp.where(mid == 1, mod_ref[1, k:k + 1, :], mod_ref[2, k:k + 1, :]))

        x1 = gbuf[slot].reshape(ts, D_MODEL)
        xm = (x1 * (1.0 + mod_row(4)) + mod_row(3)).astype(BF16)
        z, lane_i, lane, g_p, _ = _router_probs(xm, wr_ref, br_ref)
        e0 = MOE_GROUPS + MOE_PER_GROUP * grp
        emask = (lane_i >= e0) & (lane_i < e0 + MOE_PER_GROUP)
        ze = jnp.where(emask, z, -jnp.inf)
        pe = jnp.exp(ze - jnp.max(ze, axis=-1, keepdims=True))
        e_prob = pe / jnp.sum(pe, axis=-1, keepdims=True)
        cand = jnp.where(emask, e_prob, -1.0)
        p1 = jnp.max(cand, axis=-1, keepdims=True)
        i1 = jnp.min(jnp.where(cand == p1, lane, float(LANES)), axis=-1, keepdims=True)
        cand2 = jnp.where(lane == i1, -1.0, cand)
        p2 = jnp.max(cand2, axis=-1, keepdims=True)
        i2 = jnp.min(jnp.where(cand2 == p2, lane, float(LANES)), axis=-1, keepdims=True)
        denom = p1 + p2
        comb = jnp.where(lane == i1, g_p * p1 / denom, 0.0) + jnp.where(lane == i2, g_p * p2 / denom, 0.0)
        y = None
        for e in range(MOE_PER_GROUP):
            c = jnp.sum(jnp.where(lane_i == e0 + e, comb, 0.0), axis=-1, keepdims=True)
            h = jnp.dot(xm, w13s[e], preferred_element_type=F32)
            hid = (_silu(h[:, :MOE_FF]) * h[:, MOE_FF:] * c).astype(BF16)
            part = jnp.dot(hid, w2s[e], preferred_element_type=F32)
            y = part if y is None else y + part
        r = DEEPNORM_ALPHA * x1 + mod_row(5) * y

        @pl.when(i >= 2)
        def _():
            wait_scatter(slot)

        obuf[slot] = _layer_norm(r, g_ref[...], b_ref[...]).reshape(n_blk, SUBLANES, D_MODEL)
        start_scatter(slot)

    @pl.when(i == n_steps - 1)
    def _():
        @pl.when(n_tiles >= 2)
        def _():
            wait_scatter(lax.rem(n_tiles, 2))

        @pl.when(n_tiles >= 1)
        def _():
            wait_scatter(lax.rem(n_tiles + 1, 2))


def _inverse_perm_kernel(pos_ref, out_ref):
    def clear(s, carry):
        out_ref[s] = 0
        return carry

    def place(t, carry):
        out_ref[pos_ref[t]] = t + 1
        return carry

    lax.fori_loop(0, out_ref.shape[0], clear, 0, unroll=8)
    lax.fori_loop(0, pos_ref.shape[0], place, 0, unroll=8)


def _inverse_perm(pos):
    return pl.pallas_call(
        _inverse_perm_kernel,
        in_specs=[pl.BlockSpec(memory_space=pltpu.SMEM)],
        out_specs=pl.BlockSpec(memory_space=pltpu.SMEM),
        out_shape=jax.ShapeDtypeStruct((MOE_ROWS,), jnp.int32),
        name="moe_inverse_perm",
    )(pos)


def _moe_plan(meta, counts):
    ts = MOE_TILE
    cnt = counts[0, :MOE_GROUPS].astype(jnp.int32)
    tiles_g = (cnt + ts - 1) // ts
    tile_end = jnp.cumsum(tiles_g)
    row0_g = (tile_end - tiles_g) * ts
    gid = meta[:, 0].astype(jnp.int32)
    rank = meta[:, 1].astype(jnp.int32)
    pos = row0_g[gid] + rank
    tok1 = _inverse_perm(pos)
    rows = jnp.arange(MOE_ROWS, dtype=jnp.int32)
    src = jnp.maximum(tok1 - 1, 0)
    spare = NTOK + ((rows // ts) % 2) * ts + rows % ts
    dst = jnp.where(tok1 > 0, src, spare)
    mid = jnp.where(src < NTOK_P, 0, 1 + (src - NTOK_P) // DEC_SEQ)
    tile_group = jnp.minimum(jnp.sum(jnp.arange(MOE_TILES)[:, None] >= tile_end[None, :], axis=1),
                             MOE_GROUPS - 1).astype(jnp.int32)
    n_tiles = tile_end[-1:].astype(jnp.int32)
    return (tile_group, n_tiles, src.reshape(MOE_TILES, 1, ts), dst.reshape(MOE_TILES, 1, ts),
            mid.reshape(MOE_TILES, ts // LANES, LANES))


def _moe(x1, plan, mods, w_router, b_router, w1, w3, w2, ln_g, ln_b, layer):
    ts = MOE_TILE
    tile_group, n_tiles, src, dst, mid = plan
    grp_shape = (DEPTH, MOE_GROUPS, MOE_PER_GROUP)
    w_in_spec = pl.BlockSpec((None, None, MOE_PER_GROUP, D_MODEL, MOE_FF), lambda i, tg, nt: (layer, tg[i], 0, 0, 0))
    w_out_spec = pl.BlockSpec((None, None, MOE_PER_GROUP, MOE_FF, D_MODEL), lambda i, tg, nt: (layer, tg[i], 0, 0, 0))
    smem_tile = functools.partial(pl.BlockSpec, (None, 1, ts), memory_space=pltpu.SMEM)
    grid_spec = pltpu.PrefetchScalarGridSpec(
        num_scalar_prefetch=2,
        grid=(MOE_TILES,),
        in_specs=[
            smem_tile(lambda i, tg, nt: (i, 0, 0)),
            smem_tile(lambda i, tg, nt: (jnp.minimum(i + 1, MOE_TILES - 1), 0, 0)),
            smem_tile(lambda i, tg, nt: (i, 0, 0)),
            pl.BlockSpec((None, ts // LANES, LANES), lambda i, tg, nt: (i, 0, 0)),
            pl.BlockSpec(memory_space=pl.ANY),
            pl.BlockSpec((None, N_COND, 6, D_MODEL), lambda i, tg, nt: (layer, 0, 0, 0)),
            pl.BlockSpec((D_MODEL, LANES), lambda i, tg, nt: (0, 0)),
            pl.BlockSpec((1, LANES), lambda i, tg, nt: (0, 0)),
            w_in_spec,
            w_in_spec,
            w_out_spec,
            pl.BlockSpec((None, 1, D_MODEL), lambda i, tg, nt: (layer, 0, 0)),
            pl.BlockSpec((None, 1, D_MODEL), lambda i, tg, nt: (layer, 0, 0)),
        ],
        out_specs=pl.BlockSpec(memory_space=pl.ANY),
        scratch_shapes=[
            pltpu.VMEM((2, ts // SUBLANES, SUBLANES, D_MODEL), F32),
            pltpu.VMEM((2, ts // SUBLANES, SUBLANES, D_MODEL), F32),
            pltpu.SemaphoreType.DMA((2,)),
            pltpu.SemaphoreType.DMA((2,)),
            pltpu.VMEM((MOE_PER_GROUP, D_MODEL, 2 * MOE_FF), BF16),
            pltpu.VMEM((MOE_PER_GROUP, MOE_FF, D_MODEL), BF16),
        ],
    )
    return pl.pallas_call(
        _moe_kernel,
        grid_spec=grid_spec,
        out_shape=jax.ShapeDtypeStruct((NTOK + 2 * ts, D_MODEL), F32),
        compiler_params=_params(("arbitrary",)),
        name="hier_moe_ln",
    )(tile_group, n_tiles, src, src, dst, mid, x1, mods, w_router, b_router,
      w1.reshape(grp_shape + (D_MODEL, MOE_FF)), w3.reshape(grp_shape + (D_MODEL, MOE_FF)),
      w2.reshape(grp_shape + (MOE_FF, D_MODEL)),
      ln_g.reshape(DEPTH, 1, D_MODEL), ln_b.reshape(DEPTH, 1, D_MODEL))


def _swap_halves(x, lane, half):
    return jnp.where(_mod_pow2(lane, 2 * half) < half,
                     pltpu.roll(x, LANES - half, 1), pltpu.roll(x, half, 1))


def _rope128(x, cos, sin_signed, lane, half):
    return x * cos + _swap_halves(x, lane, half) * sin_signed


def _rope_tables(rot_dim):
    rows = DEC_SEQ // GRID_W
    row = jnp.repeat(jnp.arange(rows, dtype=F32), GRID_W)
    col = jnp.tile(jnp.arange(GRID_W, dtype=F32), rows)
    n_freq = rot_dim // 4
    inv_freq = ROPE_BASE ** (-jnp.arange(n_freq, dtype=F32) / n_freq)
    ang = jnp.concatenate([row[:, None] * inv_freq, col[:, None] * inv_freq], axis=-1)
    cos, sin = jnp.cos(ang), jnp.sin(ang)
    reps = LANES // rot_dim
    cos_full = jnp.tile(jnp.concatenate([cos, cos], axis=-1), (1, reps))
    sin_signed = jnp.tile(jnp.concatenate([-sin, sin], axis=-1), (1, reps))
    return cos_full, sin_signed


def _exp_parts(s_list, scale):
    m = None
    for s in s_list:
        sm = jnp.max(s, axis=-1, keepdims=True)
        m = sm if m is None else jnp.maximum(m, sm)
    return [jnp.exp2((s - m) * (scale * LOG2E)).astype(BF16) for s in s_list]


def _pv_normalised(p_list, v_list):
    o = None
    for p, v in zip(p_list, v_list):
        part = jnp.dot(p, v, preferred_element_type=F32)
        o = part if o is None else o + part
    return o[:, :LANES] / o[:, LANES:]


def _mla_kv_kernel(new_tokens, *refs):
    if new_tokens:
        (ckv_ref, kr_ref, cos_ref, sin_ref, g_ref, wuk_ref, wuv_ref) = refs[:7]
        ckvc_ref, krc_ref, kcat_ref, vm_ref = refs[-4:]
        i = pl.program_id(0)
        x = ckv_ref[...].astype(F32)
        c = x * lax.rsqrt(jnp.mean(x * x, axis=-1, keepdims=True) + RMS_EPS) * g_ref[...]
        kr_raw = kr_ref[...].astype(F32)

        @pl.when(i < NTOK_P // ckv_ref.shape[0])
        def _():
            for b in range(ckvc_ref.shape[0]):
                ckvc_ref[b] = c[SEQ * b:SEQ * (b + 1), :]
                krc_ref[b] = kr_raw[SEQ * b:SEQ * (b + 1), :MLA_ROPE]

        lane = _lane_iota(kr_ref.shape)
        kr = _rope128(kr_raw, cos_ref[...], sin_ref[...], lane, MLA_ROPE // 2)
    else:
        ckv_ref, kr_ref, wuk_ref, wuv_ref, kcat_ref, vm_ref = refs
        c = ckv_ref[...]
        kr = kr_ref[...]
    cb = c.astype(BF16)
    kn = jnp.dot(cb, wuk_ref[...].astype(BF16), preferred_element_type=F32).astype(BF16)
    vv = jnp.dot(cb, wuv_ref[...].astype(BF16), preferred_element_type=F32).astype(BF16)
    krb = kr.astype(BF16)
    ones = jnp.ones((c.shape[0], LANES), BF16)
    for p in range(MLA_HEADS // 2):
        kcat_ref[:, 256 * p:256 * p + LANES] = kn[:, LANES * p:LANES * (p + 1)]
        kcat_ref[:, 256 * p + LANES:256 * (p + 1)] = krb
        vm_ref[:, 256 * p:256 * p + LANES] = vv[:, LANES * p:LANES * (p + 1)]
        vm_ref[:, 256 * p + LANES:256 * (p + 1)] = ones


def _mla_kv_new(zmla, cos_t, sin_t, kv_norm_g, w_uk, w_uv, j, prev):
    tm = 512
    npt = NTOK_P // tm
    nst = DEC_SEQ // tm
    nb = tm // SEQ

    def tab(i):
        return (jnp.where(i < npt, 0, 1 + (i - npt) % nst), 0)

    def cache_idx(i):
        return (jnp.minimum(i, npt - 1), j, 0, 0)

    in_specs = [
        pl.BlockSpec((tm, MLA_KV_RANK), lambda i: (i, 768 // MLA_KV_RANK)),
        pl.BlockSpec((tm, LANES), lambda i: (i, 1024 // LANES)),
        pl.BlockSpec((tm, LANES), tab),
        pl.BlockSpec((tm, LANES), tab),
        pl.BlockSpec((None, 1, MLA_KV_RANK), lambda i: (j, 0, 0)),
        pl.BlockSpec((None, MLA_KV_RANK, 512), lambda i: (j, 0, 0)),
        pl.BlockSpec((None, MLA_KV_RANK, 512), lambda i: (j, 0, 0)),
    ]
    args = [zmla, zmla, cos_t, sin_t, kv_norm_g.reshape(-1, 1, MLA_KV_RANK), w_uk, w_uv]
    aliases = {}
    if prev is not None:
        aliases = {len(args): 0, len(args) + 1: 1}
        in_specs += [pl.BlockSpec(memory_space=pl.ANY)] * 2
        args += list(prev)
    return pl.pallas_call(
        functools.partial(_mla_kv_kernel, True),
        grid=(NTOK // tm,),
        in_specs=in_specs,
        out_specs=[
            pl.BlockSpec((nb, None, SEQ, MLA_KV_RANK), cache_idx),
            pl.BlockSpec((nb, None, SEQ, MLA_ROPE), cache_idx),
            pl.BlockSpec((tm, 1024), lambda i: (i, 0)),
            pl.BlockSpec((tm, 1024), lambda i: (i, 0)),
        ],
        out_shape=[
            jax.ShapeDtypeStruct((BATCH, N_EVEN, SEQ, MLA_KV_RANK), F32),
            jax.ShapeDtypeStruct((BATCH, N_EVEN, SEQ, MLA_ROPE), F32),
            jax.ShapeDtypeStruct((NTOK, 1024), BF16),
            jax.ShapeDtypeStruct((NTOK, 1024), BF16),
        ],
        input_output_aliases=aliases,
        compiler_params=_params(("arbitrary",)),
        name="mla_kv_new",
    )(*args)


def _mla_kv_ctx(cache_ckv, kr_tiled, w_uk, w_uv, j):
    return pl.pallas_call(
        functools.partial(_mla_kv_kernel, False),
        grid=(DEC_BATCH,),
        in_specs=[
            pl.BlockSpec((None, None, PAST_LEN, MLA_KV_RANK), lambda b: (b, j, 0, 0)),
            pl.BlockSpec((None, PAST_LEN, LANES), lambda b: (b, 0, 0)),
            pl.BlockSpec((None, MLA_KV_RANK, 512), lambda b: (j, 0, 0)),
            pl.BlockSpec((None, MLA_KV_RANK, 512), lambda b: (j, 0, 0)),
        ],
        out_specs=[
            pl.BlockSpec((PAST_LEN, 1024), lambda b: (b, 0)),
            pl.BlockSpec((PAST_LEN, 1024), lambda b: (b, 0)),
        ],
        out_shape=[
            jax.ShapeDtypeStruct((DEC_BATCH * PAST_LEN, 1024), BF16),
            jax.ShapeDtypeStruct((DEC_BATCH * PAST_LEN, 1024), BF16),
        ],
        compiler_params=_params(("parallel",)),
        name="mla_kv_ctx",
    )(cache_ckv, kr_tiled, w_uk, w_uv)


def _mla_attn_kernel(latent, *refs):
    if latent:
        qn_ref, qr_ref, cos_ref, sin_ref, kc_ref, vc_ref, kn_ref, vn_ref, o_ref = refs
        k_refs, v_refs = (kc_ref, kn_ref), (vc_ref, vn_ref)
    else:
        qn_ref, qr_ref, kn_ref, vn_ref, o_ref = refs
        k_refs, v_refs = (kn_ref,), (vn_ref,)
    tq = ATT_TQ if latent else SEQ
    n_seq = qn_ref.shape[0] // tq
    lane = _lane_iota((tq, LANES))
    scale = (MLA_NOPE + MLA_ROPE) ** -0.5
    for sq in range(n_seq):
        rows = slice(tq * sq, tq * (sq + 1))
        krows = slice(None) if latent else rows
        qr_cols = []
        for cidx in range(2):
            x = qr_ref[rows, LANES * cidx:LANES * (cidx + 1)].astype(F32)
            if latent:
                x = _rope128(x, cos_ref[...], sin_ref[...], lane, MLA_ROPE // 2)
            qr_cols.append(x)
        o_prev = None
        for h in range(MLA_HEADS):
            p, half = divmod(h, 2)
            cidx, slot = divmod(h, 4)
            qa = jnp.where(_div_pow2(lane, MLA_NOPE) == half,
                           qn_ref[rows, LANES * p:LANES * (p + 1)].astype(F32), 0.0)
            qb = jnp.where(_div_pow2(lane, MLA_ROPE) == slot, qr_cols[cidx], 0.0)
            qcat = jnp.concatenate([qa, qb], axis=1).astype(BF16)
            pair_cols = slice(256 * p, 256 * (p + 1))
            s_list = [lax.dot_general(qcat, k_ref[krows, pair_cols], NT_DIMS, preferred_element_type=F32)
                      for k_ref in k_refs]
            o = _pv_normalised(_exp_parts(s_list, scale), [v_ref[krows, pair_cols] for v_ref in v_refs])
            if half == 0:
                o_prev = o
            else:
                o_ref[rows, LANES * p:LANES * (p + 1)] = jnp.where(lane < MLA_DV, o_prev, o).astype(BF16)


def _mla_attn_prompt(zmla, kcat, vm):
    rows = 4 * SEQ
    return pl.pallas_call(
        functools.partial(_mla_attn_kernel, False),
        grid=(NTOK_P // rows,),
        in_specs=[
            pl.BlockSpec((rows, 512), lambda b: (b, 0)),
            pl.BlockSpec((rows, 256), lambda b: (b, 2)),
            pl.BlockSpec((rows, 1024), lambda b: (b, 0)),
            pl.BlockSpec((rows, 1024), lambda b: (b, 0)),
        ],
        out_specs=pl.BlockSpec((rows, 512), lambda b: (b, 0)),
        out_shape=jax.ShapeDtypeStruct((NTOK_P, 512), BF16),
        compiler_params=_params(("parallel",)),
        name="mla_attn_prompt",
    )(zmla, zmla, kcat, vm)


def _mla_attn_latent(zmla, cos_t, sin_t, kcat_ctx, vm_ctx, kcat, vm):
    nq = DEC_SEQ // ATT_TQ
    row0 = NTOK_P // ATT_TQ
    seq0 = NTOK_P // DEC_SEQ
    return pl.pallas_call(
        functools.partial(_mla_attn_kernel, True),
        grid=(DEC_BATCH, nq),
        in_specs=[
            pl.BlockSpec((ATT_TQ, 512), lambda b, q: (row0 + b * nq + q, 0)),
            pl.BlockSpec((ATT_TQ, 256), lambda b, q: (row0 + b * nq + q, 2)),
            pl.BlockSpec((ATT_TQ, LANES), lambda b, q: (q, 0)),
            pl.BlockSpec((ATT_TQ, LANES), lambda b, q: (q, 0)),
            pl.BlockSpec((PAST_LEN, 1024), lambda b, q: (b, 0)),
            pl.BlockSpec((PAST_LEN, 1024), lambda b, q: (b, 0)),
            pl.BlockSpec((DEC_SEQ, 1024), lambda b, q: (seq0 + b, 0)),
            pl.BlockSpec((DEC_SEQ, 1024), lambda b, q: (seq0 + b, 0)),
        ],
        out_specs=pl.BlockSpec((ATT_TQ, 512), lambda b, q: (b * nq + q, 0)),
        out_shape=jax.ShapeDtypeStruct((NTOK_S, 512), BF16),
        compiler_params=_params(("parallel", "arbitrary")),
        name="mla_attn_latent",
    )(zmla, zmla, cos_t, sin_t, kcat_ctx, vm_ctx, kcat, vm)


def _log_sigmoid(x):
    return jnp.minimum(x, 0.0) - jnp.log1p(jnp.exp(-jnp.abs(x)))


def _retention_kernel(seq_len, n_seq, has_init, emit_state, has_prev, *refs):
    refs = list(refs)
    decf_ref, decb_ref, q_ref, k_ref, v_ref, g_ref = refs[:6]
    refs = refs[6:]
    if has_init:
        sf0_ref, sb0_ref = refs[:2]
        refs = refs[2:]
    if has_prev:
        refs = refs[2:]
    o_ref = refs[0]
    refs = refs[1:]
    if emit_state:
        sf_ref, sb_ref = refs[:2]
        refs = refs[2:]
    of_scr, ob_scr, dec_scr, wts_scr = refs

    c = RET_CHUNK
    n_chunks = seq_len // c
    pair = pl.program_id(0)
    lane = _lane_iota((c, LANES))
    zeros_half = jnp.zeros((RET_DK, RET_DV), F32)

    def log_gammas(half):
        head = 2 * pair + half
        return (_log_sigmoid(decf_ref[pl.ds(head, 1), :]),
                _log_sigmoid(decb_ref[pl.ds(head, 1), :]))

    @pl.when(pl.program_id(1) == 0)
    def _():
        ri = lax.broadcasted_iota(jnp.int32, (c, c), 0)
        ci = lax.broadcasted_iota(jnp.int32, (c, c), 1)
        rel = (ri - ci).astype(F32)
        row = lax.broadcasted_iota(jnp.int32, (c, LANES), 0).astype(F32)
        for half in range(2):
            lgf, lgb = log_gammas(half)
            dec_scr[half] = (jnp.where(rel >= 0, jnp.exp(lgf[:, 0:1] * jnp.maximum(rel, 0.0)), 0.0)
                             + jnp.where(rel <= 0, jnp.exp(lgb[:, 0:1] * jnp.maximum(-rel, 0.0)), 0.0))
            wts_scr[half, 0] = jnp.exp(lgf * (row + 1.0))
            wts_scr[half, 1] = jnp.exp(lgf * (c - 1.0 - row))
            wts_scr[half, 2] = jnp.exp(lgb * (c - row))
            wts_scr[half, 3] = jnp.exp(lgb * row)

    cross = has_init or n_chunks > 1
    chains = [(sq, half) for sq in range(n_seq) for half in range(2)]
    chunk_decay = []
    for half in range(2):
        lgf, lgb = log_gammas(half)
        chunk_decay.append((jnp.exp(lgf * float(c)), jnp.exp(lgb * float(c))))

    def rows_of(sq, n):
        start = sq * seq_len + n * c
        return pl.ds(start if isinstance(n, int) else pl.multiple_of(start, c), c)

    def load(sq, half, n):
        rows = rows_of(sq, n)
        vsl = slice(RET_DV * half, RET_DV * (half + 1))
        qm = jnp.where(_div_pow2(lane, RET_DK) == half, q_ref[rows, :].astype(F32), 0.0)
        kk = k_ref[rows, :].astype(F32) * (RET_DK ** -0.5)
        return rows, vsl, qm, kk, v_ref[rows, vsl]

    def init_state(s0_ref, half):
        if not has_init:
            return jnp.zeros((LANES, RET_DV), F32)
        s0 = s0_ref[half]
        return jnp.concatenate([s0, zeros_half] if half == 0 else [zeros_half, s0], axis=0)

    def fwd_step(sq, half, n, s_f):
        rows, vsl, qm, kk, vb = load(sq, half, n)
        s = lax.dot_general(qm.astype(BF16), kk.astype(BF16), NT_DIMS, preferred_element_type=F32)
        o = jnp.dot((s * dec_scr[half]).astype(BF16), vb, preferred_element_type=F32)
        if cross:
            o = o + jnp.dot((qm * wts_scr[half, 0]).astype(BF16), s_f.astype(BF16), preferred_element_type=F32)
        of_scr[rows, vsl] = o
        kv = lax.dot_general((kk * wts_scr[half, 1]).astype(BF16), vb, TN_DIMS, preferred_element_type=F32)
        return chunk_decay[half][0] * s_f + kv

    def bwd_step(sq, half, n, s_b):
        rows, vsl, qm, kk, vb = load(sq, half, n)
        if cross:
            ob_scr[rows, vsl] = jnp.dot((qm * wts_scr[half, 2]).astype(BF16), s_b.astype(BF16),
                                        preferred_element_type=F32)
        kv = lax.dot_general((kk * wts_scr[half, 3]).astype(BF16), vb, TN_DIMS, preferred_element_type=F32)
        return chunk_decay[half][1] * s_b + kv

    def finish(sq, n):
        rows = rows_of(sq, n)
        o2 = of_scr[rows, :] + ob_scr[rows, :] if cross else of_scr[rows, :]
        for half in range(2):
            vsl = slice(RET_DV * half, RET_DV * (half + 1))
            o = o2[:, vsl]
            mu = jnp.mean(o, axis=-1, keepdims=True)
            d = o - mu
            var = jnp.mean(d * d, axis=-1, keepdims=True)
            o_ref[rows, vsl] = (_silu(g_ref[rows, vsl].astype(F32)) * (d * lax.rsqrt(var + LN_EPS))).astype(BF16)

    s_f = tuple(init_state(sf0_ref if has_init else None, half) for _, half in chains)
    s_b = tuple(init_state(sb0_ref if has_init else None, half) for _, half in chains)
    if n_chunks == 1:
        s_f = tuple(fwd_step(sq, half, 0, s) for (sq, half), s in zip(chains, s_f))
        s_b = tuple(bwd_step(sq, half, 0, s) for (sq, half), s in zip(chains, s_b))
        for sq in range(n_seq):
            finish(sq, 0)
    else:
        def scan_step(n, carry):
            sf, sb = carry
            sf = tuple(fwd_step(sq, half, n, s) for (sq, half), s in zip(chains, sf))
            sb = tuple(bwd_step(sq, half, n_chunks - 1 - n, s) for (sq, half), s in zip(chains, sb))
            return sf, sb

        s_f, s_b = lax.fori_loop(0, n_chunks, scan_step, (s_f, s_b))

        def finish_step(n, carry):
            for sq in range(n_seq):
                finish(sq, n)
            return carry

        lax.fori_loop(0, n_chunks, finish_step, 0)
    if emit_state:
        for (sq, half), sf, sb in zip(chains, s_f, s_b):
            sf_ref[sq, half] = sf[RET_DK * half:RET_DK * (half + 1), :]
            sb_ref[sq, half] = sb[RET_DK * half:RET_DK * (half + 1), :]


def _retention(zret, decf, decb, j, latent, state_f=None, state_b=None, prev=None):
    seq_len = DEC_SEQ if latent else SEQ
    n_seq = 1 if latent else 4
    n_b = (DEC_BATCH if latent else BATCH) // n_seq
    rows = n_seq * seq_len
    row0 = NTOK_P // rows if latent else 0
    n_pairs = RET_HEADS // 2
    in_specs = [
        pl.BlockSpec((None, RET_HEADS, LANES), lambda p, b: (j, 0, 0)),
        pl.BlockSpec((None, RET_HEADS, LANES), lambda p, b: (j, 0, 0)),
        pl.BlockSpec((rows, LANES), lambda p, b: (row0 + b, p)),
        pl.BlockSpec((rows, LANES), lambda p, b: (row0 + b, 4 + p)),
        pl.BlockSpec((rows, 256), lambda p, b: (row0 + b, 4 + p)),
        pl.BlockSpec((rows, 256), lambda p, b: (row0 + b, 8 + p)),
    ]
    args = [decf, decb, zret, zret, zret, zret]
    out_specs = [pl.BlockSpec((rows, 256), lambda p, b: (b, p))]
    out_shape = [jax.ShapeDtypeStruct((n_b * rows, RET_HEADS * RET_DV), BF16)]
    aliases = {}
    if latent:
        st_spec = pl.BlockSpec((None, None, 2, RET_DK, RET_DV), lambda p, b: (b, j, p, 0, 0))
        in_specs += [st_spec, st_spec]
        args += [state_f, state_b]
    else:
        st_spec = pl.BlockSpec((n_seq, None, 2, RET_DK, RET_DV), lambda p, b: (b, j, p, 0, 0))
        out_specs += [st_spec, st_spec]
        out_shape += [jax.ShapeDtypeStruct((BATCH, N_EVEN, RET_HEADS, RET_DK, RET_DV), F32)] * 2
        if prev is not None:
            aliases = {len(args): 1, len(args) + 1: 2}
            in_specs += [pl.BlockSpec(memory_space=pl.ANY)] * 2
            args += list(prev)
    return pl.pallas_call(
        functools.partial(_retention_kernel, seq_len, n_seq, latent, not latent, bool(aliases)),
        grid=(n_pairs, n_b),
        in_specs=in_specs,
        out_specs=out_specs,
        out_shape=out_shape,
        scratch_shapes=[
            pltpu.VMEM((rows, 2 * RET_DV), F32),
            pltpu.VMEM((rows, 2 * RET_DV), F32),
            pltpu.VMEM((2, RET_CHUNK, RET_CHUNK), F32),
            pltpu.VMEM((2, 4, RET_CHUNK, LANES), F32),
        ],
        input_output_aliases=aliases,
        compiler_params=_params(("arbitrary", "arbitrary")),
        name="retention_latent" if latent else "retention_prompt",
    )(*args)


def _diff_prep_kernel(k_ref, v_ref, cos_ref, sin_ref, kr_ref, va_ref):
    lane = _lane_iota(cos_ref.shape)
    cos = cos_ref[...]
    sin = sin_ref[...]
    ones = jnp.ones(cos_ref.shape, BF16)
    for h in range(DIFF_HEADS):
        sl = slice(LANES * h, LANES * (h + 1))
        kr_ref[:, sl] = _rope128(k_ref[:, sl].astype(F32), cos, sin, lane, DIFF_DH // 2).astype(BF16)
        va_ref[:, 256 * h:256 * h + LANES] = v_ref[:, sl].astype(BF16)
        va_ref[:, 256 * h + LANES:256 * (h + 1)] = ones


def _diff_prep(zodd, cos_t, sin_t):
    tm = 512
    row0 = NTOK_P // tm
    nst = DEC_SEQ // tm
    return pl.pallas_call(
        _diff_prep_kernel,
        grid=(NTOK_S // tm,),
        in_specs=[
            pl.BlockSpec((tm, 1024), lambda i: (row0 + i, 1)),
            pl.BlockSpec((tm, 1024), lambda i: (row0 + i, 2)),
            pl.BlockSpec((tm, LANES), lambda i: (i % nst, 0)),
            pl.BlockSpec((tm, LANES), lambda i: (i % nst, 0)),
        ],
        out_specs=[pl.BlockSpec((tm, 1024), lambda i: (i, 0)), pl.BlockSpec((tm, 2048), lambda i: (i, 0))],
        out_shape=[jax.ShapeDtypeStruct((NTOK_S, 1024), BF16), jax.ShapeDtypeStruct((NTOK_S, 2048), BF16)],
        compiler_params=_params(("parallel",)),
        name="diff_rope_keys",
    )(zodd, zodd, cos_t, sin_t)


def _diff_attn_kernel(latent, lam_init, *refs):
    if latent:
        (lam_ref, ng_ref, q_ref, cos_ref, sin_ref, kc_ref, vc_ref, kn_ref, vn_ref, o_ref) = refs
    else:
        lam_ref, ng_ref, q_ref, k_ref, v_ref = refs[:5]
        o_ref, kout_ref, vout_ref = refs[-3:]
    tq = ATT_TQ if latent else SEQ
    n_seq = q_ref.shape[0] // tq
    lane = _lane_iota((tq, LANES))
    scale = DIFF_DH ** -0.5
    lp = lam_ref[...]
    lam = (jnp.exp(jnp.sum(lp[0:1, :] * lp[1:2, :], axis=-1, keepdims=True))
           - jnp.exp(jnp.sum(lp[2:3, :] * lp[3:4, :], axis=-1, keepdims=True)) + lam_init)
    ng = ng_ref[...]
    ones = jnp.ones((PAST_LEN if latent else tq, LANES), BF16)
    for sq, h in [(sq, h) for sq in range(n_seq) for h in range(DIFF_HEADS)]:
        sl = slice(LANES * h, LANES * (h + 1))
        rows = slice(tq * sq, tq * (sq + 1))
        qh = q_ref[rows, sl].astype(F32)
        if latent:
            qh = _rope128(qh, cos_ref[...], sin_ref[...], lane, DIFF_DH // 2)
            k_list = [kc_ref[h].astype(BF16), kn_ref[:, sl]]
            v_list = [jnp.concatenate([vc_ref[h].astype(BF16), ones], axis=1),
                      vn_ref[:, 256 * h:256 * (h + 1)]]
        else:
            kh = k_ref[rows, sl]
            vh = v_ref[rows, sl]
            kout_ref[sq, h] = kh.astype(F32)
            vout_ref[sq, h] = vh.astype(F32)
            k_list = [kh.astype(BF16)]
            v_list = [jnp.concatenate([vh.astype(BF16), ones], axis=1)]
        q1 = jnp.where(lane < DIFF_DH, qh, 0.0).astype(BF16)
        q2 = jnp.where(lane >= DIFF_DH, qh, 0.0).astype(BF16)
        s1 = [lax.dot_general(q1, kk, NT_DIMS, preferred_element_type=F32) for kk in k_list]
        s2 = [lax.dot_general(q2, kk, NT_DIMS, preferred_element_type=F32) for kk in k_list]
        o = _pv_normalised(_exp_parts(s1, scale), v_list) - lam * _pv_normalised(_exp_parts(s2, scale), v_list)
        y = o * lax.rsqrt(jnp.mean(o * o, axis=-1, keepdims=True) + RMS_EPS) * ng
        o_ref[rows, sl] = (y * (1.0 - lam_init)).astype(BF16)


def _diff_attn_prompt(zodd, lam_p, norm_g, j, lam_init, prev):
    n_seq = 2
    rows = n_seq * SEQ
    cache_shape = jax.ShapeDtypeStruct((BATCH, N_ODD, DIFF_HEADS, SEQ, LANES), F32)
    cache_spec = pl.BlockSpec((n_seq, None, DIFF_HEADS, SEQ, LANES), lambda b: (b, j, 0, 0, 0))
    in_specs = [
        pl.BlockSpec((None, 4, DIFF_DH), lambda b: (j, 0, 0)),
        pl.BlockSpec((None, 1, DIFF_DV), lambda b: (j, 0, 0)),
        pl.BlockSpec((rows, 1024), lambda b: (b, 0)),
        pl.BlockSpec((rows, 1024), lambda b: (b, 1)),
        pl.BlockSpec((rows, 1024), lambda b: (b, 2)),
    ]
    args = [lam_p, norm_g.reshape(-1, 1, DIFF_DV), zodd, zodd, zodd]
    aliases = {}
    if prev is not None:
        aliases = {len(args): 1, len(args) + 1: 2}
        in_specs += [pl.BlockSpec(memory_space=pl.ANY)] * 2
        args += list(prev)
    return pl.pallas_call(
        functools.partial(_diff_attn_kernel, False, lam_init),
        grid=(BATCH // n_seq,),
        in_specs=in_specs,
        out_specs=[pl.BlockSpec((rows, 1024), lambda b: (b, 0)), cache_spec, cache_spec],
        out_shape=[jax.ShapeDtypeStruct((NTOK_P, 1024), BF16), cache_shape, cache_shape],
        input_output_aliases=aliases,
        compiler_params=_params(("arbitrary",)),
        name="diff_attn_prompt",
    )(*args)


def _diff_attn_latent(zodd, lam_p, norm_g, cos_t, sin_t, cache_k, cache_v, k_rot, v_aug, j, lam_init):
    nq = DEC_SEQ // ATT_TQ
    row0 = NTOK_P // ATT_TQ
    ctx_spec = pl.BlockSpec((None, None, DIFF_HEADS, PAST_LEN, LANES), lambda b, q: (b, j, 0, 0, 0))
    return pl.pallas_call(
        functools.partial(_diff_attn_kernel, True, lam_init),
        grid=(DEC_BATCH, nq),
        in_specs=[
            pl.BlockSpec((None, 4, DIFF_DH), lambda b, q: (j, 0, 0)),
            pl.BlockSpec((None, 1, DIFF_DV), lambda b, q: (j, 0, 0)),
            pl.BlockSpec((ATT_TQ, 1024), lambda b, q: (row0 + b * nq + q, 0)),
            pl.BlockSpec((ATT_TQ, LANES), lambda b, q: (q, 0)),
            pl.BlockSpec((ATT_TQ, LANES), lambda b, q: (q, 0)),
            ctx_spec,
            ctx_spec,
            pl.BlockSpec((DEC_SEQ, 1024), lambda b, q: (b, 0)),
            pl.BlockSpec((DEC_SEQ, 2048), lambda b, q: (b, 0)),
        ],
        out_specs=pl.BlockSpec((ATT_TQ, 1024), lambda b, q: (b * nq + q, 0)),
        out_shape=jax.ShapeDtypeStruct((NTOK_S, 1024), BF16),
        compiler_params=_params(("parallel", "arbitrary")),
        name="diff_attn_latent",
    )(lam_p, norm_g.reshape(-1, 1, DIFF_DV), zodd, cos_t, sin_t, cache_k, cache_v, k_rot, v_aug)


def _mla_weight_t(w_in_t):
    base = RET_COLS
    mq = w_in_t[base:base + MLA_HEADS * (MLA_NOPE + MLA_ROPE)].reshape(MLA_HEADS, MLA_NOPE + MLA_ROPE, D_MODEL)
    qn = mq[:, :MLA_NOPE].reshape(MLA_HEADS * MLA_NOPE, D_MODEL)
    qr = mq[:, MLA_NOPE:].reshape(MLA_HEADS * MLA_ROPE, D_MODEL)
    ckv0 = base + MLA_HEADS * (MLA_NOPE + MLA_ROPE)
    ckv = w_in_t[ckv0:ckv0 + MLA_KV_RANK]
    kr = w_in_t[ckv0 + MLA_KV_RANK:]
    return jnp.concatenate([qn, qr, ckv, jnp.tile(kr, (LANES // MLA_ROPE, 1))], axis=0)


def kernel(x_prompt, x_sample, state_ret_fwd, state_ret_bwd, cache_mla_ckv, cache_mla_krope, cache_diff_k, cache_diff_v, c, c_ctx, ada_w, ada_b, ln1_g, ln1_b, ln2_g, ln2_b, ev_w_in, ev_w_out, ret_decay_fwd, ret_decay_bwd, mla_kv_norm_g, mla_w_uk, mla_w_uv, od_w_in, od_w_out, diff_lambda, diff_norm_g, moe_w_group, moe_b_group, moe_w_expert, moe_b_expert, moe_w1, moe_w3, moe_w2):
    x = (x_prompt.reshape(NTOK_P, D_MODEL), x_sample.reshape(NTOK_S, D_MODEL), 0)
    cond =jnp.concatenate([c_ctx[None, :], c, jnp.zeros((N_COND - 1 - DEC_BATCH, D_MODEL), F32)], axis=0)
    mods = _ada_all(cond, ada_w, ada_b).reshape(DEPTH, N_COND, 6, D_MODEL)

    cos_m, sin_m = _rope_tables(MLA_ROPE)
    cos_d, sin_d = _rope_tables(DIFF_DH)
    ident = 512
    cos_m_id = jnp.concatenate([jnp.ones((ident, LANES), F32), cos_m], axis=0)
    sin_m_id = jnp.concatenate([jnp.zeros((ident, LANES), F32), sin_m], axis=0)
    decf = jnp.broadcast_to(ret_decay_fwd[:, :, None], ret_decay_fwd.shape + (LANES,))
    decb = jnp.broadcast_to(ret_decay_bwd[:, :, None], ret_decay_bwd.shape + (LANES,))

    ev_w_in_t = jnp.swapaxes(ev_w_in, 1, 2)
    pad = LANES - MOE_GROUPS - MOE_EXPERTS
    ret_states = mla_caches = diff_caches = None
    for i in range(DEPTH):
        j = i // 2
        mod = mods[i]
        w_router = jnp.concatenate([moe_w_group[i], moe_w_expert[i], jnp.zeros((D_MODEL, pad), F32)], axis=1)
        b_router = jnp.concatenate([moe_b_group[i], moe_b_expert[i], jnp.zeros((pad,), F32)])[None, :]
        if i % 2 == 0:
            zret = _mm_mod(x, mod, ev_w_in_t, (j,), RET_COLS, 1024, "in_proj_retention", w_is_transposed=True)
            zmla = _mm_mod(x, mod, _mla_weight_t(ev_w_in_t[j]), (), MLA_COLS, MLA_COLS, "in_proj_mla",
                           w_is_transposed=True)
            *mla_caches, kcat, vm = _mla_kv_new(zmla, cos_m_id, sin_m_id, mla_kv_norm_g, mla_w_uk, mla_w_uv,
                                                j, mla_caches)
            kr_ctx = jnp.tile(cache_mla_krope[:, j], (1, 1, LANES // MLA_ROPE))
            kcat_ctx, vm_ctx = _mla_kv_ctx(cache_mla_ckv, kr_ctx, mla_w_uk, mla_w_uv, j)
            a_ret_p, *ret_states = _retention(zret, decf, decb, j, False, prev=ret_states)
            (a_ret_s,) = _retention(zret, decf, decb, j, True, state_ret_fwd, state_ret_bwd)
            a_mla_p = _mla_attn_prompt(zmla, kcat, vm)
            a_mla_s = _mla_attn_latent(zmla, cos_m, sin_m, kcat_ctx, vm_ctx, kcat, vm)
            x1, meta, counts = _mm_ln([(a_ret_p, a_ret_s), (a_mla_p, a_mla_s)], ev_w_out, j, x, mod,
                                      ln1_g, ln1_b, w_router, b_router, i)
        else:
            lam_init = 0.8 - 0.6 * math.exp(-0.3 * i)
            zodd = _mm_mod(x, mod, od_w_in, (j,), 3072, 1024, "in_proj_diff")
            a_p, *diff_caches = _diff_attn_prompt(zodd, diff_lambda, diff_norm_g, j, lam_init, diff_caches)
            k_rot, v_aug = _diff_prep(zodd, cos_d, sin_d)
            a_s = _diff_attn_latent(zodd, diff_lambda, diff_norm_g, cos_d, sin_d, cache_diff_k, cache_diff_v,
                                    k_rot, v_aug, j, lam_init)
            x1, meta, counts = _mm_ln([(a_p, a_s)], od_w_out, j, x, mod, ln1_g, ln1_b, w_router, b_router, i)
        y = _moe(x1, _moe_plan(meta, counts), mods, w_router, b_router, moe_w1, moe_w3, moe_w2, ln2_g, ln2_b, i)
        x = (y, y, NTOK_P)

    y_prompt = x[0][:NTOK_P].reshape(BATCH, SEQ, D_MODEL)
    y_sample = x[1][NTOK_P:NTOK].reshape(DEC_BATCH, DEC_SEQ, D_MODEL)
    return (y_prompt, y_sample, ret_states[0], ret_states[1], mla_caches[0], mla_caches[1],
            diff_caches[0], diff_caches[1])
```

```python
import functools
import math

import jax
import jax.numpy as jnp
from jax import lax
from jax.experimental import pallas as pl
from jax.experimental.pallas import tpu as pltpu

D_MODEL = 1024
BATCH = 32
SEQ = 256
DEPTH = 4
N_EVEN = 2
N_ODD = 2
DEC_BATCH = 2
DEC_SEQ = 2048
PAST_LEN = 256
GRID_W = 64
LN_EPS = 1e-5
RMS_EPS = 1e-6
DEEPNORM_ALPHA = (2.0 * DEPTH) ** 0.25
ROPE_BASE = 10000.0
RET_HEADS = 8
RET_DK = 64
RET_DV = 128
MLA_HEADS = 8
MLA_NOPE = 64
MLA_ROPE = 32
MLA_DV = 64
MLA_KV_RANK = 256
DIFF_HEADS = 8
DIFF_DH = 64
DIFF_DV = 128
MOE_GROUPS = 4
MOE_PER_GROUP = 4
MOE_EXPERTS = 16
MOE_FF = 256

NTOK_P = BATCH * SEQ
NTOK_S = DEC_BATCH * DEC_SEQ
NTOK = NTOK_P + NTOK_S
N_COND = 8
LANES = 128
SUBLANES = 8
RET_COLS = 3072
MLA_COLS = 1152
ATT_TQ = 256
RET_CHUNK = 256
MOE_TILE = 512
MOE_TILES = (NTOK + MOE_GROUPS * (MOE_TILE - 1)) // MOE_TILE
MOE_ROWS = MOE_TILES * MOE_TILE
MOE_KEY_BASE = 16384
VMEM_LIMIT = 56 * 1024 * 1024
LOG2E = 1.4426950408889634

F32 = jnp.float32
BF16 = jnp.bfloat16
NT_DIMS = (((1,), (1,)), ((), ()))
TN_DIMS = (((0,), (0,)), ((), ()))


def _params(sem):
    return pltpu.CompilerParams(dimension_semantics=sem, vmem_limit_bytes=VMEM_LIMIT)


def _group_of_tile(i, tm):
    npt = NTOK_P // tm
    nst = DEC_SEQ // tm
    return jnp.where(i < npt, 0, 1 + (i - npt) // nst)


def _split_specs(tm, width, m_of, s_row0=0):
    npt = NTOK_P // tm
    s_blk0 = s_row0 // tm
    return [pl.BlockSpec((tm, width), lambda *g: (jnp.minimum(m_of(*g), npt - 1), 0)),
            pl.BlockSpec((tm, width), lambda *g: (jnp.maximum(m_of(*g) - npt, 0) + s_blk0, 0))]


def _read_split(p_ref, s_ref, m):
    return jnp.where(m < NTOK_P // p_ref.shape[0], p_ref[...], s_ref[...])


def _silu(x):
    return x * (1.0 / (1.0 + jnp.exp(-x)))


def _layer_norm(r, g, b):
    mu = jnp.mean(r, axis=-1, keepdims=True)
    d = r - mu
    var = jnp.mean(d * d, axis=-1, keepdims=True)
    return d * lax.rsqrt(var + LN_EPS) * g + b


def _lane_iota(shape):
    return lax.broadcasted_iota(jnp.int32, shape, 1)


def _div_pow2(x, d):
    assert d & (d - 1) == 0
    return jnp.right_shift(x, d.bit_length() - 1)


def _mod_pow2(x, d):
    assert d & (d - 1) == 0
    return jnp.bitwise_and(x, d - 1)


def _ada_kernel(c_ref, w_ref, b_ref, o_ref):
    h = _silu(c_ref[...]).astype(BF16)
    o_ref[...] = jnp.dot(h, w_ref[...].astype(BF16), preferred_element_type=F32) + b_ref[...]


def _ada_all(cond, ada_w, ada_b):
    tn = 768
    return pl.pallas_call(
        _ada_kernel,
        grid=(DEPTH, 6 * D_MODEL // tn),
        in_specs=[
            pl.BlockSpec((N_COND, D_MODEL), lambda l, n: (0, 0)),
            pl.BlockSpec((None, D_MODEL, tn), lambda l, n: (l, 0, n)),
            pl.BlockSpec((None, 1, tn), lambda l, n: (l, 0, n)),
        ],
        out_specs=pl.BlockSpec((None, N_COND, tn), lambda l, n: (l, 0, n)),
        out_shape=jax.ShapeDtypeStruct((DEPTH, N_COND, 6 * D_MODEL), F32),
        compiler_params=_params(("parallel", "parallel")),
        name="ada_modulation",
    )(cond, ada_w, ada_b.reshape(DEPTH, 1, 6 * D_MODEL))


def _mm_mod_kernel(w_is_transposed, xp_ref, xs_ref, mod_ref, w_ref, o_ref, wscr):
    m = pl.program_id(1)

    @pl.when(m == 0)
    def _():
        wscr[...] = w_ref[...].astype(BF16)

    sh = mod_ref[0:1, :]
    sc = mod_ref[1:2, :]
    xm = (_read_split(xp_ref, xs_ref, m) * (1.0 + sc) + sh).astype(BF16)
    if w_is_transposed:
        z = lax.dot_general(xm, wscr[...], NT_DIMS, preferred_element_type=F32)
    else:
        z = jnp.dot(xm, wscr[...], preferred_element_type=F32)
    o_ref[...] = z.astype(o_ref.dtype)


def _mm_mod(x, mod, w, w_index, n_cols, tn, name, w_is_transposed=False):
    tm = 1024
    if w_is_transposed:
        w_spec = pl.BlockSpec((None,) * len(w_index) + (tn, D_MODEL), lambda n, m: tuple(w_index) + (n, 0))
        w_scratch = pltpu.VMEM((tn, D_MODEL), BF16)
    else:
        w_spec = pl.BlockSpec((None,) * len(w_index) + (D_MODEL, tn), lambda n, m: tuple(w_index) + (0, n))
        w_scratch = pltpu.VMEM((D_MODEL, tn), BF16)
    return pl.pallas_call(
        functools.partial(_mm_mod_kernel, w_is_transposed),
        grid=(n_cols // tn, NTOK // tm),
        in_specs=_split_specs(tm, D_MODEL, lambda n, m: m, x[2]) + [
            pl.BlockSpec((None, 6, D_MODEL), lambda n, m: (_group_of_tile(m, tm), 0, 0)),
            w_spec,
        ],
        out_specs=pl.BlockSpec((tm, tn), lambda n, m: (m, n)),
        out_shape=jax.ShapeDtypeStruct((NTOK, n_cols), BF16),
        scratch_shapes=[w_scratch],
        compiler_params=_params(("arbitrary", "arbitrary")),
        name=name,
    )(x[0], x[1], mod, w)


def _router_probs(xm, wr_ref, br_ref):
    rows = xm.shape[0]
    z = jnp.dot(xm, wr_ref[...].astype(BF16), preferred_element_type=F32) + br_ref[...]
    lane_i = _lane_iota((rows, LANES))
    lane = lane_i.astype(F32)
    gmask = lane_i < MOE_GROUPS
    zg = jnp.where(gmask, z, -jnp.inf)
    pg = jnp.exp(zg - jnp.max(zg, axis=-1, keepdims=True))
    g_prob = pg / jnp.sum(pg, axis=-1, keepdims=True)
    g_p = jnp.max(g_prob, axis=-1, keepdims=True)
    g_idx = jnp.min(jnp.where(gmask & (g_prob == g_p), lane, float(LANES)), axis=-1, keepdims=True)
    return z, lane_i, lane, g_p, g_idx


def _mm_ln_kernel(k_sizes, *refs):
    n_a = len(k_sizes)
    a_refs = refs[:2 * n_a]
    (w_ref, xp_ref, xs_ref, mod_ref, g_ref, b_ref, wr_ref, br_ref,
     o_ref, meta_ref, cnt_ref, wscr, tri_scr, carry_scr) = refs[2 * n_a:]
    m = pl.program_id(0)
    tm = o_ref.shape[0]

    @pl.when(m == 0)
    def _():
        wscr[...] = w_ref[...].astype(BF16)
        ri = lax.broadcasted_iota(jnp.int32, (tm, tm), 0)
        ci = lax.broadcasted_iota(jnp.int32, (tm, tm), 1)
        tri_scr[...] = jnp.where(ci < ri, 1.0, 0.0).astype(BF16)
        carry_scr[...] = jnp.zeros_like(carry_scr)

    y = None
    k0 = 0
    for i, ks in enumerate(k_sizes):
        a = _read_split(a_refs[2 * i], a_refs[2 * i + 1], m)
        part = jnp.dot(a, wscr[k0:k0 + ks, :], preferred_element_type=F32)
        y = part if y is None else y + part
        k0 += ks
    gate = mod_ref[2:3, :]
    r = DEEPNORM_ALPHA * _read_split(xp_ref, xs_ref, m) + gate * y
    x1 = _layer_norm(r, g_ref[...], b_ref[...])
    o_ref[...] = x1

    xm = (x1 * (1.0 + mod_ref[4:5, :]) + mod_ref[3:4, :]).astype(BF16)
    _, lane_i, lane, _, g_idx = _router_probs(xm, wr_ref, br_ref)
    onehot = jnp.where(lane == g_idx, 1.0, 0.0)
    before = jnp.dot(tri_scr[...], onehot.astype(BF16), preferred_element_type=F32) + carry_scr[0:1, :]
    rank = jnp.sum(jnp.where(lane == g_idx, before, 0.0), axis=-1, keepdims=True)
    key_col = g_idx * float(MOE_KEY_BASE) + rank
    row_i = lax.broadcasted_iota(jnp.int32, (tm, LANES), 0)
    diag = jnp.where(lane_i == _mod_pow2(row_i, LANES), key_col, 0.0)
    meta_ref[...] = jnp.sum(diag.reshape(tm // LANES, LANES, LANES), axis=1).astype(jnp.int32)
    total = carry_scr[0:1, :] + jnp.sum(onehot, axis=0, keepdims=True)
    carry_scr[...] = jnp.broadcast_to(total, carry_scr.shape)
    cnt_ref[...] = jnp.broadcast_to(total, cnt_ref.shape)


def _mm_ln(a_pairs, w, j, x, mod, ln_g, ln_b, w_router, b_router, layer):
    tm = 512
    k_sizes = tuple(ap.shape[1] for ap, _ in a_pairs)
    k_tot = sum(k_sizes)
    in_specs = []
    args = []
    for (ap, a_s), ks in zip(a_pairs, k_sizes):
        in_specs += _split_specs(tm, ks, lambda m: m)
        args += [ap, a_s]
    in_specs += [pl.BlockSpec((None, k_tot, D_MODEL), lambda m: (j, 0, 0))]
    in_specs += _split_specs(tm, D_MODEL, lambda m: m, x[2])
    in_specs += [
        pl.BlockSpec((None, 6, D_MODEL), lambda m: (_group_of_tile(m, tm), 0, 0)),
        pl.BlockSpec((None, 1, D_MODEL), lambda m: (layer, 0, 0)),
        pl.BlockSpec((None, 1, D_MODEL), lambda m: (layer, 0, 0)),
        pl.BlockSpec((D_MODEL, LANES), lambda m: (0, 0)),
        pl.BlockSpec((1, LANES), lambda m: (0, 0)),
    ]
    return pl.pallas_call(
        functools.partial(_mm_ln_kernel, k_sizes),
        grid=(NTOK // tm,),
        in_specs=in_specs,
        out_specs=[
            pl.BlockSpec((tm, D_MODEL), lambda m: (m, 0)),
            pl.BlockSpec((None, tm // LANES, LANES), lambda m: (m, 0, 0)),
            pl.BlockSpec((N_COND, LANES), lambda m: (0, 0)),
        ],
        out_shape=[
            jax.ShapeDtypeStruct((NTOK, D_MODEL), F32),
            jax.ShapeDtypeStruct((NTOK // tm, tm // LANES, LANES), jnp.int32),
            jax.ShapeDtypeStruct((N_COND, LANES), F32),
        ],
        scratch_shapes=[
            pltpu.VMEM((k_tot, D_MODEL), BF16),
            pltpu.VMEM((tm, tm), BF16),
            pltpu.VMEM((N_COND, LANES), F32),
        ],
        compiler_params=_params(("arbitrary",)),
        name="out_proj_ln",
    )(*args, w, x[0], x[1], mod, ln_g.reshape(DEPTH, 1, D_MODEL), ln_b.reshape(DEPTH, 1, D_MODEL),
      w_router, b_router)


def _moe_kernel(tgrp_ref, ntile_ref, src0_ref, src1_ref, dst_ref, mid_ref, x_hbm, mod_ref, wr_ref, br_ref,
                w1_ref, w3_ref, w2_ref, g_ref, b_ref, y_hbm,
                gbuf, obuf, gsem, ssem, w13s, w2s):
    i = pl.program_id(0)
    n_steps = pl.num_programs(0)
    n_tiles = ntile_ref[0]
    n_blk = gbuf.shape[1]
    ts = n_blk * SUBLANES
    slot = lax.rem(i, 2)

    def start_gather(src_ref, s):
        def body(k, carry):
            for u in range(SUBLANES):
                tok = src_ref[0, k * SUBLANES + u]
                pltpu.make_async_copy(x_hbm.at[pl.ds(tok, 1), :], gbuf.at[s, k, pl.ds(u, 1), :],
                                      gsem.at[s]).start()
            return carry
        lax.fori_loop(0, n_blk, body, 0)

    def wait_gather(s):
        pltpu.make_async_copy(gbuf.at[s], gbuf.at[s], gsem.at[s]).wait()

    def start_scatter(s):
        def body(k, carry):
            for u in range(SUBLANES):
                tok = dst_ref[0, k * SUBLANES + u]
                pltpu.make_async_copy(obuf.at[s, k, pl.ds(u, 1), :], y_hbm.at[pl.ds(tok, 1), :],
                                      ssem.at[s]).start()
            return carry
        lax.fori_loop(0, n_blk, body, 0)

    def wait_scatter(s):
        pltpu.make_async_copy(obuf.at[s], obuf.at[s], ssem.at[s]).wait()

    @pl.when(i == 0)
    def _():
        start_gather(src0_ref, 0)

    @pl.when(i < n_tiles)
    def _():
        grp = tgrp_ref[i]
        wait_gather(slot)

        @pl.when(i + 1 < n_tiles)
        def _():
            start_gather(src1_ref, 1 - slot)

        @pl.when((i == 0) | (grp != tgrp_ref[jnp.maximum(i - 1, 0)]))
        def _():
            w13s[:, :, :MOE_FF] = w1_ref[...].astype(BF16)
            w13s[:, :, MOE_FF:] = w3_ref[...].astype(BF16)
            w2s[...] = w2_ref[...].astype(BF16)

        mid_rows = mid_ref[...].astype(F32)
        spread = jnp.concatenate([jnp.broadcast_to(mid_rows[a:a + 1, :], (LANES, LANES))
                                  for a in range(ts // LANES)], axis=0)
        row_i = lax.broadcasted_iota(jnp.int32, (ts, LANES), 0)
        mid = jnp.sum(jnp.where(_lane_iota((ts, LANES)) == _mod_pow2(row_i, LANES), spread, 0.0),
                      axis=-1, keepdims=True)

        def mod_row(k):
            return jnp.where(mid == 0, mod_ref[0, k:k + 1, :],
                             jnp.where(mid == 1, mod_ref[1, k:k + 1, :], mod_ref[2, k:k + 1, :]))

        x1 = gbuf[slot].reshape(ts, D_MODEL)
        xm = (x1 * (1.0 + mod_row(4)) + mod_row(3)).astype(BF16)
        z, lane_i, lane, g_p, _ = _router_probs(xm, wr_ref, br_ref)
        e0 = MOE_GROUPS + MOE_PER_GROUP * grp
        emask = (lane_i >= e0) & (lane_i < e0 + MOE_PER_GROUP)
        ze = jnp.where(emask, z, -jnp.inf)
        pe = jnp.exp(ze - jnp.max(ze, axis=-1, keepdims=True))
        e_prob = pe / jnp.sum(pe, axis=-1, keepdims=True)
        cand = jnp.where(emask, e_prob, -1.0)
        p1 = jnp.max(cand, axis=-1, keepdims=True)
        i1 = jnp.min(jnp.where(cand == p1, lane, float(LANES)), axis=-1, keepdims=True)
        cand2 = jnp.where(lane == i1, -1.0, cand)
        p2 = jnp.max(cand2, axis=-1, keepdims=True)
        i2 = jnp.min(jnp.where(cand2 == p2, lane, float(LANES)), axis=-1, keepdims=True)
        denom = p1 + p2
        comb = jnp.where(lane == i1, g_p * p1 / denom, 0.0) + jnp.where(lane == i2, g_p * p2 / denom, 0.0)
        y = None
        for e in range(MOE_PER_GROUP):
            c = jnp.sum(jnp.where(lane_i == e0 + e, comb, 0.0), axis=-1, keepdims=True)
            h = jnp.dot(xm, w13s[e], preferred_element_type=F32)
            hid = (_silu(h[:, :MOE_FF]) * h[:, MOE_FF:] * c).astype(BF16)
            part = jnp.dot(hid, w2s[e], preferred_element_type=F32)
            y = part if y is None else y + part
        r = DEEPNORM_ALPHA * x1 + mod_row(5) * y

        @pl.when(i >= 2)
        def _():
            wait_scatter(slot)

        obuf[slot] = _layer_norm(r, g_ref[...], b_ref[...]).reshape(n_blk, SUBLANES, D_MODEL)
        start_scatter(slot)

    @pl.when(i == n_steps - 1)
    def _():
        @pl.when(n_tiles >= 2)
        def _():
            wait_scatter(lax.rem(n_tiles, 2))

        @pl.when(n_tiles >= 1)
        def _():
            wait_scatter(lax.rem(n_tiles + 1, 2))


def _inverse_perm_kernel(pos_ref, out_ref):
    def clear(s, carry):
        out_ref[s] = 0
        return carry

    def place(t, carry):
        out_ref[pos_ref[t]] = t + 1
        return carry

    lax.fori_loop(0, out_ref.shape[0], clear, 0, unroll=8)
    lax.fori_loop(0, pos_ref.shape[0], place, 0, unroll=8)


def _inverse_perm(pos):
    return pl.pallas_call(
        _inverse_perm_kernel,
        in_specs=[pl.BlockSpec(memory_space=pltpu.SMEM)],
        out_specs=pl.BlockSpec(memory_space=pltpu.SMEM),
        out_shape=jax.ShapeDtypeStruct((MOE_ROWS,), jnp.int32),
        name="moe_inverse_perm",
    )(pos)


def _moe_plan(meta, counts):
    ts = MOE_TILE
    cnt = counts[0, :MOE_GROUPS].astype(jnp.int32)
    tiles_g = (cnt + ts - 1) // ts
    tile_end = jnp.cumsum(tiles_g)
    row0_g = (tile_end - tiles_g) * ts
    keys = meta.reshape(NTOK)
    gid = keys // MOE_KEY_BASE
    rank = keys % MOE_KEY_BASE
    pos = row0_g[gid] + rank
    tok1 = _inverse_perm(pos)
    rows = jnp.arange(MOE_ROWS, dtype=jnp.int32)
    src = jnp.maximum(tok1 - 1, 0)
    spare = NTOK + ((rows // ts) % 2) * ts + rows % ts
    dst = jnp.where(tok1 > 0, src, spare)
    mid = jnp.where(src < NTOK_P, 0, 1 + (src - NTOK_P) // DEC_SEQ)
    tile_group = jnp.minimum(jnp.sum(jnp.arange(MOE_TILES)[:, None] >= tile_end[None, :], axis=1),
                             MOE_GROUPS - 1).astype(jnp.int32)
    n_tiles = tile_end[-1:].astype(jnp.int32)
    return (tile_group, n_tiles, src.reshape(MOE_TILES, 1, ts), dst.reshape(MOE_TILES, 1, ts),
            mid.reshape(MOE_TILES, ts // LANES, LANES))


def _moe(x1, plan, mods, w_router, b_router, w1, w3, w2, ln_g, ln_b, layer):
    ts = MOE_TILE
    tile_group, n_tiles, src, dst, mid = plan
    grp_shape = (DEPTH, MOE_GROUPS, MOE_PER_GROUP)
    w_in_spec = pl.BlockSpec((None, None, MOE_PER_GROUP, D_MODEL, MOE_FF), lambda i, tg, nt: (layer, tg[i], 0, 0, 0))
    w_out_spec = pl.BlockSpec((None, None, MOE_PER_GROUP, MOE_FF, D_MODEL), lambda i, tg, nt: (layer, tg[i], 0, 0, 0))
    smem_tile = functools.partial(pl.BlockSpec, (None, 1, ts), memory_space=pltpu.SMEM)
    grid_spec = pltpu.PrefetchScalarGridSpec(
        num_scalar_prefetch=2,
        grid=(MOE_TILES,),
        in_specs=[
            smem_tile(lambda i, tg, nt: (i, 0, 0)),
            smem_tile(lambda i, tg, nt: (jnp.minimum(i + 1, MOE_TILES - 1), 0, 0)),
            smem_tile(lambda i, tg, nt: (i, 0, 0)),
            pl.BlockSpec((None, ts // LANES, LANES), lambda i, tg, nt: (i, 0, 0)),
            pl.BlockSpec(memory_space=pl.ANY),
            pl.BlockSpec((None, N_COND, 6, D_MODEL), lambda i, tg, nt: (layer, 0, 0, 0)),
            pl.BlockSpec((D_MODEL, LANES), lambda i, tg, nt: (0, 0)),
            pl.BlockSpec((1, LANES), lambda i, tg, nt: (0, 0)),
            w_in_spec,
            w_in_spec,
            w_out_spec,
            pl.BlockSpec((None, 1, D_MODEL), lambda i, tg, nt: (layer, 0, 0)),
            pl.BlockSpec((None, 1, D_MODEL), lambda i, tg, nt: (layer, 0, 0)),
        ],
        out_specs=pl.BlockSpec(memory_space=pl.ANY),
        scratch_shapes=[
            pltpu.VMEM((2, ts // SUBLANES, SUBLANES, D_MODEL), F32),
            pltpu.VMEM((2, ts // SUBLANES, SUBLANES, D_MODEL), F32),
            pltpu.SemaphoreType.DMA((2,)),
            pltpu.SemaphoreType.DMA((2,)),
            pltpu.VMEM((MOE_PER_GROUP, D_MODEL, 2 * MOE_FF), BF16),
            pltpu.VMEM((MOE_PER_GROUP, MOE_FF, D_MODEL), BF16),
        ],
    )
    return pl.pallas_call(
        _moe_kernel,
        grid_spec=grid_spec,
        out_shape=jax.ShapeDtypeStruct((NTOK + 2 * ts, D_MODEL), F32),
        compiler_params=_params(("arbitrary",)),
        name="hier_moe_ln",
    )(tile_group, n_tiles, src, src, dst, mid, x1, mods, w_router, b_router,
      w1.reshape(grp_shape + (D_MODEL, MOE_FF)), w3.reshape(grp_shape + (D_MODEL, MOE_FF)),
      w2.reshape(grp_shape + (MOE_FF, D_MODEL)),
      ln_g.reshape(DEPTH, 1, D_MODEL), ln_b.reshape(DEPTH, 1, D_MODEL))


def _swap_halves(x, lane, half):
    return jnp.where(_mod_pow2(lane, 2 * half) < half,
                     pltpu.roll(x, LANES - half, 1), pltpu.roll(x, half, 1))


def _rope128(x, cos, sin_signed, lane, half):
    return x * cos + _swap_halves(x, lane, half) * sin_signed


def _rope_tables(rot_dim):
    rows = DEC_SEQ // GRID_W
    row = jnp.repeat(jnp.arange(rows, dtype=F32), GRID_W)
    col = jnp.tile(jnp.arange(GRID_W, dtype=F32), rows)
    n_freq = rot_dim // 4
    inv_freq = ROPE_BASE ** (-jnp.arange(n_freq, dtype=F32) / n_freq)
    ang = jnp.concatenate([row[:, None] * inv_freq, col[:, None] * inv_freq], axis=-1)
    cos, sin = jnp.cos(ang), jnp.sin(ang)
    reps = LANES // rot_dim
    cos_full = jnp.tile(jnp.concatenate([cos, cos], axis=-1), (1, reps))
    sin_signed = jnp.tile(jnp.concatenate([-sin, sin], axis=-1), (1, reps))
    return cos_full, sin_signed


def _exp_parts(s_list, scale):
    m = None
    for s in s_list:
        sm = jnp.max(s, axis=-1, keepdims=True)
        m = sm if m is None else jnp.maximum(m, sm)
    return [jnp.exp2((s - m) * (scale * LOG2E)).astype(BF16) for s in s_list]


def _pv_normalised(p_list, v_list):
    o = None
    for p, v in zip(p_list, v_list):
        part = jnp.dot(p, v, preferred_element_type=F32)
        o = part if o is None else o + part
    return o[:, :LANES] / o[:, LANES:]


def _mla_kv_kernel(new_tokens, *refs):
    if new_tokens:
        (ckv_ref, kr_ref, cos_ref, sin_ref, g_ref, wuk_ref, wuv_ref) = refs[:7]
        ckvc_ref, krc_ref, kcat_ref, vm_ref = refs[-4:]
        i = pl.program_id(0)
        x = ckv_ref[...].astype(F32)
        c = x * lax.rsqrt(jnp.mean(x * x, axis=-1, keepdims=True) + RMS_EPS) * g_ref[...]
        kr_raw = kr_ref[...].astype(F32)

        @pl.when(i < NTOK_P // ckv_ref.shape[0])
        def _():
            for b in range(ckvc_ref.shape[0]):
                ckvc_ref[b] = c[SEQ * b:SEQ * (b + 1), :]
                krc_ref[b] = kr_raw[SEQ * b:SEQ * (b + 1), :MLA_ROPE]

        lane = _lane_iota(kr_ref.shape)
        kr = _rope128(kr_raw, cos_ref[...], sin_ref[...], lane, MLA_ROPE // 2)
    else:
        ckv_ref, kr_ref, wuk_ref, wuv_ref, kcat_ref, vm_ref = refs
        c = ckv_ref[...]
        kr = kr_ref[...]
    cb = c.astype(BF16)
    kn = jnp.dot(cb, wuk_ref[...].astype(BF16), preferred_element_type=F32).astype(BF16)
    vv = jnp.dot(cb, wuv_ref[...].astype(BF16), preferred_element_type=F32).astype(BF16)
    krb = kr.astype(BF16)
    ones = jnp.ones((c.shape[0], LANES), BF16)
    for p in range(MLA_HEADS // 2):
        kcat_ref[:, 256 * p:256 * p + LANES] = kn[:, LANES * p:LANES * (p + 1)]
        kcat_ref[:, 256 * p + LANES:256 * (p + 1)] = krb
        vm_ref[:, 256 * p:256 * p + LANES] = vv[:, LANES * p:LANES * (p + 1)]
        vm_ref[:, 256 * p + LANES:256 * (p + 1)] = ones


def _mla_kv_new(zmla, cos_t, sin_t, kv_norm_g, w_uk, w_uv, j, prev):
    tm = 512
    npt = NTOK_P // tm
    nst = DEC_SEQ // tm
    nb = tm // SEQ

    def tab(i):
        return (jnp.where(i < npt, 0, 1 + (i - npt) % nst), 0)

    def cache_idx(i):
        return (jnp.minimum(i, npt - 1), j, 0, 0)

    in_specs = [
        pl.BlockSpec((tm, MLA_KV_RANK), lambda i: (i, 768 // MLA_KV_RANK)),
        pl.BlockSpec((tm, LANES), lambda i: (i, 1024 // LANES)),
        pl.BlockSpec((tm, LANES), tab),
        pl.BlockSpec((tm, LANES), tab),
        pl.BlockSpec((None, 1, MLA_KV_RANK), lambda i: (j, 0, 0)),
        pl.BlockSpec((None, MLA_KV_RANK, 512), lambda i: (j, 0, 0)),
        pl.BlockSpec((None, MLA_KV_RANK, 512), lambda i: (j, 0, 0)),
    ]
    args = [zmla, zmla, cos_t, sin_t, kv_norm_g.reshape(-1, 1, MLA_KV_RANK), w_uk, w_uv]
    aliases = {}
    if prev is not None:
        aliases = {len(args): 0, len(args) + 1: 1}
        in_specs += [pl.BlockSpec(memory_space=pl.ANY)] * 2
        args += list(prev)
    return pl.pallas_call(
        functools.partial(_mla_kv_kernel, True),
        grid=(NTOK // tm,),
        in_specs=in_specs,
        out_specs=[
            pl.BlockSpec((nb, None, SEQ, MLA_KV_RANK), cache_idx),
            pl.BlockSpec((nb, None, SEQ, MLA_ROPE), cache_idx),
            pl.BlockSpec((tm, 1024), lambda i: (i, 0)),
            pl.BlockSpec((tm, 1024), lambda i: (i, 0)),
        ],
        out_shape=[
            jax.ShapeDtypeStruct((BATCH, N_EVEN, SEQ, MLA_KV_RANK), F32),
            jax.ShapeDtypeStruct((BATCH, N_EVEN, SEQ, MLA_ROPE), F32),
            jax.ShapeDtypeStruct((NTOK, 1024), BF16),
            jax.ShapeDtypeStruct((NTOK, 1024), BF16),
        ],
        input_output_aliases=aliases,
        compiler_params=_params(("arbitrary",)),
        name="mla_kv_new",
    )(*args)


def _mla_kv_ctx(cache_ckv, kr_tiled, w_uk, w_uv, j):
    return pl.pallas_call(
        functools.partial(_mla_kv_kernel, False),
        grid=(DEC_BATCH,),
        in_specs=[
            pl.BlockSpec((None, None, PAST_LEN, MLA_KV_RANK), lambda b: (b, j, 0, 0)),
            pl.BlockSpec((None, PAST_LEN, LANES), lambda b: (b, 0, 0)),
            pl.BlockSpec((None, MLA_KV_RANK, 512), lambda b: (j, 0, 0)),
            pl.BlockSpec((None, MLA_KV_RANK, 512), lambda b: (j, 0, 0)),
        ],
        out_specs=[
            pl.BlockSpec((PAST_LEN, 1024), lambda b: (b, 0)),
            pl.BlockSpec((PAST_LEN, 1024), lambda b: (b, 0)),
        ],
        out_shape=[
            jax.ShapeDtypeStruct((DEC_BATCH * PAST_LEN, 1024), BF16),
            jax.ShapeDtypeStruct((DEC_BATCH * PAST_LEN, 1024), BF16),
        ],
        compiler_params=_params(("parallel",)),
        name="mla_kv_ctx",
    )(cache_ckv, kr_tiled, w_uk, w_uv)


def _mla_attn_kernel(latent, *refs):
    if latent:
        qn_ref, qr_ref, cos_ref, sin_ref, kc_ref, vc_ref, kn_ref, vn_ref, o_ref = refs
        k_refs, v_refs = (kc_ref, kn_ref), (vc_ref, vn_ref)
    else:
        qn_ref, qr_ref, kn_ref, vn_ref, o_ref = refs
        k_refs, v_refs = (kn_ref,), (vn_ref,)
    tq = ATT_TQ if latent else SEQ
    n_seq = qn_ref.shape[0] // tq
    lane = _lane_iota((tq, LANES))
    scale = (MLA_NOPE + MLA_ROPE) ** -0.5
    for sq in range(n_seq):
        rows = slice(tq * sq, tq * (sq + 1))
        krows = slice(None) if latent else rows
        qr_cols = []
        for cidx in range(2):
            x = qr_ref[rows, LANES * cidx:LANES * (cidx + 1)].astype(F32)
            if latent:
                x = _rope128(x, cos_ref[...], sin_ref[...], lane, MLA_ROPE // 2)
            qr_cols.append(x)
        o_prev = None
        for h in range(MLA_HEADS):
            p, half = divmod(h, 2)
            cidx, slot = divmod(h, 4)
            qa = jnp.where(_div_pow2(lane, MLA_NOPE) == half,
                           qn_ref[rows, LANES * p:LANES * (p + 1)].astype(F32), 0.0)
            qb = jnp.where(_div_pow2(lane, MLA_ROPE) == slot, qr_cols[cidx], 0.0)
            qcat = jnp.concatenate([qa, qb], axis=1).astype(BF16)
            pair_cols = slice(256 * p, 256 * (p + 1))
            s_list = [lax.dot_general(qcat, k_ref[krows, pair_cols], NT_DIMS, preferred_element_type=F32)
                      for k_ref in k_refs]
            o = _pv_normalised(_exp_parts(s_list, scale), [v_ref[krows, pair_cols] for v_ref in v_refs])
            if half == 0:
                o_prev = o
            else:
                o_ref[rows, LANES * p:LANES * (p + 1)] = jnp.where(lane < MLA_DV, o_prev, o).astype(BF16)


def _mla_attn_prompt(zmla, kcat, vm):
    rows = 4 * SEQ
    return pl.pallas_call(
        functools.partial(_mla_attn_kernel, False),
        grid=(NTOK_P // rows,),
        in_specs=[
            pl.BlockSpec((rows, 512), lambda b: (b, 0)),
            pl.BlockSpec((rows, 256), lambda b: (b, 2)),
            pl.BlockSpec((rows, 1024), lambda b: (b, 0)),
            pl.BlockSpec((rows, 1024), lambda b: (b, 0)),
        ],
        out_specs=pl.BlockSpec((rows, 512), lambda b: (b, 0)),
        out_shape=jax.ShapeDtypeStruct((NTOK_P, 512), BF16),
        compiler_params=_params(("parallel",)),
        name="mla_attn_prompt",
    )(zmla, zmla, kcat, vm)


def _mla_attn_latent(zmla, cos_t, sin_t, kcat_ctx, vm_ctx, kcat, vm):
    nq = DEC_SEQ // ATT_TQ
    row0 = NTOK_P // ATT_TQ
    seq0 = NTOK_P // DEC_SEQ
    return pl.pallas_call(
        functools.partial(_mla_attn_kernel, True),
        grid=(DEC_BATCH, nq),
        in_specs=[
            pl.BlockSpec((ATT_TQ, 512), lambda b, q: (row0 + b * nq + q, 0)),
            pl.BlockSpec((ATT_TQ, 256), lambda b, q: (row0 + b * nq + q, 2)),
            pl.BlockSpec((ATT_TQ, LANES), lambda b, q: (q, 0)),
            pl.BlockSpec((ATT_TQ, LANES), lambda b, q: (q, 0)),
            pl.BlockSpec((PAST_LEN, 1024), lambda b, q: (b, 0)),
            pl.BlockSpec((PAST_LEN, 1024), lambda b, q: (b, 0)),
            pl.BlockSpec((DEC_SEQ, 1024), lambda b, q: (seq0 + b, 0)),
            pl.BlockSpec((DEC_SEQ, 1024), lambda b, q: (seq0 + b, 0)),
        ],
        out_specs=pl.BlockSpec((ATT_TQ, 512), lambda b, q: (b * nq + q, 0)),
        out_shape=jax.ShapeDtypeStruct((NTOK_S, 512), BF16),
        compiler_params=_params(("parallel", "arbitrary")),
        name="mla_attn_latent",
    )(zmla, zmla, cos_t, sin_t, kcat_ctx, vm_ctx, kcat, vm)


def _log_sigmoid(x):
    return jnp.minimum(x, 0.0) - jnp.log1p(jnp.exp(-jnp.abs(x)))


def _retention_kernel(seq_len, n_seq, has_init, emit_state, has_prev, *refs):
    refs = list(refs)
    decf_ref, decb_ref, q_ref, k_ref, v_ref, g_ref = refs[:6]
    refs = refs[6:]
    if has_init:
        sf0_ref, sb0_ref = refs[:2]
        refs = refs[2:]
    if has_prev:
        refs = refs[2:]
    o_ref = refs[0]
    refs = refs[1:]
    if emit_state:
        sf_ref, sb_ref = refs[:2]
        refs = refs[2:]
    of_scr, ob_scr, dec_scr, wts_scr = refs

    c = RET_CHUNK
    n_chunks = seq_len // c
    pair = pl.program_id(0)
    lane = _lane_iota((c, LANES))
    zeros_half = jnp.zeros((RET_DK, RET_DV), F32)

    def log_gammas(half):
        head = 2 * pair + half
        return (_log_sigmoid(decf_ref[pl.ds(head, 1), :]),
                _log_sigmoid(decb_ref[pl.ds(head, 1), :]))

    @pl.when(pl.program_id(1) == 0)
    def _():
        ri = lax.broadcasted_iota(jnp.int32, (c, c), 0)
        ci = lax.broadcasted_iota(jnp.int32, (c, c), 1)
        rel = (ri - ci).astype(F32)
        row = lax.broadcasted_iota(jnp.int32, (c, LANES), 0).astype(F32)
        for half in range(2):
            lgf, lgb = log_gammas(half)
            dec_scr[half] = (jnp.where(rel >= 0, jnp.exp(lgf[:, 0:1] * jnp.maximum(rel, 0.0)), 0.0)
                             + jnp.where(rel <= 0, jnp.exp(lgb[:, 0:1] * jnp.maximum(-rel, 0.0)), 0.0))
            wts_scr[half, 0] = jnp.exp(lgf * (row + 1.0))
            wts_scr[half, 1] = jnp.exp(lgf * (c - 1.0 - row))
            wts_scr[half, 2] = jnp.exp(lgb * (c - row))
            wts_scr[half, 3] = jnp.exp(lgb * row)

    cross = has_init or n_chunks > 1
    chains = [(sq, half) for sq in range(n_seq) for half in range(2)]
    chunk_decay = []
    for half in range(2):
        lgf, lgb = log_gammas(half)
        chunk_decay.append((jnp.exp(lgf * float(c)), jnp.exp(lgb * float(c))))

    def rows_of(sq, n):
        start = sq * seq_len + n * c
        return pl.ds(start if isinstance(n, int) else pl.multiple_of(start, c), c)

    def load(sq, half, n):
        rows = rows_of(sq, n)
        vsl = slice(RET_DV * half, RET_DV * (half + 1))
        qm = jnp.where(_div_pow2(lane, RET_DK) == half, q_ref[rows, :].astype(F32), 0.0)
        kk = k_ref[rows, :].astype(F32) * (RET_DK ** -0.5)
        return rows, vsl, qm, kk, v_ref[rows, vsl]

    def init_state(s0_ref, half):
        if not has_init:
            return jnp.zeros((LANES, RET_DV), F32)
        s0 = s0_ref[half]
        return jnp.concatenate([s0, zeros_half] if half == 0 else [zeros_half, s0], axis=0)

    def fwd_step(sq, half, n, s_f):
        rows, vsl, qm, kk, vb = load(sq, half, n)
        s = lax.dot_general(qm.astype(BF16), kk.astype(BF16), NT_DIMS, preferred_element_type=F32)
        o = jnp.dot((s * dec_scr[half]).astype(BF16), vb, preferred_element_type=F32)
        if cross:
            o = o + jnp.dot((qm * wts_scr[half, 0]).astype(BF16), s_f.astype(BF16), preferred_element_type=F32)
        of_scr[rows, vsl] = o
        kv = lax.dot_general((kk * wts_scr[half, 1]).astype(BF16), vb, TN_DIMS, preferred_element_type=F32)
        return chunk_decay[half][0] * s_f + kv

    def bwd_step(sq, half, n, s_b):
        rows, vsl, qm, kk, vb = load(sq, half, n)
        if cross:
            ob_scr[rows, vsl] = jnp.dot((qm * wts_scr[half, 2]).astype(BF16), s_b.astype(BF16),
                                        preferred_element_type=F32)
        kv = lax.dot_general((kk * wts_scr[half, 3]).astype(BF16), vb, TN_DIMS, preferred_element_type=F32)
        return chunk_decay[half][1] * s_b + kv

    def finish(sq, n):
        rows = rows_of(sq, n)
        o2 = of_scr[rows, :] + ob_scr[rows, :] if cross else of_scr[rows, :]
        for half in range(2):
            vsl = slice(RET_DV * half, RET_DV * (half + 1))
            o = o2[:, vsl]
            mu = jnp.mean(o, axis=-1, keepdims=True)
            d = o - mu
            var = jnp.mean(d * d, axis=-1, keepdims=True)
            o_ref[rows, vsl] = (_silu(g_ref[rows, vsl].astype(F32)) * (d * lax.rsqrt(var + LN_EPS))).astype(BF16)

    s_f = tuple(init_state(sf0_ref if has_init else None, half) for _, half in chains)
    s_b = tuple(init_state(sb0_ref if has_init else None, half) for _, half in chains)
    if n_chunks == 1:
        s_f = tuple(fwd_step(sq, half, 0, s) for (sq, half), s in zip(chains, s_f))
        s_b = tuple(bwd_step(sq, half, 0, s) for (sq, half), s in zip(chains, s_b))
        for sq in range(n_seq):
            finish(sq, 0)
    else:
        def scan_step(n, carry):
            sf, sb = carry
            sf = tuple(fwd_step(sq, half, n, s) for (sq, half), s in zip(chains, sf))
            sb = tuple(bwd_step(sq, half, n_chunks - 1 - n, s) for (sq, half), s in zip(chains, sb))
            return sf, sb

        s_f, s_b = lax.fori_loop(0, n_chunks, scan_step, (s_f, s_b))

        def finish_step(n, carry):
            for sq in range(n_seq):
                finish(sq, n)
            return carry

        lax.fori_loop(0, n_chunks, finish_step, 0)
    if emit_state:
        for (sq, half), sf, sb in zip(chains, s_f, s_b):
            sf_ref[sq, half] = sf[RET_DK * half:RET_DK * (half + 1), :]
            sb_ref[sq, half] = sb[RET_DK * half:RET_DK * (half + 1), :]


def _retention(zret, decf, decb, j, latent, state_f=None, state_b=None, prev=None):
    seq_len = DEC_SEQ if latent else SEQ
    n_seq = 1 if latent else 4
    n_b = (DEC_BATCH if latent else BATCH) // n_seq
    rows = n_seq * seq_len
    row0 = NTOK_P // rows if latent else 0
    n_pairs = RET_HEADS // 2
    in_specs = [
        pl.BlockSpec((None, RET_HEADS, LANES), lambda p, b: (j, 0, 0)),
        pl.BlockSpec((None, RET_HEADS, LANES), lambda p, b: (j, 0, 0)),
        pl.BlockSpec((rows, LANES), lambda p, b: (row0 + b, p)),
        pl.BlockSpec((rows, LANES), lambda p, b: (row0 + b, 4 + p)),
        pl.BlockSpec((rows, 256), lambda p, b: (row0 + b, 4 + p)),
        pl.BlockSpec((rows, 256), lambda p, b: (row0 + b, 8 + p)),
    ]
    args = [decf, decb, zret, zret, zret, zret]
    out_specs = [pl.BlockSpec((rows, 256), lambda p, b: (b, p))]
    out_shape = [jax.ShapeDtypeStruct((n_b * rows, RET_HEADS * RET_DV), BF16)]
    aliases = {}
    if latent:
        st_spec = pl.BlockSpec((None, None, 2, RET_DK, RET_DV), lambda p, b: (b, j, p, 0, 0))
        in_specs += [st_spec, st_spec]
        args += [state_f, state_b]
    else:
        st_spec = pl.BlockSpec((n_seq, None, 2, RET_DK, RET_DV), lambda p, b: (b, j, p, 0, 0))
        out_specs += [st_spec, st_spec]
        out_shape += [jax.ShapeDtypeStruct((BATCH, N_EVEN, RET_HEADS, RET_DK, RET_DV), F32)] * 2
        if prev is not None:
            aliases = {len(args): 1, len(args) + 1: 2}
            in_specs += [pl.BlockSpec(memory_space=pl.ANY)] * 2
            args += list(prev)
    return pl.pallas_call(
        functools.partial(_retention_kernel, seq_len, n_seq, latent, not latent, bool(aliases)),
        grid=(n_pairs, n_b),
        in_specs=in_specs,
        out_specs=out_specs,
        out_shape=out_shape,
        scratch_shapes=[
            pltpu.VMEM((rows, 2 * RET_DV), F32),
            pltpu.VMEM((rows, 2 * RET_DV), F32),
            pltpu.VMEM((2, RET_CHUNK, RET_CHUNK), F32),
            pltpu.VMEM((2, 4, RET_CHUNK, LANES), F32),
        ],
        input_output_aliases=aliases,
        compiler_params=_params(("arbitrary", "arbitrary")),
        name="retention_latent" if latent else "retention_prompt",
    )(*args)


def _diff_prep_kernel(k_ref, v_ref, cos_ref, sin_ref, kr_ref, va_ref):
    lane = _lane_iota(cos_ref.shape)
    cos = cos_ref[...]
    sin = sin_ref[...]
    ones = jnp.ones(cos_ref.shape, BF16)
    for h in range(DIFF_HEADS):
        sl = slice(LANES * h, LANES * (h + 1))
        kr_ref[:, sl] = _rope128(k_ref[:, sl].astype(F32), cos, sin, lane, DIFF_DH // 2).astype(BF16)
        va_ref[:, 256 * h:256 * h + LANES] = v_ref[:, sl].astype(BF16)
        va_ref[:, 256 * h + LANES:256 * (h + 1)] = ones


def _diff_prep(zodd, cos_t, sin_t):
    tm = 512
    row0 = NTOK_P // tm
    nst = DEC_SEQ // tm
    return pl.pallas_call(
        _diff_prep_kernel,
        grid=(NTOK_S // tm,),
        in_specs=[
            pl.BlockSpec((tm, 1024), lambda i: (row0 + i, 1)),
            pl.BlockSpec((tm, 1024), lambda i: (row0 + i, 2)),
            pl.BlockSpec((tm, LANES), lambda i: (i % nst, 0)),
            pl.BlockSpec((tm, LANES), lambda i: (i % nst, 0)),
        ],
        out_specs=[pl.BlockSpec((tm, 1024), lambda i: (i, 0)), pl.BlockSpec((tm, 2048), lambda i: (i, 0))],
        out_shape=[jax.ShapeDtypeStruct((NTOK_S, 1024), BF16), jax.ShapeDtypeStruct((NTOK_S, 2048), BF16)],
        compiler_params=_params(("parallel",)),
        name="diff_rope_keys",
    )(zodd, zodd, cos_t, sin_t)


def _diff_attn_kernel(latent, lam_init, *refs):
    if latent:
        (lam_ref, ng_ref, q_ref, cos_ref, sin_ref, kc_ref, vc_ref, kn_ref, vn_ref, o_ref) = refs
    else:
        lam_ref, ng_ref, q_ref, k_ref, v_ref = refs[:5]
        o_ref, kout_ref, vout_ref = refs[-3:]
    tq = ATT_TQ if latent else SEQ
    n_seq = q_ref.shape[0] // tq
    lane = _lane_iota((tq, LANES))
    scale = DIFF_DH ** -0.5
    lp = lam_ref[...]
    lam = (jnp.exp(jnp.sum(lp[0:1, :] * lp[1:2, :], axis=-1, keepdims=True))
           - jnp.exp(jnp.sum(lp[2:3, :] * lp[3:4, :], axis=-1, keepdims=True)) + lam_init)
    ng = ng_ref[...]
    ones = jnp.ones((PAST_LEN if latent else tq, LANES), BF16)
    for sq, h in [(sq, h) for sq in range(n_seq) for h in range(DIFF_HEADS)]:
        sl = slice(LANES * h, LANES * (h + 1))
        rows = slice(tq * sq, tq * (sq + 1))
        qh = q_ref[rows, sl].astype(F32)
        if latent:
            qh = _rope128(qh, cos_ref[...], sin_ref[...], lane, DIFF_DH // 2)
            k_list = [kc_ref[h].astype(BF16), kn_ref[:, sl]]
            v_list = [jnp.concatenate([vc_ref[h].astype(BF16), ones], axis=1),
                      vn_ref[:, 256 * h:256 * (h + 1)]]
        else:
            kh = k_ref[rows, sl]
            vh = v_ref[rows, sl]
            kout_ref[sq, h] = kh.astype(F32)
            vout_ref[sq, h] = vh.astype(F32)
            k_list = [kh.astype(BF16)]
            v_list = [jnp.concatenate([vh.astype(BF16), ones], axis=1)]
        q1 = jnp.where(lane < DIFF_DH, qh, 0.0).astype(BF16)
        q2 = jnp.where(lane >= DIFF_DH, qh, 0.0).astype(BF16)
        s1 = [lax.dot_general(q1, kk, NT_DIMS, preferred_element_type=F32) for kk in k_list]
        s2 = [lax.dot_general(q2, kk, NT_DIMS, preferred_element_type=F32) for kk in k_list]
        o = _pv_normalised(_exp_parts(s1, scale), v_list) - lam * _pv_normalised(_exp_parts(s2, scale), v_list)
        y = o * lax.rsqrt(jnp.mean(o * o, axis=-1, keepdims=True) + RMS_EPS) * ng
        o_ref[rows, sl] = (y * (1.0 - lam_init)).astype(BF16)


def _diff_attn_prompt(zodd, lam_p, norm_g, j, lam_init, prev):
    n_seq = 2
    rows = n_seq * SEQ
    cache_shape = jax.ShapeDtypeStruct((BATCH, N_ODD, DIFF_HEADS, SEQ, LANES), F32)
    cache_spec = pl.BlockSpec((n_seq, None, DIFF_HEADS, SEQ, LANES), lambda b: (b, j, 0, 0, 0))
    in_specs = [
        pl.BlockSpec((None, 4, DIFF_DH), lambda b: (j, 0, 0)),
        pl.BlockSpec((None, 1, DIFF_DV), lambda b: (j, 0, 0)),
        pl.BlockSpec((rows, 1024), lambda b: (b, 0)),
        pl.BlockSpec((rows, 1024), lambda b: (b, 1)),
        pl.BlockSpec((rows, 1024), lambda b: (b, 2)),
    ]
    args = [lam_p, norm_g.reshape(-1, 1, DIFF_DV), zodd, zodd, zodd]
    aliases = {}
    if prev is not None:
        aliases = {len(args): 1, len(args) + 1: 2}
        in_specs += [pl.BlockSpec(memory_space=pl.ANY)] * 2
        args += list(prev)
    return pl.pallas_call(
        functools.partial(_diff_attn_kernel, False, lam_init),
        grid=(BATCH // n_seq,),
        in_specs=in_specs,
        out_specs=[pl.BlockSpec((rows, 1024), lambda b: (b, 0)), cache_spec, cache_spec],
        out_shape=[jax.ShapeDtypeStruct((NTOK_P, 1024), BF16), cache_shape, cache_shape],
        input_output_aliases=aliases,
        compiler_params=_params(("arbitrary",)),
        name="diff_attn_prompt",
    )(*args)


def _diff_attn_latent(zodd, lam_p, norm_g, cos_t, sin_t, cache_k, cache_v, k_rot, v_aug, j, lam_init):
    nq = DEC_SEQ // ATT_TQ
    row0 = NTOK_P // ATT_TQ
    ctx_spec = pl.BlockSpec((None, None, DIFF_HEADS, PAST_LEN, LANES), lambda b, q: (b, j, 0, 0, 0))
    return pl.pallas_call(
        functools.partial(_diff_attn_kernel, True, lam_init),
        grid=(DEC_BATCH, nq),
        in_specs=[
            pl.BlockSpec((None, 4, DIFF_DH), lambda b, q: (j, 0, 0)),
            pl.BlockSpec((None, 1, DIFF_DV), lambda b, q: (j, 0, 0)),
            pl.BlockSpec((ATT_TQ, 1024), lambda b, q: (row0 + b * nq + q, 0)),
            pl.BlockSpec((ATT_TQ, LANES), lambda b, q: (q, 0)),
            pl.BlockSpec((ATT_TQ, LANES), lambda b, q: (q, 0)),
            ctx_spec,
            ctx_spec,
            pl.BlockSpec((DEC_SEQ, 1024), lambda b, q: (b, 0)),
            pl.BlockSpec((DEC_SEQ, 2048), lambda b, q: (b, 0)),
        ],
        out_specs=pl.BlockSpec((ATT_TQ, 1024), lambda b, q: (b * nq + q, 0)),
        out_shape=jax.ShapeDtypeStruct((NTOK_S, 1024), BF16),
        compiler_params=_params(("parallel", "arbitrary")),
        name="diff_attn_latent",
    )(lam_p, norm_g.reshape(-1, 1, DIFF_DV), zodd, cos_t, sin_t, cache_k, cache_v, k_rot, v_aug)


def _mla_weight_t(w_in_t, j):
    base = RET_COLS
    mq = w_in_t[j, base:base + MLA_HEADS * (MLA_NOPE + MLA_ROPE)].reshape(MLA_HEADS, MLA_NOPE + MLA_ROPE, D_MODEL)
    qn = mq[:, :MLA_NOPE].reshape(MLA_HEADS * MLA_NOPE, D_MODEL)
    qr = mq[:, MLA_NOPE:].reshape(MLA_HEADS * MLA_ROPE, D_MODEL)
    ckv0 = base + MLA_HEADS * (MLA_NOPE + MLA_ROPE)
    ckv = w_in_t[j, ckv0:ckv0 + MLA_KV_RANK]
    kr = w_in_t[j, ckv0 + MLA_KV_RANK:]
    return jnp.concatenate([qn, qr, ckv, jnp.tile(kr, (LANES // MLA_ROPE, 1))], axis=0)


def kernel(x_prompt, x_sample, state_ret_fwd, state_ret_bwd, cache_mla_ckv, cache_mla_krope, cache_diff_k, cache_diff_v, c, c_ctx, ada_w, ada_b, ln1_g, ln1_b, ln2_g, ln2_b, ev_w_in, ev_w_out, ret_decay_fwd, ret_decay_bwd, mla_kv_norm_g, mla_w_uk, mla_w_uv, od_w_in, od_w_out, diff_lambda, diff_norm_g, moe_w_group, moe_b_group, moe_w_expert, moe_b_expert, moe_w1, moe_w3, moe_w2):
    x = (x_prompt.reshape(NTOK_P, D_MODEL), x_sample.reshape(NTOK_S, D_MODEL), 0)
    cond =jnp.concatenate([c_ctx[None, :], c, jnp.zeros((N_COND - 1 - DEC_BATCH, D_MODEL), F32)], axis=0)
    mods = _ada_all(cond, ada_w, ada_b).reshape(DEPTH, N_COND, 6, D_MODEL)

    cos_m, sin_m = _rope_tables(MLA_ROPE)
    cos_d, sin_d = _rope_tables(DIFF_DH)
    ident = 512
    cos_m_id = jnp.concatenate([jnp.ones((ident, LANES), F32), cos_m], axis=0)
    sin_m_id = jnp.concatenate([jnp.zeros((ident, LANES), F32), sin_m], axis=0)
    decf = jnp.broadcast_to(ret_decay_fwd[:, :, None], ret_decay_fwd.shape + (LANES,))
    decb = jnp.broadcast_to(ret_decay_bwd[:, :, None], ret_decay_bwd.shape + (LANES,))

    ev_w_in_t = jnp.swapaxes(ev_w_in, 1, 2)
    pad = LANES - MOE_GROUPS - MOE_EXPERTS
    ret_states = mla_caches = diff_caches = None
    for i in range(DEPTH):
        j = i // 2
        mod = mods[i]
        w_router = jnp.concatenate([moe_w_group[i], moe_w_expert[i], jnp.zeros((D_MODEL, pad), F32)], axis=1)
        b_router = jnp.concatenate([moe_b_group[i], moe_b_expert[i], jnp.zeros((pad,), F32)])[None, :]
        if i % 2 == 0:
            zret = _mm_mod(x, mod, ev_w_in_t, (j,), RET_COLS, 1536, "in_proj_retention", w_is_transposed=True)
            zmla = _mm_mod(x, mod, _mla_weight_t(ev_w_in_t, j), (), MLA_COLS, MLA_COLS, "in_proj_mla",
                           w_is_transposed=True)
            *mla_caches, kcat, vm = _mla_kv_new(zmla, cos_m_id, sin_m_id, mla_kv_norm_g, mla_w_uk, mla_w_uv,
                                                j, mla_caches)
            kr_ctx = jnp.tile(cache_mla_krope[:, j], (1, 1, LANES // MLA_ROPE))
            kcat_ctx, vm_ctx = _mla_kv_ctx(cache_mla_ckv, kr_ctx, mla_w_uk, mla_w_uv, j)
            a_ret_p, *ret_states = _retention(zret, decf, decb, j, False, prev=ret_states)
            (a_ret_s,) = _retention(zret, decf, decb, j, True, state_ret_fwd, state_ret_bwd)
            a_mla_p = _mla_attn_prompt(zmla, kcat, vm)
            a_mla_s = _mla_attn_latent(zmla, cos_m, sin_m, kcat_ctx, vm_ctx, kcat, vm)
            x1, meta, counts = _mm_ln([(a_ret_p, a_ret_s), (a_mla_p, a_mla_s)], ev_w_out, j, x, mod,
                                      ln1_g, ln1_b, w_router, b_router, i)
        else:
            lam_init = 0.8 - 0.6 * math.exp(-0.3 * i)
            zodd = _mm_mod(x, mod, od_w_in, (j,), 3072, 1536, "in_proj_diff")
            a_p, *diff_caches = _diff_attn_prompt(zodd, diff_lambda, diff_norm_g, j, lam_init, diff_caches)
            k_rot, v_aug = _diff_prep(zodd, cos_d, sin_d)
            a_s = _diff_attn_latent(zodd, diff_lambda, diff_norm_g, cos_d, sin_d, cache_diff_k, cache_diff_v,
                                    k_rot, v_aug, j, lam_init)
            x1, meta, counts = _mm_ln([(a_p, a_s)], od_w_out, j, x, mod, ln1_g, ln1_b, w_router, b_router, i)
        y = _moe(x1, _moe_plan(meta, counts), mods, w_router, b_router, moe_w1, moe_w3, moe_w2, ln2_g, ln2_b, i)
        x = (y, y, NTOK_P)

    y_prompt = x[0][:NTOK_P].reshape(BATCH, SEQ, D_MODEL)
    y_sample = x[1][NTOK_P:NTOK].reshape(DEC_BATCH, DEC_SEQ, D_MODEL)
    return (y_prompt, y_sample, ret_states[0], ret_states[1], mla_caches[0], mla_caches[1],
            diff_caches[0], diff_caches[1])
```

```python
import functools
import math

import jax
import jax.numpy as jnp
from jax import lax
from jax.experimental import pallas as pl
from jax.experimental.pallas import tpu as pltpu

D_MODEL = 1024
BATCH = 32
SEQ = 256
DEPTH = 4
N_EVEN = 2
N_ODD = 2
DEC_BATCH = 2
DEC_SEQ = 2048
PAST_LEN = 256
GRID_W = 64
LN_EPS = 1e-5
RMS_EPS = 1e-6
DEEPNORM_ALPHA = (2.0 * DEPTH) ** 0.25
ROPE_BASE = 10000.0
RET_HEADS = 8
RET_DK = 64
RET_DV = 128
MLA_HEADS = 8
MLA_NOPE = 64
MLA_ROPE = 32
MLA_DV = 64
MLA_KV_RANK = 256
DIFF_HEADS = 8
DIFF_DH = 64
DIFF_DV = 128
MOE_GROUPS = 4
MOE_PER_GROUP = 4
MOE_EXPERTS = 16
MOE_FF = 256

NTOK_P = BATCH * SEQ
NTOK_S = DEC_BATCH * DEC_SEQ
NTOK = NTOK_P + NTOK_S
N_COND = 8
LANES = 128
SUBLANES = 8
RET_COLS = 3072
MLA_COLS = 1152
ATT_TQ = 512
RET_CHUNK = 256
MOE_TILE = 512
MOE_TILES = (NTOK + MOE_GROUPS * (MOE_TILE - 1)) // MOE_TILE
MOE_ROWS = MOE_TILES * MOE_TILE
MOE_KEY_BASE = 16384
VMEM_LIMIT = 56 * 1024 * 1024
LOG2E = 1.4426950408889634

F32 = jnp.float32
BF16 = jnp.bfloat16
NT_DIMS = (((1,), (1,)), ((), ()))
TN_DIMS = (((0,), (0,)), ((), ()))


def _params(sem):
    return pltpu.CompilerParams(dimension_semantics=sem, vmem_limit_bytes=VMEM_LIMIT)


def _group_of_tile(i, tm):
    npt = NTOK_P // tm
    nst = DEC_SEQ // tm
    return jnp.where(i < npt, 0, 1 + (i - npt) // nst)


def _split_specs(tm, width, m_of, s_row0=0):
    npt = NTOK_P // tm
    s_blk0 = s_row0 // tm
    return [pl.BlockSpec((tm, width), lambda *g: (jnp.minimum(m_of(*g), npt - 1), 0)),
            pl.BlockSpec((tm, width), lambda *g: (jnp.maximum(m_of(*g) - npt, 0) + s_blk0, 0))]


def _read_split(p_ref, s_ref, m):
    return jnp.where(m < NTOK_P // p_ref.shape[0], p_ref[...], s_ref[...])


def _silu(x):
    return x * (1.0 / (1.0 + jnp.exp(-x)))


def _layer_norm(r, g, b):
    mu = jnp.mean(r, axis=-1, keepdims=True)
    d = r - mu
    var = jnp.mean(d * d, axis=-1, keepdims=True)
    return d * lax.rsqrt(var + LN_EPS) * g + b


def _lane_iota(shape):
    return lax.broadcasted_iota(jnp.int32, shape, 1)


def _div_pow2(x, d):
    assert d & (d - 1) == 0
    return jnp.right_shift(x, d.bit_length() - 1)


def _mod_pow2(x, d):
    assert d & (d - 1) == 0
    return jnp.bitwise_and(x, d - 1)


def _ada_kernel(c_ref, w_ref, b_ref, o_ref):
    h = _silu(c_ref[...]).astype(BF16)
    o_ref[...] = jnp.dot(h, w_ref[...].astype(BF16), preferred_element_type=F32) + b_ref[...]


def _ada_all(cond, ada_w, ada_b):
    tn = 768
    return pl.pallas_call(
        _ada_kernel,
        grid=(DEPTH, 6 * D_MODEL // tn),
        in_specs=[
            pl.BlockSpec((N_COND, D_MODEL), lambda l, n: (0, 0)),
            pl.BlockSpec((None, D_MODEL, tn), lambda l, n: (l, 0, n)),
            pl.BlockSpec((None, 1, tn), lambda l, n: (l, 0, n)),
        ],
        out_specs=pl.BlockSpec((None, N_COND, tn), lambda l, n: (l, 0, n)),
        out_shape=jax.ShapeDtypeStruct((DEPTH, N_COND, 6 * D_MODEL), F32),
        compiler_params=_params(("parallel", "parallel")),
        name="ada_modulation",
    )(cond, ada_w, ada_b.reshape(DEPTH, 1, 6 * D_MODEL))


def _mm_mod_kernel(w_is_transposed, xp_ref, xs_ref, mod_ref, w_ref, o_ref, wscr):
    m = pl.program_id(1)

    @pl.when(m == 0)
    def _():
        wscr[...] = w_ref[...].astype(BF16)

    sh = mod_ref[0:1, :]
    sc = mod_ref[1:2, :]
    xm = (_read_split(xp_ref, xs_ref, m) * (1.0 + sc) + sh).astype(BF16)
    if w_is_transposed:
        z = lax.dot_general(xm, wscr[...], NT_DIMS, preferred_element_type=F32)
    else:
        z = jnp.dot(xm, wscr[...], preferred_element_type=F32)
    o_ref[...] = z.astype(o_ref.dtype)


def _mm_mod(x, mod, w, w_index, n_cols, tn, name, w_is_transposed=False):
    tm = 1024
    if w_is_transposed:
        w_spec = pl.BlockSpec((None,) * len(w_index) + (tn, D_MODEL), lambda n, m: tuple(w_index) + (n, 0))
        w_scratch = pltpu.VMEM((tn, D_MODEL), BF16)
    else:
        w_spec = pl.BlockSpec((None,) * len(w_index) + (D_MODEL, tn), lambda n, m: tuple(w_index) + (0, n))
        w_scratch = pltpu.VMEM((D_MODEL, tn), BF16)
    return pl.pallas_call(
        functools.partial(_mm_mod_kernel, w_is_transposed),
        grid=(n_cols // tn, NTOK // tm),
        in_specs=_split_specs(tm, D_MODEL, lambda n, m: m, x[2]) + [
            pl.BlockSpec((None, 6, D_MODEL), lambda n, m: (_group_of_tile(m, tm), 0, 0)),
            w_spec,
        ],
        out_specs=pl.BlockSpec((tm, tn), lambda n, m: (m, n)),
        out_shape=jax.ShapeDtypeStruct((NTOK, n_cols), BF16),
        scratch_shapes=[w_scratch],
        compiler_params=_params(("arbitrary", "arbitrary")),
        name=name,
    )(x[0], x[1], mod, w)


def _router_probs(xm, wr_ref, br_ref):
    rows = xm.shape[0]
    z = jnp.dot(xm, wr_ref[...].astype(BF16), preferred_element_type=F32) + br_ref[...]
    lane_i = _lane_iota((rows, LANES))
    lane = lane_i.astype(F32)
    gmask = lane_i < MOE_GROUPS
    zg = jnp.where(gmask, z, -jnp.inf)
    pg = jnp.exp(zg - jnp.max(zg, axis=-1, keepdims=True))
    g_prob = pg / jnp.sum(pg, axis=-1, keepdims=True)
    g_p = jnp.max(g_prob, axis=-1, keepdims=True)
    g_idx = jnp.min(jnp.where(gmask & (g_prob == g_p), lane, float(LANES)), axis=-1, keepdims=True)
    return z, lane_i, lane, g_p, g_idx


def _mm_ln_kernel(k_sizes, *refs):
    n_a = len(k_sizes)
    a_refs = refs[:2 * n_a]
    (w_ref, xp_ref, xs_ref, mod_ref, g_ref, b_ref, wr_ref, br_ref,
     o_ref, meta_ref, cnt_ref, wscr, tri_scr, carry_scr) = refs[2 * n_a:]
    m = pl.program_id(0)
    tm = o_ref.shape[0]

    @pl.when(m == 0)
    def _():
        wscr[...] = w_ref[...].astype(BF16)
        ri = lax.broadcasted_iota(jnp.int32, (tm, tm), 0)
        ci = lax.broadcasted_iota(jnp.int32, (tm, tm), 1)
        tri_scr[...] = jnp.where(ci < ri, 1.0, 0.0).astype(BF16)
        carry_scr[...] = jnp.zeros_like(carry_scr)

    y = None
    k0 = 0
    for i, ks in enumerate(k_sizes):
        a = _read_split(a_refs[2 * i], a_refs[2 * i + 1], m)
        part = jnp.dot(a, wscr[k0:k0 + ks, :], preferred_element_type=F32)
        y = part if y is None else y + part
        k0 += ks
    gate = mod_ref[2:3, :]
    r = DEEPNORM_ALPHA * _read_split(xp_ref, xs_ref, m) + gate * y
    x1 = _layer_norm(r, g_ref[...], b_ref[...])
    o_ref[...] = x1

    xm = (x1 * (1.0 + mod_ref[4:5, :]) + mod_ref[3:4, :]).astype(BF16)
    _, lane_i, lane, _, g_idx = _router_probs(xm, wr_ref, br_ref)
    onehot = jnp.where(lane == g_idx, 1.0, 0.0)
    before = jnp.dot(tri_scr[...], onehot.astype(BF16), preferred_element_type=F32) + carry_scr[0:1, :]
    rank = jnp.sum(jnp.where(lane == g_idx, before, 0.0), axis=-1, keepdims=True)
    key_col = g_idx * float(MOE_KEY_BASE) + rank
    row_i = lax.broadcasted_iota(jnp.int32, (tm, LANES), 0)
    diag = jnp.where(lane_i == _mod_pow2(row_i, LANES), key_col, 0.0)
    meta_ref[...] = jnp.sum(diag.reshape(tm // LANES, LANES, LANES), axis=1).astype(jnp.int32)
    total = carry_scr[0:1, :] + jnp.sum(onehot, axis=0, keepdims=True)
    carry_scr[...] = jnp.broadcast_to(total, carry_scr.shape)
    cnt_ref[...] = jnp.broadcast_to(total, cnt_ref.shape)


def _mm_ln(a_pairs, w, j, x, mod, ln_g, ln_b, w_router, b_router, layer):
    tm = 512
    k_sizes = tuple(ap.shape[1] for ap, _ in a_pairs)
    k_tot = sum(k_sizes)
    in_specs = []
    args = []
    for (ap, a_s), ks in zip(a_pairs, k_sizes):
        in_specs += _split_specs(tm, ks, lambda m: m)
        args += [ap, a_s]
    in_specs += [pl.BlockSpec((None, k_tot, D_MODEL), lambda m: (j, 0, 0))]
    in_specs += _split_specs(tm, D_MODEL, lambda m: m, x[2])
    in_specs += [
        pl.BlockSpec((None, 6, D_MODEL), lambda m: (_group_of_tile(m, tm), 0, 0)),
        pl.BlockSpec((None, 1, D_MODEL), lambda m: (layer, 0, 0)),
        pl.BlockSpec((None, 1, D_MODEL), lambda m: (layer, 0, 0)),
        pl.BlockSpec((D_MODEL, LANES), lambda m: (0, 0)),
        pl.BlockSpec((1, LANES), lambda m: (0, 0)),
    ]
    return pl.pallas_call(
        functools.partial(_mm_ln_kernel, k_sizes),
        grid=(NTOK // tm,),
        in_specs=in_specs,
        out_specs=[
            pl.BlockSpec((tm, D_MODEL), lambda m: (m, 0)),
            pl.BlockSpec((None, tm // LANES, LANES), lambda m: (m, 0, 0)),
            pl.BlockSpec((N_COND, LANES), lambda m: (0, 0)),
        ],
        out_shape=[
            jax.ShapeDtypeStruct((NTOK, D_MODEL), F32),
            jax.ShapeDtypeStruct((NTOK // tm, tm // LANES, LANES), jnp.int32),
            jax.ShapeDtypeStruct((N_COND, LANES), F32),
        ],
        scratch_shapes=[
            pltpu.VMEM((k_tot, D_MODEL), BF16),
            pltpu.VMEM((tm, tm), BF16),
            pltpu.VMEM((N_COND, LANES), F32),
        ],
        compiler_params=_params(("arbitrary",)),
        name="out_proj_ln",
    )(*args, w, x[0], x[1], mod, ln_g.reshape(DEPTH, 1, D_MODEL), ln_b.reshape(DEPTH, 1, D_MODEL),
      w_router, b_router)


def _moe_kernel(tgrp_ref, ntile_ref, src0_ref, src1_ref, dst_ref, mid_ref, x_hbm, mod_ref, wr_ref, br_ref,
                w1_ref, w3_ref, w2_ref, g_ref, b_ref, y_hbm,
                gbuf, obuf, gsem, ssem, w13s, w2s):
    i = pl.program_id(0)
    n_steps = pl.num_programs(0)
    n_tiles = ntile_ref[0]
    n_blk = gbuf.shape[1]
    ts = n_blk * SUBLANES
    slot = lax.rem(i, 2)

    def start_gather(src_ref, s):
        def body(k, carry):
            for u in range(SUBLANES):
                tok = src_ref[0, k * SUBLANES + u]
                pltpu.make_async_copy(x_hbm.at[pl.ds(tok, 1), :], gbuf.at[s, k, pl.ds(u, 1), :],
                                      gsem.at[s]).start()
            return carry
        lax.fori_loop(0, n_blk, body, 0)

    def wait_gather(s):
        pltpu.make_async_copy(gbuf.at[s], gbuf.at[s], gsem.at[s]).wait()

    def start_scatter(s):
        def body(k, carry):
            for u in range(SUBLANES):
                tok = dst_ref[0, k * SUBLANES + u]
                pltpu.make_async_copy(obuf.at[s, k, pl.ds(u, 1), :], y_hbm.at[pl.ds(tok, 1), :],
                                      ssem.at[s]).start()
            return carry
        lax.fori_loop(0, n_blk, body, 0)

    def wait_scatter(s):
        pltpu.make_async_copy(obuf.at[s], obuf.at[s], ssem.at[s]).wait()

    @pl.when(i == 0)
    def _():
        start_gather(src0_ref, 0)

    @pl.when(i < n_tiles)
    def _():
        grp = tgrp_ref[i]
        wait_gather(slot)

        @pl.when(i + 1 < n_tiles)
        def _():
            start_gather(src1_ref, 1 - slot)

        @pl.when((i == 0) | (grp != tgrp_ref[jnp.maximum(i - 1, 0)]))
        def _():
            w13s[:, :, :MOE_FF] = w1_ref[...].astype(BF16)
            w13s[:, :, MOE_FF:] = w3_ref[...].astype(BF16)
            w2s[...] = w2_ref[...].astype(BF16)

        mid_rows = mid_ref[...].astype(F32)
        spread = jnp.concatenate([jnp.broadcast_to(mid_rows[a:a + 1, :], (LANES, LANES))
                                  for a in range(ts // LANES)], axis=0)
        row_i = lax.broadcasted_iota(jnp.int32, (ts, LANES), 0)
        mid = jnp.sum(jnp.where(_lane_iota((ts, LANES)) == _mod_pow2(row_i, LANES), spread, 0.0),
                      axis=-1, keepdims=True)

        def mod_row(k):
            return jnp.where(mid == 0, mod_ref[0, k:k + 1, :],
                             jnp.where(mid == 1, mod_ref[1, k:k + 1, :], mod_ref[2, k:k + 1, :]))

        x1 = gbuf[slot].reshape(ts, D_MODEL)
        xm = (x1 * (1.0 + mod_row(4)) + mod_row(3)).astype(BF16)
        z, lane_i, lane, g_p, _ = _router_probs(xm, wr_ref, br_ref)
        e0 = MOE_GROUPS + MOE_PER_GROUP * grp
        emask = (lane_i >= e0) & (lane_i < e0 + MOE_PER_GROUP)
        ze = jnp.where(emask, z, -jnp.inf)
        pe = jnp.exp(ze - jnp.max(ze, axis=-1, keepdims=True))
        e_prob = pe / jnp.sum(pe, axis=-1, keepdims=True)
        cand = jnp.where(emask, e_prob, -1.0)
        p1 = jnp.max(cand, axis=-1, keepdims=True)
        i1 = jnp.min(jnp.where(cand == p1, lane, float(LANES)), axis=-1, keepdims=True)
        cand2 = jnp.where(lane == i1, -1.0, cand)
        p2 = jnp.max(cand2, axis=-1, keepdims=True)
        i2 = jnp.min(jnp.where(cand2 == p2, lane, float(LANES)), axis=-1, keepdims=True)
        denom = p1 + p2
        comb = jnp.where(lane == i1, g_p * p1 / denom, 0.0) + jnp.where(lane == i2, g_p * p2 / denom, 0.0)
        y = None
        for e in range(MOE_PER_GROUP):
            c = jnp.sum(jnp.where(lane_i == e0 + e, comb, 0.0), axis=-1, keepdims=True)
            h = jnp.dot(xm, w13s[e], preferred_element_type=F32)
            hid = (_silu(h[:, :MOE_FF]) * h[:, MOE_FF:] * c).astype(BF16)
            part = jnp.dot(hid, w2s[e], preferred_element_type=F32)
            y = part if y is None else y + part
        r = DEEPNORM_ALPHA * x1 + mod_row(5) * y

        @pl.when(i >= 2)
        def _():
            wait_scatter(slot)

        obuf[slot] = _layer_norm(r, g_ref[...], b_ref[...]).reshape(n_blk, SUBLANES, D_MODEL)
        start_scatter(slot)

    @pl.when(i == n_steps - 1)
    def _():
        @pl.when(n_tiles >= 2)
        def _():
            wait_scatter(lax.rem(n_tiles, 2))

        @pl.when(n_tiles >= 1)
        def _():
            wait_scatter(lax.rem(n_tiles + 1, 2))


def _inverse_perm_kernel(pos_ref, out_ref):
    def clear(s, carry):
        out_ref[s] = 0
        return carry

    def place(t, carry):
        out_ref[pos_ref[t]] = t + 1
        return carry

    lax.fori_loop(0, out_ref.shape[0], clear, 0, unroll=8)
    lax.fori_loop(0, pos_ref.shape[0], place, 0, unroll=8)


def _inverse_perm(pos):
    return pl.pallas_call(
        _inverse_perm_kernel,
        in_specs=[pl.BlockSpec(memory_space=pltpu.SMEM)],
        out_specs=pl.BlockSpec(memory_space=pltpu.SMEM),
        out_shape=jax.ShapeDtypeStruct((MOE_ROWS,), jnp.int32),
        name="moe_inverse_perm",
    )(pos)


def _moe_plan(meta, counts):
    ts = MOE_TILE
    cnt = counts[0, :MOE_GROUPS].astype(jnp.int32)
    tiles_g = (cnt + ts - 1) // ts
    tile_end = jnp.cumsum(tiles_g)
    row0_g = (tile_end - tiles_g) * ts
    keys = meta.reshape(NTOK)
    gid = keys // MOE_KEY_BASE
    rank = keys % MOE_KEY_BASE
    pos = row0_g[gid] + rank
    tok1 = _inverse_perm(pos)
    rows = jnp.arange(MOE_ROWS, dtype=jnp.int32)
    src = jnp.maximum(tok1 - 1, 0)
    spare = NTOK + ((rows // ts) % 2) * ts + rows % ts
    dst = jnp.where(tok1 > 0, src, spare)
    mid = jnp.where(src < NTOK_P, 0, 1 + (src - NTOK_P) // DEC_SEQ)
    tile_group = jnp.minimum(jnp.sum(jnp.arange(MOE_TILES)[:, None] >= tile_end[None, :], axis=1),
                             MOE_GROUPS - 1).astype(jnp.int32)
    n_tiles = tile_end[-1:].astype(jnp.int32)
    return (tile_group, n_tiles, src.reshape(MOE_TILES, 1, ts), dst.reshape(MOE_TILES, 1, ts),
            mid.reshape(MOE_TILES, ts // LANES, LANES))


def _moe(x1, plan, mods, w_router, b_router, w1, w3, w2, ln_g, ln_b, layer):
    ts = MOE_TILE
    tile_group, n_tiles, src, dst, mid = plan
    grp_shape = (DEPTH, MOE_GROUPS, MOE_PER_GROUP)
    w_in_spec = pl.BlockSpec((None, None, MOE_PER_GROUP, D_MODEL, MOE_FF), lambda i, tg, nt: (layer, tg[i], 0, 0, 0))
    w_out_spec = pl.BlockSpec((None, None, MOE_PER_GROUP, MOE_FF, D_MODEL), lambda i, tg, nt: (layer, tg[i], 0, 0, 0))
    smem_tile = functools.partial(pl.BlockSpec, (None, 1, ts), memory_space=pltpu.SMEM)
    grid_spec = pltpu.PrefetchScalarGridSpec(
        num_scalar_prefetch=2,
        grid=(MOE_TILES,),
        in_specs=[
            smem_tile(lambda i, tg, nt: (i, 0, 0)),
            smem_tile(lambda i, tg, nt: (jnp.minimum(i + 1, MOE_TILES - 1), 0, 0)),
            smem_tile(lambda i, tg, nt: (i, 0, 0)),
            pl.BlockSpec((None, ts // LANES, LANES), lambda i, tg, nt: (i, 0, 0)),
            pl.BlockSpec(memory_space=pl.ANY),
            pl.BlockSpec((None, N_COND, 6, D_MODEL), lambda i, tg, nt: (layer, 0, 0, 0)),
            pl.BlockSpec((D_MODEL, LANES), lambda i, tg, nt: (0, 0)),
            pl.BlockSpec((1, LANES), lambda i, tg, nt: (0, 0)),
            w_in_spec,
            w_in_spec,
            w_out_spec,
            pl.BlockSpec((None, 1, D_MODEL), lambda i, tg, nt: (layer, 0, 0)),
            pl.BlockSpec((None, 1, D_MODEL), lambda i, tg, nt: (layer, 0, 0)),
        ],
        out_specs=pl.BlockSpec(memory_space=pl.ANY),
        scratch_shapes=[
            pltpu.VMEM((2, ts // SUBLANES, SUBLANES, D_MODEL), F32),
            pltpu.VMEM((2, ts // SUBLANES, SUBLANES, D_MODEL), F32),
            pltpu.SemaphoreType.DMA((2,)),
            pltpu.SemaphoreType.DMA((2,)),
            pltpu.VMEM((MOE_PER_GROUP, D_MODEL, 2 * MOE_FF), BF16),
            pltpu.VMEM((MOE_PER_GROUP, MOE_FF, D_MODEL), BF16),
        ],
    )
    return pl.pallas_call(
        _moe_kernel,
        grid_spec=grid_spec,
        out_shape=jax.ShapeDtypeStruct((NTOK + 2 * ts, D_MODEL), F32),
        compiler_params=_params(("arbitrary",)),
        name="hier_moe_ln",
    )(tile_group, n_tiles, src, src, dst, mid, x1, mods, w_router, b_router,
      w1.reshape(grp_shape + (D_MODEL, MOE_FF)), w3.reshape(grp_shape + (D_MODEL, MOE_FF)),
      w2.reshape(grp_shape + (MOE_FF, D_MODEL)),
      ln_g.reshape(DEPTH, 1, D_MODEL), ln_b.reshape(DEPTH, 1, D_MODEL))


def _swap_halves(x, lane, half):
    return jnp.where(_mod_pow2(lane, 2 * half) < half,
                     pltpu.roll(x, LANES - half, 1), pltpu.roll(x, half, 1))


def _rope128(x, cos, sin_signed, lane, half):
    return x * cos + _swap_halves(x, lane, half) * sin_signed


def _rope_tables(rot_dim):
    rows = DEC_SEQ // GRID_W
    row = jnp.repeat(jnp.arange(rows, dtype=F32), GRID_W)
    col = jnp.tile(jnp.arange(GRID_W, dtype=F32), rows)
    n_freq = rot_dim // 4
    inv_freq = ROPE_BASE ** (-jnp.arange(n_freq, dtype=F32) / n_freq)
    ang = jnp.concatenate([row[:, None] * inv_freq, col[:, None] * inv_freq], axis=-1)
    cos, sin = jnp.cos(ang), jnp.sin(ang)
    reps = LANES // rot_dim
    cos_full = jnp.tile(jnp.concatenate([cos, cos], axis=-1), (1, reps))
    sin_signed = jnp.tile(jnp.concatenate([-sin, sin], axis=-1), (1, reps))
    return cos_full, sin_signed


def _exp_parts(s_list, scale):
    m = None
    for s in s_list:
        sm = jnp.max(s, axis=-1, keepdims=True)
        m = sm if m is None else jnp.maximum(m, sm)
    return [jnp.exp2((s - m) * (scale * LOG2E)).astype(BF16) for s in s_list]


def _pv_normalised(p_list, v_list):
    o = None
    for p, v in zip(p_list, v_list):
        part = jnp.dot(p, v, preferred_element_type=F32)
        o = part if o is None else o + part
    return o[:, :LANES] / o[:, LANES:]


def _mla_kv_kernel(new_tokens, *refs):
    if new_tokens:
        (ckv_ref, kr_ref, cos_ref, sin_ref, g_ref, wuk_ref, wuv_ref) = refs[:7]
        ckvc_ref, krc_ref, kcat_ref, vm_ref = refs[-4:]
        i = pl.program_id(0)
        x = ckv_ref[...].astype(F32)
        c = x * lax.rsqrt(jnp.mean(x * x, axis=-1, keepdims=True) + RMS_EPS) * g_ref[...]
        kr_raw = kr_ref[...].astype(F32)

        @pl.when(i < NTOK_P // ckv_ref.shape[0])
        def _():
            for b in range(ckvc_ref.shape[0]):
                ckvc_ref[b] = c[SEQ * b:SEQ * (b + 1), :]
                krc_ref[b] = kr_raw[SEQ * b:SEQ * (b + 1), :MLA_ROPE]

        lane = _lane_iota(kr_ref.shape)
        kr = _rope128(kr_raw, cos_ref[...], sin_ref[...], lane, MLA_ROPE // 2)
    else:
        ckv_ref, kr_ref, wuk_ref, wuv_ref, kcat_ref, vm_ref = refs
        c = ckv_ref[...]
        kr = kr_ref[...]
    cb = c.astype(BF16)
    kn = jnp.dot(cb, wuk_ref[...].astype(BF16), preferred_element_type=F32).astype(BF16)
    vv = jnp.dot(cb, wuv_ref[...].astype(BF16), preferred_element_type=F32).astype(BF16)
    krb = kr.astype(BF16)
    ones = jnp.ones((c.shape[0], LANES), BF16)
    for p in range(MLA_HEADS // 2):
        kcat_ref[:, 256 * p:256 * p + LANES] = kn[:, LANES * p:LANES * (p + 1)]
        kcat_ref[:, 256 * p + LANES:256 * (p + 1)] = krb
        vm_ref[:, 256 * p:256 * p + LANES] = vv[:, LANES * p:LANES * (p + 1)]
        vm_ref[:, 256 * p + LANES:256 * (p + 1)] = ones


def _mla_kv_new(zmla, cos_t, sin_t, kv_norm_g, w_uk, w_uv, j, prev):
    tm = 512
    npt = NTOK_P // tm
    nst = DEC_SEQ // tm
    nb = tm // SEQ

    def tab(i):
        return (jnp.where(i < npt, 0, 1 + (i - npt) % nst), 0)

    def cache_idx(i):
        return (jnp.minimum(i, npt - 1), j, 0, 0)

    in_specs = [
        pl.BlockSpec((tm, MLA_KV_RANK), lambda i: (i, 768 // MLA_KV_RANK)),
        pl.BlockSpec((tm, LANES), lambda i: (i, 1024 // LANES)),
        pl.BlockSpec((tm, LANES), tab),
        pl.BlockSpec((tm, LANES), tab),
        pl.BlockSpec((None, 1, MLA_KV_RANK), lambda i: (j, 0, 0)),
        pl.BlockSpec((None, MLA_KV_RANK, 512), lambda i: (j, 0, 0)),
        pl.BlockSpec((None, MLA_KV_RANK, 512), lambda i: (j, 0, 0)),
    ]
    args = [zmla, zmla, cos_t, sin_t, kv_norm_g.reshape(-1, 1, MLA_KV_RANK), w_uk, w_uv]
    aliases = {}
    if prev is not None:
        aliases = {len(args): 0, len(args) + 1: 1}
        in_specs += [pl.BlockSpec(memory_space=pl.ANY)] * 2
        args += list(prev)
    return pl.pallas_call(
        functools.partial(_mla_kv_kernel, True),
        grid=(NTOK // tm,),
        in_specs=in_specs,
        out_specs=[
            pl.BlockSpec((nb, None, SEQ, MLA_KV_RANK), cache_idx),
            pl.BlockSpec((nb, None, SEQ, MLA_ROPE), cache_idx),
            pl.BlockSpec((tm, 1024), lambda i: (i, 0)),
            pl.BlockSpec((tm, 1024), lambda i: (i, 0)),
        ],
        out_shape=[
            jax.ShapeDtypeStruct((BATCH, N_EVEN, SEQ, MLA_KV_RANK), F32),
            jax.ShapeDtypeStruct((BATCH, N_EVEN, SEQ, MLA_ROPE), F32),
            jax.ShapeDtypeStruct((NTOK, 1024), BF16),
            jax.ShapeDtypeStruct((NTOK, 1024), BF16),
        ],
        input_output_aliases=aliases,
        compiler_params=_params(("arbitrary",)),
        name="mla_kv_new",
    )(*args)


def _mla_kv_ctx(cache_ckv, kr_tiled, w_uk, w_uv, j):
    return pl.pallas_call(
        functools.partial(_mla_kv_kernel, False),
        grid=(DEC_BATCH,),
        in_specs=[
            pl.BlockSpec((None, None, PAST_LEN, MLA_KV_RANK), lambda b: (b, j, 0, 0)),
            pl.BlockSpec((None, PAST_LEN, LANES), lambda b: (b, 0, 0)),
            pl.BlockSpec((None, MLA_KV_RANK, 512), lambda b: (j, 0, 0)),
            pl.BlockSpec((None, MLA_KV_RANK, 512), lambda b: (j, 0, 0)),
        ],
        out_specs=[
            pl.BlockSpec((PAST_LEN, 1024), lambda b: (b, 0)),
            pl.BlockSpec((PAST_LEN, 1024), lambda b: (b, 0)),
        ],
        out_shape=[
            jax.ShapeDtypeStruct((DEC_BATCH * PAST_LEN, 1024), BF16),
            jax.ShapeDtypeStruct((DEC_BATCH * PAST_LEN, 1024), BF16),
        ],
        compiler_params=_params(("parallel",)),
        name="mla_kv_ctx",
    )(cache_ckv, kr_tiled, w_uk, w_uv)


def _mla_attn_kernel(latent, *refs):
    if latent:
        qn_ref, qr_ref, cos_ref, sin_ref, kc_ref, vc_ref, kn_ref, vn_ref, o_ref = refs
        k_refs, v_refs = (kc_ref, kn_ref), (vc_ref, vn_ref)
    else:
        qn_ref, qr_ref, kn_ref, vn_ref, o_ref = refs
        k_refs, v_refs = (kn_ref,), (vn_ref,)
    tq = ATT_TQ if latent else SEQ
    n_seq = qn_ref.shape[0] // tq
    lane = _lane_iota((tq, LANES))
    scale = (MLA_NOPE + MLA_ROPE) ** -0.5
    for sq in range(n_seq):
        rows = slice(tq * sq, tq * (sq + 1))
        krows = slice(None) if latent else rows
        qr_cols = []
        for cidx in range(2):
            x = qr_ref[rows, LANES * cidx:LANES * (cidx + 1)].astype(F32)
            if latent:
                x = _rope128(x, cos_ref[...], sin_ref[...], lane, MLA_ROPE // 2)
            qr_cols.append(x)
        o_prev = None
        for h in range(MLA_HEADS):
            p, half = divmod(h, 2)
            cidx, slot = divmod(h, 4)
            qa = jnp.where(_div_pow2(lane, MLA_NOPE) == half,
                           qn_ref[rows, LANES * p:LANES * (p + 1)].astype(F32), 0.0)
            qb = jnp.where(_div_pow2(lane, MLA_ROPE) == slot, qr_cols[cidx], 0.0)
            qcat = jnp.concatenate([qa, qb], axis=1).astype(BF16)
            pair_cols = slice(256 * p, 256 * (p + 1))
            s_list = [lax.dot_general(qcat, k_ref[krows, pair_cols], NT_DIMS, preferred_element_type=F32)
                      for k_ref in k_refs]
            o = _pv_normalised(_exp_parts(s_list, scale), [v_ref[krows, pair_cols] for v_ref in v_refs])
            if half == 0:
                o_prev = o
            else:
                o_ref[rows, LANES * p:LANES * (p + 1)] = jnp.where(lane < MLA_DV, o_prev, o).astype(BF16)


def _mla_attn_prompt(zmla, kcat, vm):
    rows = 4 * SEQ
    return pl.pallas_call(
        functools.partial(_mla_attn_kernel, False),
        grid=(NTOK_P // rows,),
        in_specs=[
            pl.BlockSpec((rows, 512), lambda b: (b, 0)),
            pl.BlockSpec((rows, 256), lambda b: (b, 2)),
            pl.BlockSpec((rows, 1024), lambda b: (b, 0)),
            pl.BlockSpec((rows, 1024), lambda b: (b, 0)),
        ],
        out_specs=pl.BlockSpec((rows, 512), lambda b: (b, 0)),
        out_shape=jax.ShapeDtypeStruct((NTOK_P, 512), BF16),
        compiler_params=_params(("parallel",)),
        name="mla_attn_prompt",
    )(zmla, zmla, kcat, vm)


def _mla_attn_latent(zmla, cos_t, sin_t, kcat_ctx, vm_ctx, kcat, vm):
    nq = DEC_SEQ // ATT_TQ
    row0 = NTOK_P // ATT_TQ
    seq0 = NTOK_P // DEC_SEQ
    return pl.pallas_call(
        functools.partial(_mla_attn_kernel, True),
        grid=(DEC_BATCH, nq),
        in_specs=[
            pl.BlockSpec((ATT_TQ, 512), lambda b, q: (row0 + b * nq + q, 0)),
            pl.BlockSpec((ATT_TQ, 256), lambda b, q: (row0 + b * nq + q, 2)),
            pl.BlockSpec((ATT_TQ, LANES), lambda b, q: (q, 0)),
            pl.BlockSpec((ATT_TQ, LANES), lambda b, q: (q, 0)),
            pl.BlockSpec((PAST_LEN, 1024), lambda b, q: (b, 0)),
            pl.BlockSpec((PAST_LEN, 1024), lambda b, q: (b, 0)),
            pl.BlockSpec((DEC_SEQ, 1024), lambda b, q: (seq0 + b, 0)),
            pl.BlockSpec((DEC_SEQ, 1024), lambda b, q: (seq0 + b, 0)),
        ],
        out_specs=pl.BlockSpec((ATT_TQ, 512), lambda b, q: (b * nq + q, 0)),
        out_shape=jax.ShapeDtypeStruct((NTOK_S, 512), BF16),
        compiler_params=_params(("parallel", "arbitrary")),
        name="mla_attn_latent",
    )(zmla, zmla, cos_t, sin_t, kcat_ctx, vm_ctx, kcat, vm)


def _log_sigmoid(x):
    return jnp.minimum(x, 0.0) - jnp.log1p(jnp.exp(-jnp.abs(x)))


def _retention_kernel(seq_len, n_seq, has_init, emit_state, has_prev, *refs):
    refs = list(refs)
    decf_ref, decb_ref, q_ref, k_ref, v_ref, g_ref = refs[:6]
    refs = refs[6:]
    if has_init:
        sf0_ref, sb0_ref = refs[:2]
        refs = refs[2:]
    if has_prev:
        refs = refs[2:]
    o_ref = refs[0]
    refs = refs[1:]
    if emit_state:
        sf_ref, sb_ref = refs[:2]
        refs = refs[2:]
    of_scr, ob_scr, dec_scr, wts_scr = refs

    c = RET_CHUNK
    n_chunks = seq_len // c
    pair = pl.program_id(0)
    lane = _lane_iota((c, LANES))
    zeros_half = jnp.zeros((RET_DK, RET_DV), F32)

    def log_gammas(half):
        head = 2 * pair + half
        return (_log_sigmoid(decf_ref[pl.ds(head, 1), :]),
                _log_sigmoid(decb_ref[pl.ds(head, 1), :]))

    @pl.when(pl.program_id(1) == 0)
    def _():
        ri = lax.broadcasted_iota(jnp.int32, (c, c), 0)
        ci = lax.broadcasted_iota(jnp.int32, (c, c), 1)
        rel = (ri - ci).astype(F32)
        row = lax.broadcasted_iota(jnp.int32, (c, LANES), 0).astype(F32)
        for half in range(2):
            lgf, lgb = log_gammas(half)
            dec_scr[half] = (jnp.where(rel >= 0, jnp.exp(lgf[:, 0:1] * jnp.maximum(rel, 0.0)), 0.0)
                             + jnp.where(rel <= 0, jnp.exp(lgb[:, 0:1] * jnp.maximum(-rel, 0.0)), 0.0))
            wts_scr[half, 0] = jnp.exp(lgf * (row + 1.0))
            wts_scr[half, 1] = jnp.exp(lgf * (c - 1.0 - row))
            wts_scr[half, 2] = jnp.exp(lgb * (c - row))
            wts_scr[half, 3] = jnp.exp(lgb * row)

    cross = has_init or n_chunks > 1
    chains = [(sq, half) for sq in range(n_seq) for half in range(2)]
    chunk_decay = []
    for half in range(2):
        lgf, lgb = log_gammas(half)
        chunk_decay.append((jnp.exp(lgf * float(c)), jnp.exp(lgb * float(c))))

    def rows_of(sq, n):
        start = sq * seq_len + n * c
        return pl.ds(start if isinstance(n, int) else pl.multiple_of(start, c), c)

    def load(sq, half, n):
        rows = rows_of(sq, n)
        vsl = slice(RET_DV * half, RET_DV * (half + 1))
        qm = jnp.where(_div_pow2(lane, RET_DK) == half, q_ref[rows, :].astype(F32), 0.0)
        kk = k_ref[rows, :].astype(F32) * (RET_DK ** -0.5)
        return rows, vsl, qm, kk, v_ref[rows, vsl]

    def init_state(s0_ref, half):
        if not has_init:
            return jnp.zeros((LANES, RET_DV), F32)
        s0 = s0_ref[half]
        return jnp.concatenate([s0, zeros_half] if half == 0 else [zeros_half, s0], axis=0)

    def fwd_step(sq, half, n, s_f):
        rows, vsl, qm, kk, vb = load(sq, half, n)
        s = lax.dot_general(qm.astype(BF16), kk.astype(BF16), NT_DIMS, preferred_element_type=F32)
        o = jnp.dot((s * dec_scr[half]).astype(BF16), vb, preferred_element_type=F32)
        if cross:
            o = o + jnp.dot((qm * wts_scr[half, 0]).astype(BF16), s_f.astype(BF16), preferred_element_type=F32)
        of_scr[rows, vsl] = o
        kv = lax.dot_general((kk * wts_scr[half, 1]).astype(BF16), vb, TN_DIMS, preferred_element_type=F32)
        return chunk_decay[half][0] * s_f + kv

    def bwd_step(sq, half, n, s_b):
        rows, vsl, qm, kk, vb = load(sq, half, n)
        if cross:
            ob_scr[rows, vsl] = jnp.dot((qm * wts_scr[half, 2]).astype(BF16), s_b.astype(BF16),
                                        preferred_element_type=F32)
        kv = lax.dot_general((kk * wts_scr[half, 3]).astype(BF16), vb, TN_DIMS, preferred_element_type=F32)
        return chunk_decay[half][1] * s_b + kv

    def finish(sq, n):
        rows = rows_of(sq, n)
        o2 = of_scr[rows, :] + ob_scr[rows, :] if cross else of_scr[rows, :]
        for half in range(2):
            vsl = slice(RET_DV * half, RET_DV * (half + 1))
            o = o2[:, vsl]
            mu = jnp.mean(o, axis=-1, keepdims=True)
            d = o - mu
            var = jnp.mean(d * d, axis=-1, keepdims=True)
            o_ref[rows, vsl] = (_silu(g_ref[rows, vsl].astype(F32)) * (d * lax.rsqrt(var + LN_EPS))).astype(BF16)

    s_f = tuple(init_state(sf0_ref if has_init else None, half) for _, half in chains)
    s_b = tuple(init_state(sb0_ref if has_init else None, half) for _, half in chains)
    if n_chunks == 1:
        s_f = tuple(fwd_step(sq, half, 0, s) for (sq, half), s in zip(chains, s_f))
        s_b = tuple(bwd_step(sq, half, 0, s) for (sq, half), s in zip(chains, s_b))
        for sq in range(n_seq):
            finish(sq, 0)
    else:
        def scan_step(n, carry):
            sf, sb = carry
            sf = tuple(fwd_step(sq, half, n, s) for (sq, half), s in zip(chains, sf))
            sb = tuple(bwd_step(sq, half, n_chunks - 1 - n, s) for (sq, half), s in zip(chains, sb))
            return sf, sb

        s_f, s_b = lax.fori_loop(0, n_chunks, scan_step, (s_f, s_b))

        def finish_step(n, carry):
            for sq in range(n_seq):
                finish(sq, n)
            return carry

        lax.fori_loop(0, n_chunks, finish_step, 0)
    if emit_state:
        for (sq, half), sf, sb in zip(chains, s_f, s_b):
            sf_ref[sq, half] = sf[RET_DK * half:RET_DK * (half + 1), :]
            sb_ref[sq, half] = sb[RET_DK * half:RET_DK * (half + 1), :]


def _retention(zret, decf, decb, j, latent, state_f=None, state_b=None, prev=None):
    seq_len = DEC_SEQ if latent else SEQ
    n_seq = 1 if latent else 4
    n_b = (DEC_BATCH if latent else BATCH) // n_seq
    rows = n_seq * seq_len
    row0 = NTOK_P // rows if latent else 0
    n_pairs = RET_HEADS // 2
    in_specs = [
        pl.BlockSpec((None, RET_HEADS, LANES), lambda p, b: (j, 0, 0)),
        pl.BlockSpec((None, RET_HEADS, LANES), lambda p, b: (j, 0, 0)),
        pl.BlockSpec((rows, LANES), lambda p, b: (row0 + b, p)),
        pl.BlockSpec((rows, LANES), lambda p, b: (row0 + b, 4 + p)),
        pl.BlockSpec((rows, 256), lambda p, b: (row0 + b, 4 + p)),
        pl.BlockSpec((rows, 256), lambda p, b: (row0 + b, 8 + p)),
    ]
    args = [decf, decb, zret, zret, zret, zret]
    out_specs = [pl.BlockSpec((rows, 256), lambda p, b: (b, p))]
    out_shape = [jax.ShapeDtypeStruct((n_b * rows, RET_HEADS * RET_DV), BF16)]
    aliases = {}
    if latent:
        st_spec = pl.BlockSpec((None, None, 2, RET_DK, RET_DV), lambda p, b: (b, j, p, 0, 0))
        in_specs += [st_spec, st_spec]
        args += [state_f, state_b]
    else:
        st_spec = pl.BlockSpec((n_seq, None, 2, RET_DK, RET_DV), lambda p, b: (b, j, p, 0, 0))
        out_specs += [st_spec, st_spec]
        out_shape += [jax.ShapeDtypeStruct((BATCH, N_EVEN, RET_HEADS, RET_DK, RET_DV), F32)] * 2
        if prev is not None:
            aliases = {len(args): 1, len(args) + 1: 2}
            in_specs += [pl.BlockSpec(memory_space=pl.ANY)] * 2
            args += list(prev)
    return pl.pallas_call(
        functools.partial(_retention_kernel, seq_len, n_seq, latent, not latent, bool(aliases)),
        grid=(n_pairs, n_b),
        in_specs=in_specs,
        out_specs=out_specs,
        out_shape=out_shape,
        scratch_shapes=[
            pltpu.VMEM((rows, 2 * RET_DV), F32),
            pltpu.VMEM((rows, 2 * RET_DV), F32),
            pltpu.VMEM((2, RET_CHUNK, RET_CHUNK), F32),
            pltpu.VMEM((2, 4, RET_CHUNK, LANES), F32),
        ],
        input_output_aliases=aliases,
        compiler_params=_params(("arbitrary", "arbitrary")),
        name="retention_latent" if latent else "retention_prompt",
    )(*args)


def _diff_prep_kernel(k_ref, v_ref, cos_ref, sin_ref, kr_ref, va_ref):
    lane = _lane_iota(cos_ref.shape)
    cos = cos_ref[...]
    sin = sin_ref[...]
    ones = jnp.ones(cos_ref.shape, BF16)
    for h in range(DIFF_HEADS):
        sl = slice(LANES * h, LANES * (h + 1))
        kr_ref[:, sl] = _rope128(k_ref[:, sl].astype(F32), cos, sin, lane, DIFF_DH // 2).astype(BF16)
        va_ref[:, 256 * h:256 * h + LANES] = v_ref[:, sl].astype(BF16)
        va_ref[:, 256 * h + LANES:256 * (h + 1)] = ones


def _diff_prep(zodd, cos_t, sin_t):
    tm = 512
    row0 = NTOK_P // tm
    nst = DEC_SEQ // tm
    return pl.pallas_call(
        _diff_prep_kernel,
        grid=(NTOK_S // tm,),
        in_specs=[
            pl.BlockSpec((tm, 1024), lambda i: (row0 + i, 1)),
            pl.BlockSpec((tm, 1024), lambda i: (row0 + i, 2)),
            pl.BlockSpec((tm, LANES), lambda i: (i % nst, 0)),
            pl.BlockSpec((tm, LANES), lambda i: (i % nst, 0)),
        ],
        out_specs=[pl.BlockSpec((tm, 1024), lambda i: (i, 0)), pl.BlockSpec((tm, 2048), lambda i: (i, 0))],
        out_shape=[jax.ShapeDtypeStruct((NTOK_S, 1024), BF16), jax.ShapeDtypeStruct((NTOK_S, 2048), BF16)],
        compiler_params=_params(("parallel",)),
        name="diff_rope_keys",
    )(zodd, zodd, cos_t, sin_t)


def _diff_attn_kernel(latent, lam_init, *refs):
    if latent:
        (lam_ref, ng_ref, q_ref, cos_ref, sin_ref, kc_ref, vc_ref, kn_ref, vn_ref, o_ref) = refs
    else:
        lam_ref, ng_ref, q_ref, k_ref, v_ref = refs[:5]
        o_ref, kout_ref, vout_ref = refs[-3:]
    tq = ATT_TQ if latent else SEQ
    n_seq = q_ref.shape[0] // tq
    lane = _lane_iota((tq, LANES))
    scale = DIFF_DH ** -0.5
    lp = lam_ref[...]
    lam = (jnp.exp(jnp.sum(lp[0:1, :] * lp[1:2, :], axis=-1, keepdims=True))
           - jnp.exp(jnp.sum(lp[2:3, :] * lp[3:4, :], axis=-1, keepdims=True)) + lam_init)
    ng = ng_ref[...]
    ones = jnp.ones((PAST_LEN if latent else tq, LANES), BF16)
    for sq, h in [(sq, h) for sq in range(n_seq) for h in range(DIFF_HEADS)]:
        sl = slice(LANES * h, LANES * (h + 1))
        rows = slice(tq * sq, tq * (sq + 1))
        qh = q_ref[rows, sl].astype(F32)
        if latent:
            qh = _rope128(qh, cos_ref[...], sin_ref[...], lane, DIFF_DH // 2)
            k_list = [kc_ref[h].astype(BF16), kn_ref[:, sl]]
            v_list = [jnp.concatenate([vc_ref[h].astype(BF16), ones], axis=1),
                      vn_ref[:, 256 * h:256 * (h + 1)]]
        else:
            kh = k_ref[rows, sl]
            vh = v_ref[rows, sl]
            kout_ref[sq, h] = kh.astype(F32)
            vout_ref[sq, h] = vh.astype(F32)
            k_list = [kh.astype(BF16)]
            v_list = [jnp.concatenate([vh.astype(BF16), ones], axis=1)]
        q1 = jnp.where(lane < DIFF_DH, qh, 0.0).astype(BF16)
        q2 = jnp.where(lane >= DIFF_DH, qh, 0.0).astype(BF16)
        s1 = [lax.dot_general(q1, kk, NT_DIMS, preferred_element_type=F32) for kk in k_list]
        s2 = [lax.dot_general(q2, kk, NT_DIMS, preferred_element_type=F32) for kk in k_list]
        o = _pv_normalised(_exp_parts(s1, scale), v_list) - lam * _pv_normalised(_exp_parts(s2, scale), v_list)
        y = o * lax.rsqrt(jnp.mean(o * o, axis=-1, keepdims=True) + RMS_EPS) * ng
        o_ref[rows, sl] = (y * (1.0 - lam_init)).astype(BF16)


def _diff_attn_prompt(zodd, lam_p, norm_g, j, lam_init, prev):
    n_seq = 2
    rows = n_seq * SEQ
    cache_shape = jax.ShapeDtypeStruct((BATCH, N_ODD, DIFF_HEADS, SEQ, LANES), F32)
    cache_spec = pl.BlockSpec((n_seq, None, DIFF_HEADS, SEQ, LANES), lambda b: (b, j, 0, 0, 0))
    in_specs = [
        pl.BlockSpec((None, 4, DIFF_DH), lambda b: (j, 0, 0)),
        pl.BlockSpec((None, 1, DIFF_DV), lambda b: (j, 0, 0)),
        pl.BlockSpec((rows, 1024), lambda b: (b, 0)),
        pl.BlockSpec((rows, 1024), lambda b: (b, 1)),
        pl.BlockSpec((rows, 1024), lambda b: (b, 2)),
    ]
    args = [lam_p, norm_g.reshape(-1, 1, DIFF_DV), zodd, zodd, zodd]
    aliases = {}
    if prev is not None:
        aliases = {len(args): 1, len(args) + 1: 2}
        in_specs += [pl.BlockSpec(memory_space=pl.ANY)] * 2
        args += list(prev)
    return pl.pallas_call(
        functools.partial(_diff_attn_kernel, False, lam_init),
        grid=(BATCH // n_seq,),
        in_specs=in_specs,
        out_specs=[pl.BlockSpec((rows, 1024), lambda b: (b, 0)), cache_spec, cache_spec],
        out_shape=[jax.ShapeDtypeStruct((NTOK_P, 1024), BF16), cache_shape, cache_shape],
        input_output_aliases=aliases,
        compiler_params=_params(("arbitrary",)),
        name="diff_attn_prompt",
    )(*args)


def _diff_attn_latent(zodd, lam_p, norm_g, cos_t, sin_t, cache_k, cache_v, k_rot, v_aug, j, lam_init):
    nq = DEC_SEQ // ATT_TQ
    row0 = NTOK_P // ATT_TQ
    ctx_spec = pl.BlockSpec((None, None, DIFF_HEADS, PAST_LEN, LANES), lambda b, q: (b, j, 0, 0, 0))
    return pl.pallas_call(
        functools.partial(_diff_attn_kernel, True, lam_init),
        grid=(DEC_BATCH, nq),
        in_specs=[
            pl.BlockSpec((None, 4, DIFF_DH), lambda b, q: (j, 0, 0)),
            pl.BlockSpec((None, 1, DIFF_DV), lambda b, q: (j, 0, 0)),
            pl.BlockSpec((ATT_TQ, 1024), lambda b, q: (row0 + b * nq + q, 0)),
            pl.BlockSpec((ATT_TQ, LANES), lambda b, q: (q, 0)),
            pl.BlockSpec((ATT_TQ, LANES), lambda b, q: (q, 0)),
            ctx_spec,
            ctx_spec,
            pl.BlockSpec((DEC_SEQ, 1024), lambda b, q: (b, 0), pipeline_mode=pl.Buffered(1)),
            pl.BlockSpec((DEC_SEQ, 2048), lambda b, q: (b, 0), pipeline_mode=pl.Buffered(1)),
        ],
        out_specs=pl.BlockSpec((ATT_TQ, 1024), lambda b, q: (b * nq + q, 0)),
        out_shape=jax.ShapeDtypeStruct((NTOK_S, 1024), BF16),
        compiler_params=_params(("parallel", "arbitrary")),
        name="diff_attn_latent",
    )(lam_p, norm_g.reshape(-1, 1, DIFF_DV), zodd, cos_t, sin_t, cache_k, cache_v, k_rot, v_aug)


def _mla_weight_t(w_in_t, j):
    base = RET_COLS
    mq = w_in_t[j, base:base + MLA_HEADS * (MLA_NOPE + MLA_ROPE)].reshape(MLA_HEADS, MLA_NOPE + MLA_ROPE, D_MODEL)
    qn = mq[:, :MLA_NOPE].reshape(MLA_HEADS * MLA_NOPE, D_MODEL)
    qr = mq[:, MLA_NOPE:].reshape(MLA_HEADS * MLA_ROPE, D_MODEL)
    ckv0 = base + MLA_HEADS * (MLA_NOPE + MLA_ROPE)
    ckv = w_in_t[j, ckv0:ckv0 + MLA_KV_RANK]
    kr = w_in_t[j, ckv0 + MLA_KV_RANK:]
    return jnp.concatenate([qn, qr, ckv, jnp.tile(kr, (LANES // MLA_ROPE, 1))], axis=0)


def kernel(x_prompt, x_sample, state_ret_fwd, state_ret_bwd, cache_mla_ckv, cache_mla_krope, cache_diff_k, cache_diff_v, c, c_ctx, ada_w, ada_b, ln1_g, ln1_b, ln2_g, ln2_b, ev_w_in, ev_w_out, ret_decay_fwd, ret_decay_bwd, mla_kv_norm_g, mla_w_uk, mla_w_uv, od_w_in, od_w_out, diff_lambda, diff_norm_g, moe_w_group, moe_b_group, moe_w_expert, moe_b_expert, moe_w1, moe_w3, moe_w2):
    x = (x_prompt.reshape(NTOK_P, D_MODEL), x_sample.reshape(NTOK_S, D_MODEL), 0)
    cond =jnp.concatenate([c_ctx[None, :], c, jnp.zeros((N_COND - 1 - DEC_BATCH, D_MODEL), F32)], axis=0)
    mods = _ada_all(cond, ada_w, ada_b).reshape(DEPTH, N_COND, 6, D_MODEL)

    cos_m, sin_m = _rope_tables(MLA_ROPE)
    cos_d, sin_d = _rope_tables(DIFF_DH)
    ident = 512
    cos_m_id = jnp.concatenate([jnp.ones((ident, LANES), F32), cos_m], axis=0)
    sin_m_id = jnp.concatenate([jnp.zeros((ident, LANES), F32), sin_m], axis=0)
    decf = jnp.broadcast_to(ret_decay_fwd[:, :, None], ret_decay_fwd.shape + (LANES,))
    decb = jnp.broadcast_to(ret_decay_bwd[:, :, None], ret_decay_bwd.shape + (LANES,))

    ev_w_in_t = jnp.swapaxes(ev_w_in, 1, 2)
    pad = LANES - MOE_GROUPS - MOE_EXPERTS
    ret_states = mla_caches = diff_caches = None
    for i in range(DEPTH):
        j = i // 2
        mod = mods[i]
        w_router = jnp.concatenate([moe_w_group[i], moe_w_expert[i], jnp.zeros((D_MODEL, pad), F32)], axis=1)
        b_router = jnp.concatenate([moe_b_group[i], moe_b_expert[i], jnp.zeros((pad,), F32)])[None, :]
        if i % 2 == 0:
            zret = _mm_mod(x, mod, ev_w_in_t, (j,), RET_COLS, 1536, "in_proj_retention", w_is_transposed=True)
            zmla = _mm_mod(x, mod, _mla_weight_t(ev_w_in_t, j), (), MLA_COLS, MLA_COLS, "in_proj_mla",
                           w_is_transposed=True)
            *mla_caches, kcat, vm = _mla_kv_new(zmla, cos_m_id, sin_m_id, mla_kv_norm_g, mla_w_uk, mla_w_uv,
                                                j, mla_caches)
            kr_ctx = jnp.tile(cache_mla_krope[:, j], (1, 1, LANES // MLA_ROPE))
            kcat_ctx, vm_ctx = _mla_kv_ctx(cache_mla_ckv, kr_ctx, mla_w_uk, mla_w_uv, j)
            a_ret_p, *ret_states = _retention(zret, decf, decb, j, False, prev=ret_states)
            (a_ret_s,) = _retention(zret, decf, decb, j, True, state_ret_fwd, state_ret_bwd)
            a_mla_p = _mla_attn_prompt(zmla, kcat, vm)
            a_mla_s = _mla_attn_latent(zmla, cos_m, sin_m, kcat_ctx, vm_ctx, kcat, vm)
            x1, meta, counts = _mm_ln([(a_ret_p, a_ret_s), (a_mla_p, a_mla_s)], ev_w_out, j, x, mod,
                                      ln1_g, ln1_b, w_router, b_router, i)
        else:
            lam_init = 0.8 - 0.6 * math.exp(-0.3 * i)
            zodd = _mm_mod(x, mod, od_w_in, (j,), 3072, 1536, "in_proj_diff")
            a_p, *diff_caches = _diff_attn_prompt(zodd, diff_lambda, diff_norm_g, j, lam_init, diff_caches)
            k_rot, v_aug = _diff_prep(zodd, cos_d, sin_d)
            a_s = _diff_attn_latent(zodd, diff_lambda, diff_norm_g, cos_d, sin_d, cache_diff_k, cache_diff_v,
                                    k_rot, v_aug, j, lam_init)
            x1, meta, counts = _mm_ln([(a_p, a_s)], od_w_out, j, x, mod, ln1_g, ln1_b, w_router, b_router, i)
        y = _moe(x1, _moe_plan(meta, counts), mods, w_router, b_router, moe_w1, moe_w3, moe_w2, ln2_g, ln2_b, i)
        x = (y, y, NTOK_P)

    y_prompt = x[0][:NTOK_P].reshape(BATCH, SEQ, D_MODEL)
    y_sample = x[1][NTOK_P:NTOK].reshape(DEC_BATCH, DEC_SEQ, D_MODEL)
    return (y_prompt, y_sample, ret_states[0], ret_states[1], mla_caches[0], mla_caches[1],
            diff_caches[0], diff_caches[1])
```

```python
import functools
import math

import jax
import jax.numpy as jnp
from jax import lax
from jax.experimental import pallas as pl
from jax.experimental.pallas import tpu as pltpu

D_MODEL = 1024
BATCH = 32
SEQ = 256
DEPTH = 4
N_EVEN = 2
N_ODD = 2
DEC_BATCH = 2
DEC_SEQ = 2048
PAST_LEN = 256
GRID_W = 64
LN_EPS = 1e-5
RMS_EPS = 1e-6
DEEPNORM_ALPHA = (2.0 * DEPTH) ** 0.25
ROPE_BASE = 10000.0
RET_HEADS = 8
RET_DK = 64
RET_DV = 128
MLA_HEADS = 8
MLA_NOPE = 64
MLA_ROPE = 32
MLA_DV = 64
MLA_KV_RANK = 256
DIFF_HEADS = 8
DIFF_DH = 64
DIFF_DV = 128
MOE_GROUPS = 4
MOE_PER_GROUP = 4
MOE_EXPERTS = 16
MOE_FF = 256

NTOK_P = BATCH * SEQ
NTOK_S = DEC_BATCH * DEC_SEQ
NTOK = NTOK_P + NTOK_S
N_COND = 8
LANES = 128
SUBLANES = 8
RET_COLS = 3072
MLA_COLS = 1152
MLA_PROJ_TM = 1024
ATT_TQ = 512
RET_CHUNK = 256
MOE_TILE = 512
MOE_TILES = (NTOK + MOE_GROUPS * (MOE_TILE - 1)) // MOE_TILE
MOE_ROWS = MOE_TILES * MOE_TILE
MOE_KEY_BASE = 16384
VMEM_LIMIT = 56 * 1024 * 1024
LOG2E = 1.4426950408889634

F32 = jnp.float32
BF16 = jnp.bfloat16
NT_DIMS = (((1,), (1,)), ((), ()))
TN_DIMS = (((0,), (0,)), ((), ()))


def _params(sem):
    return pltpu.CompilerParams(dimension_semantics=sem, vmem_limit_bytes=VMEM_LIMIT)


def _group_of_tile(i, tm):
    npt = NTOK_P // tm
    nst = DEC_SEQ // tm
    return jnp.where(i < npt, 0, 1 + (i - npt) // nst)


def _split_specs(tm, width, m_of, s_row0=0):
    npt = NTOK_P // tm
    s_blk0 = s_row0 // tm
    return [pl.BlockSpec((tm, width), lambda *g: (jnp.minimum(m_of(*g), npt - 1), 0)),
            pl.BlockSpec((tm, width), lambda *g: (jnp.maximum(m_of(*g) - npt, 0) + s_blk0, 0))]


def _read_split(p_ref, s_ref, m):
    return jnp.where(m < NTOK_P // p_ref.shape[0], p_ref[...], s_ref[...])


def _silu(x):
    return x * (1.0 / (1.0 + jnp.exp(-x)))


def _layer_norm(r, g, b):
    mu = jnp.mean(r, axis=-1, keepdims=True)
    d = r - mu
    var = jnp.mean(d * d, axis=-1, keepdims=True)
    return d * lax.rsqrt(var + LN_EPS) * g + b


def _lane_iota(shape):
    return lax.broadcasted_iota(jnp.int32, shape, 1)


def _div_pow2(x, d):
    assert d & (d - 1) == 0
    return jnp.right_shift(x, d.bit_length() - 1)


def _mod_pow2(x, d):
    assert d & (d - 1) == 0
    return jnp.bitwise_and(x, d - 1)


def _ada_kernel(c_ref, w_ref, b_ref, o_ref):
    h = _silu(c_ref[...]).astype(BF16)
    o_ref[...] = jnp.dot(h, w_ref[...].astype(BF16), preferred_element_type=F32) + b_ref[...]


def _ada_all(cond, ada_w, ada_b):
    tn = 768
    return pl.pallas_call(
        _ada_kernel,
        grid=(DEPTH, 6 * D_MODEL // tn),
        in_specs=[
            pl.BlockSpec((N_COND, D_MODEL), lambda l, n: (0, 0)),
            pl.BlockSpec((None, D_MODEL, tn), lambda l, n: (l, 0, n)),
            pl.BlockSpec((None, 1, tn), lambda l, n: (l, 0, n)),
        ],
        out_specs=pl.BlockSpec((None, N_COND, tn), lambda l, n: (l, 0, n)),
        out_shape=jax.ShapeDtypeStruct((DEPTH, N_COND, 6 * D_MODEL), F32),
        compiler_params=_params(("parallel", "parallel")),
        name="ada_modulation",
    )(cond, ada_w, ada_b.reshape(DEPTH, 1, 6 * D_MODEL))


def _mm_mod_kernel(w_is_transposed, xp_ref, xs_ref, mod_ref, w_ref, o_ref, wscr):
    m = pl.program_id(1)

    @pl.when(m == 0)
    def _():
        wscr[...] = w_ref[...].astype(BF16)

    sh = mod_ref[0:1, :]
    sc = mod_ref[1:2, :]
    xm = (_read_split(xp_ref, xs_ref, m) * (1.0 + sc) + sh).astype(BF16)
    if w_is_transposed:
        z = lax.dot_general(xm, wscr[...], NT_DIMS, preferred_element_type=F32)
    else:
        z = jnp.dot(xm, wscr[...], preferred_element_type=F32)
    o_ref[...] = z.astype(o_ref.dtype)


def _mm_mod(x, mod, w, w_index, n_cols, tn, name, w_is_transposed=False):
    tm = 1024
    if w_is_transposed:
        w_spec = pl.BlockSpec((None,) * len(w_index) + (tn, D_MODEL), lambda n, m: tuple(w_index) + (n, 0))
        w_scratch = pltpu.VMEM((tn, D_MODEL), BF16)
    else:
        w_spec = pl.BlockSpec((None,) * len(w_index) + (D_MODEL, tn), lambda n, m: tuple(w_index) + (0, n))
        w_scratch = pltpu.VMEM((D_MODEL, tn), BF16)
    return pl.pallas_call(
        functools.partial(_mm_mod_kernel, w_is_transposed),
        grid=(n_cols // tn, NTOK // tm),
        in_specs=_split_specs(tm, D_MODEL, lambda n, m: m, x[2]) + [
            pl.BlockSpec((None, 6, D_MODEL), lambda n, m: (_group_of_tile(m, tm), 0, 0)),
            w_spec,
        ],
        out_specs=pl.BlockSpec((tm, tn), lambda n, m: (m, n)),
        out_shape=jax.ShapeDtypeStruct((NTOK, n_cols), BF16),
        scratch_shapes=[w_scratch],
        compiler_params=_params(("arbitrary", "arbitrary")),
        name=name,
    )(x[0], x[1], mod, w)


def _router_probs(xm, wr_ref, br_ref):
    rows = xm.shape[0]
    z = jnp.dot(xm, wr_ref[...].astype(BF16), preferred_element_type=F32) + br_ref[...]
    lane_i = _lane_iota((rows, LANES))
    lane = lane_i.astype(F32)
    gmask = lane_i < MOE_GROUPS
    zg = jnp.where(gmask, z, -jnp.inf)
    pg = jnp.exp(zg - jnp.max(zg, axis=-1, keepdims=True))
    g_prob = pg / jnp.sum(pg, axis=-1, keepdims=True)
    g_p = jnp.max(g_prob, axis=-1, keepdims=True)
    g_idx = jnp.min(jnp.where(gmask & (g_prob == g_p), lane, float(LANES)), axis=-1, keepdims=True)
    return z, lane_i, lane, g_p, g_idx


def _mm_ln_kernel(k_sizes, *refs):
    n_a = len(k_sizes)
    a_refs = refs[:2 * n_a]
    (w_ref, xp_ref, xs_ref, mod_ref, g_ref, b_ref, wr_ref, br_ref,
     o_ref, meta_ref, cnt_ref, wscr, tri_scr, carry_scr) = refs[2 * n_a:]
    m = pl.program_id(0)
    tm = o_ref.shape[0]

    @pl.when(m == 0)
    def _():
        wscr[...] = w_ref[...].astype(BF16)
        ri = lax.broadcasted_iota(jnp.int32, (tm, tm), 0)
        ci = lax.broadcasted_iota(jnp.int32, (tm, tm), 1)
        tri_scr[...] = jnp.where(ci < ri, 1.0, 0.0).astype(BF16)
        carry_scr[...] = jnp.zeros_like(carry_scr)

    y = None
    k0 = 0
    for i, ks in enumerate(k_sizes):
        a = _read_split(a_refs[2 * i], a_refs[2 * i + 1], m)
        part = jnp.dot(a, wscr[k0:k0 + ks, :], preferred_element_type=F32)
        y = part if y is None else y + part
        k0 += ks
    gate = mod_ref[2:3, :]
    r = DEEPNORM_ALPHA * _read_split(xp_ref, xs_ref, m) + gate * y
    x1 = _layer_norm(r, g_ref[...], b_ref[...])
    o_ref[...] = x1

    xm = (x1 * (1.0 + mod_ref[4:5, :]) + mod_ref[3:4, :]).astype(BF16)
    _, lane_i, lane, _, g_idx = _router_probs(xm, wr_ref, br_ref)
    onehot = jnp.where(lane == g_idx, 1.0, 0.0)
    before = jnp.dot(tri_scr[...], onehot.astype(BF16), preferred_element_type=F32) + carry_scr[0:1, :]
    rank = jnp.sum(jnp.where(lane == g_idx, before, 0.0), axis=-1, keepdims=True)
    key_col = g_idx * float(MOE_KEY_BASE) + rank
    row_i = lax.broadcasted_iota(jnp.int32, (tm, LANES), 0)
    diag = jnp.where(lane_i == _mod_pow2(row_i, LANES), key_col, 0.0)
    meta_ref[...] = jnp.sum(diag.reshape(tm // LANES, LANES, LANES), axis=1).astype(jnp.int32)
    total = carry_scr[0:1, :] + jnp.sum(onehot, axis=0, keepdims=True)
    carry_scr[...] = jnp.broadcast_to(total, carry_scr.shape)
    cnt_ref[...] = jnp.broadcast_to(total, cnt_ref.shape)


def _mm_ln(a_pairs, w, j, x, mod, ln_g, ln_b, w_router, b_router, layer):
    tm = 512
    k_sizes = tuple(ap.shape[1] for ap, _ in a_pairs)
    k_tot = sum(k_sizes)
    in_specs = []
    args = []
    for (ap, a_s), ks in zip(a_pairs, k_sizes):
        in_specs += _split_specs(tm, ks, lambda m: m)
        args += [ap, a_s]
    in_specs += [pl.BlockSpec((None, k_tot, D_MODEL), lambda m: (j, 0, 0))]
    in_specs += _split_specs(tm, D_MODEL, lambda m: m, x[2])
    in_specs += [
        pl.BlockSpec((None, 6, D_MODEL), lambda m: (_group_of_tile(m, tm), 0, 0)),
        pl.BlockSpec((None, 1, D_MODEL), lambda m: (layer, 0, 0)),
        pl.BlockSpec((None, 1, D_MODEL), lambda m: (layer, 0, 0)),
        pl.BlockSpec((D_MODEL, LANES), lambda m: (0, 0)),
        pl.BlockSpec((1, LANES), lambda m: (0, 0)),
    ]
    return pl.pallas_call(
        functools.partial(_mm_ln_kernel, k_sizes),
        grid=(NTOK // tm,),
        in_specs=in_specs,
        out_specs=[
            pl.BlockSpec((tm, D_MODEL), lambda m: (m, 0)),
            pl.BlockSpec((None, tm // LANES, LANES), lambda m: (m, 0, 0)),
            pl.BlockSpec((N_COND, LANES), lambda m: (0, 0)),
        ],
        out_shape=[
            jax.ShapeDtypeStruct((NTOK, D_MODEL), F32),
            jax.ShapeDtypeStruct((NTOK // tm, tm // LANES, LANES), jnp.int32),
            jax.ShapeDtypeStruct((N_COND, LANES), F32),
        ],
        scratch_shapes=[
            pltpu.VMEM((k_tot, D_MODEL), BF16),
            pltpu.VMEM((tm, tm), BF16),
            pltpu.VMEM((N_COND, LANES), F32),
        ],
        compiler_params=_params(("arbitrary",)),
        name="out_proj_ln",
    )(*args, w, x[0], x[1], mod, ln_g.reshape(DEPTH, 1, D_MODEL), ln_b.reshape(DEPTH, 1, D_MODEL),
      w_router, b_router)


def _moe_kernel(tgrp_ref, ntile_ref, src0_ref, src1_ref, dst_ref, mid_ref, x_hbm, mod_ref, wr_ref, br_ref,
                w1_ref, w3_ref, w2_ref, g_ref, b_ref, y_hbm,
                gbuf, obuf, gsem, ssem, w13s, w2s):
    i = pl.program_id(0)
    n_steps = pl.num_programs(0)
    n_tiles = ntile_ref[0]
    n_blk = gbuf.shape[1]
    ts = n_blk * SUBLANES
    slot = lax.rem(i, 2)

    def start_gather(src_ref, s):
        def body(k, carry):
            for u in range(SUBLANES):
                tok = src_ref[0, k * SUBLANES + u]
                pltpu.make_async_copy(x_hbm.at[pl.ds(tok, 1), :], gbuf.at[s, k, pl.ds(u, 1), :],
                                      gsem.at[s]).start()
            return carry
        lax.fori_loop(0, n_blk, body, 0)

    def wait_gather(s):
        pltpu.make_async_copy(gbuf.at[s], gbuf.at[s], gsem.at[s]).wait()

    def start_scatter(s):
        def body(k, carry):
            for u in range(SUBLANES):
                tok = dst_ref[0, k * SUBLANES + u]
                pltpu.make_async_copy(obuf.at[s, k, pl.ds(u, 1), :], y_hbm.at[pl.ds(tok, 1), :],
                                      ssem.at[s]).start()
            return carry
        lax.fori_loop(0, n_blk, body, 0)

    def wait_scatter(s):
        pltpu.make_async_copy(obuf.at[s], obuf.at[s], ssem.at[s]).wait()

    @pl.when(i == 0)
    def _():
        start_gather(src0_ref, 0)

    @pl.when(i < n_tiles)
    def _():
        grp = tgrp_ref[i]
        wait_gather(slot)

        @pl.when(i + 1 < n_tiles)
        def _():
            start_gather(src1_ref, 1 - slot)

        @pl.when((i == 0) | (grp != tgrp_ref[jnp.maximum(i - 1, 0)]))
        def _():
            w13s[:, :, :MOE_FF] = w1_ref[...].astype(BF16)
            w13s[:, :, MOE_FF:] = w3_ref[...].astype(BF16)
            w2s[...] = w2_ref[...].astype(BF16)

        mid_rows = mid_ref[...].astype(F32)
        spread = jnp.concatenate([jnp.broadcast_to(mid_rows[a:a + 1, :], (LANES, LANES))
                                  for a in range(ts // LANES)], axis=0)
        row_i = lax.broadcasted_iota(jnp.int32, (ts, LANES), 0)
        mid = jnp.sum(jnp.where(_lane_iota((ts, LANES)) == _mod_pow2(row_i, LANES), spread, 0.0),
                      axis=-1, keepdims=True)

        def mod_row(k):
            return jnp.where(mid == 0, mod_ref[0, k:k + 1, :],
                             jnp.where(mid == 1, mod_ref[1, k:k + 1, :], mod_ref[2, k:k + 1, :]))

        x1 = gbuf[slot].reshape(ts, D_MODEL)
        xm = (x1 * (1.0 + mod_row(4)) + mod_row(3)).astype(BF16)
        z, lane_i, lane, g_p, _ = _router_probs(xm, wr_ref, br_ref)
        e0 = MOE_GROUPS + MOE_PER_GROUP * grp
        emask = (lane_i >= e0) & (lane_i < e0 + MOE_PER_GROUP)
        ze = jnp.where(emask, z, -jnp.inf)
        pe = jnp.exp(ze - jnp.max(ze, axis=-1, keepdims=True))
        e_prob = pe / jnp.sum(pe, axis=-1, keepdims=True)
        cand = jnp.where(emask, e_prob, -1.0)
        p1 = jnp.max(cand, axis=-1, keepdims=True)
        i1 = jnp.min(jnp.where(cand == p1, lane, float(LANES)), axis=-1, keepdims=True)
        cand2 = jnp.where(lane == i1, -1.0, cand)
        p2 = jnp.max(cand2, axis=-1, keepdims=True)
        i2 = jnp.min(jnp.where(cand2 == p2, lane, float(LANES)), axis=-1, keepdims=True)
        denom = p1 + p2
        comb = jnp.where(lane == i1, g_p * p1 / denom, 0.0) + jnp.where(lane == i2, g_p * p2 / denom, 0.0)
        y = None
        for e in range(MOE_PER_GROUP):
            c = jnp.sum(jnp.where(lane_i == e0 + e, comb, 0.0), axis=-1, keepdims=True)
            h = jnp.dot(xm, w13s[e], preferred_element_type=F32)
            hid = (_silu(h[:, :MOE_FF]) * h[:, MOE_FF:] * c).astype(BF16)
            part = jnp.dot(hid, w2s[e], preferred_element_type=F32)
            y = part if y is None else y + part
        r = DEEPNORM_ALPHA * x1 + mod_row(5) * y

        @pl.when(i >= 2)
        def _():
            wait_scatter(slot)

        obuf[slot] = _layer_norm(r, g_ref[...], b_ref[...]).reshape(n_blk, SUBLANES, D_MODEL)
        start_scatter(slot)

    @pl.when(i == n_steps - 1)
    def _():
        @pl.when(n_tiles >= 2)
        def _():
            wait_scatter(lax.rem(n_tiles, 2))

        @pl.when(n_tiles >= 1)
        def _():
            wait_scatter(lax.rem(n_tiles + 1, 2))


def _inverse_perm_kernel(pos_ref, out_ref):
    def clear(s, carry):
        out_ref[s] = 0
        return carry

    def place(t, carry):
        out_ref[pos_ref[t]] = t + 1
        return carry

    lax.fori_loop(0, out_ref.shape[0], clear, 0, unroll=8)
    lax.fori_loop(0, pos_ref.shape[0], place, 0, unroll=8)


def _inverse_perm(pos):
    return pl.pallas_call(
        _inverse_perm_kernel,
        in_specs=[pl.BlockSpec(memory_space=pltpu.SMEM)],
        out_specs=pl.BlockSpec(memory_space=pltpu.SMEM),
        out_shape=jax.ShapeDtypeStruct((MOE_ROWS,), jnp.int32),
        name="moe_inverse_perm",
    )(pos)


def _moe_plan(meta, counts):
    ts = MOE_TILE
    cnt = counts[0, :MOE_GROUPS].astype(jnp.int32)
    tiles_g = (cnt + ts - 1) // ts
    tile_end = jnp.cumsum(tiles_g)
    row0_g = (tile_end - tiles_g) * ts
    keys = meta.reshape(NTOK)
    gid = keys // MOE_KEY_BASE
    rank = keys % MOE_KEY_BASE
    pos = row0_g[gid] + rank
    tok1 = _inverse_perm(pos)
    rows = jnp.arange(MOE_ROWS, dtype=jnp.int32)
    src = jnp.maximum(tok1 - 1, 0)
    spare = NTOK + ((rows // ts) % 2) * ts + rows % ts
    dst = jnp.where(tok1 > 0, src, spare)
    mid = jnp.where(src < NTOK_P, 0, 1 + (src - NTOK_P) // DEC_SEQ)
    tile_group = jnp.minimum(jnp.sum(jnp.arange(MOE_TILES)[:, None] >= tile_end[None, :], axis=1),
                             MOE_GROUPS - 1).astype(jnp.int32)
    n_tiles = tile_end[-1:].astype(jnp.int32)
    return (tile_group, n_tiles, src.reshape(MOE_TILES, 1, ts), dst.reshape(MOE_TILES, 1, ts),
            mid.reshape(MOE_TILES, ts // LANES, LANES))


def _moe(x1, plan, mods, w_router, b_router, w1, w3, w2, ln_g, ln_b, layer):
    ts = MOE_TILE
    tile_group, n_tiles, src, dst, mid = plan
    grp_shape = (DEPTH, MOE_GROUPS, MOE_PER_GROUP)
    w_in_spec = pl.BlockSpec((None, None, MOE_PER_GROUP, D_MODEL, MOE_FF), lambda i, tg, nt: (layer, tg[i], 0, 0, 0))
    w_out_spec = pl.BlockSpec((None, None, MOE_PER_GROUP, MOE_FF, D_MODEL), lambda i, tg, nt: (layer, tg[i], 0, 0, 0))
    smem_tile = functools.partial(pl.BlockSpec, (None, 1, ts), memory_space=pltpu.SMEM)
    grid_spec = pltpu.PrefetchScalarGridSpec(
        num_scalar_prefetch=2,
        grid=(MOE_TILES,),
        in_specs=[
            smem_tile(lambda i, tg, nt: (i, 0, 0)),
            smem_tile(lambda i, tg, nt: (jnp.minimum(i + 1, MOE_TILES - 1), 0, 0)),
            smem_tile(lambda i, tg, nt: (i, 0, 0)),
            pl.BlockSpec((None, ts // LANES, LANES), lambda i, tg, nt: (i, 0, 0)),
            pl.BlockSpec(memory_space=pl.ANY),
            pl.BlockSpec((None, N_COND, 6, D_MODEL), lambda i, tg, nt: (layer, 0, 0, 0)),
            pl.BlockSpec((D_MODEL, LANES), lambda i, tg, nt: (0, 0)),
            pl.BlockSpec((1, LANES), lambda i, tg, nt: (0, 0)),
            w_in_spec,
            w_in_spec,
            w_out_spec,
            pl.BlockSpec((None, 1, D_MODEL), lambda i, tg, nt: (layer, 0, 0)),
            pl.BlockSpec((None, 1, D_MODEL), lambda i, tg, nt: (layer, 0, 0)),
        ],
        out_specs=pl.BlockSpec(memory_space=pl.ANY),
        scratch_shapes=[
            pltpu.VMEM((2, ts // SUBLANES, SUBLANES, D_MODEL), F32),
            pltpu.VMEM((2, ts // SUBLANES, SUBLANES, D_MODEL), F32),
            pltpu.SemaphoreType.DMA((2,)),
            pltpu.SemaphoreType.DMA((2,)),
            pltpu.VMEM((MOE_PER_GROUP, D_MODEL, 2 * MOE_FF), BF16),
            pltpu.VMEM((MOE_PER_GROUP, MOE_FF, D_MODEL), BF16),
        ],
    )
    return pl.pallas_call(
        _moe_kernel,
        grid_spec=grid_spec,
        out_shape=jax.ShapeDtypeStruct((NTOK + 2 * ts, D_MODEL), F32),
        compiler_params=_params(("arbitrary",)),
        name="hier_moe_ln",
    )(tile_group, n_tiles, src, src, dst, mid, x1, mods, w_router, b_router,
      w1.reshape(grp_shape + (D_MODEL, MOE_FF)), w3.reshape(grp_shape + (D_MODEL, MOE_FF)),
      w2.reshape(grp_shape + (MOE_FF, D_MODEL)),
      ln_g.reshape(DEPTH, 1, D_MODEL), ln_b.reshape(DEPTH, 1, D_MODEL))


def _swap_halves(x, lane, half):
    return jnp.where(_mod_pow2(lane, 2 * half) < half,
                     pltpu.roll(x, LANES - half, 1), pltpu.roll(x, half, 1))


def _rope128(x, cos, sin_signed, lane, half):
    return x * cos + _swap_halves(x, lane, half) * sin_signed


def _rope_tables(rot_dim):
    rows = DEC_SEQ // GRID_W
    row = jnp.repeat(jnp.arange(rows, dtype=F32), GRID_W)
    col = jnp.tile(jnp.arange(GRID_W, dtype=F32), rows)
    n_freq = rot_dim // 4
    inv_freq = ROPE_BASE ** (-jnp.arange(n_freq, dtype=F32) / n_freq)
    ang = jnp.concatenate([row[:, None] * inv_freq, col[:, None] * inv_freq], axis=-1)
    cos, sin = jnp.cos(ang), jnp.sin(ang)
    reps = LANES // rot_dim
    cos_full = jnp.tile(jnp.concatenate([cos, cos], axis=-1), (1, reps))
    sin_signed = jnp.tile(jnp.concatenate([-sin, sin], axis=-1), (1, reps))
    return cos_full, sin_signed


def _exp_parts(s_list, scale):
    m = None
    for s in s_list:
        sm = jnp.max(s, axis=-1, keepdims=True)
        m = sm if m is None else jnp.maximum(m, sm)
    return [jnp.exp2((s - m) * (scale * LOG2E)).astype(BF16) for s in s_list]


def _pv_normalised(p_list, v_list):
    o = None
    for p, v in zip(p_list, v_list):
        part = jnp.dot(p, v, preferred_element_type=F32)
        o = part if o is None else o + part
    return o[:, :LANES] / o[:, LANES:]


def _mla_kv_kernel(new_tokens, *refs):
    if new_tokens:
        (xp_ref, xs_ref, mod_ref, w_ref, cos_ref, sin_ref, g_ref, wuk_ref, wuv_ref) = refs[:9]
        ckvc_ref, krc_ref, zq_ref, kcat_ref, vm_ref, wscr, wuk_scr, wuv_scr = refs[-8:]
        i = pl.program_id(0)

        @pl.when(i == 0)
        def _():
            wscr[...] = w_ref[...].astype(BF16)
            wuk_scr[...] = wuk_ref[...].astype(BF16)
            wuv_scr[...] = wuv_ref[...].astype(BF16)

        xm = (_read_split(xp_ref, xs_ref, i) * (1.0 + mod_ref[1:2, :]) + mod_ref[0:1, :]).astype(BF16)
        z = lax.dot_general(xm, wscr[...], NT_DIMS, preferred_element_type=F32)
        n_q = MLA_HEADS * (MLA_NOPE + MLA_ROPE)
        zq_ref[...] = z[:, :n_q].astype(BF16)
        x = z[:, n_q:n_q + MLA_KV_RANK]
        c = x * lax.rsqrt(jnp.mean(x * x, axis=-1, keepdims=True) + RMS_EPS) * g_ref[...]
        kr_raw = z[:, n_q + MLA_KV_RANK:]

        @pl.when(i < NTOK_P // zq_ref.shape[0])
        def _():
            for b in range(ckvc_ref.shape[0]):
                ckvc_ref[b] = c[SEQ * b:SEQ * (b + 1), :]
                krc_ref[b] = kr_raw[SEQ * b:SEQ * (b + 1), :MLA_ROPE]

        lane = _lane_iota(kr_raw.shape)
        kr = _rope128(kr_raw, cos_ref[...], sin_ref[...], lane, MLA_ROPE // 2)
        wuk = wuk_scr[...]
        wuv = wuv_scr[...]
    else:
        ckv_ref, kr_ref, wuk_ref, wuv_ref, kcat_ref, vm_ref = refs
        c = ckv_ref[...]
        kr = kr_ref[...]
        wuk = wuk_ref[...].astype(BF16)
        wuv = wuv_ref[...].astype(BF16)
    cb = c.astype(BF16)
    kn = jnp.dot(cb, wuk, preferred_element_type=F32).astype(BF16)
    vv = jnp.dot(cb, wuv, preferred_element_type=F32).astype(BF16)
    krb = kr.astype(BF16)
    ones = jnp.ones((c.shape[0], LANES), BF16)
    for p in range(MLA_HEADS // 2):
        kcat_ref[:, 256 * p:256 * p + LANES] = kn[:, LANES * p:LANES * (p + 1)]
        kcat_ref[:, 256 * p + LANES:256 * (p + 1)] = krb
        vm_ref[:, 256 * p:256 * p + LANES] = vv[:, LANES * p:LANES * (p + 1)]
        vm_ref[:, 256 * p + LANES:256 * (p + 1)] = ones


def _mla_proj_kv(x, mod, w_t, cos_t, sin_t, kv_norm_g, w_uk, w_uv, j, prev):
    tm = MLA_PROJ_TM
    npt = NTOK_P // tm
    nst = DEC_SEQ // tm
    nb = tm // SEQ
    n_q = MLA_HEADS * (MLA_NOPE + MLA_ROPE)

    def tab(i):
        return (jnp.where(i < npt, 0, 1 + (i - npt) % nst), 0)

    def cache_idx(i):
        return (jnp.minimum(i, npt - 1), j, 0, 0)

    in_specs = _split_specs(tm, D_MODEL, lambda i: i, x[2]) + [
        pl.BlockSpec((None, 6, D_MODEL), lambda i: (_group_of_tile(i, tm), 0, 0)),
        pl.BlockSpec((MLA_COLS, D_MODEL), lambda i: (0, 0)),
        pl.BlockSpec((tm, LANES), tab),
        pl.BlockSpec((tm, LANES), tab),
        pl.BlockSpec((None, 1, MLA_KV_RANK), lambda i: (j, 0, 0)),
        pl.BlockSpec((None, MLA_KV_RANK, 512), lambda i: (j, 0, 0)),
        pl.BlockSpec((None, MLA_KV_RANK, 512), lambda i: (j, 0, 0)),
    ]
    args = [x[0], x[1], mod, w_t, cos_t, sin_t, kv_norm_g.reshape(-1, 1, MLA_KV_RANK), w_uk, w_uv]
    aliases = {}
    if prev is not None:
        aliases = {len(args): 0, len(args) + 1: 1}
        in_specs += [pl.BlockSpec(memory_space=pl.ANY)] * 2
        args += list(prev)
    return pl.pallas_call(
        functools.partial(_mla_kv_kernel, True),
        grid=(NTOK // tm,),
        in_specs=in_specs,
        out_specs=[
            pl.BlockSpec((nb, None, SEQ, MLA_KV_RANK), cache_idx),
            pl.BlockSpec((nb, None, SEQ, MLA_ROPE), cache_idx),
            pl.BlockSpec((tm, n_q), lambda i: (i, 0)),
            pl.BlockSpec((tm, 1024), lambda i: (i, 0)),
            pl.BlockSpec((tm, 1024), lambda i: (i, 0)),
        ],
        out_shape=[
            jax.ShapeDtypeStruct((BATCH, N_EVEN, SEQ, MLA_KV_RANK), F32),
            jax.ShapeDtypeStruct((BATCH, N_EVEN, SEQ, MLA_ROPE), F32),
            jax.ShapeDtypeStruct((NTOK, n_q), BF16),
            jax.ShapeDtypeStruct((NTOK, 1024), BF16),
            jax.ShapeDtypeStruct((NTOK, 1024), BF16),
        ],
        scratch_shapes=[
            pltpu.VMEM((MLA_COLS, D_MODEL), BF16),
            pltpu.VMEM((MLA_KV_RANK, 512), BF16),
            pltpu.VMEM((MLA_KV_RANK, 512), BF16),
        ],
        input_output_aliases=aliases,
        compiler_params=_params(("arbitrary",)),
        name="mla_proj_kv",
    )(*args)


def _mla_kv_ctx(cache_ckv, kr_tiled, w_uk, w_uv, j):
    return pl.pallas_call(
        functools.partial(_mla_kv_kernel, False),
        grid=(DEC_BATCH,),
        in_specs=[
            pl.BlockSpec((None, None, PAST_LEN, MLA_KV_RANK), lambda b: (b, j, 0, 0)),
            pl.BlockSpec((None, PAST_LEN, LANES), lambda b: (b, 0, 0)),
            pl.BlockSpec((None, MLA_KV_RANK, 512), lambda b: (j, 0, 0)),
            pl.BlockSpec((None, MLA_KV_RANK, 512), lambda b: (j, 0, 0)),
        ],
        out_specs=[
            pl.BlockSpec((PAST_LEN, 1024), lambda b: (b, 0)),
            pl.BlockSpec((PAST_LEN, 1024), lambda b: (b, 0)),
        ],
        out_shape=[
            jax.ShapeDtypeStruct((DEC_BATCH * PAST_LEN, 1024), BF16),
            jax.ShapeDtypeStruct((DEC_BATCH * PAST_LEN, 1024), BF16),
        ],
        compiler_params=_params(("parallel",)),
        name="mla_kv_ctx",
    )(cache_ckv, kr_tiled, w_uk, w_uv)


def _mla_attn_kernel(latent, *refs):
    if latent:
        qn_ref, qr_ref, cos_ref, sin_ref, kc_ref, vc_ref, kn_ref, vn_ref, o_ref = refs
        k_refs, v_refs = (kc_ref, kn_ref), (vc_ref, vn_ref)
    else:
        qn_ref, qr_ref, kn_ref, vn_ref, o_ref = refs
        k_refs, v_refs = (kn_ref,), (vn_ref,)
    tq = ATT_TQ if latent else SEQ
    n_seq = qn_ref.shape[0] // tq
    lane = _lane_iota((tq, LANES))
    scale = (MLA_NOPE + MLA_ROPE) ** -0.5
    for sq in range(n_seq):
        rows = slice(tq * sq, tq * (sq + 1))
        krows = slice(None) if latent else rows
        qr_cols = []
        for cidx in range(2):
            x = qr_ref[rows, LANES * cidx:LANES * (cidx + 1)].astype(F32)
            if latent:
                x = _rope128(x, cos_ref[...], sin_ref[...], lane, MLA_ROPE // 2)
            qr_cols.append(x)
        o_prev = None
        for h in range(MLA_HEADS):
            p, half = divmod(h, 2)
            cidx, slot = divmod(h, 4)
            qa = jnp.where(_div_pow2(lane, MLA_NOPE) == half,
                           qn_ref[rows, LANES * p:LANES * (p + 1)].astype(F32), 0.0)
            qb = jnp.where(_div_pow2(lane, MLA_ROPE) == slot, qr_cols[cidx], 0.0)
            qcat = jnp.concatenate([qa, qb], axis=1).astype(BF16)
            pair_cols = slice(256 * p, 256 * (p + 1))
            s_list = [lax.dot_general(qcat, k_ref[krows, pair_cols], NT_DIMS, preferred_element_type=F32)
                      for k_ref in k_refs]
            o = _pv_normalised(_exp_parts(s_list, scale), [v_ref[krows, pair_cols] for v_ref in v_refs])
            if half == 0:
                o_prev = o
            else:
                o_ref[rows, LANES * p:LANES * (p + 1)] = jnp.where(lane < MLA_DV, o_prev, o).astype(BF16)


def _mla_attn_prompt(zmla, kcat, vm):
    rows = 4 * SEQ
    return pl.pallas_call(
        functools.partial(_mla_attn_kernel, False),
        grid=(NTOK_P // rows,),
        in_specs=[
            pl.BlockSpec((rows, 512), lambda b: (b, 0)),
            pl.BlockSpec((rows, 256), lambda b: (b, 2)),
            pl.BlockSpec((rows, 1024), lambda b: (b, 0)),
            pl.BlockSpec((rows, 1024), lambda b: (b, 0)),
        ],
        out_specs=pl.BlockSpec((rows, 512), lambda b: (b, 0)),
        out_shape=jax.ShapeDtypeStruct((NTOK_P, 512), BF16),
        compiler_params=_params(("parallel",)),
        name="mla_attn_prompt",
    )(zmla, zmla, kcat, vm)


def _mla_attn_latent(zmla, cos_t, sin_t, kcat_ctx, vm_ctx, kcat, vm):
    nq = DEC_SEQ // ATT_TQ
    row0 = NTOK_P // ATT_TQ
    seq0 = NTOK_P // DEC_SEQ
    return pl.pallas_call(
        functools.partial(_mla_attn_kernel, True),
        grid=(DEC_BATCH, nq),
        in_specs=[
            pl.BlockSpec((ATT_TQ, 512), lambda b, q: (row0 + b * nq + q, 0)),
            pl.BlockSpec((ATT_TQ, 256), lambda b, q: (row0 + b * nq + q, 2)),
            pl.BlockSpec((ATT_TQ, LANES), lambda b, q: (q, 0)),
            pl.BlockSpec((ATT_TQ, LANES), lambda b, q: (q, 0)),
            pl.BlockSpec((PAST_LEN, 1024), lambda b, q: (b, 0)),
            pl.BlockSpec((PAST_LEN, 1024), lambda b, q: (b, 0)),
            pl.BlockSpec((DEC_SEQ, 1024), lambda b, q: (seq0 + b, 0)),
            pl.BlockSpec((DEC_SEQ, 1024), lambda b, q: (seq0 + b, 0)),
        ],
        out_specs=pl.BlockSpec((ATT_TQ, 512), lambda b, q: (b * nq + q, 0)),
        out_shape=jax.ShapeDtypeStruct((NTOK_S, 512), BF16),
        compiler_params=_params(("parallel", "arbitrary")),
        name="mla_attn_latent",
    )(zmla, zmla, cos_t, sin_t, kcat_ctx, vm_ctx, kcat, vm)


def _log_sigmoid(x):
    return jnp.minimum(x, 0.0) - jnp.log1p(jnp.exp(-jnp.abs(x)))


def _retention_kernel(seq_len, n_seq, has_init, emit_state, has_prev, *refs):
    refs = list(refs)
    decf_ref, decb_ref, q_ref, k_ref, v_ref, g_ref = refs[:6]
    refs = refs[6:]
    if has_init:
        sf0_ref, sb0_ref = refs[:2]
        refs = refs[2:]
    if has_prev:
        refs = refs[2:]
    o_ref = refs[0]
    refs = refs[1:]
    if emit_state:
        sf_ref, sb_ref = refs[:2]
        refs = refs[2:]
    of_scr, ob_scr, dec_scr, wts_scr = refs

    c = RET_CHUNK
    n_chunks = seq_len // c
    pair = pl.program_id(0)
    lane = _lane_iota((c, LANES))
    zeros_half = jnp.zeros((RET_DK, RET_DV), F32)

    def log_gammas(half):
        head = 2 * pair + half
        return (_log_sigmoid(decf_ref[pl.ds(head, 1), :]),
                _log_sigmoid(decb_ref[pl.ds(head, 1), :]))

    @pl.when(pl.program_id(1) == 0)
    def _():
        ri = lax.broadcasted_iota(jnp.int32, (c, c), 0)
        ci = lax.broadcasted_iota(jnp.int32, (c, c), 1)
        rel = (ri - ci).astype(F32)
        row = lax.broadcasted_iota(jnp.int32, (c, LANES), 0).astype(F32)
        for half in range(2):
            lgf, lgb = log_gammas(half)
            dec_scr[half] = (jnp.where(rel >= 0, jnp.exp(lgf[:, 0:1] * jnp.maximum(rel, 0.0)), 0.0)
                             + jnp.where(rel <= 0, jnp.exp(lgb[:, 0:1] * jnp.maximum(-rel, 0.0)), 0.0))
            wts_scr[half, 0] = jnp.exp(lgf * (row + 1.0))
            wts_scr[half, 1] = jnp.exp(lgf * (c - 1.0 - row))
            wts_scr[half, 2] = jnp.exp(lgb * (c - row))
            wts_scr[half, 3] = jnp.exp(lgb * row)

    cross = has_init or n_chunks > 1
    chains = [(sq, half) for sq in range(n_seq) for half in range(2)]
    chunk_decay = []
    for half in range(2):
        lgf, lgb = log_gammas(half)
        chunk_decay.append((jnp.exp(lgf * float(c)), jnp.exp(lgb * float(c))))

    def rows_of(sq, n):
        start = sq * seq_len + n * c
        return pl.ds(start if isinstance(n, int) else pl.multiple_of(start, c), c)

    def load(sq, half, n):
        rows = rows_of(sq, n)
        vsl = slice(RET_DV * half, RET_DV * (half + 1))
        qm = jnp.where(_div_pow2(lane, RET_DK) == half, q_ref[rows, :].astype(F32), 0.0)
        kk = k_ref[rows, :].astype(F32) * (RET_DK ** -0.5)
        return rows, vsl, qm, kk, v_ref[rows, vsl]

    def init_state(s0_ref, half):
        if not has_init:
            return jnp.zeros((LANES, RET_DV), F32)
        s0 = s0_ref[half]
        return jnp.concatenate([s0, zeros_half] if half == 0 else [zeros_half, s0], axis=0)

    def fwd_step(sq, half, n, s_f):
        rows, vsl, qm, kk, vb = load(sq, half, n)
        s = lax.dot_general(qm.astype(BF16), kk.astype(BF16), NT_DIMS, preferred_element_type=F32)
        o = jnp.dot((s * dec_scr[half]).astype(BF16), vb, preferred_element_type=F32)
        if cross:
            o = o + jnp.dot((qm * wts_scr[half, 0]).astype(BF16), s_f.astype(BF16), preferred_element_type=F32)
        of_scr[rows, vsl] = o
        kv = lax.dot_general((kk * wts_scr[half, 1]).astype(BF16), vb, TN_DIMS, preferred_element_type=F32)
        return chunk_decay[half][0] * s_f + kv

    def bwd_step(sq, half, n, s_b):
        rows, vsl, qm, kk, vb = load(sq, half, n)
        if cross:
            ob_scr[rows, vsl] = jnp.dot((qm * wts_scr[half, 2]).astype(BF16), s_b.astype(BF16),
                                        preferred_element_type=F32)
        kv = lax.dot_general((kk * wts_scr[half, 3]).astype(BF16), vb, TN_DIMS, preferred_element_type=F32)
        return chunk_decay[half][1] * s_b + kv

    def finish(sq, n):
        rows = rows_of(sq, n)
        o2 = of_scr[rows, :] + ob_scr[rows, :] if cross else of_scr[rows, :]
        for half in range(2):
            vsl = slice(RET_DV * half, RET_DV * (half + 1))
            o = o2[:, vsl]
            mu = jnp.mean(o, axis=-1, keepdims=True)
            d = o - mu
            var = jnp.mean(d * d, axis=-1, keepdims=True)
            o_ref[rows, vsl] = (_silu(g_ref[rows, vsl].astype(F32)) * (d * lax.rsqrt(var + LN_EPS))).astype(BF16)

    s_f = tuple(init_state(sf0_ref if has_init else None, half) for _, half in chains)
    s_b = tuple(init_state(sb0_ref if has_init else None, half) for _, half in chains)
    if n_chunks == 1:
        s_f = tuple(fwd_step(sq, half, 0, s) for (sq, half), s in zip(chains, s_f))
        s_b = tuple(bwd_step(sq, half, 0, s) for (sq, half), s in zip(chains, s_b))
        for sq in range(n_seq):
            finish(sq, 0)
    else:
        def scan_step(n, carry):
            sf, sb = carry
            sf = tuple(fwd_step(sq, half, n, s) for (sq, half), s in zip(chains, sf))
            sb = tuple(bwd_step(sq, half, n_chunks - 1 - n, s) for (sq, half), s in zip(chains, sb))
            return sf, sb

        s_f, s_b = lax.fori_loop(0, n_chunks, scan_step, (s_f, s_b))

        def finish_step(n, carry):
            for sq in range(n_seq):
                finish(sq, n)
            return carry

        lax.fori_loop(0, n_chunks, finish_step, 0)
    if emit_state:
        for (sq, half), sf, sb in zip(chains, s_f, s_b):
            sf_ref[sq, half] = sf[RET_DK * half:RET_DK * (half + 1), :]
            sb_ref[sq, half] = sb[RET_DK * half:RET_DK * (half + 1), :]


def _retention(zret, decf, decb, j, latent, state_f=None, state_b=None, prev=None):
    seq_len = DEC_SEQ if latent else SEQ
    n_seq = 1 if latent else 4
    n_b = (DEC_BATCH if latent else BATCH) // n_seq
    rows = n_seq * seq_len
    row0 = NTOK_P // rows if latent else 0
    n_pairs = RET_HEADS // 2
    in_specs = [
        pl.BlockSpec((None, RET_HEADS, LANES), lambda p, b: (j, 0, 0)),
        pl.BlockSpec((None, RET_HEADS, LANES), lambda p, b: (j, 0, 0)),
        pl.BlockSpec((rows, LANES), lambda p, b: (row0 + b, p)),
        pl.BlockSpec((rows, LANES), lambda p, b: (row0 + b, 4 + p)),
        pl.BlockSpec((rows, 256), lambda p, b: (row0 + b, 4 + p)),
        pl.BlockSpec((rows, 256), lambda p, b: (row0 + b, 8 + p)),
    ]
    args = [decf, decb, zret, zret, zret, zret]
    out_specs = [pl.BlockSpec((rows, 256), lambda p, b: (b, p))]
    out_shape = [jax.ShapeDtypeStruct((n_b * rows, RET_HEADS * RET_DV), BF16)]
    aliases = {}
    if latent:
        st_spec = pl.BlockSpec((None, None, 2, RET_DK, RET_DV), lambda p, b: (b, j, p, 0, 0))
        in_specs += [st_spec, st_spec]
        args += [state_f, state_b]
    else:
        st_spec = pl.BlockSpec((n_seq, None, 2, RET_DK, RET_DV), lambda p, b: (b, j, p, 0, 0))
        out_specs += [st_spec, st_spec]
        out_shape += [jax.ShapeDtypeStruct((BATCH, N_EVEN, RET_HEADS, RET_DK, RET_DV), F32)] * 2
        if prev is not None:
            aliases = {len(args): 1, len(args) + 1: 2}
            in_specs += [pl.BlockSpec(memory_space=pl.ANY)] * 2
            args += list(prev)
    return pl.pallas_call(
        functools.partial(_retention_kernel, seq_len, n_seq, latent, not latent, bool(aliases)),
        grid=(n_pairs, n_b),
        in_specs=in_specs,
        out_specs=out_specs,
        out_shape=out_shape,
        scratch_shapes=[
            pltpu.VMEM((rows, 2 * RET_DV), F32),
            pltpu.VMEM((rows, 2 * RET_DV), F32),
            pltpu.VMEM((2, RET_CHUNK, RET_CHUNK), F32),
            pltpu.VMEM((2, 4, RET_CHUNK, LANES), F32),
        ],
        input_output_aliases=aliases,
        compiler_params=_params(("arbitrary", "arbitrary")),
        name="retention_latent" if latent else "retention_prompt",
    )(*args)


def _diff_prep_kernel(k_ref, v_ref, cos_ref, sin_ref, kr_ref, va_ref):
    lane = _lane_iota(cos_ref.shape)
    cos = cos_ref[...]
    sin = sin_ref[...]
    ones = jnp.ones(cos_ref.shape, BF16)
    for h in range(DIFF_HEADS):
        sl = slice(LANES * h, LANES * (h + 1))
        kr_ref[:, sl] = _rope128(k_ref[:, sl].astype(F32), cos, sin, lane, DIFF_DH // 2).astype(BF16)
        va_ref[:, 256 * h:256 * h + LANES] = v_ref[:, sl].astype(BF16)
        va_ref[:, 256 * h + LANES:256 * (h + 1)] = ones


def _diff_prep(zodd, cos_t, sin_t):
    tm = 512
    row0 = NTOK_P // tm
    nst = DEC_SEQ // tm
    return pl.pallas_call(
        _diff_prep_kernel,
        grid=(NTOK_S // tm,),
        in_specs=[
            pl.BlockSpec((tm, 1024), lambda i: (row0 + i, 1)),
            pl.BlockSpec((tm, 1024), lambda i: (row0 + i, 2)),
            pl.BlockSpec((tm, LANES), lambda i: (i % nst, 0)),
            pl.BlockSpec((tm, LANES), lambda i: (i % nst, 0)),
        ],
        out_specs=[pl.BlockSpec((tm, 1024), lambda i: (i, 0)), pl.BlockSpec((tm, 2048), lambda i: (i, 0))],
        out_shape=[jax.ShapeDtypeStruct((NTOK_S, 1024), BF16), jax.ShapeDtypeStruct((NTOK_S, 2048), BF16)],
        compiler_params=_params(("parallel",)),
        name="diff_rope_keys",
    )(zodd, zodd, cos_t, sin_t)


def _diff_attn_kernel(latent, lam_init, *refs):
    if latent:
        (lam_ref, ng_ref, q_ref, cos_ref, sin_ref, kc_ref, vc_ref, kn_ref, vn_ref, o_ref) = refs
    else:
        lam_ref, ng_ref, q_ref, k_ref, v_ref = refs[:5]
        o_ref, kout_ref, vout_ref = refs[-3:]
    tq = ATT_TQ if latent else SEQ
    n_seq = q_ref.shape[0] // tq
    lane = _lane_iota((tq, LANES))
    scale = DIFF_DH ** -0.5
    lp = lam_ref[...]
    lam = (jnp.exp(jnp.sum(lp[0:1, :] * lp[1:2, :], axis=-1, keepdims=True))
           - jnp.exp(jnp.sum(lp[2:3, :] * lp[3:4, :], axis=-1, keepdims=True)) + lam_init)
    ng = ng_ref[...]
    ones = jnp.ones((PAST_LEN if latent else tq, LANES), BF16)
    for sq, h in [(sq, h) for sq in range(n_seq) for h in range(DIFF_HEADS)]:
        sl = slice(LANES * h, LANES * (h + 1))
        rows = slice(tq * sq, tq * (sq + 1))
        qh = q_ref[rows, sl].astype(F32)
        if latent:
            qh = _rope128(qh, cos_ref[...], sin_ref[...], lane, DIFF_DH // 2)
            k_list = [kc_ref[h].astype(BF16), kn_ref[:, sl]]
            v_list = [jnp.concatenate([vc_ref[h].astype(BF16), ones], axis=1),
                      vn_ref[:, 256 * h:256 * (h + 1)]]
        else:
            kh = k_ref[rows, sl]
            vh = v_ref[rows, sl]
            kout_ref[sq, h] = kh.astype(F32)
            vout_ref[sq, h] = vh.astype(F32)
            k_list = [kh.astype(BF16)]
            v_list = [jnp.concatenate([vh.astype(BF16), ones], axis=1)]
        q1 = jnp.where(lane < DIFF_DH, qh, 0.0).astype(BF16)
        q2 = jnp.where(lane >= DIFF_DH, qh, 0.0).astype(BF16)
        s1 = [lax.dot_general(q1, kk, NT_DIMS, preferred_element_type=F32) for kk in k_list]
        s2 = [lax.dot_general(q2, kk, NT_DIMS, preferred_element_type=F32) for kk in k_list]
        o = _pv_normalised(_exp_parts(s1, scale), v_list) - lam * _pv_normalised(_exp_parts(s2, scale), v_list)
        y = o * lax.rsqrt(jnp.mean(o * o, axis=-1, keepdims=True) + RMS_EPS) * ng
        o_ref[rows, sl] = (y * (1.0 - lam_init)).astype(BF16)


def _diff_attn_prompt(zodd, lam_p, norm_g, j, lam_init, prev):
    n_seq = 2
    rows = n_seq * SEQ
    cache_shape = jax.ShapeDtypeStruct((BATCH, N_ODD, DIFF_HEADS, SEQ, LANES), F32)
    cache_spec = pl.BlockSpec((n_seq, None, DIFF_HEADS, SEQ, LANES), lambda b: (b, j, 0, 0, 0))
    in_specs = [
        pl.BlockSpec((None, 4, DIFF_DH), lambda b: (j, 0, 0)),
        pl.BlockSpec((None, 1, DIFF_DV), lambda b: (j, 0, 0)),
        pl.BlockSpec((rows, 1024), lambda b: (b, 0)),
        pl.BlockSpec((rows, 1024), lambda b: (b, 1)),
        pl.BlockSpec((rows, 1024), lambda b: (b, 2)),
    ]
    args = [lam_p, norm_g.reshape(-1, 1, DIFF_DV), zodd, zodd, zodd]
    aliases = {}
    if prev is not None:
        aliases = {len(args): 1, len(args) + 1: 2}
        in_specs += [pl.BlockSpec(memory_space=pl.ANY)] * 2
        args += list(prev)
    return pl.pallas_call(
        functools.partial(_diff_attn_kernel, False, lam_init),
        grid=(BATCH // n_seq,),
        in_specs=in_specs,
        out_specs=[pl.BlockSpec((rows, 1024), lambda b: (b, 0)), cache_spec, cache_spec],
        out_shape=[jax.ShapeDtypeStruct((NTOK_P, 1024), BF16), cache_shape, cache_shape],
        input_output_aliases=aliases,
        compiler_params=_params(("arbitrary",)),
        name="diff_attn_prompt",
    )(*args)


def _diff_attn_latent(zodd, lam_p, norm_g, cos_t, sin_t, cache_k, cache_v, k_rot, v_aug, j, lam_init):
    nq = DEC_SEQ // ATT_TQ
    row0 = NTOK_P // ATT_TQ
    ctx_spec = pl.BlockSpec((None, None, DIFF_HEADS, PAST_LEN, LANES), lambda b, q: (b, j, 0, 0, 0))
    return pl.pallas_call(
        functools.partial(_diff_attn_kernel, True, lam_init),
        grid=(DEC_BATCH, nq),
        in_specs=[
            pl.BlockSpec((None, 4, DIFF_DH), lambda b, q: (j, 0, 0)),
            pl.BlockSpec((None, 1, DIFF_DV), lambda b, q: (j, 0, 0)),
            pl.BlockSpec((ATT_TQ, 1024), lambda b, q: (row0 + b * nq + q, 0)),
            pl.BlockSpec((ATT_TQ, LANES), lambda b, q: (q, 0)),
            pl.BlockSpec((ATT_TQ, LANES), lambda b, q: (q, 0)),
            ctx_spec,
            ctx_spec,
            pl.BlockSpec((DEC_SEQ, 1024), lambda b, q: (b, 0), pipeline_mode=pl.Buffered(1)),
            pl.BlockSpec((DEC_SEQ, 2048), lambda b, q: (b, 0), pipeline_mode=pl.Buffered(1)),
        ],
        out_specs=pl.BlockSpec((ATT_TQ, 1024), lambda b, q: (b * nq + q, 0)),
        out_shape=jax.ShapeDtypeStruct((NTOK_S, 1024), BF16),
        compiler_params=_params(("parallel", "arbitrary")),
        name="diff_attn_latent",
    )(lam_p, norm_g.reshape(-1, 1, DIFF_DV), zodd, cos_t, sin_t, cache_k, cache_v, k_rot, v_aug)


def _mla_weight_t(w_in_t, j):
    base = RET_COLS
    mq = w_in_t[j, base:base + MLA_HEADS * (MLA_NOPE + MLA_ROPE)].reshape(MLA_HEADS, MLA_NOPE + MLA_ROPE, D_MODEL)
    qn = mq[:, :MLA_NOPE].reshape(MLA_HEADS * MLA_NOPE, D_MODEL)
    qr = mq[:, MLA_NOPE:].reshape(MLA_HEADS * MLA_ROPE, D_MODEL)
    ckv0 = base + MLA_HEADS * (MLA_NOPE + MLA_ROPE)
    ckv = w_in_t[j, ckv0:ckv0 + MLA_KV_RANK]
    kr = w_in_t[j, ckv0 + MLA_KV_RANK:]
    return jnp.concatenate([qn, qr, ckv, jnp.tile(kr, (LANES // MLA_ROPE, 1))], axis=0)


def kernel(x_prompt, x_sample, state_ret_fwd, state_ret_bwd, cache_mla_ckv, cache_mla_krope, cache_diff_k, cache_diff_v, c, c_ctx, ada_w, ada_b, ln1_g, ln1_b, ln2_g, ln2_b, ev_w_in, ev_w_out, ret_decay_fwd, ret_decay_bwd, mla_kv_norm_g, mla_w_uk, mla_w_uv, od_w_in, od_w_out, diff_lambda, diff_norm_g, moe_w_group, moe_b_group, moe_w_expert, moe_b_expert, moe_w1, moe_w3, moe_w2):
    x = (x_prompt.reshape(NTOK_P, D_MODEL), x_sample.reshape(NTOK_S, D_MODEL), 0)
    cond =jnp.concatenate([c_ctx[None, :], c, jnp.zeros((N_COND - 1 - DEC_BATCH, D_MODEL), F32)], axis=0)
    mods = _ada_all(cond, ada_w, ada_b).reshape(DEPTH, N_COND, 6, D_MODEL)

    cos_m, sin_m = _rope_tables(MLA_ROPE)
    cos_d, sin_d = _rope_tables(DIFF_DH)
    ident = MLA_PROJ_TM
    cos_m_id = jnp.concatenate([jnp.ones((ident, LANES), F32), cos_m], axis=0)
    sin_m_id = jnp.concatenate([jnp.zeros((ident, LANES), F32), sin_m], axis=0)
    decf = jnp.broadcast_to(ret_decay_fwd[:, :, None], ret_decay_fwd.shape + (LANES,))
    decb = jnp.broadcast_to(ret_decay_bwd[:, :, None], ret_decay_bwd.shape + (LANES,))

    ev_w_in_t = jnp.swapaxes(ev_w_in, 1, 2)
    pad = LANES - MOE_GROUPS - MOE_EXPERTS
    ret_states = mla_caches = diff_caches = None
    for i in range(DEPTH):
        j = i // 2
        mod = mods[i]
        w_router = jnp.concatenate([moe_w_group[i], moe_w_expert[i], jnp.zeros((D_MODEL, pad), F32)], axis=1)
        b_router = jnp.concatenate([moe_b_group[i], moe_b_expert[i], jnp.zeros((pad,), F32)])[None, :]
        if i % 2 == 0:
            zret = _mm_mod(x, mod, ev_w_in_t, (j,), RET_COLS, 1536, "in_proj_retention", w_is_transposed=True)
            *mla_caches, zmla, kcat, vm = _mla_proj_kv(x, mod, _mla_weight_t(ev_w_in_t, j), cos_m_id, sin_m_id,
                                                       mla_kv_norm_g, mla_w_uk, mla_w_uv, j, mla_caches)
            kr_ctx = jnp.tile(cache_mla_krope[:, j], (1, 1, LANES // MLA_ROPE))
            kcat_ctx, vm_ctx = _mla_kv_ctx(cache_mla_ckv, kr_ctx, mla_w_uk, mla_w_uv, j)
            a_ret_p, *ret_states = _retention(zret, decf, decb, j, False, prev=ret_states)
            (a_ret_s,) = _retention(zret, decf, decb, j, True, state_ret_fwd, state_ret_bwd)
            a_mla_p = _mla_attn_prompt(zmla, kcat, vm)
            a_mla_s = _mla_attn_latent(zmla, cos_m, sin_m, kcat_ctx, vm_ctx, kcat, vm)
            x1, meta, counts = _mm_ln([(a_ret_p, a_ret_s), (a_mla_p, a_mla_s)], ev_w_out, j, x, mod,
                                      ln1_g, ln1_b, w_router, b_router, i)
        else:
            lam_init = 0.8 - 0.6 * math.exp(-0.3 * i)
            zodd = _mm_mod(x, mod, od_w_in, (j,), 3072, 1536, "in_proj_diff")
            a_p, *diff_caches = _diff_attn_prompt(zodd, diff_lambda, diff_norm_g, j, lam_init, diff_caches)
            k_rot, v_aug = _diff_prep(zodd, cos_d, sin_d)
            a_s = _diff_attn_latent(zodd, diff_lambda, diff_norm_g, cos_d, sin_d, cache_diff_k, cache_diff_v,
                                    k_rot, v_aug, j, lam_init)
            x1, meta, counts = _mm_ln([(a_p, a_s)], od_w_out, j, x, mod, ln1_g, ln1_b, w_router, b_router, i)
        y = _moe(x1, _moe_plan(meta, counts), mods, w_router, b_router, moe_w1, moe_w3, moe_w2, ln2_g, ln2_b, i)
        x = (y, y, NTOK_P)

    y_prompt = x[0][:NTOK_P].reshape(BATCH, SEQ, D_MODEL)
    y_sample = x[1][NTOK_P:NTOK].reshape(DEC_BATCH, DEC_SEQ, D_MODEL)
    return (y_prompt, y_sample, ret_states[0], ret_states[1], mla_caches[0], mla_caches[1],
            diff_caches[0], diff_caches[1])
```

```python
import functools
import math

import jax
import jax.numpy as jnp
from jax import lax
from jax.experimental import pallas as pl
from jax.experimental.pallas import tpu as pltpu

D_MODEL = 1024
BATCH = 32
SEQ = 256
DEPTH = 4
N_EVEN = 2
N_ODD = 2
DEC_BATCH = 2
DEC_SEQ = 2048
PAST_LEN = 256
GRID_W = 64
LN_EPS = 1e-5
RMS_EPS = 1e-6
DEEPNORM_ALPHA = (2.0 * DEPTH) ** 0.25
ROPE_BASE = 10000.0
RET_HEADS = 8
RET_DK = 64
RET_DV = 128
MLA_HEADS = 8
MLA_NOPE = 64
MLA_ROPE = 32
MLA_DV = 64
MLA_KV_RANK = 256
DIFF_HEADS = 8
DIFF_DH = 64
DIFF_DV = 128
MOE_GROUPS = 4
MOE_PER_GROUP = 4
MOE_EXPERTS = 16
MOE_FF = 256

NTOK_P = BATCH * SEQ
NTOK_S = DEC_BATCH * DEC_SEQ
NTOK = NTOK_P + NTOK_S
N_COND = 8
LANES = 128
SUBLANES = 8
RET_COLS = 3072
MLA_COLS = 1152
MLA_PROJ_TM = 1024
ATT_TQ = 512
RET_CHUNK = 256
MOE_TILE = 512
MOE_TILES = (NTOK + MOE_GROUPS * (MOE_TILE - 1)) // MOE_TILE
MOE_ROWS = MOE_TILES * MOE_TILE
MOE_KEY_BASE = 16384
VMEM_LIMIT = 56 * 1024 * 1024
LOG2E = 1.4426950408889634

F32 = jnp.float32
BF16 = jnp.bfloat16
NT_DIMS = (((1,), (1,)), ((), ()))
TN_DIMS = (((0,), (0,)), ((), ()))


def _params(sem):
    return pltpu.CompilerParams(dimension_semantics=sem, vmem_limit_bytes=VMEM_LIMIT)


def _group_of_tile(i, tm):
    npt = NTOK_P // tm
    nst = DEC_SEQ // tm
    return jnp.where(i < npt, 0, 1 + (i - npt) // nst)


def _split_specs(tm, width, m_of, s_row0=0):
    npt = NTOK_P // tm
    s_blk0 = s_row0 // tm
    return [pl.BlockSpec((tm, width), lambda *g: (jnp.minimum(m_of(*g), npt - 1), 0)),
            pl.BlockSpec((tm, width), lambda *g: (jnp.maximum(m_of(*g) - npt, 0) + s_blk0, 0))]


def _read_split(p_ref, s_ref, m):
    return jnp.where(m < NTOK_P // p_ref.shape[0], p_ref[...], s_ref[...])


def _silu(x):
    return x * (1.0 / (1.0 + jnp.exp(-x)))


def _layer_norm(r, g, b):
    mu = jnp.mean(r, axis=-1, keepdims=True)
    d = r - mu
    var = jnp.mean(d * d, axis=-1, keepdims=True)
    return d * lax.rsqrt(var + LN_EPS) * g + b


def _lane_iota(shape):
    return lax.broadcasted_iota(jnp.int32, shape, 1)


def _div_pow2(x, d):
    assert d & (d - 1) == 0
    return jnp.right_shift(x, d.bit_length() - 1)


def _mod_pow2(x, d):
    assert d & (d - 1) == 0
    return jnp.bitwise_and(x, d - 1)


def _ada_kernel(c_ref, w_ref, b_ref, o_ref):
    h = _silu(c_ref[...]).astype(BF16)
    o_ref[...] = jnp.dot(h, w_ref[...].astype(BF16), preferred_element_type=F32) + b_ref[...]


def _ada_all(cond, ada_w, ada_b):
    tn = 768
    return pl.pallas_call(
        _ada_kernel,
        grid=(DEPTH, 6 * D_MODEL // tn),
        in_specs=[
            pl.BlockSpec((N_COND, D_MODEL), lambda l, n: (0, 0)),
            pl.BlockSpec((None, D_MODEL, tn), lambda l, n: (l, 0, n)),
            pl.BlockSpec((None, 1, tn), lambda l, n: (l, 0, n)),
        ],
        out_specs=pl.BlockSpec((None, N_COND, tn), lambda l, n: (l, 0, n)),
        out_shape=jax.ShapeDtypeStruct((DEPTH, N_COND, 6 * D_MODEL), F32),
        compiler_params=_params(("parallel", "parallel")),
        name="ada_modulation",
    )(cond, ada_w, ada_b.reshape(DEPTH, 1, 6 * D_MODEL))


def _mm_mod_kernel(w_is_transposed, xp_ref, xs_ref, mod_ref, w_ref, o_ref, wscr):
    m = pl.program_id(1)

    @pl.when(m == 0)
    def _():
        wscr[...] = w_ref[...].astype(BF16)

    sh = mod_ref[0:1, :]
    sc = mod_ref[1:2, :]
    xm = (_read_split(xp_ref, xs_ref, m) * (1.0 + sc) + sh).astype(BF16)
    if w_is_transposed:
        z = lax.dot_general(xm, wscr[...], NT_DIMS, preferred_element_type=F32)
    else:
        z = jnp.dot(xm, wscr[...], preferred_element_type=F32)
    o_ref[...] = z.astype(o_ref.dtype)


def _mm_mod(x, mod, w, w_index, n_cols, tn, name, w_is_transposed=False):
    tm = 1024
    if w_is_transposed:
        w_spec = pl.BlockSpec((None,) * len(w_index) + (tn, D_MODEL), lambda n, m: tuple(w_index) + (n, 0))
        w_scratch = pltpu.VMEM((tn, D_MODEL), BF16)
    else:
        w_spec = pl.BlockSpec((None,) * len(w_index) + (D_MODEL, tn), lambda n, m: tuple(w_index) + (0, n))
        w_scratch = pltpu.VMEM((D_MODEL, tn), BF16)
    return pl.pallas_call(
        functools.partial(_mm_mod_kernel, w_is_transposed),
        grid=(n_cols // tn, NTOK // tm),
        in_specs=_split_specs(tm, D_MODEL, lambda n, m: m, x[2]) + [
            pl.BlockSpec((None, 6, D_MODEL), lambda n, m: (_group_of_tile(m, tm), 0, 0)),
            w_spec,
        ],
        out_specs=pl.BlockSpec((tm, tn), lambda n, m: (m, n)),
        out_shape=jax.ShapeDtypeStruct((NTOK, n_cols), BF16),
        scratch_shapes=[w_scratch],
        compiler_params=_params(("arbitrary", "arbitrary")),
        name=name,
    )(x[0], x[1], mod, w)


def _router_probs(xm, wr_ref, br_ref):
    rows = xm.shape[0]
    z = jnp.dot(xm, wr_ref[...].astype(BF16), preferred_element_type=F32) + br_ref[...]
    lane_i = _lane_iota((rows, LANES))
    lane = lane_i.astype(F32)
    gmask = lane_i < MOE_GROUPS
    zg = jnp.where(gmask, z, -jnp.inf)
    pg = jnp.exp(zg - jnp.max(zg, axis=-1, keepdims=True))
    g_prob = pg / jnp.sum(pg, axis=-1, keepdims=True)
    g_p = jnp.max(g_prob, axis=-1, keepdims=True)
    g_idx = jnp.min(jnp.where(gmask & (g_prob == g_p), lane, float(LANES)), axis=-1, keepdims=True)
    return z, lane_i, lane, g_p, g_idx


def _mm_ln_kernel(k_sizes, *refs):
    n_a = len(k_sizes)
    a_refs = refs[:2 * n_a]
    (w_ref, xp_ref, xs_ref, mod_ref, g_ref, b_ref, wr_ref, br_ref,
     o_ref, meta_ref, cnt_ref, wscr, tri_scr, carry_scr) = refs[2 * n_a:]
    m = pl.program_id(0)
    tm = o_ref.shape[0]

    @pl.when(m == 0)
    def _():
        wscr[...] = w_ref[...].astype(BF16)
        ri = lax.broadcasted_iota(jnp.int32, (tm, tm), 0)
        ci = lax.broadcasted_iota(jnp.int32, (tm, tm), 1)
        tri_scr[...] = jnp.where(ci < ri, 1.0, 0.0).astype(BF16)
        carry_scr[...] = jnp.zeros_like(carry_scr)

    y = None
    k0 = 0
    for i, ks in enumerate(k_sizes):
        a = _read_split(a_refs[2 * i], a_refs[2 * i + 1], m)
        part = jnp.dot(a, wscr[k0:k0 + ks, :], preferred_element_type=F32)
        y = part if y is None else y + part
        k0 += ks
    gate = mod_ref[2:3, :]
    r = DEEPNORM_ALPHA * _read_split(xp_ref, xs_ref, m) + gate * y
    x1 = _layer_norm(r, g_ref[...], b_ref[...])
    o_ref[...] = x1

    xm = (x1 * (1.0 + mod_ref[4:5, :]) + mod_ref[3:4, :]).astype(BF16)
    _, lane_i, lane, _, g_idx = _router_probs(xm, wr_ref, br_ref)
    onehot = jnp.where(lane == g_idx, 1.0, 0.0)
    before = jnp.dot(tri_scr[...], onehot.astype(BF16), preferred_element_type=F32) + carry_scr[0:1, :]
    rank = jnp.sum(jnp.where(lane == g_idx, before, 0.0), axis=-1, keepdims=True)
    key_col = g_idx * float(MOE_KEY_BASE) + rank
    row_i = lax.broadcasted_iota(jnp.int32, (tm, LANES), 0)
    diag = jnp.where(lane_i == _mod_pow2(row_i, LANES), key_col, 0.0)
    meta_ref[...] = jnp.sum(diag.reshape(tm // LANES, LANES, LANES), axis=1).astype(jnp.int32)
    total = carry_scr[0:1, :] + jnp.sum(onehot, axis=0, keepdims=True)
    carry_scr[...] = jnp.broadcast_to(total, carry_scr.shape)
    cnt_ref[...] = jnp.broadcast_to(total, cnt_ref.shape)


def _mm_ln(a_pairs, w, j, x, mod, ln_g, ln_b, w_router, b_router, layer):
    tm = 512
    k_sizes = tuple(ap.shape[1] for ap, _ in a_pairs)
    k_tot = sum(k_sizes)
    in_specs = []
    args = []
    for (ap, a_s), ks in zip(a_pairs, k_sizes):
        in_specs += _split_specs(tm, ks, lambda m: m)
        args += [ap, a_s]
    in_specs += [pl.BlockSpec((None, k_tot, D_MODEL), lambda m: (j, 0, 0))]
    in_specs += _split_specs(tm, D_MODEL, lambda m: m, x[2])
    in_specs += [
        pl.BlockSpec((None, 6, D_MODEL), lambda m: (_group_of_tile(m, tm), 0, 0)),
        pl.BlockSpec((None, 1, D_MODEL), lambda m: (layer, 0, 0)),
        pl.BlockSpec((None, 1, D_MODEL), lambda m: (layer, 0, 0)),
        pl.BlockSpec((D_MODEL, LANES), lambda m: (0, 0)),
        pl.BlockSpec((1, LANES), lambda m: (0, 0)),
    ]
    return pl.pallas_call(
        functools.partial(_mm_ln_kernel, k_sizes),
        grid=(NTOK // tm,),
        in_specs=in_specs,
        out_specs=[
            pl.BlockSpec((tm, D_MODEL), lambda m: (m, 0)),
            pl.BlockSpec((None, tm // LANES, LANES), lambda m: (m, 0, 0)),
            pl.BlockSpec((N_COND, LANES), lambda m: (0, 0)),
        ],
        out_shape=[
            jax.ShapeDtypeStruct((NTOK, D_MODEL), F32),
            jax.ShapeDtypeStruct((NTOK // tm, tm // LANES, LANES), jnp.int32),
            jax.ShapeDtypeStruct((N_COND, LANES), F32),
        ],
        scratch_shapes=[
            pltpu.VMEM((k_tot, D_MODEL), BF16),
            pltpu.VMEM((tm, tm), BF16),
            pltpu.VMEM((N_COND, LANES), F32),
        ],
        compiler_params=_params(("arbitrary",)),
        name="out_proj_ln",
    )(*args, w, x[0], x[1], mod, ln_g.reshape(DEPTH, 1, D_MODEL), ln_b.reshape(DEPTH, 1, D_MODEL),
      w_router, b_router)


def _moe_kernel(tgrp_ref, ntile_ref, src0_ref, src1_ref, dst_ref, mid_ref, x_hbm, mod_ref, wr_ref, br_ref,
                w1_ref, w3_ref, w2_ref, g_ref, b_ref, y_hbm,
                gbuf, obuf, gsem, ssem, w13s, w2s):
    i = pl.program_id(0)
    n_steps = pl.num_programs(0)
    n_tiles = ntile_ref[0]
    n_blk = gbuf.shape[1]
    ts = n_blk * SUBLANES
    slot = lax.rem(i, 2)

    def start_gather(src_ref, s):
        def body(k, carry):
            for u in range(SUBLANES):
                tok = src_ref[0, k * SUBLANES + u]
                pltpu.make_async_copy(x_hbm.at[pl.ds(tok, 1), :], gbuf.at[s, k, pl.ds(u, 1), :],
                                      gsem.at[s]).start()
            return carry
        lax.fori_loop(0, n_blk, body, 0)

    def wait_gather(s):
        pltpu.make_async_copy(gbuf.at[s], gbuf.at[s], gsem.at[s]).wait()

    def start_scatter(s):
        def body(k, carry):
            for u in range(SUBLANES):
                tok = dst_ref[0, k * SUBLANES + u]
                pltpu.make_async_copy(obuf.at[s, k, pl.ds(u, 1), :], y_hbm.at[pl.ds(tok, 1), :],
                                      ssem.at[s]).start()
            return carry
        lax.fori_loop(0, n_blk, body, 0)

    def wait_scatter(s):
        pltpu.make_async_copy(obuf.at[s], obuf.at[s], ssem.at[s]).wait()

    @pl.when(i == 0)
    def _():
        start_gather(src0_ref, 0)

    @pl.when(i < n_tiles)
    def _():
        grp = tgrp_ref[i]
        wait_gather(slot)

        @pl.when(i + 1 < n_tiles)
        def _():
            start_gather(src1_ref, 1 - slot)

        @pl.when((i == 0) | (grp != tgrp_ref[jnp.maximum(i - 1, 0)]))
        def _():
            w13s[:, :, :MOE_FF] = w1_ref[...].astype(BF16)
            w13s[:, :, MOE_FF:] = w3_ref[...].astype(BF16)
            w2s[...] = w2_ref[...].astype(BF16)

        mid_rows = mid_ref[...].astype(F32)
        spread = jnp.concatenate([jnp.broadcast_to(mid_rows[a:a + 1, :], (LANES, LANES))
                                  for a in range(ts // LANES)], axis=0)
        row_i = lax.broadcasted_iota(jnp.int32, (ts, LANES), 0)
        mid = jnp.sum(jnp.where(_lane_iota((ts, LANES)) == _mod_pow2(row_i, LANES), spread, 0.0),
                      axis=-1, keepdims=True)

        def mod_row(k):
            return jnp.where(mid == 0, mod_ref[0, k:k + 1, :],
                             jnp.where(mid == 1, mod_ref[1, k:k + 1, :], mod_ref[2, k:k + 1, :]))

        x1 = gbuf[slot].reshape(ts, D_MODEL)
        xm = (x1 * (1.0 + mod_row(4)) + mod_row(3)).astype(BF16)
        z, lane_i, lane, g_p, _ = _router_probs(xm, wr_ref, br_ref)
        e0 = MOE_GROUPS + MOE_PER_GROUP * grp
        emask = (lane_i >= e0) & (lane_i < e0 + MOE_PER_GROUP)
        ze = jnp.where(emask, z, -jnp.inf)
        pe = jnp.exp(ze - jnp.max(ze, axis=-1, keepdims=True))
        e_prob = pe / jnp.sum(pe, axis=-1, keepdims=True)
        cand = jnp.where(emask, e_prob, -1.0)
        p1 = jnp.max(cand, axis=-1, keepdims=True)
        i1 = jnp.min(jnp.where(cand == p1, lane, float(LANES)), axis=-1, keepdims=True)
        cand2 = jnp.where(lane == i1, -1.0, cand)
        p2 = jnp.max(cand2, axis=-1, keepdims=True)
        i2 = jnp.min(jnp.where(cand2 == p2, lane, float(LANES)), axis=-1, keepdims=True)
        denom = p1 + p2
        comb = jnp.where(lane == i1, g_p * p1 / denom, 0.0) + jnp.where(lane == i2, g_p * p2 / denom, 0.0)
        y = None
        for e in range(MOE_PER_GROUP):
            c = jnp.sum(jnp.where(lane_i == e0 + e, comb, 0.0), axis=-1, keepdims=True)
            h = jnp.dot(xm, w13s[e], preferred_element_type=F32)
            hid = (_silu(h[:, :MOE_FF]) * h[:, MOE_FF:] * c).astype(BF16)
            part = jnp.dot(hid, w2s[e], preferred_element_type=F32)
            y = part if y is None else y + part
        r = DEEPNORM_ALPHA * x1 + mod_row(5) * y

        @pl.when(i >= 2)
        def _():
            wait_scatter(slot)

        obuf[slot] = _layer_norm(r, g_ref[...], b_ref[...]).reshape(n_blk, SUBLANES, D_MODEL)
        start_scatter(slot)

    @pl.when(i == n_steps - 1)
    def _():
        @pl.when(n_tiles >= 2)
        def _():
            wait_scatter(lax.rem(n_tiles, 2))

        @pl.when(n_tiles >= 1)
        def _():
            wait_scatter(lax.rem(n_tiles + 1, 2))


def _inverse_perm_kernel(pos_ref, pad_ref, out_ref):
    def clear(s, carry):
        out_ref[s] = 0
        return carry

    def place(t, carry):
        out_ref[pos_ref[t]] = t + 1
        return carry

    for run in range(pad_ref.shape[0] // 2):
        lax.fori_loop(pad_ref[2 * run], pad_ref[2 * run + 1], clear, 0)
    lax.fori_loop(0, pos_ref.shape[0], place, 0, unroll=8)


def _inverse_perm(pos, pad_runs):
    return pl.pallas_call(
        _inverse_perm_kernel,
        in_specs=[pl.BlockSpec(memory_space=pltpu.SMEM)] * 2,
        out_specs=pl.BlockSpec(memory_space=pltpu.SMEM),
        out_shape=jax.ShapeDtypeStruct((MOE_ROWS,), jnp.int32),
        name="moe_inverse_perm",
    )(pos, pad_runs)


def _moe_plan(meta, counts):
    ts = MOE_TILE
    cnt = counts[0, :MOE_GROUPS].astype(jnp.int32)
    tiles_g = (cnt + ts - 1) // ts
    tile_end = jnp.cumsum(tiles_g)
    row0_g = (tile_end - tiles_g) * ts
    keys = meta.reshape(NTOK)
    gid = keys // MOE_KEY_BASE
    rank = keys % MOE_KEY_BASE
    pos = row0_g[gid] + rank
    pad_runs = jnp.stack([row0_g + cnt, tile_end * ts], axis=1).reshape(-1)
    pad_runs = jnp.concatenate([pad_runs, tile_end[-1:] * ts, jnp.full((1,), MOE_ROWS, jnp.int32)])
    tok1 = _inverse_perm(pos, pad_runs.astype(jnp.int32))
    rows = jnp.arange(MOE_ROWS, dtype=jnp.int32)
    src = jnp.maximum(tok1 - 1, 0)
    spare = NTOK + ((rows // ts) % 2) * ts + rows % ts
    dst = jnp.where(tok1 > 0, src, spare)
    mid = jnp.where(src < NTOK_P, 0, 1 + (src - NTOK_P) // DEC_SEQ)
    tile_group = jnp.minimum(jnp.sum(jnp.arange(MOE_TILES)[:, None] >= tile_end[None, :], axis=1),
                             MOE_GROUPS - 1).astype(jnp.int32)
    n_tiles = tile_end[-1:].astype(jnp.int32)
    return (tile_group, n_tiles, src.reshape(MOE_TILES, 1, ts), dst.reshape(MOE_TILES, 1, ts),
            mid.reshape(MOE_TILES, ts // LANES, LANES))


def _moe(x1, plan, mods, w_router, b_router, w1, w3, w2, ln_g, ln_b, layer):
    ts = MOE_TILE
    tile_group, n_tiles, src, dst, mid = plan
    grp_shape = (DEPTH, MOE_GROUPS, MOE_PER_GROUP)
    w_in_spec = pl.BlockSpec((None, None, MOE_PER_GROUP, D_MODEL, MOE_FF), lambda i, tg, nt: (layer, tg[i], 0, 0, 0))
    w_out_spec = pl.BlockSpec((None, None, MOE_PER_GROUP, MOE_FF, D_MODEL), lambda i, tg, nt: (layer, tg[i], 0, 0, 0))
    smem_tile = functools.partial(pl.BlockSpec, (None, 1, ts), memory_space=pltpu.SMEM)
    grid_spec = pltpu.PrefetchScalarGridSpec(
        num_scalar_prefetch=2,
        grid=(MOE_TILES,),
        in_specs=[
            smem_tile(lambda i, tg, nt: (i, 0, 0)),
            smem_tile(lambda i, tg, nt: (jnp.minimum(i + 1, MOE_TILES - 1), 0, 0)),
            smem_tile(lambda i, tg, nt: (i, 0, 0)),
            pl.BlockSpec((None, ts // LANES, LANES), lambda i, tg, nt: (i, 0, 0)),
            pl.BlockSpec(memory_space=pl.ANY),
            pl.BlockSpec((None, N_COND, 6, D_MODEL), lambda i, tg, nt: (layer, 0, 0, 0)),
            pl.BlockSpec((D_MODEL, LANES), lambda i, tg, nt: (0, 0)),
            pl.BlockSpec((1, LANES), lambda i, tg, nt: (0, 0)),
            w_in_spec,
            w_in_spec,
            w_out_spec,
            pl.BlockSpec((None, 1, D_MODEL), lambda i, tg, nt: (layer, 0, 0)),
            pl.BlockSpec((None, 1, D_MODEL), lambda i, tg, nt: (layer, 0, 0)),
        ],
        out_specs=pl.BlockSpec(memory_space=pl.ANY),
        scratch_shapes=[
            pltpu.VMEM((2, ts // SUBLANES, SUBLANES, D_MODEL), F32),
            pltpu.VMEM((2, ts // SUBLANES, SUBLANES, D_MODEL), F32),
            pltpu.SemaphoreType.DMA((2,)),
            pltpu.SemaphoreType.DMA((2,)),
            pltpu.VMEM((MOE_PER_GROUP, D_MODEL, 2 * MOE_FF), BF16),
            pltpu.VMEM((MOE_PER_GROUP, MOE_FF, D_MODEL), BF16),
        ],
    )
    return pl.pallas_call(
        _moe_kernel,
        grid_spec=grid_spec,
        out_shape=jax.ShapeDtypeStruct((NTOK + 2 * ts, D_MODEL), F32),
        compiler_params=_params(("arbitrary",)),
        name="hier_moe_ln",
    )(tile_group, n_tiles, src, src, dst, mid, x1, mods, w_router, b_router,
      w1.reshape(grp_shape + (D_MODEL, MOE_FF)), w3.reshape(grp_shape + (D_MODEL, MOE_FF)),
      w2.reshape(grp_shape + (MOE_FF, D_MODEL)),
      ln_g.reshape(DEPTH, 1, D_MODEL), ln_b.reshape(DEPTH, 1, D_MODEL))


def _swap_halves(x, lane, half):
    return jnp.where(_mod_pow2(lane, 2 * half) < half,
                     pltpu.roll(x, LANES - half, 1), pltpu.roll(x, half, 1))


def _rope128(x, cos, sin_signed, lane, half):
    return x * cos + _swap_halves(x, lane, half) * sin_signed


def _rope_tables(rot_dim):
    rows = DEC_SEQ // GRID_W
    row = jnp.repeat(jnp.arange(rows, dtype=F32), GRID_W)
    col = jnp.tile(jnp.arange(GRID_W, dtype=F32), rows)
    n_freq = rot_dim // 4
    inv_freq = ROPE_BASE ** (-jnp.arange(n_freq, dtype=F32) / n_freq)
    ang = jnp.concatenate([row[:, None] * inv_freq, col[:, None] * inv_freq], axis=-1)
    cos, sin = jnp.cos(ang), jnp.sin(ang)
    reps = LANES // rot_dim
    cos_full = jnp.tile(jnp.concatenate([cos, cos], axis=-1), (1, reps))
    sin_signed = jnp.tile(jnp.concatenate([-sin, sin], axis=-1), (1, reps))
    return cos_full, sin_signed


def _exp_parts(s_list, scale):
    m = None
    for s in s_list:
        sm = jnp.max(s, axis=-1, keepdims=True)
        m = sm if m is None else jnp.maximum(m, sm)
    return [jnp.exp2((s - m) * (scale * LOG2E)).astype(BF16) for s in s_list]


def _pv_normalised(p_list, v_list):
    o = None
    for p, v in zip(p_list, v_list):
        part = jnp.dot(p, v, preferred_element_type=F32)
        o = part if o is None else o + part
    return o[:, :LANES] / o[:, LANES:]


def _mla_kv_kernel(new_tokens, *refs):
    if new_tokens:
        (xp_ref, xs_ref, mod_ref, w_ref, cos_ref, sin_ref, g_ref, wuk_ref, wuv_ref) = refs[:9]
        ckvc_ref, krc_ref, zq_ref, kcat_ref, vm_ref, wscr, wuk_scr, wuv_scr = refs[-8:]
        i = pl.program_id(0)

        @pl.when(i == 0)
        def _():
            wscr[...] = w_ref[...].astype(BF16)
            wuk_scr[...] = wuk_ref[...].astype(BF16)
            wuv_scr[...] = wuv_ref[...].astype(BF16)

        xm = (_read_split(xp_ref, xs_ref, i) * (1.0 + mod_ref[1:2, :]) + mod_ref[0:1, :]).astype(BF16)
        z = lax.dot_general(xm, wscr[...], NT_DIMS, preferred_element_type=F32)
        n_q = MLA_HEADS * (MLA_NOPE + MLA_ROPE)
        zq_ref[...] = z[:, :n_q].astype(BF16)
        x = z[:, n_q:n_q + MLA_KV_RANK]
        c = x * lax.rsqrt(jnp.mean(x * x, axis=-1, keepdims=True) + RMS_EPS) * g_ref[...]
        kr_raw = z[:, n_q + MLA_KV_RANK:]

        @pl.when(i < NTOK_P // zq_ref.shape[0])
        def _():
            for b in range(ckvc_ref.shape[0]):
                ckvc_ref[b] = c[SEQ * b:SEQ * (b + 1), :]
                krc_ref[b] = kr_raw[SEQ * b:SEQ * (b + 1), :MLA_ROPE]

        lane = _lane_iota(kr_raw.shape)
        kr = _rope128(kr_raw, cos_ref[...], sin_ref[...], lane, MLA_ROPE // 2)
        wuk = wuk_scr[...]
        wuv = wuv_scr[...]
    else:
        ckv_ref, kr_ref, wuk_ref, wuv_ref, kcat_ref, vm_ref = refs
        c = ckv_ref[...]
        kr = kr_ref[...]
        wuk = wuk_ref[...].astype(BF16)
        wuv = wuv_ref[...].astype(BF16)
    cb = c.astype(BF16)
    kn = jnp.dot(cb, wuk, preferred_element_type=F32).astype(BF16)
    vv = jnp.dot(cb, wuv, preferred_element_type=F32).astype(BF16)
    krb = kr.astype(BF16)
    ones = jnp.ones((c.shape[0], LANES), BF16)
    for p in range(MLA_HEADS // 2):
        kcat_ref[:, 256 * p:256 * p + LANES] = kn[:, LANES * p:LANES * (p + 1)]
        kcat_ref[:, 256 * p + LANES:256 * (p + 1)] = krb
        vm_ref[:, 256 * p:256 * p + LANES] = vv[:, LANES * p:LANES * (p + 1)]
        vm_ref[:, 256 * p + LANES:256 * (p + 1)] = ones


def _mla_proj_kv(x, mod, w_t, cos_t, sin_t, kv_norm_g, w_uk, w_uv, j, prev):
    tm = MLA_PROJ_TM
    npt = NTOK_P // tm
    nst = DEC_SEQ // tm
    nb = tm // SEQ
    n_q = MLA_HEADS * (MLA_NOPE + MLA_ROPE)

    def tab(i):
        return (jnp.where(i < npt, 0, 1 + (i - npt) % nst), 0)

    def cache_idx(i):
        return (jnp.minimum(i, npt - 1), j, 0, 0)

    in_specs = _split_specs(tm, D_MODEL, lambda i: i, x[2]) + [
        pl.BlockSpec((None, 6, D_MODEL), lambda i: (_group_of_tile(i, tm), 0, 0)),
        pl.BlockSpec((MLA_COLS, D_MODEL), lambda i: (0, 0)),
        pl.BlockSpec((tm, LANES), tab),
        pl.BlockSpec((tm, LANES), tab),
        pl.BlockSpec((None, 1, MLA_KV_RANK), lambda i: (j, 0, 0)),
        pl.BlockSpec((None, MLA_KV_RANK, 512), lambda i: (j, 0, 0)),
        pl.BlockSpec((None, MLA_KV_RANK, 512), lambda i: (j, 0, 0)),
    ]
    args = [x[0], x[1], mod, w_t, cos_t, sin_t, kv_norm_g.reshape(-1, 1, MLA_KV_RANK), w_uk, w_uv]
    aliases = {}
    if prev is not None:
        aliases = {len(args): 0, len(args) + 1: 1}
        in_specs += [pl.BlockSpec(memory_space=pl.ANY)] * 2
        args += list(prev)
    return pl.pallas_call(
        functools.partial(_mla_kv_kernel, True),
        grid=(NTOK // tm,),
        in_specs=in_specs,
        out_specs=[
            pl.BlockSpec((nb, None, SEQ, MLA_KV_RANK), cache_idx),
            pl.BlockSpec((nb, None, SEQ, MLA_ROPE), cache_idx),
            pl.BlockSpec((tm, n_q), lambda i: (i, 0)),
            pl.BlockSpec((tm, 1024), lambda i: (i, 0)),
            pl.BlockSpec((tm, 1024), lambda i: (i, 0)),
        ],
        out_shape=[
            jax.ShapeDtypeStruct((BATCH, N_EVEN, SEQ, MLA_KV_RANK), F32),
            jax.ShapeDtypeStruct((BATCH, N_EVEN, SEQ, MLA_ROPE), F32),
            jax.ShapeDtypeStruct((NTOK, n_q), BF16),
            jax.ShapeDtypeStruct((NTOK, 1024), BF16),
            jax.ShapeDtypeStruct((NTOK, 1024), BF16),
        ],
        scratch_shapes=[
            pltpu.VMEM((MLA_COLS, D_MODEL), BF16),
            pltpu.VMEM((MLA_KV_RANK, 512), BF16),
            pltpu.VMEM((MLA_KV_RANK, 512), BF16),
        ],
        input_output_aliases=aliases,
        compiler_params=_params(("arbitrary",)),
        name="mla_proj_kv",
    )(*args)


def _mla_kv_ctx(cache_ckv, kr_tiled, w_uk, w_uv, j):
    return pl.pallas_call(
        functools.partial(_mla_kv_kernel, False),
        grid=(DEC_BATCH,),
        in_specs=[
            pl.BlockSpec((None, None, PAST_LEN, MLA_KV_RANK), lambda b: (b, j, 0, 0)),
            pl.BlockSpec((None, PAST_LEN, LANES), lambda b: (b, 0, 0)),
            pl.BlockSpec((None, MLA_KV_RANK, 512), lambda b: (j, 0, 0)),
            pl.BlockSpec((None, MLA_KV_RANK, 512), lambda b: (j, 0, 0)),
        ],
        out_specs=[
            pl.BlockSpec((PAST_LEN, 1024), lambda b: (b, 0)),
            pl.BlockSpec((PAST_LEN, 1024), lambda b: (b, 0)),
        ],
        out_shape=[
            jax.ShapeDtypeStruct((DEC_BATCH * PAST_LEN, 1024), BF16),
            jax.ShapeDtypeStruct((DEC_BATCH * PAST_LEN, 1024), BF16),
        ],
        compiler_params=_params(("parallel",)),
        name="mla_kv_ctx",
    )(cache_ckv, kr_tiled, w_uk, w_uv)


def _mla_attn_kernel(latent, *refs):
    if latent:
        qn_ref, qr_ref, cos_ref, sin_ref, kc_ref, vc_ref, kn_ref, vn_ref, o_ref = refs
        k_refs, v_refs = (kc_ref, kn_ref), (vc_ref, vn_ref)
    else:
        qn_ref, qr_ref, kn_ref, vn_ref, o_ref = refs
        k_refs, v_refs = (kn_ref,), (vn_ref,)
    tq = ATT_TQ if latent else SEQ
    n_seq = qn_ref.shape[0] // tq
    lane = _lane_iota((tq, LANES))
    scale = (MLA_NOPE + MLA_ROPE) ** -0.5
    for sq in range(n_seq):
        rows = slice(tq * sq, tq * (sq + 1))
        krows = slice(None) if latent else rows
        qr_cols = []
        for cidx in range(2):
            x = qr_ref[rows, LANES * cidx:LANES * (cidx + 1)].astype(F32)
            if latent:
                x = _rope128(x, cos_ref[...], sin_ref[...], lane, MLA_ROPE // 2)
            qr_cols.append(x)
        o_prev = None
        for h in range(MLA_HEADS):
            p, half = divmod(h, 2)
            cidx, slot = divmod(h, 4)
            qa = jnp.where(_div_pow2(lane, MLA_NOPE) == half,
                           qn_ref[rows, LANES * p:LANES * (p + 1)].astype(F32), 0.0)
            qb = jnp.where(_div_pow2(lane, MLA_ROPE) == slot, qr_cols[cidx], 0.0)
            qcat = jnp.concatenate([qa, qb], axis=1).astype(BF16)
            pair_cols = slice(256 * p, 256 * (p + 1))
            s_list = [lax.dot_general(qcat, k_ref[krows, pair_cols], NT_DIMS, preferred_element_type=F32)
                      for k_ref in k_refs]
            o = _pv_normalised(_exp_parts(s_list, scale), [v_ref[krows, pair_cols] for v_ref in v_refs])
            if half == 0:
                o_prev = o
            else:
                o_ref[rows, LANES * p:LANES * (p + 1)] = jnp.where(lane < MLA_DV, o_prev, o).astype(BF16)


def _mla_attn_prompt(zmla, kcat, vm):
    rows = 4 * SEQ
    return pl.pallas_call(
        functools.partial(_mla_attn_kernel, False),
        grid=(NTOK_P // rows,),
        in_specs=[
            pl.BlockSpec((rows, 512), lambda b: (b, 0)),
            pl.BlockSpec((rows, 256), lambda b: (b, 2)),
            pl.BlockSpec((rows, 1024), lambda b: (b, 0)),
            pl.BlockSpec((rows, 1024), lambda b: (b, 0)),
        ],
        out_specs=pl.BlockSpec((rows, 512), lambda b: (b, 0)),
        out_shape=jax.ShapeDtypeStruct((NTOK_P, 512), BF16),
        compiler_params=_params(("parallel",)),
        name="mla_attn_prompt",
    )(zmla, zmla, kcat, vm)


def _mla_attn_latent(zmla, cos_t, sin_t, kcat_ctx, vm_ctx, kcat, vm):
    nq = DEC_SEQ // ATT_TQ
    row0 = NTOK_P // ATT_TQ
    seq0 = NTOK_P // DEC_SEQ
    return pl.pallas_call(
        functools.partial(_mla_attn_kernel, True),
        grid=(DEC_BATCH, nq),
        in_specs=[
            pl.BlockSpec((ATT_TQ, 512), lambda b, q: (row0 + b * nq + q, 0)),
            pl.BlockSpec((ATT_TQ, 256), lambda b, q: (row0 + b * nq + q, 2)),
            pl.BlockSpec((ATT_TQ, LANES), lambda b, q: (q, 0)),
            pl.BlockSpec((ATT_TQ, LANES), lambda b, q: (q, 0)),
            pl.BlockSpec((PAST_LEN, 1024), lambda b, q: (b, 0)),
            pl.BlockSpec((PAST_LEN, 1024), lambda b, q: (b, 0)),
            pl.BlockSpec((DEC_SEQ, 1024), lambda b, q: (seq0 + b, 0)),
            pl.BlockSpec((DEC_SEQ, 1024), lambda b, q: (seq0 + b, 0)),
        ],
        out_specs=pl.BlockSpec((ATT_TQ, 512), lambda b, q: (b * nq + q, 0)),
        out_shape=jax.ShapeDtypeStruct((NTOK_S, 512), BF16),
        compiler_params=_params(("parallel", "arbitrary")),
        name="mla_attn_latent",
    )(zmla, zmla, cos_t, sin_t, kcat_ctx, vm_ctx, kcat, vm)


def _log_sigmoid(x):
    return jnp.minimum(x, 0.0) - jnp.log1p(jnp.exp(-jnp.abs(x)))


def _retention_kernel(seq_len, n_seq, has_init, emit_state, has_prev, *refs):
    refs = list(refs)
    decf_ref, decb_ref, q_ref, k_ref, v_ref, g_ref = refs[:6]
    refs = refs[6:]
    if has_init:
        sf0_ref, sb0_ref = refs[:2]
        refs = refs[2:]
    if has_prev:
        refs = refs[2:]
    o_ref = refs[0]
    refs = refs[1:]
    if emit_state:
        sf_ref, sb_ref = refs[:2]
        refs = refs[2:]
    of_scr, ob_scr, dec_scr, wts_scr = refs

    c = RET_CHUNK
    n_chunks = seq_len // c
    pair = pl.program_id(0)
    lane = _lane_iota((c, LANES))
    zeros_half = jnp.zeros((RET_DK, RET_DV), F32)

    def log_gammas(half):
        head = 2 * pair + half
        return (_log_sigmoid(decf_ref[pl.ds(head, 1), :]),
                _log_sigmoid(decb_ref[pl.ds(head, 1), :]))

    @pl.when(pl.program_id(1) == 0)
    def _():
        ri = lax.broadcasted_iota(jnp.int32, (c, c), 0)
        ci = lax.broadcasted_iota(jnp.int32, (c, c), 1)
        rel = (ri - ci).astype(F32)
        row = lax.broadcasted_iota(jnp.int32, (c, LANES), 0).astype(F32)
        for half in range(2):
            lgf, lgb = log_gammas(half)
            dec_scr[half] = (jnp.where(rel >= 0, jnp.exp(lgf[:, 0:1] * jnp.maximum(rel, 0.0)), 0.0)
                             + jnp.where(rel <= 0, jnp.exp(lgb[:, 0:1] * jnp.maximum(-rel, 0.0)), 0.0))
            wts_scr[half, 0] = jnp.exp(lgf * (row + 1.0))
            wts_scr[half, 1] = jnp.exp(lgf * (c - 1.0 - row))
            wts_scr[half, 2] = jnp.exp(lgb * (c - row))
            wts_scr[half, 3] = jnp.exp(lgb * row)

    cross = has_init or n_chunks > 1
    chains = [(sq, half) for sq in range(n_seq) for half in range(2)]
    chunk_decay = []
    for half in range(2):
        lgf, lgb = log_gammas(half)
        chunk_decay.append((jnp.exp(lgf * float(c)), jnp.exp(lgb * float(c))))

    def rows_of(sq, n):
        start = sq * seq_len + n * c
        return pl.ds(start if isinstance(n, int) else pl.multiple_of(start, c), c)

    def load(sq, half, n):
        rows = rows_of(sq, n)
        vsl = slice(RET_DV * half, RET_DV * (half + 1))
        qm = jnp.where(_div_pow2(lane, RET_DK) == half, q_ref[rows, :].astype(F32), 0.0)
        kk = k_ref[rows, :].astype(F32) * (RET_DK ** -0.5)
        return rows, vsl, qm, kk, v_ref[rows, vsl]

    def init_state(s0_ref, half):
        if not has_init:
            return jnp.zeros((LANES, RET_DV), F32)
        s0 = s0_ref[half]
        return jnp.concatenate([s0, zeros_half] if half == 0 else [zeros_half, s0], axis=0)

    def fwd_step(sq, half, n, s_f):
        rows, vsl, qm, kk, vb = load(sq, half, n)
        s = lax.dot_general(qm.astype(BF16), kk.astype(BF16), NT_DIMS, preferred_element_type=F32)
        o = jnp.dot((s * dec_scr[half]).astype(BF16), vb, preferred_element_type=F32)
        if cross:
            o = o + jnp.dot((qm * wts_scr[half, 0]).astype(BF16), s_f.astype(BF16), preferred_element_type=F32)
        of_scr[rows, vsl] = o
        kv = lax.dot_general((kk * wts_scr[half, 1]).astype(BF16), vb, TN_DIMS, preferred_element_type=F32)
        return chunk_decay[half][0] * s_f + kv

    def bwd_step(sq, half, n, s_b):
        rows, vsl, qm, kk, vb = load(sq, half, n)
        if cross:
            ob_scr[rows, vsl] = jnp.dot((qm * wts_scr[half, 2]).astype(BF16), s_b.astype(BF16),
                                        preferred_element_type=F32)
        kv = lax.dot_general((kk * wts_scr[half, 3]).astype(BF16), vb, TN_DIMS, preferred_element_type=F32)
        return chunk_decay[half][1] * s_b + kv

    def finish(sq, n):
        rows = rows_of(sq, n)
        o2 = of_scr[rows, :] + ob_scr[rows, :] if cross else of_scr[rows, :]
        for half in range(2):
            vsl = slice(RET_DV * half, RET_DV * (half + 1))
            o = o2[:, vsl]
            mu = jnp.mean(o, axis=-1, keepdims=True)
            d = o - mu
            var = jnp.mean(d * d, axis=-1, keepdims=True)
            o_ref[rows, vsl] = (_silu(g_ref[rows, vsl].astype(F32)) * (d * lax.rsqrt(var + LN_EPS))).astype(BF16)

    s_f = tuple(init_state(sf0_ref if has_init else None, half) for _, half in chains)
    s_b = tuple(init_state(sb0_ref if has_init else None, half) for _, half in chains)
    if n_chunks == 1:
        s_f = tuple(fwd_step(sq, half, 0, s) for (sq, half), s in zip(chains, s_f))
        s_b = tuple(bwd_step(sq, half, 0, s) for (sq, half), s in zip(chains, s_b))
        for sq in range(n_seq):
            finish(sq, 0)
    else:
        def scan_step(n, carry):
            sf, sb = carry
            sf = tuple(fwd_step(sq, half, n, s) for (sq, half), s in zip(chains, sf))
            sb = tuple(bwd_step(sq, half, n_chunks - 1 - n, s) for (sq, half), s in zip(chains, sb))
            return sf, sb

        s_f, s_b = lax.fori_loop(0, n_chunks, scan_step, (s_f, s_b))

        def finish_step(n, carry):
            for sq in range(n_seq):
                finish(sq, n)
            return carry

        lax.fori_loop(0, n_chunks, finish_step, 0)
    if emit_state:
        for (sq, half), sf, sb in zip(chains, s_f, s_b):
            sf_ref[sq, half] = sf[RET_DK * half:RET_DK * (half + 1), :]
            sb_ref[sq, half] = sb[RET_DK * half:RET_DK * (half + 1), :]


def _retention(zret, decf, decb, j, latent, state_f=None, state_b=None, prev=None):
    seq_len = DEC_SEQ if latent else SEQ
    n_seq = 1 if latent else 4
    n_b = (DEC_BATCH if latent else BATCH) // n_seq
    rows = n_seq * seq_len
    row0 = NTOK_P // rows if latent else 0
    n_pairs = RET_HEADS // 2
    in_specs = [
        pl.BlockSpec((None, RET_HEADS, LANES), lambda p, b: (j, 0, 0)),
        pl.BlockSpec((None, RET_HEADS, LANES), lambda p, b: (j, 0, 0)),
        pl.BlockSpec((rows, LANES), lambda p, b: (row0 + b, p)),
        pl.BlockSpec((rows, LANES), lambda p, b: (row0 + b, 4 + p)),
        pl.BlockSpec((rows, 256), lambda p, b: (row0 + b, 4 + p)),
        pl.BlockSpec((rows, 256), lambda p, b: (row0 + b, 8 + p)),
    ]
    args = [decf, decb, zret, zret, zret, zret]
    out_specs = [pl.BlockSpec((rows, 256), lambda p, b: (b, p))]
    out_shape = [jax.ShapeDtypeStruct((n_b * rows, RET_HEADS * RET_DV), BF16)]
    aliases = {}
    if latent:
        st_spec = pl.BlockSpec((None, None, 2, RET_DK, RET_DV), lambda p, b: (b, j, p, 0, 0))
        in_specs += [st_spec, st_spec]
        args += [state_f, state_b]
    else:
        st_spec = pl.BlockSpec((n_seq, None, 2, RET_DK, RET_DV), lambda p, b: (b, j, p, 0, 0))
        out_specs += [st_spec, st_spec]
        out_shape += [jax.ShapeDtypeStruct((BATCH, N_EVEN, RET_HEADS, RET_DK, RET_DV), F32)] * 2
        if prev is not None:
            aliases = {len(args): 1, len(args) + 1: 2}
            in_specs += [pl.BlockSpec(memory_space=pl.ANY)] * 2
            args += list(prev)
    return pl.pallas_call(
        functools.partial(_retention_kernel, seq_len, n_seq, latent, not latent, bool(aliases)),
        grid=(n_pairs, n_b),
        in_specs=in_specs,
        out_specs=out_specs,
        out_shape=out_shape,
        scratch_shapes=[
            pltpu.VMEM((rows, 2 * RET_DV), F32),
            pltpu.VMEM((rows, 2 * RET_DV), F32),
            pltpu.VMEM((2, RET_CHUNK, RET_CHUNK), F32),
            pltpu.VMEM((2, 4, RET_CHUNK, LANES), F32),
        ],
        input_output_aliases=aliases,
        compiler_params=_params(("arbitrary", "arbitrary")),
        name="retention_latent" if latent else "retention_prompt",
    )(*args)


def _diff_prep_kernel(k_ref, v_ref, cos_ref, sin_ref, kr_ref, va_ref):
    lane = _lane_iota(cos_ref.shape)
    cos = cos_ref[...]
    sin = sin_ref[...]
    ones = jnp.ones(cos_ref.shape, BF16)
    for h in range(DIFF_HEADS):
        sl = slice(LANES * h, LANES * (h + 1))
        kr_ref[:, sl] = _rope128(k_ref[:, sl].astype(F32), cos, sin, lane, DIFF_DH // 2).astype(BF16)
        va_ref[:, 256 * h:256 * h + LANES] = v_ref[:, sl].astype(BF16)
        va_ref[:, 256 * h + LANES:256 * (h + 1)] = ones


def _diff_prep(zodd, cos_t, sin_t):
    tm = 512
    row0 = NTOK_P // tm
    nst = DEC_SEQ // tm
    return pl.pallas_call(
        _diff_prep_kernel,
        grid=(NTOK_S // tm,),
        in_specs=[
            pl.BlockSpec((tm, 1024), lambda i: (row0 + i, 1)),
            pl.BlockSpec((tm, 1024), lambda i: (row0 + i, 2)),
            pl.BlockSpec((tm, LANES), lambda i: (i % nst, 0)),
            pl.BlockSpec((tm, LANES), lambda i: (i % nst, 0)),
        ],
        out_specs=[pl.BlockSpec((tm, 1024), lambda i: (i, 0)), pl.BlockSpec((tm, 2048), lambda i: (i, 0))],
        out_shape=[jax.ShapeDtypeStruct((NTOK_S, 1024), BF16), jax.ShapeDtypeStruct((NTOK_S, 2048), BF16)],
        compiler_params=_params(("parallel",)),
        name="diff_rope_keys",
    )(zodd, zodd, cos_t, sin_t)


def _diff_attn_kernel(latent, lam_init, *refs):
    if latent:
        (lam_ref, ng_ref, q_ref, cos_ref, sin_ref, kc_ref, vc_ref, kn_ref, vn_ref, o_ref) = refs
    else:
        lam_ref, ng_ref, q_ref, k_ref, v_ref = refs[:5]
        o_ref, kout_ref, vout_ref = refs[-3:]
    tq = ATT_TQ if latent else SEQ
    n_seq = q_ref.shape[0] // tq
    lane = _lane_iota((tq, LANES))
    scale = DIFF_DH ** -0.5
    lp = lam_ref[...]
    lam = (jnp.exp(jnp.sum(lp[0:1, :] * lp[1:2, :], axis=-1, keepdims=True))
           - jnp.exp(jnp.sum(lp[2:3, :] * lp[3:4, :], axis=-1, keepdims=True)) + lam_init)
    ng = ng_ref[...]
    ones = jnp.ones((PAST_LEN if latent else tq, LANES), BF16)
    for sq, h in [(sq, h) for sq in range(n_seq) for h in range(DIFF_HEADS)]:
        sl = slice(LANES * h, LANES * (h + 1))
        rows = slice(tq * sq, tq * (sq + 1))
        qh = q_ref[rows, sl].astype(F32)
        if latent:
            qh = _rope128(qh, cos_ref[...], sin_ref[...], lane, DIFF_DH // 2)
            k_list = [kc_ref[h].astype(BF16), kn_ref[:, sl]]
            v_list = [jnp.concatenate([vc_ref[h].astype(BF16), ones], axis=1),
                      vn_ref[:, 256 * h:256 * (h + 1)]]
        else:
            kh = k_ref[rows, sl]
            vh = v_ref[rows, sl]
            kout_ref[sq, h] = kh.astype(F32)
            vout_ref[sq, h] = vh.astype(F32)
            k_list = [kh.astype(BF16)]
            v_list = [jnp.concatenate([vh.astype(BF16), ones], axis=1)]
        q1 = jnp.where(lane < DIFF_DH, qh, 0.0).astype(BF16)
        q2 = jnp.where(lane >= DIFF_DH, qh, 0.0).astype(BF16)
        s1 = [lax.dot_general(q1, kk, NT_DIMS, preferred_element_type=F32) for kk in k_list]
        s2 = [lax.dot_general(q2, kk, NT_DIMS, preferred_element_type=F32) for kk in k_list]
        o = _pv_normalised(_exp_parts(s1, scale), v_list) - lam * _pv_normalised(_exp_parts(s2, scale), v_list)
        y = o * lax.rsqrt(jnp.mean(o * o, axis=-1, keepdims=True) + RMS_EPS) * ng
        o_ref[rows, sl] = (y * (1.0 - lam_init)).astype(BF16)


def _diff_attn_prompt(zodd, lam_p, norm_g, j, lam_init, prev):
    n_seq = 2
    rows = n_seq * SEQ
    cache_shape = jax.ShapeDtypeStruct((BATCH, N_ODD, DIFF_HEADS, SEQ, LANES), F32)
    cache_spec = pl.BlockSpec((n_seq, None, DIFF_HEADS, SEQ, LANES), lambda b: (b, j, 0, 0, 0))
    in_specs = [
        pl.BlockSpec((None, 4, DIFF_DH), lambda b: (j, 0, 0)),
        pl.BlockSpec((None, 1, DIFF_DV), lambda b: (j, 0, 0)),
        pl.BlockSpec((rows, 1024), lambda b: (b, 0)),
        pl.BlockSpec((rows, 1024), lambda b: (b, 1)),
        pl.BlockSpec((rows, 1024), lambda b: (b, 2)),
    ]
    args = [lam_p, norm_g.reshape(-1, 1, DIFF_DV), zodd, zodd, zodd]
    aliases = {}
    if prev is not None:
        aliases = {len(args): 1, len(args) + 1: 2}
        in_specs += [pl.BlockSpec(memory_space=pl.ANY)] * 2
        args += list(prev)
    return pl.pallas_call(
        functools.partial(_diff_attn_kernel, False, lam_init),
        grid=(BATCH // n_seq,),
        in_specs=in_specs,
        out_specs=[pl.BlockSpec((rows, 1024), lambda b: (b, 0)), cache_spec, cache_spec],
        out_shape=[jax.ShapeDtypeStruct((NTOK_P, 1024), BF16), cache_shape, cache_shape],
        input_output_aliases=aliases,
        compiler_params=_params(("arbitrary",)),
        name="diff_attn_prompt",
    )(*args)


def _diff_attn_latent(zodd, lam_p, norm_g, cos_t, sin_t, cache_k, cache_v, k_rot, v_aug, j, lam_init):
    nq = DEC_SEQ // ATT_TQ
    row0 = NTOK_P // ATT_TQ
    ctx_spec = pl.BlockSpec((None, None, DIFF_HEADS, PAST_LEN, LANES), lambda b, q: (b, j, 0, 0, 0))
    return pl.pallas_call(
        functools.partial(_diff_attn_kernel, True, lam_init),
        grid=(DEC_BATCH, nq),
        in_specs=[
            pl.BlockSpec((None, 4, DIFF_DH), lambda b, q: (j, 0, 0)),
            pl.BlockSpec((None, 1, DIFF_DV), lambda b, q: (j, 0, 0)),
            pl.BlockSpec((ATT_TQ, 1024), lambda b, q: (row0 + b * nq + q, 0)),
            pl.BlockSpec((ATT_TQ, LANES), lambda b, q: (q, 0)),
            pl.BlockSpec((ATT_TQ, LANES), lambda b, q: (q, 0)),
            ctx_spec,
            ctx_spec,
            pl.BlockSpec((DEC_SEQ, 1024), lambda b, q: (b, 0), pipeline_mode=pl.Buffered(1)),
            pl.BlockSpec((DEC_SEQ, 2048), lambda b, q: (b, 0), pipeline_mode=pl.Buffered(1)),
        ],
        out_specs=pl.BlockSpec((ATT_TQ, 1024), lambda b, q: (b * nq + q, 0)),
        out_shape=jax.ShapeDtypeStruct((NTOK_S, 1024), BF16),
        compiler_params=_params(("parallel", "arbitrary")),
        name="diff_attn_latent",
    )(lam_p, norm_g.reshape(-1, 1, DIFF_DV), zodd, cos_t, sin_t, cache_k, cache_v, k_rot, v_aug)


def _mla_weight_t(w_in_t, j):
    base = RET_COLS
    mq = w_in_t[j, base:base + MLA_HEADS * (MLA_NOPE + MLA_ROPE)].reshape(MLA_HEADS, MLA_NOPE + MLA_ROPE, D_MODEL)
    qn = mq[:, :MLA_NOPE].reshape(MLA_HEADS * MLA_NOPE, D_MODEL)
    qr = mq[:, MLA_NOPE:].reshape(MLA_HEADS * MLA_ROPE, D_MODEL)
    ckv0 = base + MLA_HEADS * (MLA_NOPE + MLA_ROPE)
    ckv = w_in_t[j, ckv0:ckv0 + MLA_KV_RANK]
    kr = w_in_t[j, ckv0 + MLA_KV_RANK:]
    return jnp.concatenate([qn, qr, ckv, jnp.tile(kr, (LANES // MLA_ROPE, 1))], axis=0)


def kernel(x_prompt, x_sample, state_ret_fwd, state_ret_bwd, cache_mla_ckv, cache_mla_krope, cache_diff_k, cache_diff_v, c, c_ctx, ada_w, ada_b, ln1_g, ln1_b, ln2_g, ln2_b, ev_w_in, ev_w_out, ret_decay_fwd, ret_decay_bwd, mla_kv_norm_g, mla_w_uk, mla_w_uv, od_w_in, od_w_out, diff_lambda, diff_norm_g, moe_w_group, moe_b_group, moe_w_expert, moe_b_expert, moe_w1, moe_w3, moe_w2):
    x = (x_prompt.reshape(NTOK_P, D_MODEL), x_sample.reshape(NTOK_S, D_MODEL), 0)
    cond =jnp.concatenate([c_ctx[None, :], c, jnp.zeros((N_COND - 1 - DEC_BATCH, D_MODEL), F32)], axis=0)
    mods = _ada_all(cond, ada_w, ada_b).reshape(DEPTH, N_COND, 6, D_MODEL)

    cos_m, sin_m = _rope_tables(MLA_ROPE)
    cos_d, sin_d = _rope_tables(DIFF_DH)
    ident = MLA_PROJ_TM
    cos_m_id = jnp.concatenate([jnp.ones((ident, LANES), F32), cos_m], axis=0)
    sin_m_id = jnp.concatenate([jnp.zeros((ident, LANES), F32), sin_m], axis=0)
    decf = jnp.broadcast_to(ret_decay_fwd[:, :, None], ret_decay_fwd.shape + (LANES,))
    decb = jnp.broadcast_to(ret_decay_bwd[:, :, None], ret_decay_bwd.shape + (LANES,))

    ev_w_in_t = jnp.swapaxes(ev_w_in, 1, 2)
    pad = LANES - MOE_GROUPS - MOE_EXPERTS
    ret_states = mla_caches = diff_caches = None
    for i in range(DEPTH):
        j = i // 2
        mod = mods[i]
        w_router = jnp.concatenate([moe_w_group[i], moe_w_expert[i], jnp.zeros((D_MODEL, pad), F32)], axis=1)
        b_router = jnp.concatenate([moe_b_group[i], moe_b_expert[i], jnp.zeros((pad,), F32)])[None, :]
        if i % 2 == 0:
            zret = _mm_mod(x, mod, ev_w_in_t, (j,), RET_COLS, 1536, "in_proj_retention", w_is_transposed=True)
            *mla_caches, zmla, kcat, vm = _mla_proj_kv(x, mod, _mla_weight_t(ev_w_in_t, j), cos_m_id, sin_m_id,
                                                       mla_kv_norm_g, mla_w_uk, mla_w_uv, j, mla_caches)
            kr_ctx = jnp.tile(cache_mla_krope[:, j], (1, 1, LANES // MLA_ROPE))
            kcat_ctx, vm_ctx = _mla_kv_ctx(cache_mla_ckv, kr_ctx, mla_w_uk, mla_w_uv, j)
            a_ret_p, *ret_states = _retention(zret, decf, decb, j, False, prev=ret_states)
            (a_ret_s,) = _retention(zret, decf, decb, j, True, state_ret_fwd, state_ret_bwd)
            a_mla_p = _mla_attn_prompt(zmla, kcat, vm)
            a_mla_s = _mla_attn_latent(zmla, cos_m, sin_m, kcat_ctx, vm_ctx, kcat, vm)
            x1, meta, counts = _mm_ln([(a_ret_p, a_ret_s), (a_mla_p, a_mla_s)], ev_w_out, j, x, mod,
                                      ln1_g, ln1_b, w_router, b_router, i)
        else:
            lam_init = 0.8 - 0.6 * math.exp(-0.3 * i)
            zodd = _mm_mod(x, mod, od_w_in, (j,), 3072, 1536, "in_proj_diff")
            a_p, *diff_caches = _diff_attn_prompt(zodd, diff_lambda, diff_norm_g, j, lam_init, diff_caches)
            k_rot, v_aug = _diff_prep(zodd, cos_d, sin_d)
            a_s = _diff_attn_latent(zodd, diff_lambda, diff_norm_g, cos_d, sin_d, cache_diff_k, cache_diff_v,
                                    k_rot, v_aug, j, lam_init)
            x1, meta, counts = _mm_ln([(a_p, a_s)], od_w_out, j, x, mod, ln1_g, ln1_b, w_router, b_router, i)
        y = _moe(x1, _moe_plan(meta, counts), mods, w_router, b_router, moe_w1, moe_w3, moe_w2, ln2_g, ln2_b, i)
        x = (y, y, NTOK_P)

    y_prompt = x[0][:NTOK_P].reshape(BATCH, SEQ, D_MODEL)
    y_sample = x[1][NTOK_P:NTOK].reshape(DEC_BATCH, DEC_SEQ, D_MODEL)
    return (y_prompt, y_sample, ret_states[0], ret_states[1], mla_caches[0], mla_caches[1],
            diff_caches[0], diff_caches[1])
```

```python
import functools
import math

import jax
import jax.numpy as jnp
from jax import lax
from jax.experimental import pallas as pl
from jax.experimental.pallas import tpu as pltpu

D_MODEL = 1024
BATCH = 32
SEQ = 256
DEPTH = 4
N_EVEN = 2
N_ODD = 2
DEC_BATCH = 2
DEC_SEQ = 2048
PAST_LEN = 256
GRID_W = 64
LN_EPS = 1e-5
RMS_EPS = 1e-6
DEEPNORM_ALPHA = (2.0 * DEPTH) ** 0.25
ROPE_BASE = 10000.0
RET_HEADS = 8
RET_DK = 64
RET_DV = 128
MLA_HEADS = 8
MLA_NOPE = 64
MLA_ROPE = 32
MLA_DV = 64
MLA_KV_RANK = 256
DIFF_HEADS = 8
DIFF_DH = 64
DIFF_DV = 128
MOE_GROUPS = 4
MOE_PER_GROUP = 4
MOE_EXPERTS = 16
MOE_FF = 256

NTOK_P = BATCH * SEQ
NTOK_S = DEC_BATCH * DEC_SEQ
NTOK = NTOK_P + NTOK_S
N_COND = 8
LANES = 128
SUBLANES = 8
RET_COLS = 3072
MLA_COLS = 1152
MLA_PROJ_TM = 1024
ATT_TQ = 512
RET_CHUNK = 256
MOE_TILE = 512
MOE_TILES = (NTOK + MOE_GROUPS * (MOE_TILE - 1)) // MOE_TILE
MOE_ROWS = MOE_TILES * MOE_TILE
MOE_KEY_BASE = 16384
VMEM_LIMIT = 56 * 1024 * 1024
LOG2E = 1.4426950408889634

F32 = jnp.float32
BF16 = jnp.bfloat16
NT_DIMS = (((1,), (1,)), ((), ()))
TN_DIMS = (((0,), (0,)), ((), ()))


def _params(sem):
    return pltpu.CompilerParams(dimension_semantics=sem, vmem_limit_bytes=VMEM_LIMIT)


def _group_of_tile(i, tm):
    npt = NTOK_P // tm
    nst = DEC_SEQ // tm
    return jnp.where(i < npt, 0, 1 + (i - npt) // nst)


def _split_specs(tm, width, m_of, s_row0=0):
    npt = NTOK_P // tm
    s_blk0 = s_row0 // tm
    return [pl.BlockSpec((tm, width), lambda *g: (jnp.minimum(m_of(*g), npt - 1), 0)),
            pl.BlockSpec((tm, width), lambda *g: (jnp.maximum(m_of(*g) - npt, 0) + s_blk0, 0))]


def _read_split(p_ref, s_ref, m):
    return jnp.where(m < NTOK_P // p_ref.shape[0], p_ref[...], s_ref[...])


def _silu(x):
    return x * (1.0 / (1.0 + jnp.exp(-x)))


def _layer_norm(r, g, b):
    mu = jnp.mean(r, axis=-1, keepdims=True)
    d = r - mu
    var = jnp.mean(d * d, axis=-1, keepdims=True)
    return d * lax.rsqrt(var + LN_EPS) * g + b


def _lane_iota(shape):
    return lax.broadcasted_iota(jnp.int32, shape, 1)


def _div_pow2(x, d):
    assert d & (d - 1) == 0
    return jnp.right_shift(x, d.bit_length() - 1)


def _mod_pow2(x, d):
    assert d & (d - 1) == 0
    return jnp.bitwise_and(x, d - 1)


def _ada_kernel(c_ref, w_ref, b_ref, o_ref):
    h = _silu(c_ref[...]).astype(BF16)
    o_ref[...] = jnp.dot(h, w_ref[...].astype(BF16), preferred_element_type=F32) + b_ref[...]


def _ada_all(cond, ada_w, ada_b):
    tn = 768
    return pl.pallas_call(
        _ada_kernel,
        grid=(DEPTH, 6 * D_MODEL // tn),
        in_specs=[
            pl.BlockSpec((N_COND, D_MODEL), lambda l, n: (0, 0)),
            pl.BlockSpec((None, D_MODEL, tn), lambda l, n: (l, 0, n)),
            pl.BlockSpec((None, 1, tn), lambda l, n: (l, 0, n)),
        ],
        out_specs=pl.BlockSpec((None, N_COND, tn), lambda l, n: (l, 0, n)),
        out_shape=jax.ShapeDtypeStruct((DEPTH, N_COND, 6 * D_MODEL), F32),
        compiler_params=_params(("parallel", "parallel")),
        name="ada_modulation",
    )(cond, ada_w, ada_b.reshape(DEPTH, 1, 6 * D_MODEL))


def _mm_mod_kernel(w_is_transposed, xp_ref, xs_ref, mod_ref, w_ref, o_ref, wscr):
    m = pl.program_id(1)

    @pl.when(m == 0)
    def _():
        wscr[...] = w_ref[...].astype(BF16)

    sh = mod_ref[0:1, :]
    sc = mod_ref[1:2, :]
    xm = (_read_split(xp_ref, xs_ref, m) * (1.0 + sc) + sh).astype(BF16)
    if w_is_transposed:
        z = lax.dot_general(xm, wscr[...], NT_DIMS, preferred_element_type=F32)
    else:
        z = jnp.dot(xm, wscr[...], preferred_element_type=F32)
    o_ref[...] = z.astype(o_ref.dtype)


def _mm_mod(x, mod, w, w_index, n_cols, tn, name, w_is_transposed=False):
    tm = 1024
    if w_is_transposed:
        w_spec = pl.BlockSpec((None,) * len(w_index) + (tn, D_MODEL), lambda n, m: tuple(w_index) + (n, 0))
        w_scratch = pltpu.VMEM((tn, D_MODEL), BF16)
    else:
        w_spec = pl.BlockSpec((None,) * len(w_index) + (D_MODEL, tn), lambda n, m: tuple(w_index) + (0, n))
        w_scratch = pltpu.VMEM((D_MODEL, tn), BF16)
    return pl.pallas_call(
        functools.partial(_mm_mod_kernel, w_is_transposed),
        grid=(n_cols // tn, NTOK // tm),
        in_specs=_split_specs(tm, D_MODEL, lambda n, m: m, x[2]) + [
            pl.BlockSpec((None, 6, D_MODEL), lambda n, m: (_group_of_tile(m, tm), 0, 0)),
            w_spec,
        ],
        out_specs=pl.BlockSpec((tm, tn), lambda n, m: (m, n)),
        out_shape=jax.ShapeDtypeStruct((NTOK, n_cols), BF16),
        scratch_shapes=[w_scratch],
        compiler_params=_params(("arbitrary", "arbitrary")),
        name=name,
    )(x[0], x[1], mod, w)


def _router_probs(xm, wr_ref, br_ref):
    rows = xm.shape[0]
    z = jnp.dot(xm, wr_ref[...].astype(BF16), preferred_element_type=F32) + br_ref[...]
    lane_i = _lane_iota((rows, LANES))
    lane = lane_i.astype(F32)
    gmask = lane_i < MOE_GROUPS
    zg = jnp.where(gmask, z, -jnp.inf)
    pg = jnp.exp(zg - jnp.max(zg, axis=-1, keepdims=True))
    g_prob = pg / jnp.sum(pg, axis=-1, keepdims=True)
    g_p = jnp.max(g_prob, axis=-1, keepdims=True)
    g_idx = jnp.min(jnp.where(gmask & (g_prob == g_p), lane, float(LANES)), axis=-1, keepdims=True)
    return z, lane_i, lane, g_p, g_idx


def _mm_ln_kernel(k_sizes, *refs):
    n_a = len(k_sizes)
    a_refs = refs[:2 * n_a]
    (w_ref, xp_ref, xs_ref, mod_ref, g_ref, b_ref, wr_ref, br_ref,
     o_ref, meta_ref, cnt_ref, wscr, tri_scr, carry_scr) = refs[2 * n_a:]
    m = pl.program_id(0)
    tm = o_ref.shape[0]

    @pl.when(m == 0)
    def _():
        wscr[...] = w_ref[...].astype(BF16)
        ri = lax.broadcasted_iota(jnp.int32, (tm, tm), 0)
        ci = lax.broadcasted_iota(jnp.int32, (tm, tm), 1)
        tri_scr[...] = jnp.where(ci < ri, 1.0, 0.0).astype(BF16)
        carry_scr[...] = jnp.zeros_like(carry_scr)

    y = None
    k0 = 0
    for i, ks in enumerate(k_sizes):
        a = _read_split(a_refs[2 * i], a_refs[2 * i + 1], m)
        part = jnp.dot(a, wscr[k0:k0 + ks, :], preferred_element_type=F32)
        y = part if y is None else y + part
        k0 += ks
    gate = mod_ref[2:3, :]
    r = DEEPNORM_ALPHA * _read_split(xp_ref, xs_ref, m) + gate * y
    x1 = _layer_norm(r, g_ref[...], b_ref[...])
    o_ref[...] = x1

    xm = (x1 * (1.0 + mod_ref[4:5, :]) + mod_ref[3:4, :]).astype(BF16)
    _, lane_i, lane, _, g_idx = _router_probs(xm, wr_ref, br_ref)
    onehot = jnp.where(lane == g_idx, 1.0, 0.0)
    before = jnp.dot(tri_scr[...], onehot.astype(BF16), preferred_element_type=F32) + carry_scr[0:1, :]
    rank = jnp.sum(jnp.where(lane == g_idx, before, 0.0), axis=-1, keepdims=True)
    key_col = g_idx * float(MOE_KEY_BASE) + rank
    row_i = lax.broadcasted_iota(jnp.int32, (tm, LANES), 0)
    diag = jnp.where(lane_i == _mod_pow2(row_i, LANES), key_col, 0.0)
    meta_ref[...] = jnp.sum(diag.reshape(tm // LANES, LANES, LANES), axis=1).astype(jnp.int32)
    total = carry_scr[0:1, :] + jnp.sum(onehot, axis=0, keepdims=True)
    carry_scr[...] = jnp.broadcast_to(total, carry_scr.shape)
    cnt_ref[...] = jnp.broadcast_to(total, cnt_ref.shape)


def _mm_ln(a_pairs, w, j, x, mod, ln_g, ln_b, w_router, b_router, layer):
    tm = 512
    k_sizes = tuple(ap.shape[1] for ap, _ in a_pairs)
    k_tot = sum(k_sizes)
    in_specs = []
    args = []
    for (ap, a_s), ks in zip(a_pairs, k_sizes):
        in_specs += _split_specs(tm, ks, lambda m: m)
        args += [ap, a_s]
    in_specs += [pl.BlockSpec((None, k_tot, D_MODEL), lambda m: (j, 0, 0))]
    in_specs += _split_specs(tm, D_MODEL, lambda m: m, x[2])
    in_specs += [
        pl.BlockSpec((None, 6, D_MODEL), lambda m: (_group_of_tile(m, tm), 0, 0)),
        pl.BlockSpec((None, 1, D_MODEL), lambda m: (layer, 0, 0)),
        pl.BlockSpec((None, 1, D_MODEL), lambda m: (layer, 0, 0)),
        pl.BlockSpec((D_MODEL, LANES), lambda m: (0, 0)),
        pl.BlockSpec((1, LANES), lambda m: (0, 0)),
    ]
    return pl.pallas_call(
        functools.partial(_mm_ln_kernel, k_sizes),
        grid=(NTOK // tm,),
        in_specs=in_specs,
        out_specs=[
            pl.BlockSpec((tm, D_MODEL), lambda m: (m, 0)),
            pl.BlockSpec((None, tm // LANES, LANES), lambda m: (m, 0, 0)),
            pl.BlockSpec((N_COND, LANES), lambda m: (0, 0)),
        ],
        out_shape=[
            jax.ShapeDtypeStruct((NTOK, D_MODEL), F32),
            jax.ShapeDtypeStruct((NTOK // tm, tm // LANES, LANES), jnp.int32),
            jax.ShapeDtypeStruct((N_COND, LANES), F32),
        ],
        scratch_shapes=[
            pltpu.VMEM((k_tot, D_MODEL), BF16),
            pltpu.VMEM((tm, tm), BF16),
            pltpu.VMEM((N_COND, LANES), F32),
        ],
        compiler_params=_params(("arbitrary",)),
        name="out_proj_ln",
    )(*args, w, x[0], x[1], mod, ln_g.reshape(DEPTH, 1, D_MODEL), ln_b.reshape(DEPTH, 1, D_MODEL),
      w_router, b_router)


def _moe_kernel(tgrp_ref, ntile_ref, src0_ref, src1_ref, dst_ref, mid_ref, x_hbm, mod_ref, wr_ref, br_ref,
                w1_ref, w3_ref, w2_ref, g_ref, b_ref, y_hbm,
                gbuf, obuf, gsem, ssem, w13s, w2s):
    i = pl.program_id(0)
    n_steps = pl.num_programs(0)
    n_tiles = ntile_ref[0]
    n_blk = gbuf.shape[1]
    ts = n_blk * SUBLANES
    slot = lax.rem(i, 2)

    def start_gather(src_ref, s):
        def body(k, carry):
            for u in range(SUBLANES):
                tok = src_ref[0, k * SUBLANES + u]
                pltpu.make_async_copy(x_hbm.at[pl.ds(tok, 1), :], gbuf.at[s, k, pl.ds(u, 1), :],
                                      gsem.at[s]).start()
            return carry
        lax.fori_loop(0, n_blk, body, 0)

    def wait_gather(s):
        pltpu.make_async_copy(gbuf.at[s], gbuf.at[s], gsem.at[s]).wait()

    def start_scatter(s):
        def body(k, carry):
            for u in range(SUBLANES):
                tok = dst_ref[0, k * SUBLANES + u]
                pltpu.make_async_copy(obuf.at[s, k, pl.ds(u, 1), :], y_hbm.at[pl.ds(tok, 1), :],
                                      ssem.at[s]).start()
            return carry
        lax.fori_loop(0, n_blk, body, 0)

    def wait_scatter(s):
        pltpu.make_async_copy(obuf.at[s], obuf.at[s], ssem.at[s]).wait()

    @pl.when(i == 0)
    def _():
        start_gather(src0_ref, 0)

    @pl.when(i < n_tiles)
    def _():
        grp = tgrp_ref[i]
        wait_gather(slot)

        @pl.when(i + 1 < n_tiles)
        def _():
            start_gather(src1_ref, 1 - slot)

        @pl.when((i == 0) | (grp != tgrp_ref[jnp.maximum(i - 1, 0)]))
        def _():
            w13s[:, :, :MOE_FF] = w1_ref[...].astype(BF16)
            w13s[:, :, MOE_FF:] = w3_ref[...].astype(BF16)
            w2s[...] = w2_ref[...].astype(BF16)

        mid_rows = mid_ref[...].astype(F32)
        spread = jnp.concatenate([jnp.broadcast_to(mid_rows[a:a + 1, :], (LANES, LANES))
                                  for a in range(ts // LANES)], axis=0)
        row_i = lax.broadcasted_iota(jnp.int32, (ts, LANES), 0)
        mid = jnp.sum(jnp.where(_lane_iota((ts, LANES)) == _mod_pow2(row_i, LANES), spread, 0.0),
                      axis=-1, keepdims=True)

        def mod_row(k):
            return jnp.where(mid == 0, mod_ref[0, k:k + 1, :],
                             jnp.where(mid == 1, mod_ref[1, k:k + 1, :], mod_ref[2, k:k + 1, :]))

        x1 = gbuf[slot].reshape(ts, D_MODEL)
        xm = (x1 * (1.0 + mod_row(4)) + mod_row(3)).astype(BF16)
        z, lane_i, lane, g_p, _ = _router_probs(xm, wr_ref, br_ref)
        e0 = MOE_GROUPS + MOE_PER_GROUP * grp
        emask = (lane_i >= e0) & (lane_i < e0 + MOE_PER_GROUP)
        ze = jnp.where(emask, z, -jnp.inf)
        pe = jnp.exp(ze - jnp.max(ze, axis=-1, keepdims=True))
        e_prob = pe / jnp.sum(pe, axis=-1, keepdims=True)
        cand = jnp.where(emask, e_prob, -1.0)
        p1 = jnp.max(cand, axis=-1, keepdims=True)
        i1 = jnp.min(jnp.where(cand == p1, lane, float(LANES)), axis=-1, keepdims=True)
        cand2 = jnp.where(lane == i1, -1.0, cand)
        p2 = jnp.max(cand2, axis=-1, keepdims=True)
        i2 = jnp.min(jnp.where(cand2 == p2, lane, float(LANES)), axis=-1, keepdims=True)
        denom = p1 + p2
        comb = jnp.where(lane == i1, g_p * p1 / denom, 0.0) + jnp.where(lane == i2, g_p * p2 / denom, 0.0)
        y = None
        for e in range(MOE_PER_GROUP):
            c = jnp.sum(jnp.where(lane_i == e0 + e, comb, 0.0), axis=-1, keepdims=True)
            h = jnp.dot(xm, w13s[e], preferred_element_type=F32)
            hid = (_silu(h[:, :MOE_FF]) * h[:, MOE_FF:] * c).astype(BF16)
            part = jnp.dot(hid, w2s[e], preferred_element_type=F32)
            y = part if y is None else y + part
        r = DEEPNORM_ALPHA * x1 + mod_row(5) * y

        @pl.when(i >= 2)
        def _():
            wait_scatter(slot)

        obuf[slot] = _layer_norm(r, g_ref[...], b_ref[...]).reshape(n_blk, SUBLANES, D_MODEL)
        start_scatter(slot)

    @pl.when(i == n_steps - 1)
    def _():
        @pl.when(n_tiles >= 2)
        def _():
            wait_scatter(lax.rem(n_tiles, 2))

        @pl.when(n_tiles >= 1)
        def _():
            wait_scatter(lax.rem(n_tiles + 1, 2))


def _inverse_perm_kernel(pos_ref, pad_ref, out_ref):
    def clear(s, carry):
        out_ref[s] = 0
        return carry

    def place(t, carry):
        out_ref[pos_ref[t]] = t + 1
        return carry

    for run in range(pad_ref.shape[0] // 2):
        lax.fori_loop(pad_ref[2 * run], pad_ref[2 * run + 1], clear, 0)
    lax.fori_loop(0, pos_ref.shape[0], place, 0, unroll=8)


def _inverse_perm(pos, pad_runs):
    return pl.pallas_call(
        _inverse_perm_kernel,
        in_specs=[pl.BlockSpec(memory_space=pltpu.SMEM)] * 2,
        out_specs=pl.BlockSpec(memory_space=pltpu.SMEM),
        out_shape=jax.ShapeDtypeStruct((MOE_ROWS,), jnp.int32),
        name="moe_inverse_perm",
    )(pos, pad_runs)


def _moe_plan(meta, counts):
    ts = MOE_TILE
    cnt = counts[0, :MOE_GROUPS].astype(jnp.int32)
    tiles_g = (cnt + ts - 1) // ts
    tile_end = jnp.cumsum(tiles_g)
    row0_g = (tile_end - tiles_g) * ts
    keys = meta.reshape(NTOK)
    gid = keys // MOE_KEY_BASE
    rank = keys % MOE_KEY_BASE
    pos = row0_g[gid] + rank
    pad_runs = jnp.stack([row0_g + cnt, tile_end * ts], axis=1).reshape(-1)
    pad_runs = jnp.concatenate([pad_runs, tile_end[-1:] * ts, jnp.full((1,), MOE_ROWS, jnp.int32)])
    tok1 = _inverse_perm(pos, pad_runs.astype(jnp.int32))
    rows = jnp.arange(MOE_ROWS, dtype=jnp.int32)
    src = jnp.maximum(tok1 - 1, 0)
    spare = NTOK + ((rows // ts) % 2) * ts + rows % ts
    dst = jnp.where(tok1 > 0, src, spare)
    mid = jnp.where(src < NTOK_P, 0, 1 + (src - NTOK_P) // DEC_SEQ)
    tile_group = jnp.minimum(jnp.sum(jnp.arange(MOE_TILES)[:, None] >= tile_end[None, :], axis=1),
                             MOE_GROUPS - 1).astype(jnp.int32)
    n_tiles = tile_end[-1:].astype(jnp.int32)
    return (tile_group, n_tiles, src.reshape(MOE_TILES, 1, ts), dst.reshape(MOE_TILES, 1, ts),
            mid.reshape(MOE_TILES, ts // LANES, LANES))


def _moe(x1, plan, mods, w_router, b_router, w1, w3, w2, ln_g, ln_b, layer):
    ts = MOE_TILE
    tile_group, n_tiles, src, dst, mid = plan
    grp_shape = (DEPTH, MOE_GROUPS, MOE_PER_GROUP)
    w_in_spec = pl.BlockSpec((None, None, MOE_PER_GROUP, D_MODEL, MOE_FF), lambda i, tg, nt: (layer, tg[i], 0, 0, 0))
    w_out_spec = pl.BlockSpec((None, None, MOE_PER_GROUP, MOE_FF, D_MODEL), lambda i, tg, nt: (layer, tg[i], 0, 0, 0))
    smem_tile = functools.partial(pl.BlockSpec, (None, 1, ts), memory_space=pltpu.SMEM)
    grid_spec = pltpu.PrefetchScalarGridSpec(
        num_scalar_prefetch=2,
        grid=(MOE_TILES,),
        in_specs=[
            smem_tile(lambda i, tg, nt: (i, 0, 0)),
            smem_tile(lambda i, tg, nt: (jnp.minimum(i + 1, MOE_TILES - 1), 0, 0)),
            smem_tile(lambda i, tg, nt: (i, 0, 0)),
            pl.BlockSpec((None, ts // LANES, LANES), lambda i, tg, nt: (i, 0, 0)),
            pl.BlockSpec(memory_space=pl.ANY),
            pl.BlockSpec((None, N_COND, 6, D_MODEL), lambda i, tg, nt: (layer, 0, 0, 0)),
            pl.BlockSpec((D_MODEL, LANES), lambda i, tg, nt: (0, 0)),
            pl.BlockSpec((1, LANES), lambda i, tg, nt: (0, 0)),
            w_in_spec,
            w_in_spec,
            w_out_spec,
            pl.BlockSpec((None, 1, D_MODEL), lambda i, tg, nt: (layer, 0, 0)),
            pl.BlockSpec((None, 1, D_MODEL), lambda i, tg, nt: (layer, 0, 0)),
        ],
        out_specs=pl.BlockSpec(memory_space=pl.ANY),
        scratch_shapes=[
            pltpu.VMEM((2, ts // SUBLANES, SUBLANES, D_MODEL), F32),
            pltpu.VMEM((2, ts // SUBLANES, SUBLANES, D_MODEL), F32),
            pltpu.SemaphoreType.DMA((2,)),
            pltpu.SemaphoreType.DMA((2,)),
            pltpu.VMEM((MOE_PER_GROUP, D_MODEL, 2 * MOE_FF), BF16),
            pltpu.VMEM((MOE_PER_GROUP, MOE_FF, D_MODEL), BF16),
        ],
    )
    return pl.pallas_call(
        _moe_kernel,
        grid_spec=grid_spec,
        out_shape=jax.ShapeDtypeStruct((NTOK + 2 * ts, D_MODEL), F32),
        compiler_params=_params(("arbitrary",)),
        name="hier_moe_ln",
    )(tile_group, n_tiles, src, src, dst, mid, x1, mods, w_router, b_router,
      w1.reshape(grp_shape + (D_MODEL, MOE_FF)), w3.reshape(grp_shape + (D_MODEL, MOE_FF)),
      w2.reshape(grp_shape + (MOE_FF, D_MODEL)),
      ln_g.reshape(DEPTH, 1, D_MODEL), ln_b.reshape(DEPTH, 1, D_MODEL))


def _swap_halves(x, lane, half):
    return jnp.where(_mod_pow2(lane, 2 * half) < half,
                     pltpu.roll(x, LANES - half, 1), pltpu.roll(x, half, 1))


def _rope128(x, cos, sin_signed, lane, half):
    return x * cos + _swap_halves(x, lane, half) * sin_signed


def _rope_tables(rot_dim):
    rows = DEC_SEQ // GRID_W
    row = jnp.repeat(jnp.arange(rows, dtype=F32), GRID_W)
    col = jnp.tile(jnp.arange(GRID_W, dtype=F32), rows)
    n_freq = rot_dim // 4
    inv_freq = ROPE_BASE ** (-jnp.arange(n_freq, dtype=F32) / n_freq)
    ang = jnp.concatenate([row[:, None] * inv_freq, col[:, None] * inv_freq], axis=-1)
    cos, sin = jnp.cos(ang), jnp.sin(ang)
    reps = LANES // rot_dim
    cos_full = jnp.tile(jnp.concatenate([cos, cos], axis=-1), (1, reps))
    sin_signed = jnp.tile(jnp.concatenate([-sin, sin], axis=-1), (1, reps))
    return cos_full, sin_signed


def _exp_parts(s_list, scale):
    m = None
    for s in s_list:
        sm = jnp.max(s, axis=-1, keepdims=True)
        m = sm if m is None else jnp.maximum(m, sm)
    return [jnp.exp2((s - m) * (scale * LOG2E)).astype(BF16) for s in s_list]


def _pv_normalised(p_list, v_list):
    o = None
    for p, v in zip(p_list, v_list):
        part = jnp.dot(p, v, preferred_element_type=F32)
        o = part if o is None else o + part
    return o[:, :LANES] / o[:, LANES:]


def _mla_kv_kernel(new_tokens, *refs):
    if new_tokens:
        (xp_ref, xs_ref, mod_ref, w_ref, cos_ref, sin_ref, g_ref, wuk_ref, wuv_ref) = refs[:9]
        ckvc_ref, krc_ref, zq_ref, kcat_ref, vm_ref, wscr, wuk_scr, wuv_scr = refs[-8:]
        i = pl.program_id(0)

        @pl.when(i == 0)
        def _():
            wscr[...] = w_ref[...].astype(BF16)
            wuk_scr[...] = wuk_ref[...].astype(BF16)
            wuv_scr[...] = wuv_ref[...].astype(BF16)

        xm = (_read_split(xp_ref, xs_ref, i) * (1.0 + mod_ref[1:2, :]) + mod_ref[0:1, :]).astype(BF16)
        z = lax.dot_general(xm, wscr[...], NT_DIMS, preferred_element_type=F32)
        n_q = MLA_HEADS * (MLA_NOPE + MLA_ROPE)
        zq_ref[...] = z[:, :n_q].astype(BF16)
        x = z[:, n_q:n_q + MLA_KV_RANK]
        c = x * lax.rsqrt(jnp.mean(x * x, axis=-1, keepdims=True) + RMS_EPS) * g_ref[...]
        kr_raw = z[:, n_q + MLA_KV_RANK:]

        @pl.when(i < NTOK_P // zq_ref.shape[0])
        def _():
            for b in range(ckvc_ref.shape[0]):
                ckvc_ref[b] = c[SEQ * b:SEQ * (b + 1), :]
                krc_ref[b] = kr_raw[SEQ * b:SEQ * (b + 1), :MLA_ROPE]

        lane = _lane_iota(kr_raw.shape)
        kr = _rope128(kr_raw, cos_ref[...], sin_ref[...], lane, MLA_ROPE // 2)
        wuk = wuk_scr[...]
        wuv = wuv_scr[...]
    else:
        ckv_ref, kr_ref, wuk_ref, wuv_ref, kcat_ref, vm_ref = refs
        c = ckv_ref[...]
        kr = kr_ref[...]
        wuk = wuk_ref[...].astype(BF16)
        wuv = wuv_ref[...].astype(BF16)
    cb = c.astype(BF16)
    kn = jnp.dot(cb, wuk, preferred_element_type=F32).astype(BF16)
    vv = jnp.dot(cb, wuv, preferred_element_type=F32).astype(BF16)
    krb = kr.astype(BF16)
    ones = jnp.ones((c.shape[0], LANES), BF16)
    for p in range(MLA_HEADS // 2):
        kcat_ref[:, 256 * p:256 * p + LANES] = kn[:, LANES * p:LANES * (p + 1)]
        kcat_ref[:, 256 * p + LANES:256 * (p + 1)] = krb
        vm_ref[:, 256 * p:256 * p + LANES] = vv[:, LANES * p:LANES * (p + 1)]
        vm_ref[:, 256 * p + LANES:256 * (p + 1)] = ones


def _mla_proj_kv(x, mod, w_t, cos_t, sin_t, kv_norm_g, w_uk, w_uv, j, prev):
    tm = MLA_PROJ_TM
    npt = NTOK_P // tm
    nst = DEC_SEQ // tm
    nb = tm // SEQ
    n_q = MLA_HEADS * (MLA_NOPE + MLA_ROPE)

    def tab(i):
        return (jnp.where(i < npt, 0, 1 + (i - npt) % nst), 0)

    def cache_idx(i):
        return (jnp.minimum(i, npt - 1), j, 0, 0)

    in_specs = _split_specs(tm, D_MODEL, lambda i: i, x[2]) + [
        pl.BlockSpec((None, 6, D_MODEL), lambda i: (_group_of_tile(i, tm), 0, 0)),
        pl.BlockSpec((MLA_COLS, D_MODEL), lambda i: (0, 0)),
        pl.BlockSpec((tm, LANES), tab),
        pl.BlockSpec((tm, LANES), tab),
        pl.BlockSpec((None, 1, MLA_KV_RANK), lambda i: (j, 0, 0)),
        pl.BlockSpec((None, MLA_KV_RANK, 512), lambda i: (j, 0, 0)),
        pl.BlockSpec((None, MLA_KV_RANK, 512), lambda i: (j, 0, 0)),
    ]
    args = [x[0], x[1], mod, w_t, cos_t, sin_t, kv_norm_g.reshape(-1, 1, MLA_KV_RANK), w_uk, w_uv]
    aliases = {}
    if prev is not None:
        aliases = {len(args): 0, len(args) + 1: 1}
        in_specs += [pl.BlockSpec(memory_space=pl.ANY)] * 2
        args += list(prev)
    return pl.pallas_call(
        functools.partial(_mla_kv_kernel, True),
        grid=(NTOK // tm,),
        in_specs=in_specs,
        out_specs=[
            pl.BlockSpec((nb, None, SEQ, MLA_KV_RANK), cache_idx),
            pl.BlockSpec((nb, None, SEQ, MLA_ROPE), cache_idx),
            pl.BlockSpec((tm, n_q), lambda i: (i, 0)),
            pl.BlockSpec((tm, 1024), lambda i: (i, 0)),
            pl.BlockSpec((tm, 1024), lambda i: (i, 0)),
        ],
        out_shape=[
            jax.ShapeDtypeStruct((BATCH, N_EVEN, SEQ, MLA_KV_RANK), F32),
            jax.ShapeDtypeStruct((BATCH, N_EVEN, SEQ, MLA_ROPE), F32),
            jax.ShapeDtypeStruct((NTOK, n_q), BF16),
            jax.ShapeDtypeStruct((NTOK, 1024), BF16),
            jax.ShapeDtypeStruct((NTOK, 1024), BF16),
        ],
        scratch_shapes=[
            pltpu.VMEM((MLA_COLS, D_MODEL), BF16),
            pltpu.VMEM((MLA_KV_RANK, 512), BF16),
            pltpu.VMEM((MLA_KV_RANK, 512), BF16),
        ],
        input_output_aliases=aliases,
        compiler_params=_params(("arbitrary",)),
        name="mla_proj_kv",
    )(*args)


def _mla_kv_ctx(cache_ckv, kr_tiled, w_uk, w_uv, j):
    return pl.pallas_call(
        functools.partial(_mla_kv_kernel, False),
        grid=(DEC_BATCH,),
        in_specs=[
            pl.BlockSpec((None, None, PAST_LEN, MLA_KV_RANK), lambda b: (b, j, 0, 0)),
            pl.BlockSpec((None, PAST_LEN, LANES), lambda b: (b, 0, 0)),
            pl.BlockSpec((None, MLA_KV_RANK, 512), lambda b: (j, 0, 0)),
            pl.BlockSpec((None, MLA_KV_RANK, 512), lambda b: (j, 0, 0)),
        ],
        out_specs=[
            pl.BlockSpec((PAST_LEN, 1024), lambda b: (b, 0)),
            pl.BlockSpec((PAST_LEN, 1024), lambda b: (b, 0)),
        ],
        out_shape=[
            jax.ShapeDtypeStruct((DEC_BATCH * PAST_LEN, 1024), BF16),
            jax.ShapeDtypeStruct((DEC_BATCH * PAST_LEN, 1024), BF16),
        ],
        compiler_params=_params(("parallel",)),
        name="mla_kv_ctx",
    )(cache_ckv, kr_tiled, w_uk, w_uv)


def _mla_attn_kernel(latent, *refs):
    if latent:
        qn_ref, qr_ref, cos_ref, sin_ref, kc_ref, vc_ref, kn_ref, vn_ref, o_ref = refs
        k_refs, v_refs = (kc_ref, kn_ref), (vc_ref, vn_ref)
    else:
        qn_ref, qr_ref, kn_ref, vn_ref, o_ref = refs
        k_refs, v_refs = (kn_ref,), (vn_ref,)
    tq = ATT_TQ if latent else SEQ
    n_seq = qn_ref.shape[0] // tq
    lane = _lane_iota((tq, LANES))
    scale = (MLA_NOPE + MLA_ROPE) ** -0.5
    for sq in range(n_seq):
        rows = slice(tq * sq, tq * (sq + 1))
        krows = slice(None) if latent else rows
        qr_cols = []
        for cidx in range(2):
            x = qr_ref[rows, LANES * cidx:LANES * (cidx + 1)].astype(F32)
            if latent:
                x = _rope128(x, cos_ref[...], sin_ref[...], lane, MLA_ROPE // 2)
            qr_cols.append(x)
        o_prev = None
        for h in range(MLA_HEADS):
            p, half = divmod(h, 2)
            cidx, slot = divmod(h, 4)
            qa = jnp.where(_div_pow2(lane, MLA_NOPE) == half,
                           qn_ref[rows, LANES * p:LANES * (p + 1)].astype(F32), 0.0)
            qb = jnp.where(_div_pow2(lane, MLA_ROPE) == slot, qr_cols[cidx], 0.0)
            qcat = jnp.concatenate([qa, qb], axis=1).astype(BF16)
            pair_cols = slice(256 * p, 256 * (p + 1))
            s_list = [lax.dot_general(qcat, k_ref[krows, pair_cols], NT_DIMS, preferred_element_type=F32)
                      for k_ref in k_refs]
            o = _pv_normalised(_exp_parts(s_list, scale), [v_ref[krows, pair_cols] for v_ref in v_refs])
            if half == 0:
                o_prev = o
            else:
                o_ref[rows, LANES * p:LANES * (p + 1)] = jnp.where(lane < MLA_DV, o_prev, o).astype(BF16)


def _mla_attn_prompt(zmla, kcat, vm):
    rows = 4 * SEQ
    return pl.pallas_call(
        functools.partial(_mla_attn_kernel, False),
        grid=(NTOK_P // rows,),
        in_specs=[
            pl.BlockSpec((rows, 512), lambda b: (b, 0)),
            pl.BlockSpec((rows, 256), lambda b: (b, 2)),
            pl.BlockSpec((rows, 1024), lambda b: (b, 0)),
            pl.BlockSpec((rows, 1024), lambda b: (b, 0)),
        ],
        out_specs=pl.BlockSpec((rows, 512), lambda b: (b, 0)),
        out_shape=jax.ShapeDtypeStruct((NTOK_P, 512), BF16),
        compiler_params=_params(("parallel",)),
        name="mla_attn_prompt",
    )(zmla, zmla, kcat, vm)


def _mla_attn_latent(zmla, cos_t, sin_t, kcat_ctx, vm_ctx, kcat, vm):
    nq = DEC_SEQ // ATT_TQ
    row0 = NTOK_P // ATT_TQ
    seq0 = NTOK_P // DEC_SEQ
    return pl.pallas_call(
        functools.partial(_mla_attn_kernel, True),
        grid=(DEC_BATCH, nq),
        in_specs=[
            pl.BlockSpec((ATT_TQ, 512), lambda b, q: (row0 + b * nq + q, 0)),
            pl.BlockSpec((ATT_TQ, 256), lambda b, q: (row0 + b * nq + q, 2)),
            pl.BlockSpec((ATT_TQ, LANES), lambda b, q: (q, 0)),
            pl.BlockSpec((ATT_TQ, LANES), lambda b, q: (q, 0)),
            pl.BlockSpec((PAST_LEN, 1024), lambda b, q: (b, 0)),
            pl.BlockSpec((PAST_LEN, 1024), lambda b, q: (b, 0)),
            pl.BlockSpec((DEC_SEQ, 1024), lambda b, q: (seq0 + b, 0)),
            pl.BlockSpec((DEC_SEQ, 1024), lambda b, q: (seq0 + b, 0)),
        ],
        out_specs=pl.BlockSpec((ATT_TQ, 512), lambda b, q: (b * nq + q, 0)),
        out_shape=jax.ShapeDtypeStruct((NTOK_S, 512), BF16),
        compiler_params=_params(("parallel", "arbitrary")),
        name="mla_attn_latent",
    )(zmla, zmla, cos_t, sin_t, kcat_ctx, vm_ctx, kcat, vm)


def _log_sigmoid(x):
    return jnp.minimum(x, 0.0) - jnp.log1p(jnp.exp(-jnp.abs(x)))


def _retention_kernel(seq_len, n_seq, has_init, emit_state, has_prev, *refs):
    refs = list(refs)
    decf_ref, decb_ref, q_ref, k_ref, v_ref, g_ref = refs[:6]
    refs = refs[6:]
    if has_init:
        sf0_ref, sb0_ref = refs[:2]
        refs = refs[2:]
    if has_prev:
        refs = refs[2:]
    o_ref = refs[0]
    refs = refs[1:]
    if emit_state:
        sf_ref, sb_ref = refs[:2]
        refs = refs[2:]
    of_scr, ob_scr, dec_scr, wts_scr = refs

    c = RET_CHUNK
    n_chunks = seq_len // c
    pair = pl.program_id(0)
    lane = _lane_iota((c, LANES))
    zeros_half = jnp.zeros((RET_DK, RET_DV), F32)

    def log_gammas(half):
        head = 2 * pair + half
        return (_log_sigmoid(decf_ref[pl.ds(head, 1), :]),
                _log_sigmoid(decb_ref[pl.ds(head, 1), :]))

    @pl.when(pl.program_id(1) == 0)
    def _():
        ri = lax.broadcasted_iota(jnp.int32, (c, c), 0)
        ci = lax.broadcasted_iota(jnp.int32, (c, c), 1)
        rel = (ri - ci).astype(F32)
        row = lax.broadcasted_iota(jnp.int32, (c, LANES), 0).astype(F32)
        for half in range(2):
            lgf, lgb = log_gammas(half)
            dec_scr[half] = (jnp.where(rel >= 0, jnp.exp(lgf[:, 0:1] * jnp.maximum(rel, 0.0)), 0.0)
                             + jnp.where(rel <= 0, jnp.exp(lgb[:, 0:1] * jnp.maximum(-rel, 0.0)), 0.0))
            wts_scr[half, 0] = jnp.exp(lgf * (row + 1.0))
            wts_scr[half, 1] = jnp.exp(lgf * (c - 1.0 - row))
            wts_scr[half, 2] = jnp.exp(lgb * (c - row))
            wts_scr[half, 3] = jnp.exp(lgb * row)

    cross = has_init or n_chunks > 1
    chains = [(sq, half) for sq in range(n_seq) for half in range(2)]
    chunk_decay = []
    for half in range(2):
        lgf, lgb = log_gammas(half)
        chunk_decay.append((jnp.exp(lgf * float(c)), jnp.exp(lgb * float(c))))

    def rows_of(sq, n):
        start = sq * seq_len + n * c
        return pl.ds(start if isinstance(n, int) else pl.multiple_of(start, c), c)

    def load(sq, half, n):
        rows = rows_of(sq, n)
        vsl = slice(RET_DV * half, RET_DV * (half + 1))
        qm = jnp.where(_div_pow2(lane, RET_DK) == half, q_ref[rows, :].astype(F32), 0.0)
        kk = k_ref[rows, :].astype(F32) * (RET_DK ** -0.5)
        return rows, vsl, qm, kk, v_ref[rows, vsl]

    def init_state(s0_ref, sq, half):
        if not has_init:
            return jnp.zeros((LANES, RET_DV), F32)
        s0 = s0_ref[sq, half]
        return jnp.concatenate([s0, zeros_half] if half == 0 else [zeros_half, s0], axis=0)

    def fwd_step(sq, half, n, s_f):
        rows, vsl, qm, kk, vb = load(sq, half, n)
        s = lax.dot_general(qm.astype(BF16), kk.astype(BF16), NT_DIMS, preferred_element_type=F32)
        o = jnp.dot((s * dec_scr[half]).astype(BF16), vb, preferred_element_type=F32)
        if cross:
            o = o + jnp.dot((qm * wts_scr[half, 0]).astype(BF16), s_f.astype(BF16), preferred_element_type=F32)
        of_scr[rows, vsl] = o
        kv = lax.dot_general((kk * wts_scr[half, 1]).astype(BF16), vb, TN_DIMS, preferred_element_type=F32)
        return chunk_decay[half][0] * s_f + kv

    def bwd_step(sq, half, n, s_b):
        rows, vsl, qm, kk, vb = load(sq, half, n)
        if cross:
            ob_scr[rows, vsl] = jnp.dot((qm * wts_scr[half, 2]).astype(BF16), s_b.astype(BF16),
                                        preferred_element_type=F32)
        kv = lax.dot_general((kk * wts_scr[half, 3]).astype(BF16), vb, TN_DIMS, preferred_element_type=F32)
        return chunk_decay[half][1] * s_b + kv

    def finish(sq, n):
        rows = rows_of(sq, n)
        o2 = of_scr[rows, :] + ob_scr[rows, :] if cross else of_scr[rows, :]
        for half in range(2):
            vsl = slice(RET_DV * half, RET_DV * (half + 1))
            o = o2[:, vsl]
            mu = jnp.mean(o, axis=-1, keepdims=True)
            d = o - mu
            var = jnp.mean(d * d, axis=-1, keepdims=True)
            o_ref[rows, vsl] = (_silu(g_ref[rows, vsl].astype(F32)) * (d * lax.rsqrt(var + LN_EPS))).astype(BF16)

    s_f = tuple(init_state(sf0_ref if has_init else None, sq, half) for sq, half in chains)
    s_b = tuple(init_state(sb0_ref if has_init else None, sq, half) for sq, half in chains)
    if n_chunks == 1:
        s_f = tuple(fwd_step(sq, half, 0, s) for (sq, half), s in zip(chains, s_f))
        s_b = tuple(bwd_step(sq, half, 0, s) for (sq, half), s in zip(chains, s_b))
        for sq in range(n_seq):
            finish(sq, 0)
    else:
        def scan_step(n, carry):
            sf, sb = carry
            sf = tuple(fwd_step(sq, half, n, s) for (sq, half), s in zip(chains, sf))
            sb = tuple(bwd_step(sq, half, n_chunks - 1 - n, s) for (sq, half), s in zip(chains, sb))
            return sf, sb

        s_f, s_b = lax.fori_loop(0, n_chunks, scan_step, (s_f, s_b))

        def finish_step(n, carry):
            for sq in range(n_seq):
                finish(sq, n)
            return carry

        lax.fori_loop(0, n_chunks, finish_step, 0)
    if emit_state:
        for (sq, half), sf, sb in zip(chains, s_f, s_b):
            sf_ref[sq, half] = sf[RET_DK * half:RET_DK * (half + 1), :]
            sb_ref[sq, half] = sb[RET_DK * half:RET_DK * (half + 1), :]


def _retention(zret, decf, decb, j, latent, state_f=None, state_b=None, prev=None):
    seq_len = DEC_SEQ if latent else SEQ
    n_seq = DEC_BATCH if latent else 4
    n_b = (DEC_BATCH if latent else BATCH) // n_seq
    rows = n_seq * seq_len
    row0 = NTOK_P // rows if latent else 0
    n_pairs = RET_HEADS // 2
    in_specs = [
        pl.BlockSpec((None, RET_HEADS, LANES), lambda p, b: (j, 0, 0)),
        pl.BlockSpec((None, RET_HEADS, LANES), lambda p, b: (j, 0, 0)),
        pl.BlockSpec((rows, LANES), lambda p, b: (row0 + b, p)),
        pl.BlockSpec((rows, LANES), lambda p, b: (row0 + b, 4 + p)),
        pl.BlockSpec((rows, 256), lambda p, b: (row0 + b, 4 + p)),
        pl.BlockSpec((rows, 256), lambda p, b: (row0 + b, 8 + p)),
    ]
    args = [decf, decb, zret, zret, zret, zret]
    out_specs = [pl.BlockSpec((rows, 256), lambda p, b: (b, p))]
    out_shape = [jax.ShapeDtypeStruct((n_b * rows, RET_HEADS * RET_DV), BF16)]
    aliases = {}
    if latent:
        st_spec = pl.BlockSpec((n_seq, None, 2, RET_DK, RET_DV), lambda p, b: (b, j, p, 0, 0))
        in_specs += [st_spec, st_spec]
        args += [state_f, state_b]
    else:
        st_spec = pl.BlockSpec((n_seq, None, 2, RET_DK, RET_DV), lambda p, b: (b, j, p, 0, 0))
        out_specs += [st_spec, st_spec]
        out_shape += [jax.ShapeDtypeStruct((BATCH, N_EVEN, RET_HEADS, RET_DK, RET_DV), F32)] * 2
        if prev is not None:
            aliases = {len(args): 1, len(args) + 1: 2}
            in_specs += [pl.BlockSpec(memory_space=pl.ANY)] * 2
            args += list(prev)
    return pl.pallas_call(
        functools.partial(_retention_kernel, seq_len, n_seq, latent, not latent, bool(aliases)),
        grid=(n_pairs, n_b),
        in_specs=in_specs,
        out_specs=out_specs,
        out_shape=out_shape,
        scratch_shapes=[
            pltpu.VMEM((rows, 2 * RET_DV), F32),
            pltpu.VMEM((rows, 2 * RET_DV), F32),
            pltpu.VMEM((2, RET_CHUNK, RET_CHUNK), F32),
            pltpu.VMEM((2, 4, RET_CHUNK, LANES), F32),
        ],
        input_output_aliases=aliases,
        compiler_params=_params(("arbitrary", "arbitrary")),
        name="retention_latent" if latent else "retention_prompt",
    )(*args)


def _diff_prep_kernel(k_ref, v_ref, cos_ref, sin_ref, kr_ref, va_ref):
    lane = _lane_iota(cos_ref.shape)
    cos = cos_ref[...]
    sin = sin_ref[...]
    ones = jnp.ones(cos_ref.shape, BF16)
    for h in range(DIFF_HEADS):
        sl = slice(LANES * h, LANES * (h + 1))
        kr_ref[:, sl] = _rope128(k_ref[:, sl].astype(F32), cos, sin, lane, DIFF_DH // 2).astype(BF16)
        va_ref[:, 256 * h:256 * h + LANES] = v_ref[:, sl].astype(BF16)
        va_ref[:, 256 * h + LANES:256 * (h + 1)] = ones


def _diff_prep(zodd, cos_t, sin_t):
    tm = 512
    row0 = NTOK_P // tm
    nst = DEC_SEQ // tm
    return pl.pallas_call(
        _diff_prep_kernel,
        grid=(NTOK_S // tm,),
        in_specs=[
            pl.BlockSpec((tm, 1024), lambda i: (row0 + i, 1)),
            pl.BlockSpec((tm, 1024), lambda i: (row0 + i, 2)),
            pl.BlockSpec((tm, LANES), lambda i: (i % nst, 0)),
            pl.BlockSpec((tm, LANES), lambda i: (i % nst, 0)),
        ],
        out_specs=[pl.BlockSpec((tm, 1024), lambda i: (i, 0)), pl.BlockSpec((tm, 2048), lambda i: (i, 0))],
        out_shape=[jax.ShapeDtypeStruct((NTOK_S, 1024), BF16), jax.ShapeDtypeStruct((NTOK_S, 2048), BF16)],
        compiler_params=_params(("parallel",)),
        name="diff_rope_keys",
    )(zodd, zodd, cos_t, sin_t)


def _diff_attn_kernel(latent, lam_init, *refs):
    if latent:
        (lam_ref, ng_ref, q_ref, cos_ref, sin_ref, kc_ref, vc_ref, kn_ref, vn_ref, o_ref) = refs
    else:
        lam_ref, ng_ref, q_ref, k_ref, v_ref = refs[:5]
        o_ref, kout_ref, vout_ref = refs[-3:]
    tq = ATT_TQ if latent else SEQ
    n_seq = q_ref.shape[0] // tq
    lane = _lane_iota((tq, LANES))
    scale = DIFF_DH ** -0.5
    lp = lam_ref[...]
    lam = (jnp.exp(jnp.sum(lp[0:1, :] * lp[1:2, :], axis=-1, keepdims=True))
           - jnp.exp(jnp.sum(lp[2:3, :] * lp[3:4, :], axis=-1, keepdims=True)) + lam_init)
    ng = ng_ref[...]
    ones = jnp.ones((PAST_LEN if latent else tq, LANES), BF16)
    for sq, h in [(sq, h) for sq in range(n_seq) for h in range(DIFF_HEADS)]:
        sl = slice(LANES * h, LANES * (h + 1))
        rows = slice(tq * sq, tq * (sq + 1))
        qh = q_ref[rows, sl].astype(F32)
        if latent:
            qh = _rope128(qh, cos_ref[...], sin_ref[...], lane, DIFF_DH // 2)
            k_list = [kc_ref[h].astype(BF16), kn_ref[:, sl]]
            v_list = [jnp.concatenate([vc_ref[h].astype(BF16), ones], axis=1),
                      vn_ref[:, 256 * h:256 * (h + 1)]]
        else:
            kh = k_ref[rows, sl]
            vh = v_ref[rows, sl]
            kout_ref[sq, h] = kh.astype(F32)
            vout_ref[sq, h] = vh.astype(F32)
            k_list = [kh.astype(BF16)]
            v_list = [jnp.concatenate([vh.astype(BF16), ones], axis=1)]
        q1 = jnp.where(lane < DIFF_DH, qh, 0.0).astype(BF16)
        q2 = jnp.where(lane >= DIFF_DH, qh, 0.0).astype(BF16)
        s1 = [lax.dot_general(q1, kk, NT_DIMS, preferred_element_type=F32) for kk in k_list]
        s2 = [lax.dot_general(q2, kk, NT_DIMS, preferred_element_type=F32) for kk in k_list]
        o = _pv_normalised(_exp_parts(s1, scale), v_list) - lam * _pv_normalised(_exp_parts(s2, scale), v_list)
        y = o * lax.rsqrt(jnp.mean(o * o, axis=-1, keepdims=True) + RMS_EPS) * ng
        o_ref[rows, sl] = (y * (1.0 - lam_init)).astype(BF16)


def _diff_attn_prompt(zodd, lam_p, norm_g, j, lam_init, prev):
    n_seq = 2
    rows = n_seq * SEQ
    cache_shape = jax.ShapeDtypeStruct((BATCH, N_ODD, DIFF_HEADS, SEQ, LANES), F32)
    cache_spec = pl.BlockSpec((n_seq, None, DIFF_HEADS, SEQ, LANES), lambda b: (b, j, 0, 0, 0))
    in_specs = [
        pl.BlockSpec((None, 4, DIFF_DH), lambda b: (j, 0, 0)),
        pl.BlockSpec((None, 1, DIFF_DV), lambda b: (j, 0, 0)),
        pl.BlockSpec((rows, 1024), lambda b: (b, 0)),
        pl.BlockSpec((rows, 1024), lambda b: (b, 1)),
        pl.BlockSpec((rows, 1024), lambda b: (b, 2)),
    ]
    args = [lam_p, norm_g.reshape(-1, 1, DIFF_DV), zodd, zodd, zodd]
    aliases = {}
    if prev is not None:
        aliases = {len(args): 1, len(args) + 1: 2}
        in_specs += [pl.BlockSpec(memory_space=pl.ANY)] * 2
        args += list(prev)
    return pl.pallas_call(
        functools.partial(_diff_attn_kernel, False, lam_init),
        grid=(BATCH // n_seq,),
        in_specs=in_specs,
        out_specs=[pl.BlockSpec((rows, 1024), lambda b: (b, 0)), cache_spec, cache_spec],
        out_shape=[jax.ShapeDtypeStruct((NTOK_P, 1024), BF16), cache_shape, cache_shape],
        input_output_aliases=aliases,
        compiler_params=_params(("arbitrary",)),
        name="diff_attn_prompt",
    )(*args)


def _diff_attn_latent(zodd, lam_p, norm_g, cos_t, sin_t, cache_k, cache_v, k_rot, v_aug, j, lam_init):
    nq = DEC_SEQ // ATT_TQ
    row0 = NTOK_P // ATT_TQ
    ctx_spec = pl.BlockSpec((None, None, DIFF_HEADS, PAST_LEN, LANES), lambda b, q: (b, j, 0, 0, 0))
    return pl.pallas_call(
        functools.partial(_diff_attn_kernel, True, lam_init),
        grid=(DEC_BATCH, nq),
        in_specs=[
            pl.BlockSpec((None, 4, DIFF_DH), lambda b, q: (j, 0, 0)),
            pl.BlockSpec((None, 1, DIFF_DV), lambda b, q: (j, 0, 0)),
            pl.BlockSpec((ATT_TQ, 1024), lambda b, q: (row0 + b * nq + q, 0)),
            pl.BlockSpec((ATT_TQ, LANES), lambda b, q: (q, 0)),
            pl.BlockSpec((ATT_TQ, LANES), lambda b, q: (q, 0)),
            ctx_spec,
            ctx_spec,
            pl.BlockSpec((DEC_SEQ, 1024), lambda b, q: (b, 0), pipeline_mode=pl.Buffered(1)),
            pl.BlockSpec((DEC_SEQ, 2048), lambda b, q: (b, 0), pipeline_mode=pl.Buffered(1)),
        ],
        out_specs=pl.BlockSpec((ATT_TQ, 1024), lambda b, q: (b * nq + q, 0)),
        out_shape=jax.ShapeDtypeStruct((NTOK_S, 1024), BF16),
        compiler_params=_params(("parallel", "arbitrary")),
        name="diff_attn_latent",
    )(lam_p, norm_g.reshape(-1, 1, DIFF_DV), zodd, cos_t, sin_t, cache_k, cache_v, k_rot, v_aug)


def _mla_weight_t(w_in_t, j):
    base = RET_COLS
    mq = w_in_t[j, base:base + MLA_HEADS * (MLA_NOPE + MLA_ROPE)].reshape(MLA_HEADS, MLA_NOPE + MLA_ROPE, D_MODEL)
    qn = mq[:, :MLA_NOPE].reshape(MLA_HEADS * MLA_NOPE, D_MODEL)
    qr = mq[:, MLA_NOPE:].reshape(MLA_HEADS * MLA_ROPE, D_MODEL)
    ckv0 = base + MLA_HEADS * (MLA_NOPE + MLA_ROPE)
    ckv = w_in_t[j, ckv0:ckv0 + MLA_KV_RANK]
    kr = w_in_t[j, ckv0 + MLA_KV_RANK:]
    return jnp.concatenate([qn, qr, ckv, jnp.tile(kr, (LANES // MLA_ROPE, 1))], axis=0)


def kernel(x_prompt, x_sample, state_ret_fwd, state_ret_bwd, cache_mla_ckv, cache_mla_krope, cache_diff_k, cache_diff_v, c, c_ctx, ada_w, ada_b, ln1_g, ln1_b, ln2_g, ln2_b, ev_w_in, ev_w_out, ret_decay_fwd, ret_decay_bwd, mla_kv_norm_g, mla_w_uk, mla_w_uv, od_w_in, od_w_out, diff_lambda, diff_norm_g, moe_w_group, moe_b_group, moe_w_expert, moe_b_expert, moe_w1, moe_w3, moe_w2):
    x = (x_prompt.reshape(NTOK_P, D_MODEL), x_sample.reshape(NTOK_S, D_MODEL), 0)
    cond =jnp.concatenate([c_ctx[None, :], c, jnp.zeros((N_COND - 1 - DEC_BATCH, D_MODEL), F32)], axis=0)
    mods = _ada_all(cond, ada_w, ada_b).reshape(DEPTH, N_COND, 6, D_MODEL)

    cos_m, sin_m = _rope_tables(MLA_ROPE)
    cos_d, sin_d = _rope_tables(DIFF_DH)
    ident = MLA_PROJ_TM
    cos_m_id = jnp.concatenate([jnp.ones((ident, LANES), F32), cos_m], axis=0)
    sin_m_id = jnp.concatenate([jnp.zeros((ident, LANES), F32), sin_m], axis=0)
    decf = jnp.broadcast_to(ret_decay_fwd[:, :, None], ret_decay_fwd.shape + (LANES,))
    decb = jnp.broadcast_to(ret_decay_bwd[:, :, None], ret_decay_bwd.shape + (LANES,))

    ev_w_in_t = jnp.swapaxes(ev_w_in, 1, 2)
    pad = LANES - MOE_GROUPS - MOE_EXPERTS
    ret_states = mla_caches = diff_caches = None
    for i in range(DEPTH):
        j = i // 2
        mod = mods[i]
        w_router = jnp.concatenate([moe_w_group[i], moe_w_expert[i], jnp.zeros((D_MODEL, pad), F32)], axis=1)
        b_router = jnp.concatenate([moe_b_group[i], moe_b_expert[i], jnp.zeros((pad,), F32)])[None, :]
        if i % 2 == 0:
            zret = _mm_mod(x, mod, ev_w_in_t, (j,), RET_COLS, 1536, "in_proj_retention", w_is_transposed=True)
            *mla_caches, zmla, kcat, vm = _mla_proj_kv(x, mod, _mla_weight_t(ev_w_in_t, j), cos_m_id, sin_m_id,
                                                       mla_kv_norm_g, mla_w_uk, mla_w_uv, j, mla_caches)
            kr_ctx = jnp.tile(cache_mla_krope[:, j], (1, 1, LANES // MLA_ROPE))
            kcat_ctx, vm_ctx = _mla_kv_ctx(cache_mla_ckv, kr_ctx, mla_w_uk, mla_w_uv, j)
            a_ret_p, *ret_states = _retention(zret, decf, decb, j, False, prev=ret_states)
            (a_ret_s,) = _retention(zret, decf, decb, j, True, state_ret_fwd, state_ret_bwd)
            a_mla_p = _mla_attn_prompt(zmla, kcat, vm)
            a_mla_s = _mla_attn_latent(zmla, cos_m, sin_m, kcat_ctx, vm_ctx, kcat, vm)
            x1, meta, counts = _mm_ln([(a_ret_p, a_ret_s), (a_mla_p, a_mla_s)], ev_w_out, j, x, mod,
                                      ln1_g, ln1_b, w_router, b_router, i)
        else:
            lam_init = 0.8 - 0.6 * math.exp(-0.3 * i)
            zodd = _mm_mod(x, mod, od_w_in, (j,), 3072, 1536, "in_proj_diff")
            a_p, *diff_caches = _diff_attn_prompt(zodd, diff_lambda, diff_norm_g, j, lam_init, diff_caches)
            k_rot, v_aug = _diff_prep(zodd, cos_d, sin_d)
            a_s = _diff_attn_latent(zodd, diff_lambda, diff_norm_g, cos_d, sin_d, cache_diff_k, cache_diff_v,
                                    k_rot, v_aug, j, lam_init)
            x1, meta, counts = _mm_ln([(a_p, a_s)], od_w_out, j, x, mod, ln1_g, ln1_b, w_router, b_router, i)
        y = _moe(x1, _moe_plan(meta, counts), mods, w_router, b_router, moe_w1, moe_w3, moe_w2, ln2_g, ln2_b, i)
        x = (y, y, NTOK_P)

    y_prompt = x[0][:NTOK_P].reshape(BATCH, SEQ, D_MODEL)
    y_sample = x[1][NTOK_P:NTOK].reshape(DEC_BATCH, DEC_SEQ, D_MODEL)
    return (y_prompt, y_sample, ret_states[0], ret_states[1], mla_caches[0], mla_caches[1],
            diff_caches[0], diff_caches[1])
```

```python
import functools
import math

import jax
import jax.numpy as jnp
from jax import lax
from jax.experimental import pallas as pl
from jax.experimental.pallas import tpu as pltpu

D_MODEL = 1024
BATCH = 32
SEQ = 256
DEPTH = 4
N_EVEN = 2
N_ODD = 2
DEC_BATCH = 2
DEC_SEQ = 2048
PAST_LEN = 256
GRID_W = 64
LN_EPS = 1e-5
RMS_EPS = 1e-6
DEEPNORM_ALPHA = (2.0 * DEPTH) ** 0.25
ROPE_BASE = 10000.0
RET_HEADS = 8
RET_DK = 64
RET_DV = 128
MLA_HEADS = 8
MLA_NOPE = 64
MLA_ROPE = 32
MLA_DV = 64
MLA_KV_RANK = 256
DIFF_HEADS = 8
DIFF_DH = 64
DIFF_DV = 128
MOE_GROUPS = 4
MOE_PER_GROUP = 4
MOE_EXPERTS = 16
MOE_FF = 256

NTOK_P = BATCH * SEQ
NTOK_S = DEC_BATCH * DEC_SEQ
NTOK = NTOK_P + NTOK_S
N_COND = 8
LANES = 128
SUBLANES = 8
RET_COLS = 3072
MLA_COLS = 1152
MLA_PROJ_TM = 1024
ATT_TQ = 512
RET_CHUNK = 256
MOE_TILE = 512
MOE_TILES = (NTOK + MOE_GROUPS * (MOE_TILE - 1)) // MOE_TILE
MOE_ROWS = MOE_TILES * MOE_TILE
MOE_KEY_BASE = 16384
VMEM_LIMIT = 56 * 1024 * 1024
LOG2E = 1.4426950408889634

F32 = jnp.float32
BF16 = jnp.bfloat16
NT_DIMS = (((1,), (1,)), ((), ()))
TN_DIMS = (((0,), (0,)), ((), ()))


def _params(sem):
    return pltpu.CompilerParams(dimension_semantics=sem, vmem_limit_bytes=VMEM_LIMIT)


def _group_of_tile(i, tm):
    npt = NTOK_P // tm
    nst = DEC_SEQ // tm
    return jnp.where(i < npt, 0, 1 + (i - npt) // nst)


def _split_specs(tm, width, m_of, s_row0=0):
    npt = NTOK_P // tm
    s_blk0 = s_row0 // tm
    return [pl.BlockSpec((tm, width), lambda *g: (jnp.minimum(m_of(*g), npt - 1), 0)),
            pl.BlockSpec((tm, width), lambda *g: (jnp.maximum(m_of(*g) - npt, 0) + s_blk0, 0))]


def _read_split(p_ref, s_ref, m):
    return jnp.where(m < NTOK_P // p_ref.shape[0], p_ref[...], s_ref[...])


def _silu(x):
    return x * (1.0 / (1.0 + jnp.exp(-x)))


def _layer_norm(r, g, b):
    mu = jnp.mean(r, axis=-1, keepdims=True)
    d = r - mu
    var = jnp.mean(d * d, axis=-1, keepdims=True)
    return d * lax.rsqrt(var + LN_EPS) * g + b


def _lane_iota(shape):
    return lax.broadcasted_iota(jnp.int32, shape, 1)


def _div_pow2(x, d):
    assert d & (d - 1) == 0
    return jnp.right_shift(x, d.bit_length() - 1)


def _mod_pow2(x, d):
    assert d & (d - 1) == 0
    return jnp.bitwise_and(x, d - 1)


def _ada_kernel(c_ref, w_ref, b_ref, o_ref):
    h = _silu(c_ref[...]).astype(BF16)
    o_ref[...] = jnp.dot(h, w_ref[...].astype(BF16), preferred_element_type=F32) + b_ref[...]


def _ada_all(cond, ada_w, ada_b):
    tn = 768
    return pl.pallas_call(
        _ada_kernel,
        grid=(DEPTH, 6 * D_MODEL // tn),
        in_specs=[
            pl.BlockSpec((N_COND, D_MODEL), lambda l, n: (0, 0)),
            pl.BlockSpec((None, D_MODEL, tn), lambda l, n: (l, 0, n)),
            pl.BlockSpec((None, 1, tn), lambda l, n: (l, 0, n)),
        ],
        out_specs=pl.BlockSpec((None, N_COND, tn), lambda l, n: (l, 0, n)),
        out_shape=jax.ShapeDtypeStruct((DEPTH, N_COND, 6 * D_MODEL), F32),
        compiler_params=_params(("parallel", "parallel")),
        name="ada_modulation",
    )(cond, ada_w, ada_b.reshape(DEPTH, 1, 6 * D_MODEL))


def _mm_mod_kernel(w_is_transposed, xp_ref, xs_ref, mod_ref, w_ref, o_ref, wscr):
    m = pl.program_id(1)

    @pl.when(m == 0)
    def _():
        wscr[...] = w_ref[...].astype(BF16)

    sh = mod_ref[0:1, :]
    sc = mod_ref[1:2, :]
    xm = (_read_split(xp_ref, xs_ref, m) * (1.0 + sc) + sh).astype(BF16)
    if w_is_transposed:
        z = lax.dot_general(xm, wscr[...], NT_DIMS, preferred_element_type=F32)
    else:
        z = jnp.dot(xm, wscr[...], preferred_element_type=F32)
    o_ref[...] = z.astype(o_ref.dtype)


def _mm_mod(x, mod, w, w_index, n_cols, tn, name, w_is_transposed=False):
    tm = 1024
    if w_is_transposed:
        w_spec = pl.BlockSpec((None,) * len(w_index) + (tn, D_MODEL), lambda n, m: tuple(w_index) + (n, 0))
        w_scratch = pltpu.VMEM((tn, D_MODEL), BF16)
    else:
        w_spec = pl.BlockSpec((None,) * len(w_index) + (D_MODEL, tn), lambda n, m: tuple(w_index) + (0, n))
        w_scratch = pltpu.VMEM((D_MODEL, tn), BF16)
    return pl.pallas_call(
        functools.partial(_mm_mod_kernel, w_is_transposed),
        grid=(n_cols // tn, NTOK // tm),
        in_specs=_split_specs(tm, D_MODEL, lambda n, m: m, x[2]) + [
            pl.BlockSpec((None, 6, D_MODEL), lambda n, m: (_group_of_tile(m, tm), 0, 0)),
            w_spec,
        ],
        out_specs=pl.BlockSpec((tm, tn), lambda n, m: (m, n)),
        out_shape=jax.ShapeDtypeStruct((NTOK, n_cols), BF16),
        scratch_shapes=[w_scratch],
        compiler_params=_params(("arbitrary", "arbitrary")),
        name=name,
    )(x[0], x[1], mod, w)


def _router_probs(xm, wr_ref, br_ref):
    rows = xm.shape[0]
    z = jnp.dot(xm, wr_ref[...].astype(BF16), preferred_element_type=F32) + br_ref[...]
    lane_i = _lane_iota((rows, LANES))
    lane = lane_i.astype(F32)
    gmask = lane_i < MOE_GROUPS
    zg = jnp.where(gmask, z, -jnp.inf)
    pg = jnp.exp(zg - jnp.max(zg, axis=-1, keepdims=True))
    g_prob = pg / jnp.sum(pg, axis=-1, keepdims=True)
    g_p = jnp.max(g_prob, axis=-1, keepdims=True)
    g_idx = jnp.min(jnp.where(gmask & (g_prob == g_p), lane, float(LANES)), axis=-1, keepdims=True)
    return z, lane_i, lane, g_p, g_idx


def _mm_ln_kernel(k_sizes, *refs):
    n_a = len(k_sizes)
    a_refs = refs[:2 * n_a]
    (w_ref, xp_ref, xs_ref, mod_ref, g_ref, b_ref, wr_ref, br_ref,
     o_ref, meta_ref, cnt_ref, wscr, tri_scr, carry_scr) = refs[2 * n_a:]
    m = pl.program_id(0)
    tm = o_ref.shape[0]

    @pl.when(m == 0)
    def _():
        wscr[...] = w_ref[...].astype(BF16)
        ri = lax.broadcasted_iota(jnp.int32, (tm, tm), 0)
        ci = lax.broadcasted_iota(jnp.int32, (tm, tm), 1)
        tri_scr[...] = jnp.where(ci < ri, 1.0, 0.0).astype(BF16)
        carry_scr[...] = jnp.zeros_like(carry_scr)

    y = None
    k0 = 0
    for i, ks in enumerate(k_sizes):
        a = _read_split(a_refs[2 * i], a_refs[2 * i + 1], m)
        part = jnp.dot(a, wscr[k0:k0 + ks, :], preferred_element_type=F32)
        y = part if y is None else y + part
        k0 += ks
    gate = mod_ref[2:3, :]
    r = DEEPNORM_ALPHA * _read_split(xp_ref, xs_ref, m) + gate * y
    x1 = _layer_norm(r, g_ref[...], b_ref[...])
    o_ref[...] = x1

    xm = (x1 * (1.0 + mod_ref[4:5, :]) + mod_ref[3:4, :]).astype(BF16)
    _, lane_i, lane, _, g_idx = _router_probs(xm, wr_ref, br_ref)
    onehot = jnp.where(lane == g_idx, 1.0, 0.0)
    before = jnp.dot(tri_scr[...], onehot.astype(BF16), preferred_element_type=F32) + carry_scr[0:1, :]
    rank = jnp.sum(jnp.where(lane == g_idx, before, 0.0), axis=-1, keepdims=True)
    key_col = g_idx * float(MOE_KEY_BASE) + rank
    row_i = lax.broadcasted_iota(jnp.int32, (tm, LANES), 0)
    diag = jnp.where(lane_i == _mod_pow2(row_i, LANES), key_col, 0.0)
    meta_ref[...] = jnp.sum(diag.reshape(tm // LANES, LANES, LANES), axis=1).astype(jnp.int32)
    total = carry_scr[0:1, :] + jnp.sum(onehot, axis=0, keepdims=True)
    carry_scr[...] = jnp.broadcast_to(total, carry_scr.shape)
    cnt_ref[...] = jnp.broadcast_to(total, cnt_ref.shape)


def _mm_ln(a_pairs, w, j, x, mod, ln_g, ln_b, w_router, b_router, layer):
    tm = 512
    k_sizes = tuple(ap.shape[1] for ap, _ in a_pairs)
    k_tot = sum(k_sizes)
    in_specs = []
    args = []
    for (ap, a_s), ks in zip(a_pairs, k_sizes):
        in_specs += _split_specs(tm, ks, lambda m: m)
        args += [ap, a_s]
    in_specs += [pl.BlockSpec((None, k_tot, D_MODEL), lambda m: (j, 0, 0))]
    in_specs += _split_specs(tm, D_MODEL, lambda m: m, x[2])
    in_specs += [
        pl.BlockSpec((None, 6, D_MODEL), lambda m: (_group_of_tile(m, tm), 0, 0)),
        pl.BlockSpec((None, 1, D_MODEL), lambda m: (layer, 0, 0)),
        pl.BlockSpec((None, 1, D_MODEL), lambda m: (layer, 0, 0)),
        pl.BlockSpec((D_MODEL, LANES), lambda m: (0, 0)),
        pl.BlockSpec((1, LANES), lambda m: (0, 0)),
    ]
    return pl.pallas_call(
        functools.partial(_mm_ln_kernel, k_sizes),
        grid=(NTOK // tm,),
        in_specs=in_specs,
        out_specs=[
            pl.BlockSpec((tm, D_MODEL), lambda m: (m, 0)),
            pl.BlockSpec((None, tm // LANES, LANES), lambda m: (m, 0, 0)),
            pl.BlockSpec((N_COND, LANES), lambda m: (0, 0)),
        ],
        out_shape=[
            jax.ShapeDtypeStruct((NTOK, D_MODEL), F32),
            jax.ShapeDtypeStruct((NTOK // tm, tm // LANES, LANES), jnp.int32),
            jax.ShapeDtypeStruct((N_COND, LANES), F32),
        ],
        scratch_shapes=[
            pltpu.VMEM((k_tot, D_MODEL), BF16),
            pltpu.VMEM((tm, tm), BF16),
            pltpu.VMEM((N_COND, LANES), F32),
        ],
        compiler_params=_params(("arbitrary",)),
        name="out_proj_ln",
    )(*args, w, x[0], x[1], mod, ln_g.reshape(DEPTH, 1, D_MODEL), ln_b.reshape(DEPTH, 1, D_MODEL),
      w_router, b_router)


def _moe_kernel(tgrp_ref, ntile_ref, src0_ref, src1_ref, dst_ref, mid_ref, x_hbm, mod_ref, wr_ref, br_ref,
                w1_ref, w3_ref, w2_ref, g_ref, b_ref, y_hbm,
                gbuf, obuf, gsem, ssem, w13s, w2s):
    i = pl.program_id(0)
    n_steps = pl.num_programs(0)
    n_tiles = ntile_ref[0]
    n_blk = gbuf.shape[1]
    ts = n_blk * SUBLANES
    slot = lax.rem(i, 2)

    def start_gather(src_ref, s):
        def body(k, carry):
            for u in range(SUBLANES):
                tok = src_ref[0, k * SUBLANES + u]
                pltpu.make_async_copy(x_hbm.at[pl.ds(tok, 1), :], gbuf.at[s, k, pl.ds(u, 1), :],
                                      gsem.at[s]).start()
            return carry
        lax.fori_loop(0, n_blk, body, 0)

    def wait_gather(s):
        pltpu.make_async_copy(gbuf.at[s], gbuf.at[s], gsem.at[s]).wait()

    def start_scatter(s):
        def body(k, carry):
            for u in range(SUBLANES):
                tok = dst_ref[0, k * SUBLANES + u]
                pltpu.make_async_copy(obuf.at[s, k, pl.ds(u, 1), :], y_hbm.at[pl.ds(tok, 1), :],
                                      ssem.at[s]).start()
            return carry
        lax.fori_loop(0, n_blk, body, 0)

    def wait_scatter(s):
        pltpu.make_async_copy(obuf.at[s], obuf.at[s], ssem.at[s]).wait()

    @pl.when(i == 0)
    def _():
        start_gather(src0_ref, 0)

    @pl.when(i < n_tiles)
    def _():
        grp = tgrp_ref[i]
        wait_gather(slot)

        @pl.when(i + 1 < n_tiles)
        def _():
            start_gather(src1_ref, 1 - slot)

        @pl.when((i == 0) | (grp != tgrp_ref[jnp.maximum(i - 1, 0)]))
        def _():
            w13s[:, :, :MOE_FF] = w1_ref[...].astype(BF16)
            w13s[:, :, MOE_FF:] = w3_ref[...].astype(BF16)
            w2s[...] = w2_ref[...].astype(BF16)

        mid_rows = mid_ref[...].astype(F32)
        spread = jnp.concatenate([jnp.broadcast_to(mid_rows[a:a + 1, :], (LANES, LANES))
                                  for a in range(ts // LANES)], axis=0)
        row_i = lax.broadcasted_iota(jnp.int32, (ts, LANES), 0)
        mid = jnp.sum(jnp.where(_lane_iota((ts, LANES)) == _mod_pow2(row_i, LANES), spread, 0.0),
                      axis=-1, keepdims=True)

        def mod_row(k):
            return jnp.where(mid == 0, mod_ref[0, k:k + 1, :],
                             jnp.where(mid == 1, mod_ref[1, k:k + 1, :], mod_ref[2, k:k + 1, :]))

        x1 = gbuf[slot].reshape(ts, D_MODEL)
        xm = (x1 * (1.0 + mod_row(4)) + mod_row(3)).astype(BF16)
        z, lane_i, lane, g_p, _ = _router_probs(xm, wr_ref, br_ref)
        e0 = MOE_GROUPS + MOE_PER_GROUP * grp
        emask = (lane_i >= e0) & (lane_i < e0 + MOE_PER_GROUP)
        ze = jnp.where(emask, z, -jnp.inf)
        pe = jnp.exp(ze - jnp.max(ze, axis=-1, keepdims=True))
        e_prob = pe / jnp.sum(pe, axis=-1, keepdims=True)
        cand = jnp.where(emask, e_prob, -1.0)
        p1 = jnp.max(cand, axis=-1, keepdims=True)
        i1 = jnp.min(jnp.where(cand == p1, lane, float(LANES)), axis=-1, keepdims=True)
        cand2 = jnp.where(lane == i1, -1.0, cand)
        p2 = jnp.max(cand2, axis=-1, keepdims=True)
        i2 = jnp.min(jnp.where(cand2 == p2, lane, float(LANES)), axis=-1, keepdims=True)
        denom = p1 + p2
        comb = jnp.where(lane == i1, g_p * p1 / denom, 0.0) + jnp.where(lane == i2, g_p * p2 / denom, 0.0)
        y = None
        for e in range(MOE_PER_GROUP):
            c = jnp.sum(jnp.where(lane_i == e0 + e, comb, 0.0), axis=-1, keepdims=True)
            h = jnp.dot(xm, w13s[e], preferred_element_type=F32)
            hid = (_silu(h[:, :MOE_FF]) * h[:, MOE_FF:] * c).astype(BF16)
            part = jnp.dot(hid, w2s[e], preferred_element_type=F32)
            y = part if y is None else y + part
        r = DEEPNORM_ALPHA * x1 + mod_row(5) * y

        @pl.when(i >= 2)
        def _():
            wait_scatter(slot)

        obuf[slot] = _layer_norm(r, g_ref[...], b_ref[...]).reshape(n_blk, SUBLANES, D_MODEL)
        start_scatter(slot)

    @pl.when(i == n_steps - 1)
    def _():
        @pl.when(n_tiles >= 2)
        def _():
            wait_scatter(lax.rem(n_tiles, 2))

        @pl.when(n_tiles >= 1)
        def _():
            wait_scatter(lax.rem(n_tiles + 1, 2))


def _inverse_perm_kernel(pos_ref, pad_ref, out_ref):
    def clear(s, carry):
        out_ref[s] = 0
        return carry

    def place(t, carry):
        out_ref[pos_ref[t]] = t + 1
        return carry

    for run in range(pad_ref.shape[0] // 2):
        lax.fori_loop(pad_ref[2 * run], pad_ref[2 * run + 1], clear, 0)
    lax.fori_loop(0, pos_ref.shape[0], place, 0, unroll=8)


def _inverse_perm(pos, pad_runs):
    return pl.pallas_call(
        _inverse_perm_kernel,
        in_specs=[pl.BlockSpec(memory_space=pltpu.SMEM)] * 2,
        out_specs=pl.BlockSpec(memory_space=pltpu.SMEM),
        out_shape=jax.ShapeDtypeStruct((MOE_ROWS,), jnp.int32),
        name="moe_inverse_perm",
    )(pos, pad_runs)


def _moe_plan(meta, counts):
    ts = MOE_TILE
    cnt = counts[0, :MOE_GROUPS].astype(jnp.int32)
    tiles_g = (cnt + ts - 1) // ts
    tile_end = jnp.cumsum(tiles_g)
    row0_g = (tile_end - tiles_g) * ts
    keys = meta.reshape(NTOK)
    gid = keys // MOE_KEY_BASE
    rank = keys % MOE_KEY_BASE
    pos = row0_g[gid] + rank
    pad_runs = jnp.stack([row0_g + cnt, tile_end * ts], axis=1).reshape(-1)
    pad_runs = jnp.concatenate([pad_runs, tile_end[-1:] * ts, jnp.full((1,), MOE_ROWS, jnp.int32)])
    tok1 = _inverse_perm(pos, pad_runs.astype(jnp.int32))
    rows = jnp.arange(MOE_ROWS, dtype=jnp.int32)
    src = jnp.maximum(tok1 - 1, 0)
    spare = NTOK + ((rows // ts) % 2) * ts + rows % ts
    dst = jnp.where(tok1 > 0, src, spare)
    mid = jnp.where(src < NTOK_P, 0, 1 + (src - NTOK_P) // DEC_SEQ)
    tile_group = jnp.minimum(jnp.sum(jnp.arange(MOE_TILES)[:, None] >= tile_end[None, :], axis=1),
                             MOE_GROUPS - 1).astype(jnp.int32)
    n_tiles = tile_end[-1:].astype(jnp.int32)
    return (tile_group, n_tiles, src.reshape(MOE_TILES, 1, ts), dst.reshape(MOE_TILES, 1, ts),
            mid.reshape(MOE_TILES, ts // LANES, LANES))


def _moe(x1, plan, mods, w_router, b_router, w1, w3, w2, ln_g, ln_b, layer):
    ts = MOE_TILE
    tile_group, n_tiles, src, dst, mid = plan
    grp_shape = (DEPTH, MOE_GROUPS, MOE_PER_GROUP)
    w_in_spec = pl.BlockSpec((None, None, MOE_PER_GROUP, D_MODEL, MOE_FF), lambda i, tg, nt: (layer, tg[i], 0, 0, 0))
    w_out_spec = pl.BlockSpec((None, None, MOE_PER_GROUP, MOE_FF, D_MODEL), lambda i, tg, nt: (layer, tg[i], 0, 0, 0))
    smem_tile = functools.partial(pl.BlockSpec, (None, 1, ts), memory_space=pltpu.SMEM)
    grid_spec = pltpu.PrefetchScalarGridSpec(
        num_scalar_prefetch=2,
        grid=(MOE_TILES,),
        in_specs=[
            smem_tile(lambda i, tg, nt: (i, 0, 0)),
            smem_tile(lambda i, tg, nt: (jnp.minimum(i + 1, MOE_TILES - 1), 0, 0)),
            smem_tile(lambda i, tg, nt: (i, 0, 0)),
            pl.BlockSpec((None, ts // LANES, LANES), lambda i, tg, nt: (i, 0, 0)),
            pl.BlockSpec(memory_space=pl.ANY),
            pl.BlockSpec((None, N_COND, 6, D_MODEL), lambda i, tg, nt: (layer, 0, 0, 0)),
            pl.BlockSpec((D_MODEL, LANES), lambda i, tg, nt: (0, 0)),
            pl.BlockSpec((1, LANES), lambda i, tg, nt: (0, 0)),
            w_in_spec,
            w_in_spec,
            w_out_spec,
            pl.BlockSpec((None, 1, D_MODEL), lambda i, tg, nt: (layer, 0, 0)),
            pl.BlockSpec((None, 1, D_MODEL), lambda i, tg, nt: (layer, 0, 0)),
        ],
        out_specs=pl.BlockSpec(memory_space=pl.ANY),
        scratch_shapes=[
            pltpu.VMEM((2, ts // SUBLANES, SUBLANES, D_MODEL), F32),
            pltpu.VMEM((2, ts // SUBLANES, SUBLANES, D_MODEL), F32),
            pltpu.SemaphoreType.DMA((2,)),
            pltpu.SemaphoreType.DMA((2,)),
            pltpu.VMEM((MOE_PER_GROUP, D_MODEL, 2 * MOE_FF), BF16),
            pltpu.VMEM((MOE_PER_GROUP, MOE_FF, D_MODEL), BF16),
        ],
    )
    return pl.pallas_call(
        _moe_kernel,
        grid_spec=grid_spec,
        out_shape=jax.ShapeDtypeStruct((NTOK + 2 * ts, D_MODEL), F32),
        compiler_params=_params(("arbitrary",)),
        name="hier_moe_ln",
    )(tile_group, n_tiles, src, src, dst, mid, x1, mods, w_router, b_router,
      w1.reshape(grp_shape + (D_MODEL, MOE_FF)), w3.reshape(grp_shape + (D_MODEL, MOE_FF)),
      w2.reshape(grp_shape + (MOE_FF, D_MODEL)),
      ln_g.reshape(DEPTH, 1, D_MODEL), ln_b.reshape(DEPTH, 1, D_MODEL))


def _swap_halves(x, lane, half):
    return jnp.where(_mod_pow2(lane, 2 * half) < half,
                     pltpu.roll(x, LANES - half, 1), pltpu.roll(x, half, 1))


def _rope128(x, cos, sin_signed, lane, half):
    return x * cos + _swap_halves(x, lane, half) * sin_signed


def _rope_tables(rot_dim):
    rows = DEC_SEQ // GRID_W
    row = jnp.repeat(jnp.arange(rows, dtype=F32), GRID_W)
    col = jnp.tile(jnp.arange(GRID_W, dtype=F32), rows)
    n_freq = rot_dim // 4
    inv_freq = ROPE_BASE ** (-jnp.arange(n_freq, dtype=F32) / n_freq)
    ang = jnp.concatenate([row[:, None] * inv_freq, col[:, None] * inv_freq], axis=-1)
    cos, sin = jnp.cos(ang), jnp.sin(ang)
    reps = LANES // rot_dim
    cos_full = jnp.tile(jnp.concatenate([cos, cos], axis=-1), (1, reps))
    sin_signed = jnp.tile(jnp.concatenate([-sin, sin], axis=-1), (1, reps))
    return cos_full, sin_signed


def _exp_parts(s_list, scale):
    m = None
    for s in s_list:
        sm = jnp.max(s, axis=-1, keepdims=True)
        m = sm if m is None else jnp.maximum(m, sm)
    return [jnp.exp2((s - m) * (scale * LOG2E)).astype(BF16) for s in s_list]


def _pv_normalised(p_list, v_list):
    o = None
    for p, v in zip(p_list, v_list):
        part = jnp.dot(p, v, preferred_element_type=F32)
        o = part if o is None else o + part
    return o[:, :LANES] / o[:, LANES:]


def _mla_kv_kernel(new_tokens, *refs):
    if new_tokens:
        (xp_ref, xs_ref, mod_ref, w_ref, cos_ref, sin_ref, g_ref, wuk_ref, wuv_ref) = refs[:9]
        ckvc_ref, krc_ref, zq_ref, kcat_ref, vm_ref, wscr, wuk_scr, wuv_scr = refs[-8:]
        i = pl.program_id(0)

        @pl.when(i == 0)
        def _():
            wscr[...] = w_ref[...].astype(BF16)
            wuk_scr[...] = wuk_ref[...].astype(BF16)
            wuv_scr[...] = wuv_ref[...].astype(BF16)

        xm = (_read_split(xp_ref, xs_ref, i) * (1.0 + mod_ref[1:2, :]) + mod_ref[0:1, :]).astype(BF16)
        z = lax.dot_general(xm, wscr[...], NT_DIMS, preferred_element_type=F32)
        n_q = MLA_HEADS * (MLA_NOPE + MLA_ROPE)
        zq_ref[...] = z[:, :n_q].astype(BF16)
        x = z[:, n_q:n_q + MLA_KV_RANK]
        c = x * lax.rsqrt(jnp.mean(x * x, axis=-1, keepdims=True) + RMS_EPS) * g_ref[...]
        kr_raw = z[:, n_q + MLA_KV_RANK:]

        @pl.when(i < NTOK_P // zq_ref.shape[0])
        def _():
            for b in range(ckvc_ref.shape[0]):
                ckvc_ref[b] = c[SEQ * b:SEQ * (b + 1), :]
                krc_ref[b] = kr_raw[SEQ * b:SEQ * (b + 1), :MLA_ROPE]

        lane = _lane_iota(kr_raw.shape)
        kr = _rope128(kr_raw, cos_ref[...], sin_ref[...], lane, MLA_ROPE // 2)
        wuk = wuk_scr[...]
        wuv = wuv_scr[...]
    else:
        ckv_ref, kr_ref, wuk_ref, wuv_ref, kcat_ref, vm_ref = refs
        c = ckv_ref[...]
        kr = kr_ref[...]
        wuk = wuk_ref[...].astype(BF16)
        wuv = wuv_ref[...].astype(BF16)
    cb = c.astype(BF16)
    kn = jnp.dot(cb, wuk, preferred_element_type=F32).astype(BF16)
    vv = jnp.dot(cb, wuv, preferred_element_type=F32).astype(BF16)
    krb = kr.astype(BF16)
    ones = jnp.ones((c.shape[0], LANES), BF16)
    for p in range(MLA_HEADS // 2):
        kcat_ref[:, 256 * p:256 * p + LANES] = kn[:, LANES * p:LANES * (p + 1)]
        kcat_ref[:, 256 * p + LANES:256 * (p + 1)] = krb
        vm_ref[:, 256 * p:256 * p + LANES] = vv[:, LANES * p:LANES * (p + 1)]
        vm_ref[:, 256 * p + LANES:256 * (p + 1)] = ones


def _mla_proj_kv(x, mod, w_t, cos_t, sin_t, kv_norm_g, w_uk, w_uv, j, prev):
    tm = MLA_PROJ_TM
    npt = NTOK_P // tm
    nst = DEC_SEQ // tm
    nb = tm // SEQ
    n_q = MLA_HEADS * (MLA_NOPE + MLA_ROPE)

    def tab(i):
        return (jnp.where(i < npt, 0, 1 + (i - npt) % nst), 0)

    def cache_idx(i):
        return (jnp.minimum(i, npt - 1), j, 0, 0)

    in_specs = _split_specs(tm, D_MODEL, lambda i: i, x[2]) + [
        pl.BlockSpec((None, 6, D_MODEL), lambda i: (_group_of_tile(i, tm), 0, 0)),
        pl.BlockSpec((MLA_COLS, D_MODEL), lambda i: (0, 0)),
        pl.BlockSpec((tm, LANES), tab),
        pl.BlockSpec((tm, LANES), tab),
        pl.BlockSpec((None, 1, MLA_KV_RANK), lambda i: (j, 0, 0)),
        pl.BlockSpec((None, MLA_KV_RANK, 512), lambda i: (j, 0, 0)),
        pl.BlockSpec((None, MLA_KV_RANK, 512), lambda i: (j, 0, 0)),
    ]
    args = [x[0], x[1], mod, w_t, cos_t, sin_t, kv_norm_g.reshape(-1, 1, MLA_KV_RANK), w_uk, w_uv]
    aliases = {}
    if prev is not None:
        aliases = {len(args): 0, len(args) + 1: 1}
        in_specs += [pl.BlockSpec(memory_space=pl.ANY)] * 2
        args += list(prev)
    return pl.pallas_call(
        functools.partial(_mla_kv_kernel, True),
        grid=(NTOK // tm,),
        in_specs=in_specs,
        out_specs=[
            pl.BlockSpec((nb, None, SEQ, MLA_KV_RANK), cache_idx),
            pl.BlockSpec((nb, None, SEQ, MLA_ROPE), cache_idx),
            pl.BlockSpec((tm, n_q), lambda i: (i, 0)),
            pl.BlockSpec((tm, 1024), lambda i: (i, 0)),
            pl.BlockSpec((tm, 1024), lambda i: (i, 0)),
        ],
        out_shape=[
            jax.ShapeDtypeStruct((BATCH, N_EVEN, SEQ, MLA_KV_RANK), F32),
            jax.ShapeDtypeStruct((BATCH, N_EVEN, SEQ, MLA_ROPE), F32),
            jax.ShapeDtypeStruct((NTOK, n_q), BF16),
            jax.ShapeDtypeStruct((NTOK, 1024), BF16),
            jax.ShapeDtypeStruct((NTOK, 1024), BF16),
        ],
        scratch_shapes=[
            pltpu.VMEM((MLA_COLS, D_MODEL), BF16),
            pltpu.VMEM((MLA_KV_RANK, 512), BF16),
            pltpu.VMEM((MLA_KV_RANK, 512), BF16),
        ],
        input_output_aliases=aliases,
        compiler_params=_params(("arbitrary",)),
        name="mla_proj_kv",
    )(*args)


def _mla_kv_ctx(cache_ckv, kr_tiled, w_uk, w_uv, j):
    return pl.pallas_call(
        functools.partial(_mla_kv_kernel, False),
        grid=(DEC_BATCH,),
        in_specs=[
            pl.BlockSpec((None, None, PAST_LEN, MLA_KV_RANK), lambda b: (b, j, 0, 0)),
            pl.BlockSpec((None, PAST_LEN, LANES), lambda b: (b, 0, 0)),
            pl.BlockSpec((None, MLA_KV_RANK, 512), lambda b: (j, 0, 0)),
            pl.BlockSpec((None, MLA_KV_RANK, 512), lambda b: (j, 0, 0)),
        ],
        out_specs=[
            pl.BlockSpec((PAST_LEN, 1024), lambda b: (b, 0)),
            pl.BlockSpec((PAST_LEN, 1024), lambda b: (b, 0)),
        ],
        out_shape=[
            jax.ShapeDtypeStruct((DEC_BATCH * PAST_LEN, 1024), BF16),
            jax.ShapeDtypeStruct((DEC_BATCH * PAST_LEN, 1024), BF16),
        ],
        compiler_params=_params(("parallel",)),
        name="mla_kv_ctx",
    )(cache_ckv, kr_tiled, w_uk, w_uv)


def _mla_attn_kernel(latent, *refs):
    if latent:
        qn_ref, qr_ref, cos_ref, sin_ref, kc_ref, vc_ref, kn_ref, vn_ref, o_ref = refs
        k_refs, v_refs = (kc_ref, kn_ref), (vc_ref, vn_ref)
    else:
        qn_ref, qr_ref, kn_ref, vn_ref, o_ref = refs
        k_refs, v_refs = (kn_ref,), (vn_ref,)
    tq = ATT_TQ if latent else SEQ
    n_seq = qn_ref.shape[0] // tq
    lane = _lane_iota((tq, LANES))
    scale = (MLA_NOPE + MLA_ROPE) ** -0.5
    for sq in range(n_seq):
        rows = slice(tq * sq, tq * (sq + 1))
        krows = slice(None) if latent else rows
        qr_cols = []
        for cidx in range(2):
            x = qr_ref[rows, LANES * cidx:LANES * (cidx + 1)].astype(F32)
            if latent:
                x = _rope128(x, cos_ref[...], sin_ref[...], lane, MLA_ROPE // 2)
            qr_cols.append(x)
        o_prev = None
        for h in range(MLA_HEADS):
            p, half = divmod(h, 2)
            cidx, slot = divmod(h, 4)
            qa = jnp.where(_div_pow2(lane, MLA_NOPE) == half,
                           qn_ref[rows, LANES * p:LANES * (p + 1)].astype(F32), 0.0)
            qb = jnp.where(_div_pow2(lane, MLA_ROPE) == slot, qr_cols[cidx], 0.0)
            qcat = jnp.concatenate([qa, qb], axis=1).astype(BF16)
            pair_cols = slice(256 * p, 256 * (p + 1))
            s_list = [lax.dot_general(qcat, k_ref[krows, pair_cols], NT_DIMS, preferred_element_type=F32)
                      for k_ref in k_refs]
            o = _pv_normalised(_exp_parts(s_list, scale), [v_ref[krows, pair_cols] for v_ref in v_refs])
            if half == 0:
                o_prev = o
            else:
                o_ref[rows, LANES * p:LANES * (p + 1)] = jnp.where(lane < MLA_DV, o_prev, o).astype(BF16)


def _mla_attn_prompt(zmla, kcat, vm):
    rows = 4 * SEQ
    return pl.pallas_call(
        functools.partial(_mla_attn_kernel, False),
        grid=(NTOK_P // rows,),
        in_specs=[
            pl.BlockSpec((rows, 512), lambda b: (b, 0)),
            pl.BlockSpec((rows, 256), lambda b: (b, 2)),
            pl.BlockSpec((rows, 1024), lambda b: (b, 0)),
            pl.BlockSpec((rows, 1024), lambda b: (b, 0)),
        ],
        out_specs=pl.BlockSpec((rows, 512), lambda b: (b, 0)),
        out_shape=jax.ShapeDtypeStruct((NTOK_P, 512), BF16),
        compiler_params=_params(("parallel",)),
        name="mla_attn_prompt",
    )(zmla, zmla, kcat, vm)


def _mla_attn_latent(zmla, cos_t, sin_t, kcat_ctx, vm_ctx, kcat, vm):
    nq = DEC_SEQ // ATT_TQ
    row0 = NTOK_P // ATT_TQ
    seq0 = NTOK_P // DEC_SEQ
    return pl.pallas_call(
        functools.partial(_mla_attn_kernel, True),
        grid=(DEC_BATCH, nq),
        in_specs=[
            pl.BlockSpec((ATT_TQ, 512), lambda b, q: (row0 + b * nq + q, 0)),
            pl.BlockSpec((ATT_TQ, 256), lambda b, q: (row0 + b * nq + q, 2)),
            pl.BlockSpec((ATT_TQ, LANES), lambda b, q: (q, 0)),
            pl.BlockSpec((ATT_TQ, LANES), lambda b, q: (q, 0)),
            pl.BlockSpec((PAST_LEN, 1024), lambda b, q: (b, 0)),
            pl.BlockSpec((PAST_LEN, 1024), lambda b, q: (b, 0)),
            pl.BlockSpec((DEC_SEQ, 1024), lambda b, q: (seq0 + b, 0)),
            pl.BlockSpec((DEC_SEQ, 1024), lambda b, q: (seq0 + b, 0)),
        ],
        out_specs=pl.BlockSpec((ATT_TQ, 512), lambda b, q: (b * nq + q, 0)),
        out_shape=jax.ShapeDtypeStruct((NTOK_S, 512), BF16),
        compiler_params=_params(("parallel", "arbitrary")),
        name="mla_attn_latent",
    )(zmla, zmla, cos_t, sin_t, kcat_ctx, vm_ctx, kcat, vm)


def _log_sigmoid(x):
    return jnp.minimum(x, 0.0) - jnp.log1p(jnp.exp(-jnp.abs(x)))


def _retention_kernel(seq_len, n_seq, has_init, emit_state, has_prev, *refs):
    refs = list(refs)
    decf_ref, decb_ref, q_ref, k_ref, v_ref, g_ref = refs[:6]
    refs = refs[6:]
    if has_init:
        sf0_ref, sb0_ref = refs[:2]
        refs = refs[2:]
    if has_prev:
        refs = refs[2:]
    o_ref = refs[0]
    refs = refs[1:]
    if emit_state:
        sf_ref, sb_ref = refs[:2]
        refs = refs[2:]
    of_scr, ob_scr, dec_scr, wts_scr = refs

    c = RET_CHUNK
    n_chunks = seq_len // c
    pair = pl.program_id(0)
    lane = _lane_iota((c, LANES))
    zeros_half = jnp.zeros((RET_DK, RET_DV), F32)

    def log_gammas(half):
        head = 2 * pair + half
        return (_log_sigmoid(decf_ref[pl.ds(head, 1), :]),
                _log_sigmoid(decb_ref[pl.ds(head, 1), :]))

    @pl.when(pl.program_id(1) == 0)
    def _():
        ri = lax.broadcasted_iota(jnp.int32, (c, c), 0)
        ci = lax.broadcasted_iota(jnp.int32, (c, c), 1)
        rel = (ri - ci).astype(F32)
        row = lax.broadcasted_iota(jnp.int32, (c, LANES), 0).astype(F32)
        for half in range(2):
            lgf, lgb = log_gammas(half)
            dec_scr[half] = (jnp.where(rel >= 0, jnp.exp(lgf[:, 0:1] * jnp.maximum(rel, 0.0)), 0.0)
                             + jnp.where(rel <= 0, jnp.exp(lgb[:, 0:1] * jnp.maximum(-rel, 0.0)), 0.0))
            wts_scr[half, 0] = jnp.exp(lgf * (row + 1.0))
            wts_scr[half, 1] = jnp.exp(lgf * (c - 1.0 - row))
            wts_scr[half, 2] = jnp.exp(lgb * (c - row))
            wts_scr[half, 3] = jnp.exp(lgb * row)

    cross = has_init or n_chunks > 1
    chains = [(sq, half) for sq in range(n_seq) for half in range(2)]
    chunk_decay = []
    for half in range(2):
        lgf, lgb = log_gammas(half)
        chunk_decay.append((jnp.exp(lgf * float(c)), jnp.exp(lgb * float(c))))

    def rows_of(sq, n):
        start = sq * seq_len + n * c
        return pl.ds(start if isinstance(n, int) else pl.multiple_of(start, c), c)

    def load(sq, half, n):
        rows = rows_of(sq, n)
        vsl = slice(RET_DV * half, RET_DV * (half + 1))
        qm = jnp.where(_div_pow2(lane, RET_DK) == half, q_ref[rows, :].astype(F32), 0.0)
        kk = k_ref[rows, :].astype(F32) * (RET_DK ** -0.5)
        return rows, vsl, qm, kk, v_ref[rows, vsl]

    def init_state(s0_ref, sq, half):
        if not has_init:
            return jnp.zeros((LANES, RET_DV), F32)
        s0 = s0_ref[sq, half]
        return jnp.concatenate([s0, zeros_half] if half == 0 else [zeros_half, s0], axis=0)

    def fwd_step(sq, half, n, s_f):
        rows, vsl, qm, kk, vb = load(sq, half, n)
        s = lax.dot_general(qm.astype(BF16), kk.astype(BF16), NT_DIMS, preferred_element_type=F32)
        o = jnp.dot((s * dec_scr[half]).astype(BF16), vb, preferred_element_type=F32)
        if cross:
            o = o + jnp.dot((qm * wts_scr[half, 0]).astype(BF16), s_f.astype(BF16), preferred_element_type=F32)
        of_scr[rows, vsl] = o
        kv = lax.dot_general((kk * wts_scr[half, 1]).astype(BF16), vb, TN_DIMS, preferred_element_type=F32)
        return chunk_decay[half][0] * s_f + kv

    def bwd_step(sq, half, n, s_b):
        rows, vsl, qm, kk, vb = load(sq, half, n)
        if cross:
            ob_scr[rows, vsl] = jnp.dot((qm * wts_scr[half, 2]).astype(BF16), s_b.astype(BF16),
                                        preferred_element_type=F32)
        kv = lax.dot_general((kk * wts_scr[half, 3]).astype(BF16), vb, TN_DIMS, preferred_element_type=F32)
        return chunk_decay[half][1] * s_b + kv

    def finish(sq, n):
        rows = rows_of(sq, n)
        o2 = of_scr[rows, :] + ob_scr[rows, :] if cross else of_scr[rows, :]
        for half in range(2):
            vsl = slice(RET_DV * half, RET_DV * (half + 1))
            o = o2[:, vsl]
            mu = jnp.mean(o, axis=-1, keepdims=True)
            d = o - mu
            var = jnp.mean(d * d, axis=-1, keepdims=True)
            o_ref[rows, vsl] = (_silu(g_ref[rows, vsl].astype(F32)) * (d * lax.rsqrt(var + LN_EPS))).astype(BF16)

    s_f = tuple(init_state(sf0_ref if has_init else None, sq, half) for sq, half in chains)
    s_b = tuple(init_state(sb0_ref if has_init else None, sq, half) for sq, half in chains)
    if n_chunks == 1:
        s_f = tuple(fwd_step(sq, half, 0, s) for (sq, half), s in zip(chains, s_f))
        s_b = tuple(bwd_step(sq, half, 0, s) for (sq, half), s in zip(chains, s_b))
        for sq in range(n_seq):
            finish(sq, 0)
    else:
        def scan_step(n, carry):
            sf, sb = carry
            sf = tuple(fwd_step(sq, half, n, s) for (sq, half), s in zip(chains, sf))
            sb = tuple(bwd_step(sq, half, n_chunks - 1 - n, s) for (sq, half), s in zip(chains, sb))
            return sf, sb

        s_f, s_b = lax.fori_loop(0, n_chunks, scan_step, (s_f, s_b))

        def finish_step(n, carry):
            for sq in range(n_seq):
                finish(sq, n)
            return carry

        lax.fori_loop(0, n_chunks, finish_step, 0)
    if emit_state:
        for (sq, half), sf, sb in zip(chains, s_f, s_b):
            sf_ref[sq, half] = sf[RET_DK * half:RET_DK * (half + 1), :]
            sb_ref[sq, half] = sb[RET_DK * half:RET_DK * (half + 1), :]


def _retention(zret, decf, decb, j, latent, state_f=None, state_b=None, prev=None):
    seq_len = DEC_SEQ if latent else SEQ
    n_seq = DEC_BATCH if latent else 8
    n_b = (DEC_BATCH if latent else BATCH) // n_seq
    rows = n_seq * seq_len
    row0 = NTOK_P // rows if latent else 0
    n_pairs = RET_HEADS // 2
    in_specs = [
        pl.BlockSpec((None, RET_HEADS, LANES), lambda p, b: (j, 0, 0)),
        pl.BlockSpec((None, RET_HEADS, LANES), lambda p, b: (j, 0, 0)),
        pl.BlockSpec((rows, LANES), lambda p, b: (row0 + b, p)),
        pl.BlockSpec((rows, LANES), lambda p, b: (row0 + b, 4 + p)),
        pl.BlockSpec((rows, 256), lambda p, b: (row0 + b, 4 + p)),
        pl.BlockSpec((rows, 256), lambda p, b: (row0 + b, 8 + p)),
    ]
    args = [decf, decb, zret, zret, zret, zret]
    out_specs = [pl.BlockSpec((rows, 256), lambda p, b: (b, p))]
    out_shape = [jax.ShapeDtypeStruct((n_b * rows, RET_HEADS * RET_DV), BF16)]
    aliases = {}
    if latent:
        st_spec = pl.BlockSpec((n_seq, None, 2, RET_DK, RET_DV), lambda p, b: (b, j, p, 0, 0))
        in_specs += [st_spec, st_spec]
        args += [state_f, state_b]
    else:
        st_spec = pl.BlockSpec((n_seq, None, 2, RET_DK, RET_DV), lambda p, b: (b, j, p, 0, 0))
        out_specs += [st_spec, st_spec]
        out_shape += [jax.ShapeDtypeStruct((BATCH, N_EVEN, RET_HEADS, RET_DK, RET_DV), F32)] * 2
        if prev is not None:
            aliases = {len(args): 1, len(args) + 1: 2}
            in_specs += [pl.BlockSpec(memory_space=pl.ANY)] * 2
            args += list(prev)
    return pl.pallas_call(
        functools.partial(_retention_kernel, seq_len, n_seq, latent, not latent, bool(aliases)),
        grid=(n_pairs, n_b),
        in_specs=in_specs,
        out_specs=out_specs,
        out_shape=out_shape,
        scratch_shapes=[
            pltpu.VMEM((rows, 2 * RET_DV), F32),
            pltpu.VMEM((rows, 2 * RET_DV), F32),
            pltpu.VMEM((2, RET_CHUNK, RET_CHUNK), F32),
            pltpu.VMEM((2, 4, RET_CHUNK, LANES), F32),
        ],
        input_output_aliases=aliases,
        compiler_params=_params(("arbitrary", "arbitrary")),
        name="retention_latent" if latent else "retention_prompt",
    )(*args)


def _diff_prep_kernel(k_ref, v_ref, cos_ref, sin_ref, kr_ref, va_ref):
    lane = _lane_iota(cos_ref.shape)
    cos = cos_ref[...]
    sin = sin_ref[...]
    ones = jnp.ones(cos_ref.shape, BF16)
    for h in range(DIFF_HEADS):
        sl = slice(LANES * h, LANES * (h + 1))
        kr_ref[:, sl] = _rope128(k_ref[:, sl].astype(F32), cos, sin, lane, DIFF_DH // 2).astype(BF16)
        va_ref[:, 256 * h:256 * h + LANES] = v_ref[:, sl].astype(BF16)
        va_ref[:, 256 * h + LANES:256 * (h + 1)] = ones


def _diff_prep(zodd, cos_t, sin_t):
    tm = 512
    row0 = NTOK_P // tm
    nst = DEC_SEQ // tm
    return pl.pallas_call(
        _diff_prep_kernel,
        grid=(NTOK_S // tm,),
        in_specs=[
            pl.BlockSpec((tm, 1024), lambda i: (row0 + i, 1)),
            pl.BlockSpec((tm, 1024), lambda i: (row0 + i, 2)),
            pl.BlockSpec((tm, LANES), lambda i: (i % nst, 0)),
            pl.BlockSpec((tm, LANES), lambda i: (i % nst, 0)),
        ],
        out_specs=[pl.BlockSpec((tm, 1024), lambda i: (i, 0)), pl.BlockSpec((tm, 2048), lambda i: (i, 0))],
        out_shape=[jax.ShapeDtypeStruct((NTOK_S, 1024), BF16), jax.ShapeDtypeStruct((NTOK_S, 2048), BF16)],
        compiler_params=_params(("parallel",)),
        name="diff_rope_keys",
    )(zodd, zodd, cos_t, sin_t)


def _diff_attn_kernel(latent, lam_init, *refs):
    if latent:
        (lam_ref, ng_ref, q_ref, cos_ref, sin_ref, kc_ref, vc_ref, kn_ref, vn_ref, o_ref) = refs
    else:
        lam_ref, ng_ref, q_ref, k_ref, v_ref = refs[:5]
        o_ref, kout_ref, vout_ref = refs[-3:]
    tq = ATT_TQ if latent else SEQ
    n_seq = q_ref.shape[0] // tq
    lane = _lane_iota((tq, LANES))
    scale = DIFF_DH ** -0.5
    lp = lam_ref[...]
    lam = (jnp.exp(jnp.sum(lp[0:1, :] * lp[1:2, :], axis=-1, keepdims=True))
           - jnp.exp(jnp.sum(lp[2:3, :] * lp[3:4, :], axis=-1, keepdims=True)) + lam_init)
    ng = ng_ref[...]
    ones = jnp.ones((PAST_LEN if latent else tq, LANES), BF16)
    for sq, h in [(sq, h) for sq in range(n_seq) for h in range(DIFF_HEADS)]:
        sl = slice(LANES * h, LANES * (h + 1))
        rows = slice(tq * sq, tq * (sq + 1))
        qh = q_ref[rows, sl].astype(F32)
        if latent:
            qh = _rope128(qh, cos_ref[...], sin_ref[...], lane, DIFF_DH // 2)
            k_list = [kc_ref[h].astype(BF16), kn_ref[:, sl]]
            v_list = [jnp.concatenate([vc_ref[h].astype(BF16), ones], axis=1),
                      vn_ref[:, 256 * h:256 * (h + 1)]]
        else:
            kh = k_ref[rows, sl]
            vh = v_ref[rows, sl]
            kout_ref[sq, h] = kh.astype(F32)
            vout_ref[sq, h] = vh.astype(F32)
            k_list = [kh.astype(BF16)]
            v_list = [jnp.concatenate([vh.astype(BF16), ones], axis=1)]
        q1 = jnp.where(lane < DIFF_DH, qh, 0.0).astype(BF16)
        q2 = jnp.where(lane >= DIFF_DH, qh, 0.0).astype(BF16)
        s1 = [lax.dot_general(q1, kk, NT_DIMS, preferred_element_type=F32) for kk in k_list]
        s2 = [lax.dot_general(q2, kk, NT_DIMS, preferred_element_type=F32) for kk in k_list]
        o = _pv_normalised(_exp_parts(s1, scale), v_list) - lam * _pv_normalised(_exp_parts(s2, scale), v_list)
        y = o * lax.rsqrt(jnp.mean(o * o, axis=-1, keepdims=True) + RMS_EPS) * ng
        o_ref[rows, sl] = (y * (1.0 - lam_init)).astype(BF16)


def _diff_attn_prompt(zodd, lam_p, norm_g, j, lam_init, prev):
    n_seq = 2
    rows = n_seq * SEQ
    cache_shape = jax.ShapeDtypeStruct((BATCH, N_ODD, DIFF_HEADS, SEQ, LANES), F32)
    cache_spec = pl.BlockSpec((n_seq, None, DIFF_HEADS, SEQ, LANES), lambda b: (b, j, 0, 0, 0))
    in_specs = [
        pl.BlockSpec((None, 4, DIFF_DH), lambda b: (j, 0, 0)),
        pl.BlockSpec((None, 1, DIFF_DV), lambda b: (j, 0, 0)),
        pl.BlockSpec((rows, 1024), lambda b: (b, 0)),
        pl.BlockSpec((rows, 1024), lambda b: (b, 1)),
        pl.BlockSpec((rows, 1024), lambda b: (b, 2)),
    ]
    args = [lam_p, norm_g.reshape(-1, 1, DIFF_DV), zodd, zodd, zodd]
    aliases = {}
    if prev is not None:
        aliases = {len(args): 1, len(args) + 1: 2}
        in_specs += [pl.BlockSpec(memory_space=pl.ANY)] * 2
        args += list(prev)
    return pl.pallas_call(
        functools.partial(_diff_attn_kernel, False, lam_init),
        grid=(BATCH // n_seq,),
        in_specs=in_specs,
        out_specs=[pl.BlockSpec((rows, 1024), lambda b: (b, 0)), cache_spec, cache_spec],
        out_shape=[jax.ShapeDtypeStruct((NTOK_P, 1024), BF16), cache_shape, cache_shape],
        input_output_aliases=aliases,
        compiler_params=_params(("arbitrary",)),
        name="diff_attn_prompt",
    )(*args)


def _diff_attn_latent(zodd, lam_p, norm_g, cos_t, sin_t, cache_k, cache_v, k_rot, v_aug, j, lam_init):
    nq = DEC_SEQ // ATT_TQ
    row0 = NTOK_P // ATT_TQ
    ctx_spec = pl.BlockSpec((None, None, DIFF_HEADS, PAST_LEN, LANES), lambda b, q: (b, j, 0, 0, 0))
    return pl.pallas_call(
        functools.partial(_diff_attn_kernel, True, lam_init),
        grid=(DEC_BATCH, nq),
        in_specs=[
            pl.BlockSpec((None, 4, DIFF_DH), lambda b, q: (j, 0, 0)),
            pl.BlockSpec((None, 1, DIFF_DV), lambda b, q: (j, 0, 0)),
            pl.BlockSpec((ATT_TQ, 1024), lambda b, q: (row0 + b * nq + q, 0)),
            pl.BlockSpec((ATT_TQ, LANES), lambda b, q: (q, 0)),
            pl.BlockSpec((ATT_TQ, LANES), lambda b, q: (q, 0)),
            ctx_spec,
            ctx_spec,
            pl.BlockSpec((DEC_SEQ, 1024), lambda b, q: (b, 0), pipeline_mode=pl.Buffered(1)),
            pl.BlockSpec((DEC_SEQ, 2048), lambda b, q: (b, 0), pipeline_mode=pl.Buffered(1)),
        ],
        out_specs=pl.BlockSpec((ATT_TQ, 1024), lambda b, q: (b * nq + q, 0)),
        out_shape=jax.ShapeDtypeStruct((NTOK_S, 1024), BF16),
        compiler_params=_params(("parallel", "arbitrary")),
        name="diff_attn_latent",
    )(lam_p, norm_g.reshape(-1, 1, DIFF_DV), zodd, cos_t, sin_t, cache_k, cache_v, k_rot, v_aug)


def _mla_weight_t(w_in_t, j):
    base = RET_COLS
    mq = w_in_t[j, base:base + MLA_HEADS * (MLA_NOPE + MLA_ROPE)].reshape(MLA_HEADS, MLA_NOPE + MLA_ROPE, D_MODEL)
    qn = mq[:, :MLA_NOPE].reshape(MLA_HEADS * MLA_NOPE, D_MODEL)
    qr = mq[:, MLA_NOPE:].reshape(MLA_HEADS * MLA_ROPE, D_MODEL)
    ckv0 = base + MLA_HEADS * (MLA_NOPE + MLA_ROPE)
    ckv = w_in_t[j, ckv0:ckv0 + MLA_KV_RANK]
    kr = w_in_t[j, ckv0 + MLA_KV_RANK:]
    return jnp.concatenate([qn, qr, ckv, jnp.tile(kr, (LANES // MLA_ROPE, 1))], axis=0)


def kernel(x_prompt, x_sample, state_ret_fwd, state_ret_bwd, cache_mla_ckv, cache_mla_krope, cache_diff_k, cache_diff_v, c, c_ctx, ada_w, ada_b, ln1_g, ln1_b, ln2_g, ln2_b, ev_w_in, ev_w_out, ret_decay_fwd, ret_decay_bwd, mla_kv_norm_g, mla_w_uk, mla_w_uv, od_w_in, od_w_out, diff_lambda, diff_norm_g, moe_w_group, moe_b_group, moe_w_expert, moe_b_expert, moe_w1, moe_w3, moe_w2):
    x = (x_prompt.reshape(NTOK_P, D_MODEL), x_sample.reshape(NTOK_S, D_MODEL), 0)
    cond =jnp.concatenate([c_ctx[None, :], c, jnp.zeros((N_COND - 1 - DEC_BATCH, D_MODEL), F32)], axis=0)
    mods = _ada_all(cond, ada_w, ada_b).reshape(DEPTH, N_COND, 6, D_MODEL)

    cos_m, sin_m = _rope_tables(MLA_ROPE)
    cos_d, sin_d = _rope_tables(DIFF_DH)
    ident = MLA_PROJ_TM
    cos_m_id = jnp.concatenate([jnp.ones((ident, LANES), F32), cos_m], axis=0)
    sin_m_id = jnp.concatenate([jnp.zeros((ident, LANES), F32), sin_m], axis=0)
    decf = jnp.broadcast_to(ret_decay_fwd[:, :, None], ret_decay_fwd.shape + (LANES,))
    decb = jnp.broadcast_to(ret_decay_bwd[:, :, None], ret_decay_bwd.shape + (LANES,))

    ev_w_in_t = jnp.swapaxes(ev_w_in, 1, 2)
    pad = LANES - MOE_GROUPS - MOE_EXPERTS
    ret_states = mla_caches = diff_caches = None
    for i in range(DEPTH):
        j = i // 2
        mod = mods[i]
        w_router = jnp.concatenate([moe_w_group[i], moe_w_expert[i], jnp.zeros((D_MODEL, pad), F32)], axis=1)
        b_router = jnp.concatenate([moe_b_group[i], moe_b_expert[i], jnp.zeros((pad,), F32)])[None, :]
        if i % 2 == 0:
            zret = _mm_mod(x, mod, ev_w_in_t, (j,), RET_COLS, 1536, "in_proj_retention", w_is_transposed=True)
            *mla_caches, zmla, kcat, vm = _mla_proj_kv(x, mod, _mla_weight_t(ev_w_in_t, j), cos_m_id, sin_m_id,
                                                       mla_kv_norm_g, mla_w_uk, mla_w_uv, j, mla_caches)
            kr_ctx = jnp.tile(cache_mla_krope[:, j], (1, 1, LANES // MLA_ROPE))
            kcat_ctx, vm_ctx = _mla_kv_ctx(cache_mla_ckv, kr_ctx, mla_w_uk, mla_w_uv, j)
            a_ret_p, *ret_states = _retention(zret, decf, decb, j, False, prev=ret_states)
            (a_ret_s,) = _retention(zret, decf, decb, j, True, state_ret_fwd, state_ret_bwd)
            a_mla_p = _mla_attn_prompt(zmla, kcat, vm)
            a_mla_s = _mla_attn_latent(zmla, cos_m, sin_m, kcat_ctx, vm_ctx, kcat, vm)
            x1, meta, counts = _mm_ln([(a_ret_p, a_ret_s), (a_mla_p, a_mla_s)], ev_w_out, j, x, mod,
                                      ln1_g, ln1_b, w_router, b_router, i)
        else:
            lam_init = 0.8 - 0.6 * math.exp(-0.3 * i)
            zodd = _mm_mod(x, mod, od_w_in, (j,), 3072, 1536, "in_proj_diff")
            a_p, *diff_caches = _diff_attn_prompt(zodd, diff_lambda, diff_norm_g, j, lam_init, diff_caches)
            k_rot, v_aug = _diff_prep(zodd, cos_d, sin_d)
            a_s = _diff_attn_latent(zodd, diff_lambda, diff_norm_g, cos_d, sin_d, cache_diff_k, cache_diff_v,
                                    k_rot, v_aug, j, lam_init)
            x1, meta, counts = _mm_ln([(a_p, a_s)], od_w_out, j, x, mod, ln1_g, ln1_b, w_router, b_router, i)
        y = _moe(x1, _moe_plan(meta, counts), mods, w_router, b_router, moe_w1, moe_w3, moe_w2, ln2_g, ln2_b, i)
        x = (y, y, NTOK_P)

    y_prompt = x[0][:NTOK_P].reshape(BATCH, SEQ, D_MODEL)
    y_sample = x[1][NTOK_P:NTOK].reshape(DEC_BATCH, DEC_SEQ, D_MODEL)
    return (y_prompt, y_sample, ret_states[0], ret_states[1], mla_caches[0], mla_caches[1],
            diff_caches[0], diff_caches[1])
```

```python
import functools
import math

import jax
import jax.numpy as jnp
from jax import lax
from jax.experimental import pallas as pl
from jax.experimental.pallas import tpu as pltpu

D_MODEL = 1024
BATCH = 32
SEQ = 256
DEPTH = 4
N_EVEN = 2
N_ODD = 2
DEC_BATCH = 2
DEC_SEQ = 2048
PAST_LEN = 256
GRID_W = 64
LN_EPS = 1e-5
RMS_EPS = 1e-6
DEEPNORM_ALPHA = (2.0 * DEPTH) ** 0.25
ROPE_BASE = 10000.0
RET_HEADS = 8
RET_DK = 64
RET_DV = 128
MLA_HEADS = 8
MLA_NOPE = 64
MLA_ROPE = 32
MLA_DV = 64
MLA_KV_RANK = 256
DIFF_HEADS = 8
DIFF_DH = 64
DIFF_DV = 128
MOE_GROUPS = 4
MOE_PER_GROUP = 4
MOE_EXPERTS = 16
MOE_FF = 256

NTOK_P = BATCH * SEQ
NTOK_S = DEC_BATCH * DEC_SEQ
NTOK = NTOK_P + NTOK_S
N_COND = 8
LANES = 128
SUBLANES = 8
RET_COLS = 3072
MLA_COLS = 1152
MLA_PROJ_TM = 1024
ATT_TQ = 512
RET_CHUNK = 256
MOE_TILE = 512
MOE_TILES = (NTOK + MOE_GROUPS * (MOE_TILE - 1)) // MOE_TILE
MOE_ROWS = MOE_TILES * MOE_TILE
MOE_KEY_BASE = 16384
VMEM_LIMIT = 56 * 1024 * 1024
LOG2E = 1.4426950408889634

F32 = jnp.float32
BF16 = jnp.bfloat16
NT_DIMS = (((1,), (1,)), ((), ()))
TN_DIMS = (((0,), (0,)), ((), ()))


def _params(sem):
    return pltpu.CompilerParams(dimension_semantics=sem, vmem_limit_bytes=VMEM_LIMIT)


def _group_of_tile(i, tm):
    npt = NTOK_P // tm
    nst = DEC_SEQ // tm
    return jnp.where(i < npt, 0, 1 + (i - npt) // nst)


def _split_specs(tm, width, m_of, s_row0=0):
    npt = NTOK_P // tm
    s_blk0 = s_row0 // tm
    return [pl.BlockSpec((tm, width), lambda *g: (jnp.minimum(m_of(*g), npt - 1), 0)),
            pl.BlockSpec((tm, width), lambda *g: (jnp.maximum(m_of(*g) - npt, 0) + s_blk0, 0))]


def _read_split(p_ref, s_ref, m):
    return jnp.where(m < NTOK_P // p_ref.shape[0], p_ref[...], s_ref[...])


def _silu(x):
    return x * (1.0 / (1.0 + jnp.exp(-x)))


def _layer_norm(r, g, b):
    mu = jnp.mean(r, axis=-1, keepdims=True)
    d = r - mu
    var = jnp.mean(d * d, axis=-1, keepdims=True)
    return d * lax.rsqrt(var + LN_EPS) * g + b


def _lane_iota(shape):
    return lax.broadcasted_iota(jnp.int32, shape, 1)


def _div_pow2(x, d):
    assert d & (d - 1) == 0
    return jnp.right_shift(x, d.bit_length() - 1)


def _mod_pow2(x, d):
    assert d & (d - 1) == 0
    return jnp.bitwise_and(x, d - 1)


def _ada_kernel(c_ref, w_ref, b_ref, o_ref):
    h = _silu(c_ref[...]).astype(BF16)
    o_ref[...] = jnp.dot(h, w_ref[...].astype(BF16), preferred_element_type=F32) + b_ref[...]


def _ada_all(cond, ada_w, ada_b):
    tn = 768
    return pl.pallas_call(
        _ada_kernel,
        grid=(DEPTH, 6 * D_MODEL // tn),
        in_specs=[
            pl.BlockSpec((N_COND, D_MODEL), lambda l, n: (0, 0)),
            pl.BlockSpec((None, D_MODEL, tn), lambda l, n: (l, 0, n)),
            pl.BlockSpec((None, 1, tn), lambda l, n: (l, 0, n)),
        ],
        out_specs=pl.BlockSpec((None, N_COND, tn), lambda l, n: (l, 0, n)),
        out_shape=jax.ShapeDtypeStruct((DEPTH, N_COND, 6 * D_MODEL), F32),
        compiler_params=_params(("parallel", "parallel")),
        name="ada_modulation",
    )(cond, ada_w, ada_b.reshape(DEPTH, 1, 6 * D_MODEL))


def _mm_mod_kernel(w_is_transposed, xp_ref, xs_ref, mod_ref, w_ref, o_ref, wscr):
    m = pl.program_id(1)

    @pl.when(m == 0)
    def _():
        wscr[...] = w_ref[...].astype(BF16)

    sh = mod_ref[0:1, :]
    sc = mod_ref[1:2, :]
    xm = (_read_split(xp_ref, xs_ref, m) * (1.0 + sc) + sh).astype(BF16)
    if w_is_transposed:
        z = lax.dot_general(xm, wscr[...], NT_DIMS, preferred_element_type=F32)
    else:
        z = jnp.dot(xm, wscr[...], preferred_element_type=F32)
    o_ref[...] = z.astype(o_ref.dtype)


def _mm_mod(x, mod, w, w_index, n_cols, tn, name, w_is_transposed=False):
    tm = 1024
    if w_is_transposed:
        w_spec = pl.BlockSpec((None,) * len(w_index) + (tn, D_MODEL), lambda n, m: tuple(w_index) + (n, 0))
        w_scratch = pltpu.VMEM((tn, D_MODEL), BF16)
    else:
        w_spec = pl.BlockSpec((None,) * len(w_index) + (D_MODEL, tn), lambda n, m: tuple(w_index) + (0, n))
        w_scratch = pltpu.VMEM((D_MODEL, tn), BF16)
    return pl.pallas_call(
        functools.partial(_mm_mod_kernel, w_is_transposed),
        grid=(n_cols // tn, NTOK // tm),
        in_specs=_split_specs(tm, D_MODEL, lambda n, m: m, x[2]) + [
            pl.BlockSpec((None, 6, D_MODEL), lambda n, m: (_group_of_tile(m, tm), 0, 0)),
            w_spec,
        ],
        out_specs=pl.BlockSpec((tm, tn), lambda n, m: (m, n)),
        out_shape=jax.ShapeDtypeStruct((NTOK, n_cols), BF16),
        scratch_shapes=[w_scratch],
        compiler_params=_params(("arbitrary", "arbitrary")),
        name=name,
    )(x[0], x[1], mod, w)


def _router_probs(xm, wr_ref, br_ref):
    rows = xm.shape[0]
    z = jnp.dot(xm, wr_ref[...].astype(BF16), preferred_element_type=F32) + br_ref[...]
    lane_i = _lane_iota((rows, LANES))
    lane = lane_i.astype(F32)
    gmask = lane_i < MOE_GROUPS
    zg = jnp.where(gmask, z, -jnp.inf)
    pg = jnp.exp(zg - jnp.max(zg, axis=-1, keepdims=True))
    g_prob = pg / jnp.sum(pg, axis=-1, keepdims=True)
    g_p = jnp.max(g_prob, axis=-1, keepdims=True)
    g_idx = jnp.min(jnp.where(gmask & (g_prob == g_p), lane, float(LANES)), axis=-1, keepdims=True)
    return z, lane_i, lane, g_p, g_idx


def _mm_ln_kernel(k_sizes, *refs):
    n_a = len(k_sizes)
    a_refs = refs[:2 * n_a]
    (w_ref, xp_ref, xs_ref, mod_ref, g_ref, b_ref, wr_ref, br_ref,
     o_ref, meta_ref, cnt_ref, wscr, tri_scr, carry_scr) = refs[2 * n_a:]
    m = pl.program_id(0)
    tm = o_ref.shape[0]

    @pl.when(m == 0)
    def _():
        wscr[...] = w_ref[...].astype(BF16)
        ri = lax.broadcasted_iota(jnp.int32, (tm, tm), 0)
        ci = lax.broadcasted_iota(jnp.int32, (tm, tm), 1)
        tri_scr[...] = jnp.where(ci < ri, 1.0, 0.0).astype(BF16)
        carry_scr[...] = jnp.zeros_like(carry_scr)

    y = None
    k0 = 0
    for i, ks in enumerate(k_sizes):
        a = _read_split(a_refs[2 * i], a_refs[2 * i + 1], m)
        part = jnp.dot(a, wscr[k0:k0 + ks, :], preferred_element_type=F32)
        y = part if y is None else y + part
        k0 += ks
    gate = mod_ref[2:3, :]
    r = DEEPNORM_ALPHA * _read_split(xp_ref, xs_ref, m) + gate * y
    x1 = _layer_norm(r, g_ref[...], b_ref[...])
    o_ref[...] = x1

    xm = (x1 * (1.0 + mod_ref[4:5, :]) + mod_ref[3:4, :]).astype(BF16)
    _, lane_i, lane, _, g_idx = _router_probs(xm, wr_ref, br_ref)
    onehot = jnp.where(lane == g_idx, 1.0, 0.0)
    before = jnp.dot(tri_scr[...], onehot.astype(BF16), preferred_element_type=F32) + carry_scr[0:1, :]
    rank = jnp.sum(jnp.where(lane == g_idx, before, 0.0), axis=-1, keepdims=True)
    key_col = g_idx * float(MOE_KEY_BASE) + rank
    row_i = lax.broadcasted_iota(jnp.int32, (tm, LANES), 0)
    diag = jnp.where(lane_i == _mod_pow2(row_i, LANES), key_col, 0.0)
    meta_ref[...] = jnp.sum(diag.reshape(tm // LANES, LANES, LANES), axis=1).astype(jnp.int32)
    total = carry_scr[0:1, :] + jnp.sum(onehot, axis=0, keepdims=True)
    carry_scr[...] = jnp.broadcast_to(total, carry_scr.shape)
    cnt_ref[...] = jnp.broadcast_to(total, cnt_ref.shape)


def _mm_ln(a_pairs, w, j, x, mod, ln_g, ln_b, w_router, b_router, layer):
    tm = 512
    k_sizes = tuple(ap.shape[1] for ap, _ in a_pairs)
    k_tot = sum(k_sizes)
    in_specs = []
    args = []
    for (ap, a_s), ks in zip(a_pairs, k_sizes):
        in_specs += _split_specs(tm, ks, lambda m: m)
        args += [ap, a_s]
    in_specs += [pl.BlockSpec((None, k_tot, D_MODEL), lambda m: (j, 0, 0))]
    in_specs += _split_specs(tm, D_MODEL, lambda m: m, x[2])
    in_specs += [
        pl.BlockSpec((None, 6, D_MODEL), lambda m: (_group_of_tile(m, tm), 0, 0)),
        pl.BlockSpec((None, 1, D_MODEL), lambda m: (layer, 0, 0)),
        pl.BlockSpec((None, 1, D_MODEL), lambda m: (layer, 0, 0)),
        pl.BlockSpec((D_MODEL, LANES), lambda m: (0, 0)),
        pl.BlockSpec((1, LANES), lambda m: (0, 0)),
    ]
    return pl.pallas_call(
        functools.partial(_mm_ln_kernel, k_sizes),
        grid=(NTOK // tm,),
        in_specs=in_specs,
        out_specs=[
            pl.BlockSpec((tm, D_MODEL), lambda m: (m, 0)),
            pl.BlockSpec((None, tm // LANES, LANES), lambda m: (m, 0, 0)),
            pl.BlockSpec((N_COND, LANES), lambda m: (0, 0)),
        ],
        out_shape=[
            jax.ShapeDtypeStruct((NTOK, D_MODEL), F32),
            jax.ShapeDtypeStruct((NTOK // tm, tm // LANES, LANES), jnp.int32),
            jax.ShapeDtypeStruct((N_COND, LANES), F32),
        ],
        scratch_shapes=[
            pltpu.VMEM((k_tot, D_MODEL), BF16),
            pltpu.VMEM((tm, tm), BF16),
            pltpu.VMEM((N_COND, LANES), F32),
        ],
        compiler_params=_params(("arbitrary",)),
        name="out_proj_ln",
    )(*args, w, x[0], x[1], mod, ln_g.reshape(DEPTH, 1, D_MODEL), ln_b.reshape(DEPTH, 1, D_MODEL),
      w_router, b_router)


def _moe_kernel(tgrp_ref, ntile_ref, src0_ref, src1_ref, dst_ref, mid_ref, x_hbm, mod_ref, wr_ref, br_ref,
                w1_ref, w3_ref, w2_ref, g_ref, b_ref, y_hbm,
                gbuf, obuf, gsem, ssem, w13s, w2s):
    i = pl.program_id(0)
    n_steps = pl.num_programs(0)
    n_tiles = ntile_ref[0]
    n_blk = gbuf.shape[1]
    ts = n_blk * SUBLANES
    slot = lax.rem(i, 2)

    def start_gather(src_ref, s):
        def body(k, carry):
            for u in range(SUBLANES):
                tok = src_ref[0, k * SUBLANES + u]
                pltpu.make_async_copy(x_hbm.at[pl.ds(tok, 1), :], gbuf.at[s, k, pl.ds(u, 1), :],
                                      gsem.at[s]).start(priority=u % 2)
            return carry
        lax.fori_loop(0, n_blk, body, 0)

    def wait_gather(s):
        pltpu.make_async_copy(gbuf.at[s], gbuf.at[s], gsem.at[s]).wait()

    def start_scatter(s):
        def body(k, carry):
            for u in range(SUBLANES):
                tok = dst_ref[0, k * SUBLANES + u]
                pltpu.make_async_copy(obuf.at[s, k, pl.ds(u, 1), :], y_hbm.at[pl.ds(tok, 1), :],
                                      ssem.at[s]).start(priority=u % 2)
            return carry
        lax.fori_loop(0, n_blk, body, 0)

    def wait_scatter(s):
        pltpu.make_async_copy(obuf.at[s], obuf.at[s], ssem.at[s]).wait()

    @pl.when(i == 0)
    def _():
        start_gather(src0_ref, 0)

    @pl.when(i < n_tiles)
    def _():
        grp = tgrp_ref[i]
        wait_gather(slot)

        @pl.when(i + 1 < n_tiles)
        def _():
            start_gather(src1_ref, 1 - slot)

        @pl.when((i == 0) | (grp != tgrp_ref[jnp.maximum(i - 1, 0)]))
        def _():
            w13s[:, :, :MOE_FF] = w1_ref[...].astype(BF16)
            w13s[:, :, MOE_FF:] = w3_ref[...].astype(BF16)
            w2s[...] = w2_ref[...].astype(BF16)

        mid_rows = mid_ref[...].astype(F32)
        spread = jnp.concatenate([jnp.broadcast_to(mid_rows[a:a + 1, :], (LANES, LANES))
                                  for a in range(ts // LANES)], axis=0)
        row_i = lax.broadcasted_iota(jnp.int32, (ts, LANES), 0)
        mid = jnp.sum(jnp.where(_lane_iota((ts, LANES)) == _mod_pow2(row_i, LANES), spread, 0.0),
                      axis=-1, keepdims=True)

        def mod_row(k):
            return jnp.where(mid == 0, mod_ref[0, k:k + 1, :],
                             jnp.where(mid == 1, mod_ref[1, k:k + 1, :], mod_ref[2, k:k + 1, :]))

        x1 = gbuf[slot].reshape(ts, D_MODEL)
        xm = (x1 * (1.0 + mod_row(4)) + mod_row(3)).astype(BF16)
        z, lane_i, lane, g_p, _ = _router_probs(xm, wr_ref, br_ref)
        e0 = MOE_GROUPS + MOE_PER_GROUP * grp
        emask = (lane_i >= e0) & (lane_i < e0 + MOE_PER_GROUP)
        ze = jnp.where(emask, z, -jnp.inf)
        pe = jnp.exp(ze - jnp.max(ze, axis=-1, keepdims=True))
        e_prob = pe / jnp.sum(pe, axis=-1, keepdims=True)
        cand = jnp.where(emask, e_prob, -1.0)
        p1 = jnp.max(cand, axis=-1, keepdims=True)
        i1 = jnp.min(jnp.where(cand == p1, lane, float(LANES)), axis=-1, keepdims=True)
        cand2 = jnp.where(lane == i1, -1.0, cand)
        p2 = jnp.max(cand2, axis=-1, keepdims=True)
        i2 = jnp.min(jnp.where(cand2 == p2, lane, float(LANES)), axis=-1, keepdims=True)
        denom = p1 + p2
        comb = jnp.where(lane == i1, g_p * p1 / denom, 0.0) + jnp.where(lane == i2, g_p * p2 / denom, 0.0)
        y = None
        for e in range(MOE_PER_GROUP):
            c = jnp.sum(jnp.where(lane_i == e0 + e, comb, 0.0), axis=-1, keepdims=True)
            h = jnp.dot(xm, w13s[e], preferred_element_type=F32)
            hid = (_silu(h[:, :MOE_FF]) * h[:, MOE_FF:] * c).astype(BF16)
            part = jnp.dot(hid, w2s[e], preferred_element_type=F32)
            y = part if y is None else y + part
        r = DEEPNORM_ALPHA * x1 + mod_row(5) * y

        @pl.when(i >= 2)
        def _():
            wait_scatter(slot)

        obuf[slot] = _layer_norm(r, g_ref[...], b_ref[...]).reshape(n_blk, SUBLANES, D_MODEL)
        start_scatter(slot)

    @pl.when(i == n_steps - 1)
    def _():
        @pl.when(n_tiles >= 2)
        def _():
            wait_scatter(lax.rem(n_tiles, 2))

        @pl.when(n_tiles >= 1)
        def _():
            wait_scatter(lax.rem(n_tiles + 1, 2))


def _inverse_perm_kernel(pos_ref, pad_ref, out_ref):
    def clear(s, carry):
        out_ref[s] = 0
        return carry

    def place(t, carry):
        out_ref[pos_ref[t]] = t + 1
        return carry

    for run in range(pad_ref.shape[0] // 2):
        lax.fori_loop(pad_ref[2 * run], pad_ref[2 * run + 1], clear, 0)
    lax.fori_loop(0, pos_ref.shape[0], place, 0, unroll=8)


def _inverse_perm(pos, pad_runs):
    return pl.pallas_call(
        _inverse_perm_kernel,
        in_specs=[pl.BlockSpec(memory_space=pltpu.SMEM)] * 2,
        out_specs=pl.BlockSpec(memory_space=pltpu.SMEM),
        out_shape=jax.ShapeDtypeStruct((MOE_ROWS,), jnp.int32),
        name="moe_inverse_perm",
    )(pos, pad_runs)


def _moe_plan(meta, counts):
    ts = MOE_TILE
    cnt = counts[0, :MOE_GROUPS].astype(jnp.int32)
    tiles_g = (cnt + ts - 1) // ts
    tile_end = jnp.cumsum(tiles_g)
    row0_g = (tile_end - tiles_g) * ts
    keys = meta.reshape(NTOK)
    gid = keys // MOE_KEY_BASE
    rank = keys % MOE_KEY_BASE
    pos = row0_g[gid] + rank
    pad_runs = jnp.stack([row0_g + cnt, tile_end * ts], axis=1).reshape(-1)
    pad_runs = jnp.concatenate([pad_runs, tile_end[-1:] * ts, jnp.full((1,), MOE_ROWS, jnp.int32)])
    tok1 = _inverse_perm(pos, pad_runs.astype(jnp.int32))
    rows = jnp.arange(MOE_ROWS, dtype=jnp.int32)
    src = jnp.maximum(tok1 - 1, 0)
    spare = NTOK + ((rows // ts) % 2) * ts + rows % ts
    dst = jnp.where(tok1 > 0, src, spare)
    mid = jnp.where(src < NTOK_P, 0, 1 + (src - NTOK_P) // DEC_SEQ)
    tile_group = jnp.minimum(jnp.sum(jnp.arange(MOE_TILES)[:, None] >= tile_end[None, :], axis=1),
                             MOE_GROUPS - 1).astype(jnp.int32)
    n_tiles = tile_end[-1:].astype(jnp.int32)
    return (tile_group, n_tiles, src.reshape(MOE_TILES, 1, ts), dst.reshape(MOE_TILES, 1, ts),
            mid.reshape(MOE_TILES, ts // LANES, LANES))


def _moe(x1, plan, mods, w_router, b_router, w1, w3, w2, ln_g, ln_b, layer):
    ts = MOE_TILE
    tile_group, n_tiles, src, dst, mid = plan
    grp_shape = (DEPTH, MOE_GROUPS, MOE_PER_GROUP)
    w_in_spec = pl.BlockSpec((None, None, MOE_PER_GROUP, D_MODEL, MOE_FF), lambda i, tg, nt: (layer, tg[i], 0, 0, 0))
    w_out_spec = pl.BlockSpec((None, None, MOE_PER_GROUP, MOE_FF, D_MODEL), lambda i, tg, nt: (layer, tg[i], 0, 0, 0))
    smem_tile = functools.partial(pl.BlockSpec, (None, 1, ts), memory_space=pltpu.SMEM)
    grid_spec = pltpu.PrefetchScalarGridSpec(
        num_scalar_prefetch=2,
        grid=(MOE_TILES,),
        in_specs=[
            smem_tile(lambda i, tg, nt: (i, 0, 0)),
            smem_tile(lambda i, tg, nt: (jnp.minimum(i + 1, MOE_TILES - 1), 0, 0)),
            smem_tile(lambda i, tg, nt: (i, 0, 0)),
            pl.BlockSpec((None, ts // LANES, LANES), lambda i, tg, nt: (i, 0, 0)),
            pl.BlockSpec(memory_space=pl.ANY),
            pl.BlockSpec((None, N_COND, 6, D_MODEL), lambda i, tg, nt: (layer, 0, 0, 0)),
            pl.BlockSpec((D_MODEL, LANES), lambda i, tg, nt: (0, 0)),
            pl.BlockSpec((1, LANES), lambda i, tg, nt: (0, 0)),
            w_in_spec,
            w_in_spec,
            w_out_spec,
            pl.BlockSpec((None, 1, D_MODEL), lambda i, tg, nt: (layer, 0, 0)),
            pl.BlockSpec((None, 1, D_MODEL), lambda i, tg, nt: (layer, 0, 0)),
        ],
        out_specs=pl.BlockSpec(memory_space=pl.ANY),
        scratch_shapes=[
            pltpu.VMEM((2, ts // SUBLANES, SUBLANES, D_MODEL), F32),
            pltpu.VMEM((2, ts // SUBLANES, SUBLANES, D_MODEL), F32),
            pltpu.SemaphoreType.DMA((2,)),
            pltpu.SemaphoreType.DMA((2,)),
            pltpu.VMEM((MOE_PER_GROUP, D_MODEL, 2 * MOE_FF), BF16),
            pltpu.VMEM((MOE_PER_GROUP, MOE_FF, D_MODEL), BF16),
        ],
    )
    return pl.pallas_call(
        _moe_kernel,
        grid_spec=grid_spec,
        out_shape=jax.ShapeDtypeStruct((NTOK + 2 * ts, D_MODEL), F32),
        compiler_params=_params(("arbitrary",)),
        name="hier_moe_ln",
    )(tile_group, n_tiles, src, src, dst, mid, x1, mods, w_router, b_router,
      w1.reshape(grp_shape + (D_MODEL, MOE_FF)), w3.reshape(grp_shape + (D_MODEL, MOE_FF)),
      w2.reshape(grp_shape + (MOE_FF, D_MODEL)),
      ln_g.reshape(DEPTH, 1, D_MODEL), ln_b.reshape(DEPTH, 1, D_MODEL))


def _swap_halves(x, lane, half):
    return jnp.where(_mod_pow2(lane, 2 * half) < half,
                     pltpu.roll(x, LANES - half, 1), pltpu.roll(x, half, 1))


def _rope128(x, cos, sin_signed, lane, half):
    return x * cos + _swap_halves(x, lane, half) * sin_signed


def _rope_tables(rot_dim):
    rows = DEC_SEQ // GRID_W
    row = jnp.repeat(jnp.arange(rows, dtype=F32), GRID_W)
    col = jnp.tile(jnp.arange(GRID_W, dtype=F32), rows)
    n_freq = rot_dim // 4
    inv_freq = ROPE_BASE ** (-jnp.arange(n_freq, dtype=F32) / n_freq)
    ang = jnp.concatenate([row[:, None] * inv_freq, col[:, None] * inv_freq], axis=-1)
    cos, sin = jnp.cos(ang), jnp.sin(ang)
    reps = LANES // rot_dim
    cos_full = jnp.tile(jnp.concatenate([cos, cos], axis=-1), (1, reps))
    sin_signed = jnp.tile(jnp.concatenate([-sin, sin], axis=-1), (1, reps))
    return cos_full, sin_signed


def _exp_parts(s_list, scale):
    m = None
    for s in s_list:
        sm = jnp.max(s, axis=-1, keepdims=True)
        m = sm if m is None else jnp.maximum(m, sm)
    return [jnp.exp2((s - m) * (scale * LOG2E)).astype(BF16) for s in s_list]


def _pv_normalised(p_list, v_list):
    o = None
    for p, v in zip(p_list, v_list):
        part = jnp.dot(p, v, preferred_element_type=F32)
        o = part if o is None else o + part
    return o[:, :LANES] / o[:, LANES:]


def _mla_kv_kernel(new_tokens, *refs):
    if new_tokens:
        (xp_ref, xs_ref, mod_ref, w_ref, cos_ref, sin_ref, g_ref, wuk_ref, wuv_ref) = refs[:9]
        ckvc_ref, krc_ref, zq_ref, kcat_ref, vm_ref, wscr, wuk_scr, wuv_scr = refs[-8:]
        i = pl.program_id(0)

        @pl.when(i == 0)
        def _():
            wscr[...] = w_ref[...].astype(BF16)
            wuk_scr[...] = wuk_ref[...].astype(BF16)
            wuv_scr[...] = wuv_ref[...].astype(BF16)

        xm = (_read_split(xp_ref, xs_ref, i) * (1.0 + mod_ref[1:2, :]) + mod_ref[0:1, :]).astype(BF16)
        z = lax.dot_general(xm, wscr[...], NT_DIMS, preferred_element_type=F32)
        n_q = MLA_HEADS * (MLA_NOPE + MLA_ROPE)
        zq_ref[...] = z[:, :n_q].astype(BF16)
        x = z[:, n_q:n_q + MLA_KV_RANK]
        c = x * lax.rsqrt(jnp.mean(x * x, axis=-1, keepdims=True) + RMS_EPS) * g_ref[...]
        kr_raw = z[:, n_q + MLA_KV_RANK:]

        @pl.when(i < NTOK_P // zq_ref.shape[0])
        def _():
            for b in range(ckvc_ref.shape[0]):
                ckvc_ref[b] = c[SEQ * b:SEQ * (b + 1), :]
                krc_ref[b] = kr_raw[SEQ * b:SEQ * (b + 1), :MLA_ROPE]

        lane = _lane_iota(kr_raw.shape)
        kr = _rope128(kr_raw, cos_ref[...], sin_ref[...], lane, MLA_ROPE // 2)
        wuk = wuk_scr[...]
        wuv = wuv_scr[...]
    else:
        ckv_ref, kr_ref, wuk_ref, wuv_ref, kcat_ref, vm_ref = refs
        c = ckv_ref[...]
        kr = kr_ref[...]
        wuk = wuk_ref[...].astype(BF16)
        wuv = wuv_ref[...].astype(BF16)
    cb = c.astype(BF16)
    kn = jnp.dot(cb, wuk, preferred_element_type=F32).astype(BF16)
    vv = jnp.dot(cb, wuv, preferred_element_type=F32).astype(BF16)
    krb = kr.astype(BF16)
    ones = jnp.ones((c.shape[0], LANES), BF16)
    for p in range(MLA_HEADS // 2):
        kcat_ref[:, 256 * p:256 * p + LANES] = kn[:, LANES * p:LANES * (p + 1)]
        kcat_ref[:, 256 * p + LANES:256 * (p + 1)] = krb
        vm_ref[:, 256 * p:256 * p + LANES] = vv[:, LANES * p:LANES * (p + 1)]
        vm_ref[:, 256 * p + LANES:256 * (p + 1)] = ones


def _mla_proj_kv(x, mod, w_t, cos_t, sin_t, kv_norm_g, w_uk, w_uv, j, prev):
    tm = MLA_PROJ_TM
    npt = NTOK_P // tm
    nst = DEC_SEQ // tm
    nb = tm // SEQ
    n_q = MLA_HEADS * (MLA_NOPE + MLA_ROPE)

    def tab(i):
        return (jnp.where(i < npt, 0, 1 + (i - npt) % nst), 0)

    def cache_idx(i):
        return (jnp.minimum(i, npt - 1), j, 0, 0)

    in_specs = _split_specs(tm, D_MODEL, lambda i: i, x[2]) + [
        pl.BlockSpec((None, 6, D_MODEL), lambda i: (_group_of_tile(i, tm), 0, 0)),
        pl.BlockSpec((MLA_COLS, D_MODEL), lambda i: (0, 0)),
        pl.BlockSpec((tm, LANES), tab),
        pl.BlockSpec((tm, LANES), tab),
        pl.BlockSpec((None, 1, MLA_KV_RANK), lambda i: (j, 0, 0)),
        pl.BlockSpec((None, MLA_KV_RANK, 512), lambda i: (j, 0, 0)),
        pl.BlockSpec((None, MLA_KV_RANK, 512), lambda i: (j, 0, 0)),
    ]
    args = [x[0], x[1], mod, w_t, cos_t, sin_t, kv_norm_g.reshape(-1, 1, MLA_KV_RANK), w_uk, w_uv]
    aliases = {}
    if prev is not None:
        aliases = {len(args): 0, len(args) + 1: 1}
        in_specs += [pl.BlockSpec(memory_space=pl.ANY)] * 2
        args += list(prev)
    return pl.pallas_call(
        functools.partial(_mla_kv_kernel, True),
        grid=(NTOK // tm,),
        in_specs=in_specs,
        out_specs=[
            pl.BlockSpec((nb, None, SEQ, MLA_KV_RANK), cache_idx),
            pl.BlockSpec((nb, None, SEQ, MLA_ROPE), cache_idx),
            pl.BlockSpec((tm, n_q), lambda i: (i, 0)),
            pl.BlockSpec((tm, 1024), lambda i: (i, 0)),
            pl.BlockSpec((tm, 1024), lambda i: (i, 0)),
        ],
        out_shape=[
            jax.ShapeDtypeStruct((BATCH, N_EVEN, SEQ, MLA_KV_RANK), F32),
            jax.ShapeDtypeStruct((BATCH, N_EVEN, SEQ, MLA_ROPE), F32),
            jax.ShapeDtypeStruct((NTOK, n_q), BF16),
            jax.ShapeDtypeStruct((NTOK, 1024), BF16),
            jax.ShapeDtypeStruct((NTOK, 1024), BF16),
        ],
        scratch_shapes=[
            pltpu.VMEM((MLA_COLS, D_MODEL), BF16),
            pltpu.VMEM((MLA_KV_RANK, 512), BF16),
            pltpu.VMEM((MLA_KV_RANK, 512), BF16),
        ],
        input_output_aliases=aliases,
        compiler_params=_params(("arbitrary",)),
        name="mla_proj_kv",
    )(*args)


def _mla_kv_ctx(cache_ckv, kr_tiled, w_uk, w_uv, j):
    return pl.pallas_call(
        functools.partial(_mla_kv_kernel, False),
        grid=(DEC_BATCH,),
        in_specs=[
            pl.BlockSpec((None, None, PAST_LEN, MLA_KV_RANK), lambda b: (b, j, 0, 0)),
            pl.BlockSpec((None, PAST_LEN, LANES), lambda b: (b, 0, 0)),
            pl.BlockSpec((None, MLA_KV_RANK, 512), lambda b: (j, 0, 0)),
            pl.BlockSpec((None, MLA_KV_RANK, 512), lambda b: (j, 0, 0)),
        ],
        out_specs=[
            pl.BlockSpec((PAST_LEN, 1024), lambda b: (b, 0)),
            pl.BlockSpec((PAST_LEN, 1024), lambda b: (b, 0)),
        ],
        out_shape=[
            jax.ShapeDtypeStruct((DEC_BATCH * PAST_LEN, 1024), BF16),
            jax.ShapeDtypeStruct((DEC_BATCH * PAST_LEN, 1024), BF16),
        ],
        compiler_params=_params(("parallel",)),
        name="mla_kv_ctx",
    )(cache_ckv, kr_tiled, w_uk, w_uv)


def _mla_attn_kernel(latent, *refs):
    if latent:
        qn_ref, qr_ref, cos_ref, sin_ref, kc_ref, vc_ref, kn_ref, vn_ref, o_ref = refs
        k_refs, v_refs = (kc_ref, kn_ref), (vc_ref, vn_ref)
    else:
        qn_ref, qr_ref, kn_ref, vn_ref, o_ref = refs
        k_refs, v_refs = (kn_ref,), (vn_ref,)
    tq = ATT_TQ if latent else SEQ
    n_seq = qn_ref.shape[0] // tq
    lane = _lane_iota((tq, LANES))
    scale = (MLA_NOPE + MLA_ROPE) ** -0.5
    for sq in range(n_seq):
        rows = slice(tq * sq, tq * (sq + 1))
        krows = slice(None) if latent else rows
        qr_cols = []
        for cidx in range(2):
            x = qr_ref[rows, LANES * cidx:LANES * (cidx + 1)].astype(F32)
            if latent:
                x = _rope128(x, cos_ref[...], sin_ref[...], lane, MLA_ROPE // 2)
            qr_cols.append(x)
        o_prev = None
        for h in range(MLA_HEADS):
            p, half = divmod(h, 2)
            cidx, slot = divmod(h, 4)
            qa = jnp.where(_div_pow2(lane, MLA_NOPE) == half,
                           qn_ref[rows, LANES * p:LANES * (p + 1)].astype(F32), 0.0)
            qb = jnp.where(_div_pow2(lane, MLA_ROPE) == slot, qr_cols[cidx], 0.0)
            qcat = jnp.concatenate([qa, qb], axis=1).astype(BF16)
            pair_cols = slice(256 * p, 256 * (p + 1))
            s_list = [lax.dot_general(qcat, k_ref[krows, pair_cols], NT_DIMS, preferred_element_type=F32)
                      for k_ref in k_refs]
            o = _pv_normalised(_exp_parts(s_list, scale), [v_ref[krows, pair_cols] for v_ref in v_refs])
            if half == 0:
                o_prev = o
            else:
                o_ref[rows, LANES * p:LANES * (p + 1)] = jnp.where(lane < MLA_DV, o_prev, o).astype(BF16)


def _mla_attn_prompt(zmla, kcat, vm):
    rows = 4 * SEQ
    return pl.pallas_call(
        functools.partial(_mla_attn_kernel, False),
        grid=(NTOK_P // rows,),
        in_specs=[
            pl.BlockSpec((rows, 512), lambda b: (b, 0)),
            pl.BlockSpec((rows, 256), lambda b: (b, 2)),
            pl.BlockSpec((rows, 1024), lambda b: (b, 0)),
            pl.BlockSpec((rows, 1024), lambda b: (b, 0)),
        ],
        out_specs=pl.BlockSpec((rows, 512), lambda b: (b, 0)),
        out_shape=jax.ShapeDtypeStruct((NTOK_P, 512), BF16),
        compiler_params=_params(("parallel",)),
        name="mla_attn_prompt",
    )(zmla, zmla, kcat, vm)


def _mla_attn_latent(zmla, cos_t, sin_t, kcat_ctx, vm_ctx, kcat, vm):
    nq = DEC_SEQ // ATT_TQ
    row0 = NTOK_P // ATT_TQ
    seq0 = NTOK_P // DEC_SEQ
    return pl.pallas_call(
        functools.partial(_mla_attn_kernel, True),
        grid=(DEC_BATCH, nq),
        in_specs=[
            pl.BlockSpec((ATT_TQ, 512), lambda b, q: (row0 + b * nq + q, 0)),
            pl.BlockSpec((ATT_TQ, 256), lambda b, q: (row0 + b * nq + q, 2)),
            pl.BlockSpec((ATT_TQ, LANES), lambda b, q: (q, 0)),
            pl.BlockSpec((ATT_TQ, LANES), lambda b, q: (q, 0)),
            pl.BlockSpec((PAST_LEN, 1024), lambda b, q: (b, 0)),
            pl.BlockSpec((PAST_LEN, 1024), lambda b, q: (b, 0)),
            pl.BlockSpec((DEC_SEQ, 1024), lambda b, q: (seq0 + b, 0)),
            pl.BlockSpec((DEC_SEQ, 1024), lambda b, q: (seq0 + b, 0)),
        ],
        out_specs=pl.BlockSpec((ATT_TQ, 512), lambda b, q: (b * nq + q, 0)),
        out_shape=jax.ShapeDtypeStruct((NTOK_S, 512), BF16),
        compiler_params=_params(("parallel", "arbitrary")),
        name="mla_attn_latent",
    )(zmla, zmla, cos_t, sin_t, kcat_ctx, vm_ctx, kcat, vm)


def _log_sigmoid(x):
    return jnp.minimum(x, 0.0) - jnp.log1p(jnp.exp(-jnp.abs(x)))


def _retention_kernel(seq_len, n_seq, has_init, emit_state, has_prev, *refs):
    refs = list(refs)
    decf_ref, decb_ref, q_ref, k_ref, v_ref, g_ref = refs[:6]
    refs = refs[6:]
    if has_init:
        sf0_ref, sb0_ref = refs[:2]
        refs = refs[2:]
    if has_prev:
        refs = refs[2:]
    o_ref = refs[0]
    refs = refs[1:]
    if emit_state:
        sf_ref, sb_ref = refs[:2]
        refs = refs[2:]
    of_scr, ob_scr, dec_scr, wts_scr = refs

    c = RET_CHUNK
    n_chunks = seq_len // c
    pair = pl.program_id(0)
    lane = _lane_iota((c, LANES))
    zeros_half = jnp.zeros((RET_DK, RET_DV), F32)

    def log_gammas(half):
        head = 2 * pair + half
        return (_log_sigmoid(decf_ref[pl.ds(head, 1), :]),
                _log_sigmoid(decb_ref[pl.ds(head, 1), :]))

    @pl.when(pl.program_id(1) == 0)
    def _():
        ri = lax.broadcasted_iota(jnp.int32, (c, c), 0)
        ci = lax.broadcasted_iota(jnp.int32, (c, c), 1)
        rel = (ri - ci).astype(F32)
        row = lax.broadcasted_iota(jnp.int32, (c, LANES), 0).astype(F32)
        for half in range(2):
            lgf, lgb = log_gammas(half)
            dec_scr[half] = (jnp.where(rel >= 0, jnp.exp(lgf[:, 0:1] * jnp.maximum(rel, 0.0)), 0.0)
                             + jnp.where(rel <= 0, jnp.exp(lgb[:, 0:1] * jnp.maximum(-rel, 0.0)), 0.0))
            wts_scr[half, 0] = jnp.exp(lgf * (row + 1.0))
            wts_scr[half, 1] = jnp.exp(lgf * (c - 1.0 - row))
            wts_scr[half, 2] = jnp.exp(lgb * (c - row))
            wts_scr[half, 3] = jnp.exp(lgb * row)

    cross = has_init or n_chunks > 1
    chains = [(sq, half) for sq in range(n_seq) for half in range(2)]
    chunk_decay = []
    for half in range(2):
        lgf, lgb = log_gammas(half)
        chunk_decay.append((jnp.exp(lgf * float(c)), jnp.exp(lgb * float(c))))

    def rows_of(sq, n):
        start = sq * seq_len + n * c
        return pl.ds(start if isinstance(n, int) else pl.multiple_of(start, c), c)

    def load(sq, half, n):
        rows = rows_of(sq, n)
        vsl = slice(RET_DV * half, RET_DV * (half + 1))
        qm = jnp.where(_div_pow2(lane, RET_DK) == half, q_ref[rows, :].astype(F32), 0.0)
        kk = k_ref[rows, :].astype(F32) * (RET_DK ** -0.5)
        return rows, vsl, qm, kk, v_ref[rows, vsl]

    def init_state(s0_ref, sq, half):
        if not has_init:
            return jnp.zeros((LANES, RET_DV), F32)
        s0 = s0_ref[sq, half]
        return jnp.concatenate([s0, zeros_half] if half == 0 else [zeros_half, s0], axis=0)

    def fwd_step(sq, half, n, s_f):
        rows, vsl, qm, kk, vb = load(sq, half, n)
        s = lax.dot_general(qm.astype(BF16), kk.astype(BF16), NT_DIMS, preferred_element_type=F32)
        o = jnp.dot((s * dec_scr[half]).astype(BF16), vb, preferred_element_type=F32)
        if cross:
            o = o + jnp.dot((qm * wts_scr[half, 0]).astype(BF16), s_f.astype(BF16), preferred_element_type=F32)
        of_scr[rows, vsl] = o
        kv = lax.dot_general((kk * wts_scr[half, 1]).astype(BF16), vb, TN_DIMS, preferred_element_type=F32)
        return chunk_decay[half][0] * s_f + kv

    def bwd_step(sq, half, n, s_b):
        rows, vsl, qm, kk, vb = load(sq, half, n)
        if cross:
            ob_scr[rows, vsl] = jnp.dot((qm * wts_scr[half, 2]).astype(BF16), s_b.astype(BF16),
                                        preferred_element_type=F32)
        kv = lax.dot_general((kk * wts_scr[half, 3]).astype(BF16), vb, TN_DIMS, preferred_element_type=F32)
        return chunk_decay[half][1] * s_b + kv

    def finish(sq, n):
        rows = rows_of(sq, n)
        o2 = of_scr[rows, :] + ob_scr[rows, :] if cross else of_scr[rows, :]
        for half in range(2):
            vsl = slice(RET_DV * half, RET_DV * (half + 1))
            o = o2[:, vsl]
            mu = jnp.mean(o, axis=-1, keepdims=True)
            d = o - mu
            var = jnp.mean(d * d, axis=-1, keepdims=True)
            o_ref[rows, vsl] = (_silu(g_ref[rows, vsl].astype(F32)) * (d * lax.rsqrt(var + LN_EPS))).astype(BF16)

    s_f = tuple(init_state(sf0_ref if has_init else None, sq, half) for sq, half in chains)
    s_b = tuple(init_state(sb0_ref if has_init else None, sq, half) for sq, half in chains)
    if n_chunks == 1:
        s_f = tuple(fwd_step(sq, half, 0, s) for (sq, half), s in zip(chains, s_f))
        s_b = tuple(bwd_step(sq, half, 0, s) for (sq, half), s in zip(chains, s_b))
        for sq in range(n_seq):
            finish(sq, 0)
    else:
        def scan_step(n, carry):
            sf, sb = carry
            sf = tuple(fwd_step(sq, half, n, s) for (sq, half), s in zip(chains, sf))
            sb = tuple(bwd_step(sq, half, n_chunks - 1 - n, s) for (sq, half), s in zip(chains, sb))
            return sf, sb

        s_f, s_b = lax.fori_loop(0, n_chunks, scan_step, (s_f, s_b))

        def finish_step(n, carry):
            for sq in range(n_seq):
                finish(sq, n)
            return carry

        lax.fori_loop(0, n_chunks, finish_step, 0)
    if emit_state:
        for (sq, half), sf, sb in zip(chains, s_f, s_b):
            sf_ref[sq, half] = sf[RET_DK * half:RET_DK * (half + 1), :]
            sb_ref[sq, half] = sb[RET_DK * half:RET_DK * (half + 1), :]


def _retention(zret, decf, decb, j, latent, state_f=None, state_b=None, prev=None):
    seq_len = DEC_SEQ if latent else SEQ
    n_seq = DEC_BATCH if latent else 8
    n_b = (DEC_BATCH if latent else BATCH) // n_seq
    rows = n_seq * seq_len
    row0 = NTOK_P // rows if latent else 0
    n_pairs = RET_HEADS // 2
    in_specs = [
        pl.BlockSpec((None, RET_HEADS, LANES), lambda p, b: (j, 0, 0)),
        pl.BlockSpec((None, RET_HEADS, LANES), lambda p, b: (j, 0, 0)),
        pl.BlockSpec((rows, LANES), lambda p, b: (row0 + b, p)),
        pl.BlockSpec((rows, LANES), lambda p, b: (row0 + b, 4 + p)),
        pl.BlockSpec((rows, 256), lambda p, b: (row0 + b, 4 + p)),
        pl.BlockSpec((rows, 256), lambda p, b: (row0 + b, 8 + p)),
    ]
    args = [decf, decb, zret, zret, zret, zret]
    out_specs = [pl.BlockSpec((rows, 256), lambda p, b: (b, p))]
    out_shape = [jax.ShapeDtypeStruct((n_b * rows, RET_HEADS * RET_DV), BF16)]
    aliases = {}
    if latent:
        st_spec = pl.BlockSpec((n_seq, None, 2, RET_DK, RET_DV), lambda p, b: (b, j, p, 0, 0))
        in_specs += [st_spec, st_spec]
        args += [state_f, state_b]
    else:
        st_spec = pl.BlockSpec((n_seq, None, 2, RET_DK, RET_DV), lambda p, b: (b, j, p, 0, 0))
        out_specs += [st_spec, st_spec]
        out_shape += [jax.ShapeDtypeStruct((BATCH, N_EVEN, RET_HEADS, RET_DK, RET_DV), F32)] * 2
        if prev is not None:
            aliases = {len(args): 1, len(args) + 1: 2}
            in_specs += [pl.BlockSpec(memory_space=pl.ANY)] * 2
            args += list(prev)
    return pl.pallas_call(
        functools.partial(_retention_kernel, seq_len, n_seq, latent, not latent, bool(aliases)),
        grid=(n_pairs, n_b),
        in_specs=in_specs,
        out_specs=out_specs,
        out_shape=out_shape,
        scratch_shapes=[
            pltpu.VMEM((rows, 2 * RET_DV), F32),
            pltpu.VMEM((rows, 2 * RET_DV), F32),
            pltpu.VMEM((2, RET_CHUNK, RET_CHUNK), F32),
            pltpu.VMEM((2, 4, RET_CHUNK, LANES), F32),
        ],
        input_output_aliases=aliases,
        compiler_params=_params(("arbitrary", "arbitrary")),
        name="retention_latent" if latent else "retention_prompt",
    )(*args)


def _diff_prep_kernel(k_ref, v_ref, cos_ref, sin_ref, kr_ref, va_ref):
    lane = _lane_iota(cos_ref.shape)
    cos = cos_ref[...]
    sin = sin_ref[...]
    ones = jnp.ones(cos_ref.shape, BF16)
    for h in range(DIFF_HEADS):
        sl = slice(LANES * h, LANES * (h + 1))
        kr_ref[:, sl] = _rope128(k_ref[:, sl].astype(F32), cos, sin, lane, DIFF_DH // 2).astype(BF16)
        va_ref[:, 256 * h:256 * h + LANES] = v_ref[:, sl].astype(BF16)
        va_ref[:, 256 * h + LANES:256 * (h + 1)] = ones


def _diff_prep(zodd, cos_t, sin_t):
    tm = 512
    row0 = NTOK_P // tm
    nst = DEC_SEQ // tm
    return pl.pallas_call(
        _diff_prep_kernel,
        grid=(NTOK_S // tm,),
        in_specs=[
            pl.BlockSpec((tm, 1024), lambda i: (row0 + i, 1)),
            pl.BlockSpec((tm, 1024), lambda i: (row0 + i, 2)),
            pl.BlockSpec((tm, LANES), lambda i: (i % nst, 0)),
            pl.BlockSpec((tm, LANES), lambda i: (i % nst, 0)),
        ],
        out_specs=[pl.BlockSpec((tm, 1024), lambda i: (i, 0)), pl.BlockSpec((tm, 2048), lambda i: (i, 0))],
        out_shape=[jax.ShapeDtypeStruct((NTOK_S, 1024), BF16), jax.ShapeDtypeStruct((NTOK_S, 2048), BF16)],
        compiler_params=_params(("parallel",)),
        name="diff_rope_keys",
    )(zodd, zodd, cos_t, sin_t)


def _diff_attn_kernel(latent, lam_init, *refs):
    if latent:
        (lam_ref, ng_ref, q_ref, cos_ref, sin_ref, kc_ref, vc_ref, kn_ref, vn_ref, o_ref) = refs
    else:
        lam_ref, ng_ref, q_ref, k_ref, v_ref = refs[:5]
        o_ref, kout_ref, vout_ref = refs[-3:]
    tq = ATT_TQ if latent else SEQ
    n_seq = q_ref.shape[0] // tq
    lane = _lane_iota((tq, LANES))
    scale = DIFF_DH ** -0.5
    lp = lam_ref[...]
    lam = (jnp.exp(jnp.sum(lp[0:1, :] * lp[1:2, :], axis=-1, keepdims=True))
           - jnp.exp(jnp.sum(lp[2:3, :] * lp[3:4, :], axis=-1, keepdims=True)) + lam_init)
    ng = ng_ref[...]
    ones = jnp.ones((PAST_LEN if latent else tq, LANES), BF16)
    for sq, h in [(sq, h) for sq in range(n_seq) for h in range(DIFF_HEADS)]:
        sl = slice(LANES * h, LANES * (h + 1))
        rows = slice(tq * sq, tq * (sq + 1))
        qh = q_ref[rows, sl].astype(F32)
        if latent:
            qh = _rope128(qh, cos_ref[...], sin_ref[...], lane, DIFF_DH // 2)
            k_list = [kc_ref[h].astype(BF16), kn_ref[:, sl]]
            v_list = [jnp.concatenate([vc_ref[h].astype(BF16), ones], axis=1),
                      vn_ref[:, 256 * h:256 * (h + 1)]]
        else:
            kh = k_ref[rows, sl]
            vh = v_ref[rows, sl]
            kout_ref[sq, h] = kh.astype(F32)
            vout_ref[sq, h] = vh.astype(F32)
            k_list = [kh.astype(BF16)]
            v_list = [jnp.concatenate([vh.astype(BF16), ones], axis=1)]
        q1 = jnp.where(lane < DIFF_DH, qh, 0.0).astype(BF16)
        q2 = jnp.where(lane >= DIFF_DH, qh, 0.0).astype(BF16)
        s1 = [lax.dot_general(q1, kk, NT_DIMS, preferred_element_type=F32) for kk in k_list]
        s2 = [lax.dot_general(q2, kk, NT_DIMS, preferred_element_type=F32) for kk in k_list]
        o = _pv_normalised(_exp_parts(s1, scale), v_list) - lam * _pv_normalised(_exp_parts(s2, scale), v_list)
        y = o * lax.rsqrt(jnp.mean(o * o, axis=-1, keepdims=True) + RMS_EPS) * ng
        o_ref[rows, sl] = (y * (1.0 - lam_init)).astype(BF16)


def _diff_attn_prompt(zodd, lam_p, norm_g, j, lam_init, prev):
    n_seq = 2
    rows = n_seq * SEQ
    cache_shape = jax.ShapeDtypeStruct((BATCH, N_ODD, DIFF_HEADS, SEQ, LANES), F32)
    cache_spec = pl.BlockSpec((n_seq, None, DIFF_HEADS, SEQ, LANES), lambda b: (b, j, 0, 0, 0))
    in_specs = [
        pl.BlockSpec((None, 4, DIFF_DH), lambda b: (j, 0, 0)),
        pl.BlockSpec((None, 1, DIFF_DV), lambda b: (j, 0, 0)),
        pl.BlockSpec((rows, 1024), lambda b: (b, 0)),
        pl.BlockSpec((rows, 1024), lambda b: (b, 1)),
        pl.BlockSpec((rows, 1024), lambda b: (b, 2)),
    ]
    args = [lam_p, norm_g.reshape(-1, 1, DIFF_DV), zodd, zodd, zodd]
    aliases = {}
    if prev is not None:
        aliases = {len(args): 1, len(args) + 1: 2}
        in_specs += [pl.BlockSpec(memory_space=pl.ANY)] * 2
        args += list(prev)
    return pl.pallas_call(
        functools.partial(_diff_attn_kernel, False, lam_init),
        grid=(BATCH // n_seq,),
        in_specs=in_specs,
        out_specs=[pl.BlockSpec((rows, 1024), lambda b: (b, 0)), cache_spec, cache_spec],
        out_shape=[jax.ShapeDtypeStruct((NTOK_P, 1024), BF16), cache_shape, cache_shape],
        input_output_aliases=aliases,
        compiler_params=_params(("arbitrary",)),
        name="diff_attn_prompt",
    )(*args)


def _diff_attn_latent(zodd, lam_p, norm_g, cos_t, sin_t, cache_k, cache_v, k_rot, v_aug, j, lam_init):
    nq = DEC_SEQ // ATT_TQ
    row0 = NTOK_P // ATT_TQ
    ctx_spec = pl.BlockSpec((None, None, DIFF_HEADS, PAST_LEN, LANES), lambda b, q: (b, j, 0, 0, 0))
    return pl.pallas_call(
        functools.partial(_diff_attn_kernel, True, lam_init),
        grid=(DEC_BATCH, nq),
        in_specs=[
            pl.BlockSpec((None, 4, DIFF_DH), lambda b, q: (j, 0, 0)),
            pl.BlockSpec((None, 1, DIFF_DV), lambda b, q: (j, 0, 0)),
            pl.BlockSpec((ATT_TQ, 1024), lambda b, q: (row0 + b * nq + q, 0)),
            pl.BlockSpec((ATT_TQ, LANES), lambda b, q: (q, 0)),
            pl.BlockSpec((ATT_TQ, LANES), lambda b, q: (q, 0)),
            ctx_spec,
            ctx_spec,
            pl.BlockSpec((DEC_SEQ, 1024), lambda b, q: (b, 0), pipeline_mode=pl.Buffered(1)),
            pl.BlockSpec((DEC_SEQ, 2048), lambda b, q: (b, 0), pipeline_mode=pl.Buffered(1)),
        ],
        out_specs=pl.BlockSpec((ATT_TQ, 1024), lambda b, q: (b * nq + q, 0)),
        out_shape=jax.ShapeDtypeStruct((NTOK_S, 1024), BF16),
        compiler_params=_params(("parallel", "arbitrary")),
        name="diff_attn_latent",
    )(lam_p, norm_g.reshape(-1, 1, DIFF_DV), zodd, cos_t, sin_t, cache_k, cache_v, k_rot, v_aug)


def _mla_weight_t(w_in_t, j):
    base = RET_COLS
    mq = w_in_t[j, base:base + MLA_HEADS * (MLA_NOPE + MLA_ROPE)].reshape(MLA_HEADS, MLA_NOPE + MLA_ROPE, D_MODEL)
    qn = mq[:, :MLA_NOPE].reshape(MLA_HEADS * MLA_NOPE, D_MODEL)
    qr = mq[:, MLA_NOPE:].reshape(MLA_HEADS * MLA_ROPE, D_MODEL)
    ckv0 = base + MLA_HEADS * (MLA_NOPE + MLA_ROPE)
    ckv = w_in_t[j, ckv0:ckv0 + MLA_KV_RANK]
    kr = w_in_t[j, ckv0 + MLA_KV_RANK:]
    return jnp.concatenate([qn, qr, ckv, jnp.tile(kr, (LANES // MLA_ROPE, 1))], axis=0)


def kernel(x_prompt, x_sample, state_ret_fwd, state_ret_bwd, cache_mla_ckv, cache_mla_krope, cache_diff_k, cache_diff_v, c, c_ctx, ada_w, ada_b, ln1_g, ln1_b, ln2_g, ln2_b, ev_w_in, ev_w_out, ret_decay_fwd, ret_decay_bwd, mla_kv_norm_g, mla_w_uk, mla_w_uv, od_w_in, od_w_out, diff_lambda, diff_norm_g, moe_w_group, moe_b_group, moe_w_expert, moe_b_expert, moe_w1, moe_w3, moe_w2):
    x = (x_prompt.reshape(NTOK_P, D_MODEL), x_sample.reshape(NTOK_S, D_MODEL), 0)
    cond =jnp.concatenate([c_ctx[None, :], c, jnp.zeros((N_COND - 1 - DEC_BATCH, D_MODEL), F32)], axis=0)
    mods = _ada_all(cond, ada_w, ada_b).reshape(DEPTH, N_COND, 6, D_MODEL)

    cos_m, sin_m = _rope_tables(MLA_ROPE)
    cos_d, sin_d = _rope_tables(DIFF_DH)
    ident = MLA_PROJ_TM
    cos_m_id = jnp.concatenate([jnp.ones((ident, LANES), F32), cos_m], axis=0)
    sin_m_id = jnp.concatenate([jnp.zeros((ident, LANES), F32), sin_m], axis=0)
    decf = jnp.broadcast_to(ret_decay_fwd[:, :, None], ret_decay_fwd.shape + (LANES,))
    decb = jnp.broadcast_to(ret_decay_bwd[:, :, None], ret_decay_bwd.shape + (LANES,))

    ev_w_in_t = jnp.swapaxes(ev_w_in, 1, 2)
    pad = LANES - MOE_GROUPS - MOE_EXPERTS
    ret_states = mla_caches = diff_caches = None
    for i in range(DEPTH):
        j = i // 2
        mod = mods[i]
        w_router = jnp.concatenate([moe_w_group[i], moe_w_expert[i], jnp.zeros((D_MODEL, pad), F32)], axis=1)
        b_router = jnp.concatenate([moe_b_group[i], moe_b_expert[i], jnp.zeros((pad,), F32)])[None, :]
        if i % 2 == 0:
            zret = _mm_mod(x, mod, ev_w_in_t, (j,), RET_COLS, 1536, "in_proj_retention", w_is_transposed=True)
            *mla_caches, zmla, kcat, vm = _mla_proj_kv(x, mod, _mla_weight_t(ev_w_in_t, j), cos_m_id, sin_m_id,
                                                       mla_kv_norm_g, mla_w_uk, mla_w_uv, j, mla_caches)
            kr_ctx = jnp.tile(cache_mla_krope[:, j], (1, 1, LANES // MLA_ROPE))
            kcat_ctx, vm_ctx = _mla_kv_ctx(cache_mla_ckv, kr_ctx, mla_w_uk, mla_w_uv, j)
            a_ret_p, *ret_states = _retention(zret, decf, decb, j, False, prev=ret_states)
            (a_ret_s,) = _retention(zret, decf, decb, j, True, state_ret_fwd, state_ret_bwd)
            a_mla_p = _mla_attn_prompt(zmla, kcat, vm)
            a_mla_s = _mla_attn_latent(zmla, cos_m, sin_m, kcat_ctx, vm_ctx, kcat, vm)
            x1, meta, counts = _mm_ln([(a_ret_p, a_ret_s), (a_mla_p, a_mla_s)], ev_w_out, j, x, mod,
                                      ln1_g, ln1_b, w_router, b_router, i)
        else:
            lam_init = 0.8 - 0.6 * math.exp(-0.3 * i)
            zodd = _mm_mod(x, mod, od_w_in, (j,), 3072, 1536, "in_proj_diff")
            a_p, *diff_caches = _diff_attn_prompt(zodd, diff_lambda, diff_norm_g, j, lam_init, diff_caches)
            k_rot, v_aug = _diff_prep(zodd, cos_d, sin_d)
            a_s = _diff_attn_latent(zodd, diff_lambda, diff_norm_g, cos_d, sin_d, cache_diff_k, cache_diff_v,
                                    k_rot, v_aug, j, lam_init)
            x1, meta, counts = _mm_ln([(a_p, a_s)], od_w_out, j, x, mod, ln1_g, ln1_b, w_router, b_router, i)
        y = _moe(x1, _moe_plan(meta, counts), mods, w_router, b_router, moe_w1, moe_w3, moe_w2, ln2_g, ln2_b, i)
        x = (y, y, NTOK_P)

    y_prompt = x[0][:NTOK_P].reshape(BATCH, SEQ, D_MODEL)
    y_sample = x[1][NTOK_P:NTOK].reshape(DEC_BATCH, DEC_SEQ, D_MODEL)
    return (y_prompt, y_sample, ret_states[0], ret_states[1], mla_caches[0], mla_caches[1],
            diff_caches[0], diff_caches[1])
```
